```python
import jax, jax.numpy as jnp
from jax import lax
import numpy as np

D_MODEL = 1024
BATCH = 8
SEQ = 8192
DEPTH = 2

N_META = 16
D_MIX = D_MODEL
ATTN_HEADS = 8
HEAD_DIM = 64
D_ATTN = ATTN_HEADS * HEAD_DIM
D_CONF = D_MIX // 4
D_SC = D_MIX - D_ATTN - D_CONF
CONF_KERNEL = 31
SC_KERNEL = 3
D_FF = 4 * D_MODEL
Q_BLOCK = 128
EPS = 1e-6
N_IN = 3 * D_ATTN + ATTN_HEADS + 2 * D_CONF + 3 * D_SC

kernel_name = 'hybrid_fox_conformer_shortconv_block'


def _rmsnorm(x, g):
    xf = x.astype(jnp.float32)
    y = xf * lax.rsqrt(jnp.mean(xf * xf, axis=-1, keepdims=True) + EPS)
    return (y * g.astype(jnp.float32)).astype(x.dtype)


def _layernorm(x, g, b):
    xf = x.astype(jnp.float32)
    mu = jnp.mean(xf, axis=-1, keepdims=True)
    xc = xf - mu
    y = xc * lax.rsqrt(jnp.mean(xc * xc, axis=-1, keepdims=True) + EPS)
    return (y * g.astype(jnp.float32) + b.astype(jnp.float32)).astype(x.dtype)


def _causal_dwconv(x, w):
    K, C = w.shape
    return lax.conv_general_dilated(
        x, w[:, None, :].astype(x.dtype), window_strides=(1,), padding=[(K - 1, 0)],
        dimension_numbers=('NWC', 'WIO', 'NWC'), feature_group_count=C)


def _fox_attend(q_blk, c_q, q_pos, k, v, c_k, k_pos):
    s = jnp.einsum('bhqd,bhkd->bhqk', q_blk, k, preferred_element_type=jnp.float32) * (HEAD_DIM ** -0.5)
    s = s + c_q[..., :, None] - c_k[..., None, :]
    s = jnp.where(q_pos[:, None] >= k_pos[None, :], s, -jnp.inf)
    p = jax.nn.softmax(s, axis=-1)
    return jnp.einsum('bhqk,bhkd->bhqd', p.astype(v.dtype), v)


def _fox_attention(q, k, v, log_f):
    bsz, L, H, dh = q.shape
    n_real = L - N_META
    nb = n_real // Q_BLOCK
    c = jnp.cumsum(log_f, axis=1).transpose(0, 2, 1)
    qh = q.transpose(0, 2, 1, 3)
    kh = k.transpose(0, 2, 1, 3)
    vh = v.transpose(0, 2, 1, 3)
    k_pos = jnp.arange(L)
    meta_out = _fox_attend(qh[:, :, :N_META], c[:, :, :N_META], k_pos[:N_META],
                           kh[:, :, :N_META], vh[:, :, :N_META], c[:, :, :N_META], k_pos[:N_META])
    q_blocks = qh[:, :, N_META:].reshape(bsz, H, nb, Q_BLOCK, dh).transpose(2, 0, 1, 3, 4)
    c_blocks = c[:, :, N_META:].reshape(bsz, H, nb, Q_BLOCK).transpose(2, 0, 1, 3)

    def body(args):
        q_blk, c_q, i = args
        q_pos = N_META + i * Q_BLOCK + jnp.arange(Q_BLOCK)
        return _fox_attend(q_blk, c_q, q_pos, kh, vh, c, k_pos)

    real_out = lax.map(body, (q_blocks, c_blocks, jnp.arange(nb)))
    real_out = real_out.transpose(1, 0, 3, 2, 4).reshape(bsz, n_real, H * dh)
    meta_out = meta_out.transpose(0, 2, 1, 3).reshape(bsz, N_META, H * dh)
    return jnp.concatenate([meta_out, real_out], axis=1)


def _fwd_setup_inputs(seed: int = 0) -> dict:
    key = jax.random.key(seed)
    ks = jax.random.split(key, 20)
    nrm = jax.random.normal
    f32 = jnp.float32
    return {
        'x': nrm(ks[0], (BATCH, SEQ, D_MODEL), f32),
        'meta_tokens': nrm(ks[1], (N_META, D_MODEL), f32),
        'mix_norm_g': 1.0 + 0.1 * nrm(ks[2], (DEPTH, D_MODEL), f32),
        'w_in': nrm(ks[3], (DEPTH, D_MODEL, N_IN), f32) * D_MODEL ** -0.5,
        'b_forget': jax.random.uniform(ks[4], (DEPTH, ATTN_HEADS), f32, 1.0, 5.0),
        'w_conf_dw': nrm(ks[5], (DEPTH, CONF_KERNEL, D_CONF), f32) * CONF_KERNEL ** -0.5,
        'b_conf_dw': 0.02 * nrm(ks[6], (DEPTH, D_CONF), f32),
        'conf_ln_g': 1.0 + 0.1 * nrm(ks[7], (DEPTH, D_CONF), f32),
        'conf_ln_b': 0.02 * nrm(ks[8], (DEPTH, D_CONF), f32),
        'w_conf_pw': nrm(ks[9], (DEPTH, D_CONF, D_CONF), f32) * D_CONF ** -0.5,
        'b_conf_pw': 0.02 * nrm(ks[10], (DEPTH, D_CONF), f32),
        'w_sc_conv': nrm(ks[11], (DEPTH, SC_KERNEL, D_SC), f32) * SC_KERNEL ** -0.5,
        'w_out': nrm(ks[12], (DEPTH, D_MIX, D_MODEL), f32) * D_MIX ** -0.5,
        'mlp_norm_g': 1.0 + 0.1 * nrm(ks[13], (DEPTH, D_MODEL), f32),
        'w_mlp1': nrm(ks[14], (DEPTH, D_MODEL, D_FF), f32) * D_MODEL ** -0.5,
        'w_mlp2': nrm(ks[15], (DEPTH, D_FF, D_MODEL), f32) * D_FF ** -0.5,
        'final_norm_g': 1.0 + 0.1 * nrm(ks[16], (D_MODEL,), f32),
    }


def _fwd_reference(x, meta_tokens, mix_norm_g, w_in, b_forget, w_conf_dw, b_conf_dw, conf_ln_g,
              conf_ln_b, w_conf_pw, b_conf_pw, w_sc_conv, w_out, mlp_norm_g, w_mlp1, w_mlp2,
              final_norm_g):
    bsz = x.shape[0]
    meta = jnp.broadcast_to(meta_tokens[None].astype(x.dtype), (bsz, N_META, D_MODEL))
    h = jnp.concatenate([meta, x], axis=1)
    L = h.shape[1]
    sizes = [D_ATTN, D_ATTN, D_ATTN, ATTN_HEADS, D_CONF, D_CONF, D_SC, D_SC, D_SC]
    splits = np.cumsum(sizes)[:-1].tolist()
    for l in range(DEPTH):
        hn = _rmsnorm(h, mix_norm_g[l])
        proj = hn @ w_in[l]
        q, k, v, f_logit, conf_a, conf_gate, sc_b, sc_c, sc_u = jnp.split(proj, splits, axis=-1)
        log_f = jax.nn.log_sigmoid(f_logit.astype(jnp.float32) + b_forget[l].astype(jnp.float32))
        attn = _fox_attention(q.reshape(bsz, L, ATTN_HEADS, HEAD_DIM),
                              k.reshape(bsz, L, ATTN_HEADS, HEAD_DIM),
                              v.reshape(bsz, L, ATTN_HEADS, HEAD_DIM), log_f)
        glu = conf_a * jax.nn.sigmoid(conf_gate)
        dw = _causal_dwconv(glu, w_conf_dw[l]) + b_conf_dw[l]
        conf = jax.nn.silu(_layernorm(dw, conf_ln_g[l], conf_ln_b[l])) @ w_conf_pw[l] + b_conf_pw[l]
        sc = sc_b * _causal_dwconv(sc_c * sc_u, w_sc_conv[l])
        h = h + jnp.concatenate([attn, conf, sc], axis=-1) @ w_out[l]
        hn = _rmsnorm(h, mlp_norm_g[l])
        h = h + jnp.square(jax.nn.relu(hn @ w_mlp1[l])) @ w_mlp2[l]
    out = _rmsnorm(h, final_norm_g)
    return out[:, N_META:]


import jax as _jax
import jax.numpy as _jnp

TWIN_FORMAT = 'train_step'
FWD_PARAMS = ['x', 'meta_tokens', 'mix_norm_g', 'w_in', 'b_forget', 'w_conf_dw', 'b_conf_dw', 'conf_ln_g', 'conf_ln_b', 'w_conf_pw', 'b_conf_pw', 'w_sc_conv', 'w_out', 'mlp_norm_g', 'w_mlp1', 'w_mlp2', 'final_norm_g']
TWIN_WEIGHTS = ['meta_tokens', 'mix_norm_g', 'w_in', 'b_forget', 'w_conf_dw', 'b_conf_dw', 'conf_ln_g', 'conf_ln_b', 'w_conf_pw', 'b_conf_pw', 'w_sc_conv', 'w_out', 'mlp_norm_g', 'w_mlp1', 'w_mlp2', 'final_norm_g']
TWIN_DIFF_INPUT = 'x'
TWIN_INPUTS = ['x', 'meta_tokens', 'mix_norm_g', 'w_in', 'b_forget', 'w_conf_dw', 'b_conf_dw', 'conf_ln_g', 'conf_ln_b', 'w_conf_pw', 'b_conf_pw', 'w_sc_conv', 'w_out', 'mlp_norm_g', 'w_mlp1', 'w_mlp2', 'final_norm_g', 'loss_target', 'm_meta_tokens', 'm_mix_norm_g', 'm_w_in', 'm_b_forget', 'm_w_conf_dw', 'm_b_conf_dw', 'm_conf_ln_g', 'm_conf_ln_b', 'm_w_conf_pw', 'm_b_conf_pw', 'm_w_sc_conv', 'm_w_out', 'm_mlp_norm_g', 'm_w_mlp1', 'm_w_mlp2', 'm_final_norm_g', 'v_meta_tokens', 'v_mix_norm_g', 'v_w_in', 'v_b_forget', 'v_w_conf_dw', 'v_b_conf_dw', 'v_conf_ln_g', 'v_conf_ln_b', 'v_w_conf_pw', 'v_b_conf_pw', 'v_w_sc_conv', 'v_w_out', 'v_mlp_norm_g', 'v_w_mlp1', 'v_w_mlp2', 'v_final_norm_g']
TWIN_OUTPUTS = ['loss', 'grad_x', 'grad_meta_tokens', 'grad_mix_norm_g', 'grad_w_in', 'grad_b_forget', 'grad_w_conf_dw', 'grad_b_conf_dw', 'grad_conf_ln_g', 'grad_conf_ln_b', 'grad_w_conf_pw', 'grad_b_conf_pw', 'grad_w_sc_conv', 'grad_w_out', 'grad_mlp_norm_g', 'grad_w_mlp1', 'grad_w_mlp2', 'grad_final_norm_g', 'delta_meta_tokens', 'delta_mix_norm_g', 'delta_w_in', 'delta_b_forget', 'delta_w_conf_dw', 'delta_b_conf_dw', 'delta_conf_ln_g', 'delta_conf_ln_b', 'delta_w_conf_pw', 'delta_b_conf_pw', 'delta_w_sc_conv', 'delta_w_out', 'delta_mlp_norm_g', 'delta_w_mlp1', 'delta_w_mlp2', 'delta_final_norm_g', 'new_m_meta_tokens', 'new_m_mix_norm_g', 'new_m_w_in', 'new_m_b_forget', 'new_m_w_conf_dw', 'new_m_b_conf_dw', 'new_m_conf_ln_g', 'new_m_conf_ln_b', 'new_m_w_conf_pw', 'new_m_b_conf_pw', 'new_m_w_sc_conv', 'new_m_w_out', 'new_m_mlp_norm_g', 'new_m_w_mlp1', 'new_m_w_mlp2', 'new_m_final_norm_g', 'new_v_meta_tokens', 'new_v_mix_norm_g', 'new_v_w_in', 'new_v_b_forget', 'new_v_w_conf_dw', 'new_v_b_conf_dw', 'new_v_conf_ln_g', 'new_v_conf_ln_b', 'new_v_w_conf_pw', 'new_v_b_conf_pw', 'new_v_w_sc_conv', 'new_v_w_out', 'new_v_mlp_norm_g', 'new_v_w_mlp1', 'new_v_w_mlp2', 'new_v_final_norm_g']
TWIN_LEAF_KINDS = {'loss': 'loss', 'grad_x': 'grad_x', 'grad_meta_tokens': 'grad_w', 'grad_mix_norm_g': 'grad_w', 'grad_w_in': 'grad_w', 'grad_b_forget': 'grad_w', 'grad_w_conf_dw': 'grad_w', 'grad_b_conf_dw': 'grad_w', 'grad_conf_ln_g': 'grad_w', 'grad_conf_ln_b': 'grad_w', 'grad_w_conf_pw': 'grad_w', 'grad_b_conf_pw': 'grad_w', 'grad_w_sc_conv': 'grad_w', 'grad_w_out': 'grad_w', 'grad_mlp_norm_g': 'grad_w', 'grad_w_mlp1': 'grad_w', 'grad_w_mlp2': 'grad_w', 'grad_final_norm_g': 'grad_w', 'delta_meta_tokens': 'delta_w', 'delta_mix_norm_g': 'delta_w', 'delta_w_in': 'delta_w', 'delta_b_forget': 'delta_w', 'delta_w_conf_dw': 'delta_w', 'delta_b_conf_dw': 'delta_w', 'delta_conf_ln_g': 'delta_w', 'delta_conf_ln_b': 'delta_w', 'delta_w_conf_pw': 'delta_w', 'delta_b_conf_pw': 'delta_w', 'delta_w_sc_conv': 'delta_w', 'delta_w_out': 'delta_w', 'delta_mlp_norm_g': 'delta_w', 'delta_w_mlp1': 'delta_w', 'delta_w_mlp2': 'delta_w', 'delta_final_norm_g': 'delta_w', 'new_m_meta_tokens': 'new_m', 'new_m_mix_norm_g': 'new_m', 'new_m_w_in': 'new_m', 'new_m_b_forget': 'new_m', 'new_m_w_conf_dw': 'new_m', 'new_m_b_conf_dw': 'new_m', 'new_m_conf_ln_g': 'new_m', 'new_m_conf_ln_b': 'new_m', 'new_m_w_conf_pw': 'new_m', 'new_m_b_conf_pw': 'new_m', 'new_m_w_sc_conv': 'new_m', 'new_m_w_out': 'new_m', 'new_m_mlp_norm_g': 'new_m', 'new_m_w_mlp1': 'new_m', 'new_m_w_mlp2': 'new_m', 'new_m_final_norm_g': 'new_m', 'new_v_meta_tokens': 'new_v', 'new_v_mix_norm_g': 'new_v', 'new_v_w_in': 'new_v', 'new_v_b_forget': 'new_v', 'new_v_w_conf_dw': 'new_v', 'new_v_b_conf_dw': 'new_v', 'new_v_conf_ln_g': 'new_v', 'new_v_conf_ln_b': 'new_v', 'new_v_w_conf_pw': 'new_v', 'new_v_b_conf_pw': 'new_v', 'new_v_w_sc_conv': 'new_v', 'new_v_w_out': 'new_v', 'new_v_mlp_norm_g': 'new_v', 'new_v_w_mlp1': 'new_v', 'new_v_w_mlp2': 'new_v', 'new_v_final_norm_g': 'new_v'}


def _forward(args):
    return _fwd_reference(*[args[k] for k in FWD_PARAMS])


def _output_shape():
    def fwd():
        inp = _fwd_setup_inputs(0)
        return _fwd_reference(*[inp[k] for k in FWD_PARAMS])
    out = _jax.eval_shape(fwd)
    return out.shape, out.dtype

N_MICROBATCH = 1
ADAM_LR = 0.001
ADAM_B1 = 0.9
ADAM_B2 = 0.999
ADAM_EPS = 1e-08
ADAM_WD = 0.01
ADAM_STEP = 10
PER_EXAMPLE_BATCH_AXIS = {'x': 0, 'loss_target': 0}
SHARED_INPUTS = []
_WEIGHT_DTYPES = {'meta_tokens': _jnp.float32, 'mix_norm_g': _jnp.float32, 'w_in': _jnp.float32, 'b_forget': _jnp.float32, 'w_conf_dw': _jnp.float32, 'b_conf_dw': _jnp.float32, 'conf_ln_g': _jnp.float32, 'conf_ln_b': _jnp.float32, 'w_conf_pw': _jnp.float32, 'b_conf_pw': _jnp.float32, 'w_sc_conv': _jnp.float32, 'w_out': _jnp.float32, 'mlp_norm_g': _jnp.float32, 'w_mlp1': _jnp.float32, 'w_mlp2': _jnp.float32, 'final_norm_g': _jnp.float32}
MOMENT_SCALE = {'meta_tokens': 8.712322e-03, 'mix_norm_g': 3.121754e-01, 'w_in': 1.903631e-01, 'b_forget': 2.892194e-01, 'w_conf_dw': 2.561329e-01, 'b_conf_dw': 2.552552e+00, 'conf_ln_g': 9.124008e-01, 'conf_ln_b': 1.398640e+00, 'w_conf_pw': 5.076665e-01, 'b_conf_pw': 2.643803e+00, 'w_sc_conv': 2.181139e-01, 'w_out': 3.654361e-01, 'mlp_norm_g': 3.247259e-01, 'w_mlp1': 1.789016e-01, 'w_mlp2': 1.123115e+00, 'final_norm_g': 6.571124e+01}


def _to_microbatches(a, axis):
    t = _jnp.moveaxis(a, axis, 0)
    t = t.reshape((N_MICROBATCH, t.shape[0] // N_MICROBATCH) + t.shape[1:])
    return _jnp.moveaxis(t, 1, axis + 1)


def setup_inputs(seed: int = 0) -> dict:
    inp = _fwd_setup_inputs(seed)
    key = _jax.random.fold_in(_jax.random.key(seed), 7919)
    shape, _ = _output_shape()
    out = dict(inp)
    out["loss_target"] = _jax.random.normal(_jax.random.fold_in(key, 0), shape, _jnp.float32)
    for i, name in enumerate(TWIN_WEIGHTS):
        w = inp[name].astype(_jnp.float32)
        if MOMENT_SCALE is None:
            s = _jnp.sqrt(_jnp.mean(_jnp.square(w)) + 1e-30)
        else:
            s = MOMENT_SCALE[name]
        km, kv = _jax.random.split(_jax.random.fold_in(key, i + 1))
        out[name] = w
        out["m_" + name] = s * _jax.random.normal(km, w.shape, _jnp.float32)
        out["v_" + name] = (s * s) * _jax.random.uniform(kv, w.shape, _jnp.float32, 0.5, 1.5)
    if N_MICROBATCH > 1:
        for name, axis in PER_EXAMPLE_BATCH_AXIS.items():
            out[name] = _to_microbatches(out[name], axis)
    return {'x': out['x'], 'meta_tokens': out['meta_tokens'], 'mix_norm_g': out['mix_norm_g'], 'w_in': out['w_in'], 'b_forget': out['b_forget'], 'w_conf_dw': out['w_conf_dw'], 'b_conf_dw': out['b_conf_dw'], 'conf_ln_g': out['conf_ln_g'], 'conf_ln_b': out['conf_ln_b'], 'w_conf_pw': out['w_conf_pw'], 'b_conf_pw': out['b_conf_pw'], 'w_sc_conv': out['w_sc_conv'], 'w_out': out['w_out'], 'mlp_norm_g': out['mlp_norm_g'], 'w_mlp1': out['w_mlp1'], 'w_mlp2': out['w_mlp2'], 'final_norm_g': out['final_norm_g'], 'loss_target': out['loss_target'], 'm_meta_tokens': out['m_meta_tokens'], 'm_mix_norm_g': out['m_mix_norm_g'], 'm_w_in': out['m_w_in'], 'm_b_forget': out['m_b_forget'], 'm_w_conf_dw': out['m_w_conf_dw'], 'm_b_conf_dw': out['m_b_conf_dw'], 'm_conf_ln_g': out['m_conf_ln_g'], 'm_conf_ln_b': out['m_conf_ln_b'], 'm_w_conf_pw': out['m_w_conf_pw'], 'm_b_conf_pw': out['m_b_conf_pw'], 'm_w_sc_conv': out['m_w_sc_conv'], 'm_w_out': out['m_w_out'], 'm_mlp_norm_g': out['m_mlp_norm_g'], 'm_w_mlp1': out['m_w_mlp1'], 'm_w_mlp2': out['m_w_mlp2'], 'm_final_norm_g': out['m_final_norm_g'], 'v_meta_tokens': out['v_meta_tokens'], 'v_mix_norm_g': out['v_mix_norm_g'], 'v_w_in': out['v_w_in'], 'v_b_forget': out['v_b_forget'], 'v_w_conf_dw': out['v_w_conf_dw'], 'v_b_conf_dw': out['v_b_conf_dw'], 'v_conf_ln_g': out['v_conf_ln_g'], 'v_conf_ln_b': out['v_conf_ln_b'], 'v_w_conf_pw': out['v_w_conf_pw'], 'v_b_conf_pw': out['v_b_conf_pw'], 'v_w_sc_conv': out['v_w_sc_conv'], 'v_w_out': out['v_w_out'], 'v_mlp_norm_g': out['v_mlp_norm_g'], 'v_w_mlp1': out['v_w_mlp1'], 'v_w_mlp2': out['v_w_mlp2'], 'v_final_norm_g': out['v_final_norm_g']}


def _loss(weights, diff, rest, loss_target):
    with _jax.named_scope("forward"):
        args = {**rest, TWIN_DIFF_INPUT: diff, **{k: w.astype(_WEIGHT_DTYPES[k]) for k, w in weights.items()}}
        y = _forward(args)
    with _jax.named_scope("loss_head"):
        err = _jnp.square(y.astype(_jnp.float32) - loss_target)
        return 0.5 * _jnp.sum(_jnp.mean(err, axis=-1)) if err.ndim else 0.5 * err


def _adamw(w, g, m, v):
    m = ADAM_B1 * m + (1.0 - ADAM_B1) * g
    v = ADAM_B2 * v + (1.0 - ADAM_B2) * _jnp.square(g)
    m_hat = m / (1.0 - ADAM_B1 ** ADAM_STEP)
    v_hat = v / (1.0 - ADAM_B2 ** ADAM_STEP)
    delta = -ADAM_LR * (m_hat / (_jnp.sqrt(v_hat) + ADAM_EPS) + ADAM_WD * w)
    return delta, m, v


def reference(x, meta_tokens, mix_norm_g, w_in, b_forget, w_conf_dw, b_conf_dw, conf_ln_g, conf_ln_b, w_conf_pw, b_conf_pw, w_sc_conv, w_out, mlp_norm_g, w_mlp1, w_mlp2, final_norm_g, loss_target, m_meta_tokens, m_mix_norm_g, m_w_in, m_b_forget, m_w_conf_dw, m_b_conf_dw, m_conf_ln_g, m_conf_ln_b, m_w_conf_pw, m_b_conf_pw, m_w_sc_conv, m_w_out, m_mlp_norm_g, m_w_mlp1, m_w_mlp2, m_final_norm_g, v_meta_tokens, v_mix_norm_g, v_w_in, v_b_forget, v_w_conf_dw, v_b_conf_dw, v_conf_ln_g, v_conf_ln_b, v_w_conf_pw, v_b_conf_pw, v_w_sc_conv, v_w_out, v_mlp_norm_g, v_w_mlp1, v_w_mlp2, v_final_norm_g):
    given = dict(x=x, meta_tokens=meta_tokens, mix_norm_g=mix_norm_g, w_in=w_in, b_forget=b_forget, w_conf_dw=w_conf_dw, b_conf_dw=b_conf_dw, conf_ln_g=conf_ln_g, conf_ln_b=conf_ln_b, w_conf_pw=w_conf_pw, b_conf_pw=b_conf_pw, w_sc_conv=w_sc_conv, w_out=w_out, mlp_norm_g=mlp_norm_g, w_mlp1=w_mlp1, w_mlp2=w_mlp2, final_norm_g=final_norm_g, loss_target=loss_target, m_meta_tokens=m_meta_tokens, m_mix_norm_g=m_mix_norm_g, m_w_in=m_w_in, m_b_forget=m_b_forget, m_w_conf_dw=m_w_conf_dw, m_b_conf_dw=m_b_conf_dw, m_conf_ln_g=m_conf_ln_g, m_conf_ln_b=m_conf_ln_b, m_w_conf_pw=m_w_conf_pw, m_b_conf_pw=m_b_conf_pw, m_w_sc_conv=m_w_sc_conv, m_w_out=m_w_out, m_mlp_norm_g=m_mlp_norm_g, m_w_mlp1=m_w_mlp1, m_w_mlp2=m_w_mlp2, m_final_norm_g=m_final_norm_g, v_meta_tokens=v_meta_tokens, v_mix_norm_g=v_mix_norm_g, v_w_in=v_w_in, v_b_forget=v_b_forget, v_w_conf_dw=v_w_conf_dw, v_b_conf_dw=v_b_conf_dw, v_conf_ln_g=v_conf_ln_g, v_conf_ln_b=v_conf_ln_b, v_w_conf_pw=v_w_conf_pw, v_b_conf_pw=v_b_conf_pw, v_w_sc_conv=v_w_sc_conv, v_w_out=v_w_out, v_mlp_norm_g=v_mlp_norm_g, v_w_mlp1=v_w_mlp1, v_w_mlp2=v_w_mlp2, v_final_norm_g=v_final_norm_g)
    weights = {n: given[n] for n in TWIN_WEIGHTS}
    shared = {n: given[n] for n in SHARED_INPUTS}
    per_example = {n: given[n] for n in ['x']}
    grad_fn = _jax.value_and_grad(_loss, argnums=(0, 1))

    def one_microbatch(ex, loss_target):
        ex = dict(ex)
        diff = ex.pop(TWIN_DIFF_INPUT)
        return grad_fn(weights, diff, {**shared, **ex}, loss_target)

    if N_MICROBATCH == 1:
        loss, (grad_w, grad_x) = one_microbatch(per_example, given["loss_target"])
    else:
        def body(carry, xs):
            loss_sum, grad_sum = carry
            l_k, (gw_k, gx_k) = one_microbatch(xs[0], xs[1])
            with _jax.named_scope("update"):
                return (loss_sum + l_k, _jax.tree.map(_jnp.add, grad_sum, gw_k)), gx_k

        init = (_jnp.zeros((), _jnp.float32), _jax.tree.map(_jnp.zeros_like, weights))
        (loss, grad_w), grad_x = _jax.lax.scan(body, init, (per_example, given["loss_target"]))
    with _jax.named_scope("update"):
        delta_w, new_m, new_v = {}, {}, {}
        for n in TWIN_WEIGHTS:
            delta_w[n], new_m[n], new_v[n] = _adamw(weights[n], grad_w[n], given["m_" + n], given["v_" + n])
    return (loss, grad_x, *[grad_w[n] for n in TWIN_WEIGHTS], *[delta_w[n] for n in TWIN_WEIGHTS],
            *[new_m[n] for n in TWIN_WEIGHTS], *[new_v[n] for n in TWIN_WEIGHTS])
```

```python
import functools

import jax
import jax.numpy as jnp
from jax import lax
from jax.experimental import pallas as pl
from jax.experimental.pallas import tpu as pltpu

F32 = jnp.float32
MM = jnp.bfloat16

D = 1024
H = 8
DH = 64
DA = H * DH
DC = 256
NM = 16
CK = 31
SK = 3
DFF = 4096
DEPTH = 2
N_IN = 3 * DA + H + 2 * DC + 3 * DC
NP_IN = 3 * DA + 5 * DC + 128
C0 = 3 * DA
F0 = 3 * DA + 5 * DC
EPS = 1e-6
TM = 640
HALO = 32
FC = 512
NDEV = 8
SCALE = DH ** -0.5
NEG = -1e30

ADAM_LR, ADAM_B1, ADAM_B2, ADAM_EPS, ADAM_WD, ADAM_STEP = 0.001, 0.9, 0.999, 1e-08, 0.01, 10

VMEM_LIMIT = 56 * 1024 * 1024

S = jax.ShapeDtypeStruct
NT = (((1,), (1,)), ((), ()))
TN = (((0,), (0,)), ((), ()))


def _params(n_grid):
    return pltpu.CompilerParams(dimension_semantics=("arbitrary",) * n_grid, vmem_limit_bytes=VMEM_LIMIT)


def _sigmoid(x):
    return 1.0 / (1.0 + jnp.exp(-x))


def _full(shape):
    n = len(shape)
    return pl.BlockSpec(shape, lambda *_: (0,) * n)


def _lane_put(dst, col, h):
    lane = lax.broadcasted_iota(jnp.int32, dst.shape, 1)
    return jnp.where(lane == h, col, dst)


def exchange(srcs, per_peer, name):
    k_n = len(srcs)
    out_shape = [S((NDEV,) + (a.shape[1:] if pp else a.shape), a.dtype) for a, pp in zip(srcs, per_peer)]

    def body(*refs):
        src, out = refs[:k_n], refs[k_n:2 * k_n]
        send_sems, recv_sems, local_sems = refs[2 * k_n:]
        x, y, c = lax.axis_index("x"), lax.axis_index("y"), lax.axis_index("c")
        me = 4 * x + 2 * y + c

        def piece(k, p):
            return src[k].at[p] if per_peer[k] else src[k]

        local = [pltpu.make_async_copy(piece(k, me), out[k].at[me], local_sems.at[k]) for k in range(k_n)]
        for cp in local:
            cp.start()
        sends, recvs = [], []
        for r in range(1, NDEV):
            rx, ry, rc = (r >> 2) & 1, (r >> 1) & 1, r & 1
            px = 1 - x if rx else x
            py = 1 - y if ry else y
            pc = 1 - c if rc else c
            pidx = 4 * px + 2 * py + pc
            for k in range(k_n):
                sends.append(pltpu.make_async_remote_copy(
                    src_ref=piece(k, pidx), dst_ref=out[k].at[me],
                    send_sem=send_sems.at[k, r - 1], recv_sem=recv_sems.at[k, r - 1],
                    device_id=(px, py, pc), device_id_type=pl.DeviceIdType.MESH))
                recvs.append(pltpu.make_async_remote_copy(
                    src_ref=piece(k, pidx), dst_ref=out[k].at[pidx],
                    send_sem=send_sems.at[k, r - 1], recv_sem=recv_sems.at[k, r - 1],
                    device_id=(px, py, pc), device_id_type=pl.DeviceIdType.MESH))
        for cp in sends:
            cp.start()
        for cp in recvs:
            cp.wait_recv()
        for cp in sends:
            cp.wait_send()
        for cp in local:
            cp.wait()

    any_spec = pl.BlockSpec(memory_space=pl.ANY)
    return pl.pallas_call(
        body, name=name, out_shape=out_shape,
        in_specs=[any_spec] * k_n, out_specs=[any_spec] * k_n,
        scratch_shapes=[pltpu.SemaphoreType.DMA((k_n, NDEV - 1)), pltpu.SemaphoreType.DMA((k_n, NDEV - 1)),
                        pltpu.SemaphoreType.DMA((k_n,))],
    )(*srcs)


def in_proj(h, g, w, bf, name):
    tp = h.shape[0]

    def body(h_ref, g_ref, w_ref, bf_ref, hn_ref, qkv_ref, pc_ref, c_ref, sg_ref, carry):
        i = pl.program_id(0)

        @pl.when(i == 0)
        def _():
            carry[...] = jnp.zeros_like(carry)

        x = h_ref[...]
        r = lax.rsqrt(jnp.mean(x * x, axis=-1, keepdims=True) + EPS)
        hn = (x * r * g_ref[...]).astype(MM)
        hn_ref[...] = hn
        qkv = jnp.dot(hn, w_ref[:, 0:C0], preferred_element_type=F32)
        qkv_ref[:, 0:DA] = (qkv[:, 0:DA] * SCALE).astype(MM)
        qkv_ref[:, DA:C0] = qkv[:, DA:C0].astype(MM)
        pc_ref[...] = jnp.dot(hn, w_ref[:, C0:F0], preferred_element_type=F32)
        z = jnp.dot(hn, w_ref[:, F0:NP_IN], preferred_element_type=F32) + bf_ref[...]
        lane = lax.broadcasted_iota(jnp.int32, z.shape, 1)
        logf = jnp.where(lane < H, jnp.minimum(z, 0.0) - jnp.log(1.0 + jnp.exp(-jnp.abs(z))), 0.0)
        sg_ref[...] = 1.0 / (1.0 + jnp.exp(z))
        row = lax.broadcasted_iota(jnp.int32, (TM, TM), 0)
        col = lax.broadcasted_iota(jnp.int32, (TM, TM), 1)
        tri = (row >= col).astype(F32)
        c = jnp.dot(tri, logf, precision=lax.Precision.HIGHEST, preferred_element_type=F32) + carry[...]
        c_ref[...] = c
        carry[...] = c[TM - 1:TM, :]

    rows = lambda n: pl.BlockSpec((TM, n), lambda i: (i, 0))
    return pl.pallas_call(
        body, name=name, grid=(tp // TM,),
        in_specs=[rows(D), _full((1, D)), _full((D, NP_IN)), _full((1, 128))],
        out_specs=[rows(D), rows(C0), rows(5 * DC), rows(128), rows(128)],
        out_shape=[S((tp, D), MM), S((tp, C0), MM), S((tp, 5 * DC), F32), S((tp, 128), F32), S((tp, 128), F32)],
        scratch_shapes=[pltpu.VMEM((1, 128), F32)],
        compiler_params=_params(1),
    )(h, g, w, bf)


def _causal_mask(i, j, transposed=False):
    a = lax.broadcasted_iota(jnp.int32, (TM, TM), 0)
    b = lax.broadcasted_iota(jnp.int32, (TM, TM), 1)
    if transposed:
        return (i * TM + b) >= (j * TM + a)
    return (i * TM + a) >= (j * TM + b)


def attn_fwd(qkv, ccol, crow, name):
    tp = qkv.shape[0]
    nb = tp // TM

    def body(q_ref, k_ref, v_ref, cc_ref, cr_ref, o_ref, lse_ref, m_scr, l_scr, acc_scr):
        i, j = pl.program_id(0), pl.program_id(1)

        @pl.when(j == 0)
        def _():
            m_scr[...] = jnp.full(m_scr.shape, NEG, F32)
            l_scr[...] = jnp.zeros_like(l_scr)
            acc_scr[...] = jnp.zeros_like(acc_scr)

        @pl.when(j <= i)
        def _():
            mask = _causal_mask(i, j)
            cc, cr = cc_ref[...], cr_ref[...]
            for h in range(H):
                hs = slice(h * DH, (h + 1) * DH)
                s = lax.dot_general(q_ref[:, hs], k_ref[:, hs], NT, preferred_element_type=F32)
                s = jnp.where(mask, s + cc[:, h:h + 1] - cr[h:h + 1, :], NEG)
                m_prev = m_scr[h]
                m_new = jnp.maximum(m_prev, jnp.max(s, axis=1, keepdims=True))
                alpha = jnp.exp(m_prev - m_new)
                p = jnp.exp(s - m_new)
                l_scr[h] = alpha * l_scr[h] + jnp.sum(p, axis=1, keepdims=True)
                acc_scr[h] = alpha * acc_scr[h] + jnp.dot(p.astype(MM), v_ref[:, hs], preferred_element_type=F32)
                m_scr[h] = m_new

        @pl.when(j == i)
        def _():
            lse = jnp.zeros((TM, 128), F32)
            outs = []
            for h in range(H):
                l = l_scr[h]
                outs.append(acc_scr[h] / l)
                lse = _lane_put(lse, m_scr[h] + jnp.log(l), h)
            o_ref[...] = jnp.concatenate(outs, axis=1).astype(MM)
            lse_ref[...] = lse

    return pl.pallas_call(
        body, name=name, grid=(nb, nb),
        in_specs=[pl.BlockSpec((TM, DA), lambda i, j: (i, 0)),
                  pl.BlockSpec((TM, DA), lambda i, j: (jnp.minimum(j, i), 1)),
                  pl.BlockSpec((TM, DA), lambda i, j: (jnp.minimum(j, i), 2)),
                  pl.BlockSpec((TM, 128), lambda i, j: (i, 0)),
                  pl.BlockSpec((H, TM), lambda i, j: (0, jnp.minimum(j, i)))],
        out_specs=[pl.BlockSpec((TM, DA), lambda i, j: (i, 0)), pl.BlockSpec((TM, 128), lambda i, j: (i, 0))],
        out_shape=[S((tp, DA), MM), S((tp, 128), F32)],
        scratch_shapes=[pltpu.VMEM((H, TM, 1), F32), pltpu.VMEM((H, TM, 1), F32), pltpu.VMEM((H, TM, DH), F32)],
        compiler_params=_params(2),
    )(qkv, qkv, qkv, ccol, crow)


def _layernorm_parts(dw, g, b):
    mu = jnp.mean(dw, axis=-1, keepdims=True)
    xc = dw - mu
    rs = lax.rsqrt(jnp.mean(xc * xc, axis=-1, keepdims=True) + EPS)
    xh = xc * rs
    return xh, rs, xh * g + b


def _fill_halo(ext, halo, cur, first):
    ext[0:HALO, :] = jnp.where(first, 0.0, halo)
    ext[HALO:HALO + TM, :] = cur


def conv_fwd(pc, wdw, bdw, lng, lnb, wpw, bpw, wsc, name):
    tp = pc.shape[0]

    def body(pc_ref, hl_ref, wdw_ref, bdw_ref, lng_ref, lnb_ref, wpw_ref, bpw_ref, wsc_ref, cc_ref, sv_ref, xe, ce):
        first = pl.program_id(0) == 0
        glu = pc_ref[:, 0:DC] * _sigmoid(pc_ref[:, DC:2 * DC])
        cu = pc_ref[:, 3 * DC:4 * DC] * pc_ref[:, 4 * DC:5 * DC]
        _fill_halo(xe, hl_ref[:, 0:DC] * _sigmoid(hl_ref[:, DC:2 * DC]), glu, first)
        _fill_halo(ce, hl_ref[:, 3 * DC:4 * DC] * hl_ref[:, 4 * DC:5 * DC], cu, first)
        dw = jnp.zeros((TM, DC), F32) + bdw_ref[...]
        for k in range(CK):
            dw = dw + wdw_ref[k:k + 1, :] * xe[pl.ds(HALO - (CK - 1) + k, TM), :]
        cv = jnp.zeros((TM, DC), F32)
        for k in range(SK):
            cv = cv + wsc_ref[k:k + 1, :] * ce[pl.ds(HALO - (SK - 1) + k, TM), :]
        _, _, ln = _layernorm_parts(dw, lng_ref[...], lnb_ref[...])
        act = ln * _sigmoid(ln)
        conf = jnp.dot(act.astype(MM), wpw_ref[...], preferred_element_type=F32) + bpw_ref[...]
        cc_ref[:, 0:DC] = conf.astype(MM)
        cc_ref[:, DC:2 * DC] = (pc_ref[:, 2 * DC:3 * DC] * cv).astype(MM)
        sv_ref[:, 0:DC] = dw
        sv_ref[:, DC:2 * DC] = cv

    hb = TM // HALO
    rows = lambda n: pl.BlockSpec((TM, n), lambda i: (i, 0))
    return pl.pallas_call(
        body, name=name, grid=(tp // TM,),
        in_specs=[rows(5 * DC), pl.BlockSpec((HALO, 5 * DC), lambda i: (jnp.maximum(i * hb - 1, 0), 0)),
                  _full((32, DC)), _full((1, DC)), _full((1, DC)), _full((1, DC)), _full((DC, DC)), _full((1, DC)),
                  _full((8, DC))],
        out_specs=[rows(2 * DC), rows(2 * DC)],
        out_shape=[S((tp, 2 * DC), MM), S((tp, 2 * DC), F32)],
        scratch_shapes=[pltpu.VMEM((HALO + TM, DC), F32), pltpu.VMEM((HALO + TM, DC), F32)],
        compiler_params=_params(1),
    )(pc, pc, wdw, bdw, lng, lnb, wpw, bpw, wsc)


def out_proj(h, o, cc, wo, g2, name):
    tp = h.shape[0]

    def body(h_ref, o_ref, cc_ref, wo_ref, g_ref, h2_ref, hn_ref):
        h2 = (h_ref[...] + jnp.dot(o_ref[...], wo_ref[0:DA, :], preferred_element_type=F32)
              + jnp.dot(cc_ref[...], wo_ref[DA:D, :], preferred_element_type=F32))
        h2_ref[...] = h2
        r = lax.rsqrt(jnp.mean(h2 * h2, axis=-1, keepdims=True) + EPS)
        hn_ref[...] = (h2 * r * g_ref[...]).astype(MM)

    rows = lambda n: pl.BlockSpec((TM, n), lambda i: (i, 0))
    return pl.pallas_call(
        body, name=name, grid=(tp // TM,),
        in_specs=[rows(D), rows(DA), rows(2 * DC), _full((D, D)), _full((1, D))],
        out_specs=[rows(D), rows(D)],
        out_shape=[S((tp, D), F32), S((tp, D), MM)],
        compiler_params=_params(1),
    )(h, o, cc, wo, g2)


def mlp_fwd(hn, h2, w1, w2, name):
    tp = hn.shape[0]
    nf = DFF // FC

    def body(hn_ref, h2_ref, w1_ref, w2_ref, r_ref, z_ref, h3_ref, acc):
        j = pl.program_id(1)

        @pl.when(j == 0)
        def _():
            acc[...] = jnp.zeros_like(acc)

        r = jnp.maximum(jnp.dot(hn_ref[...], w1_ref[...], preferred_element_type=F32), 0.0)
        zb = (r * r).astype(MM)
        r_ref[...] = r.astype(MM)
        z_ref[...] = zb
        acc[...] += jnp.dot(zb, w2_ref[...], preferred_element_type=F32)

        @pl.when(j == nf - 1)
        def _():
            h3_ref[...] = h2_ref[...] + acc[...]

    return pl.pallas_call(
        body, name=name, grid=(tp // TM, nf),
        in_specs=[pl.BlockSpec((TM, D), lambda i, j: (i, 0)), pl.BlockSpec((TM, D), lambda i, j: (i, 0)),
                  pl.BlockSpec((D, FC), lambda i, j: (0, j)), pl.BlockSpec((FC, D), lambda i, j: (j, 0))],
        out_specs=[pl.BlockSpec((TM, FC), lambda i, j: (i, j)), pl.BlockSpec((TM, FC), lambda i, j: (i, j)),
                   pl.BlockSpec((TM, D), lambda i, j: (i, 0))],
        out_shape=[S((tp, DFF), MM), S((tp, DFF), MM), S((tp, D), F32)],
        scratch_shapes=[pltpu.VMEM((TM, D), F32)],
        compiler_params=_params(2),
    )(hn, h2, w1, w2)


def loss_head(h, tgt, g, t_real, name):
    tp = h.shape[0]

    def body(h_ref, t_ref, g_ref, dh_ref, loss_ref, dg_ref):
        i = pl.program_id(0)

        @pl.when(i == 0)
        def _():
            loss_ref[...] = jnp.zeros_like(loss_ref)
            dg_ref[...] = jnp.zeros_like(dg_ref)

        x = h_ref[...]
        gg = g_ref[...]
        r = lax.rsqrt(jnp.mean(x * x, axis=-1, keepdims=True) + EPS)
        xn = x * r
        row = i * TM + lax.broadcasted_iota(jnp.int32, (TM, 1), 0)
        e = jnp.where((row >= NM) & (row < t_real), xn * gg - t_ref[...], 0.0)
        loss_ref[...] += jnp.sum(e * e) * (0.5 / D)
        dy = e * (1.0 / D)
        dg_ref[...] += jnp.sum(dy * xn, axis=0, keepdims=True)
        u = dy * gg
        dh_ref[...] = r * (u - xn * jnp.mean(u * xn, axis=-1, keepdims=True))

    rows = lambda n: pl.BlockSpec((TM, n), lambda i: (i, 0))
    return pl.pallas_call(
        body, name=name, grid=(tp // TM,),
        in_specs=[rows(D), rows(D), _full((1, D))],
        out_specs=[rows(D), _full((8, 128)), _full((1, D))],
        out_shape=[S((tp, D), F32), S((8, 128), F32), S((1, D), F32)],
        compiler_params=_params(1),
    )(h, tgt, g)


def _rmsnorm_bwd(x, g, dy):
    r = lax.rsqrt(jnp.mean(x * x, axis=-1, keepdims=True) + EPS)
    xn = x * r
    u = dy * g
    dx = r * (u - xn * jnp.mean(u * xn, axis=-1, keepdims=True))
    return dx, jnp.sum(dy * xn, axis=0, keepdims=True)


def matmul_tn(a, b, tm, tn, name, a_off=0, b_off=0, m=None, n=None):
    tp = a.shape[0]
    m = a.shape[1] if m is None else m
    n = b.shape[1] if n is None else n

    def body(a_ref, b_ref, o_ref):
        @pl.when(pl.program_id(2) == 0)
        def _():
            o_ref[...] = jnp.zeros_like(o_ref)

        o_ref[...] += lax.dot_general(a_ref[...].astype(MM), b_ref[...].astype(MM), TN, preferred_element_type=F32)

    return pl.pallas_call(
        body, name=name, grid=(m // tm, n // tn, tp // TM),
        in_specs=[pl.BlockSpec((TM, tm), lambda mi, ni, k: (k, mi + a_off)),
                  pl.BlockSpec((TM, tn), lambda mi, ni, k: (k, ni + b_off))],
        out_specs=pl.BlockSpec((tm, tn), lambda mi, ni, k: (mi, ni)),
        out_shape=S((m, n), F32),
        compiler_params=_params(3),
    )(a, b)


def mlp_bwd(dh3, r, w1, w2, h2, g, name):
    tp = dh3.shape[0]
    nf = DFF // FC

    def body(dh3_ref, r_ref, w1_ref, w2_ref, h2_ref, g_ref, da_ref, dh2_ref, dg_ref, acc, dhb):
        i, j = pl.program_id(0), pl.program_id(1)

        @pl.when((i == 0) & (j == 0))
        def _():
            dg_ref[...] = jnp.zeros_like(dg_ref)

        @pl.when(j == 0)
        def _():
            acc[...] = jnp.zeros_like(acc)
            dhb[...] = dh3_ref[...].astype(MM)

        dz = lax.dot_general(dhb[...], w2_ref[...], NT, preferred_element_type=F32)
        da = (dz * (2.0 * r_ref[...].astype(F32))).astype(MM)
        da_ref[...] = da
        acc[...] += lax.dot_general(da, w1_ref[...], NT, preferred_element_type=F32)

        @pl.when(j == nf - 1)
        def _():
            dx, dg = _rmsnorm_bwd(h2_ref[...], g_ref[...], acc[...])
            dh2_ref[...] = dh3_ref[...] + dx
            dg_ref[...] += dg

    return pl.pallas_call(
        body, name=name, grid=(tp // TM, nf),
        in_specs=[pl.BlockSpec((TM, D), lambda i, j: (i, 0)), pl.BlockSpec((TM, FC), lambda i, j: (i, j)),
                  pl.BlockSpec((D, FC), lambda i, j: (0, j)), pl.BlockSpec((FC, D), lambda i, j: (j, 0)),
                  pl.BlockSpec((TM, D), lambda i, j: (i, 0)), _full((1, D))],
        out_specs=[pl.BlockSpec((TM, FC), lambda i, j: (i, j)), pl.BlockSpec((TM, D), lambda i, j: (i, 0)),
                   _full((1, D))],
        out_shape=[S((tp, DFF), MM), S((tp, D), F32), S((1, D), F32)],
        scratch_shapes=[pltpu.VMEM((TM, D), F32), pltpu.VMEM((TM, D), MM)],
        compiler_params=_params(2),
    )(dh3, r, w1, w2, h2, g)


def out_proj_bwd(dh2, wo, name):
    tp = dh2.shape[0]

    def body(dh_ref, wo_ref, do_ref, dcc_ref):
        dcat = lax.dot_general(dh_ref[...].astype(MM), wo_ref[...], NT, preferred_element_type=F32)
        do_ref[...] = dcat[:, 0:DA].astype(MM)
        dcc_ref[...] = dcat[:, DA:D]

    rows = lambda n: pl.BlockSpec((TM, n), lambda i: (i, 0))
    return pl.pallas_call(
        body, name=name, grid=(tp // TM,),
        in_specs=[rows(D), _full((D, D))],
        out_specs=[rows(DA), rows(2 * DC)],
        out_shape=[S((tp, DA), MM), S((tp, 2 * DC), F32)],
        compiler_params=_params(1),
    )(dh2, wo)


def attn_bwd_dq(qkv, do, o, lse, ccol, crow, name):
    tp = qkv.shape[0]
    nb = tp // TM

    def body(q_ref, k_ref, v_ref, do_ref, o_ref, lse_ref, cc_ref, cr_ref, dq_ref, dcq_ref, dd_ref, dq_acc, dcq_acc, dd_scr):
        i, j = pl.program_id(0), pl.program_id(1)

        @pl.when(j == 0)
        def _():
            dq_acc[...] = jnp.zeros_like(dq_acc)
            dcq_acc[...] = jnp.zeros_like(dcq_acc)
            dd = jnp.zeros((TM, 128), F32)
            for h in range(H):
                hs = slice(h * DH, (h + 1) * DH)
                dd = _lane_put(dd, jnp.sum(do_ref[:, hs].astype(F32) * o_ref[:, hs].astype(F32), axis=1, keepdims=True), h)
            dd_scr[...] = dd

        @pl.when(j <= i)
        def _():
            mask = _causal_mask(i, j)
            cc, cr, lse, dd = cc_ref[...], cr_ref[...], lse_ref[...], dd_scr[...]
            dcq = dcq_acc[...]
            for h in range(H):
                hs = slice(h * DH, (h + 1) * DH)
                s = lax.dot_general(q_ref[:, hs], k_ref[:, hs], NT, preferred_element_type=F32)
                s = jnp.where(mask, s + cc[:, h:h + 1] - cr[h:h + 1, :], NEG)
                p = jnp.exp(s - lse[:, h:h + 1])
                dp = lax.dot_general(do_ref[:, hs], v_ref[:, hs], NT, preferred_element_type=F32)
                ds = p * (dp - dd[:, h:h + 1])
                dq_acc[h] += jnp.dot(ds.astype(MM), k_ref[:, hs], preferred_element_type=F32)
                lane = lax.broadcasted_iota(jnp.int32, dcq.shape, 1)
                dcq = dcq + jnp.where(lane == h, jnp.sum(ds, axis=1, keepdims=True), 0.0)
            dcq_acc[...] = dcq

        @pl.when(j == i)
        def _():
            dq_ref[...] = (jnp.concatenate([dq_acc[h] for h in range(H)], axis=1) * SCALE).astype(MM)
            dcq_ref[...] = dcq_acc[...]
            dd_ref[...] = dd_scr[...]

    qi = lambda n: pl.BlockSpec((TM, n), lambda i, j: (i, 0))
    return pl.pallas_call(
        body, name=name, grid=(nb, nb),
        in_specs=[qi(DA), pl.BlockSpec((TM, DA), lambda i, j: (jnp.minimum(j, i), 1)),
                  pl.BlockSpec((TM, DA), lambda i, j: (jnp.minimum(j, i), 2)),
                  qi(DA), qi(DA), qi(128), qi(128),
                  pl.BlockSpec((H, TM), lambda i, j: (0, jnp.minimum(j, i)))],
        out_specs=[qi(DA), qi(128), qi(128)],
        out_shape=[S((tp, DA), MM), S((tp, 128), F32), S((tp, 128), F32)],
        scratch_shapes=[pltpu.VMEM((H, TM, DH), F32), pltpu.VMEM((TM, 128), F32), pltpu.VMEM((TM, 128), F32)],
        compiler_params=_params(2),
    )(qkv, qkv, qkv, do, o, lse, ccol, crow)


def attn_bwd_dkv(qkv, do, lse_row, dd_row, ccol, crow, name):
    tp = qkv.shape[0]
    nb = tp // TM

    def body(q_ref, k_ref, v_ref, do_ref, lr_ref, dr_ref, cc_ref, cr_ref, dk_ref, dv_ref, dck_ref, dk_acc, dv_acc, dck_acc):
        j, i = pl.program_id(0), pl.program_id(1)

        @pl.when(i == 0)
        def _():
            dk_acc[...] = jnp.zeros_like(dk_acc)
            dv_acc[...] = jnp.zeros_like(dv_acc)
            dck_acc[...] = jnp.zeros_like(dck_acc)

        @pl.when(i >= j)
        def _():
            mask = _causal_mask(i, j, transposed=True)
            cc, cr, lr, dr = cc_ref[...], cr_ref[...], lr_ref[...], dr_ref[...]
            dck = dck_acc[...]
            for h in range(H):
                hs = slice(h * DH, (h + 1) * DH)
                st = lax.dot_general(k_ref[:, hs], q_ref[:, hs], NT, preferred_element_type=F32)
                st = jnp.where(mask, st + cr[h:h + 1, :] - cc[:, h:h + 1], NEG)
                pt = jnp.exp(st - lr[h:h + 1, :])
                dv_acc[h] += jnp.dot(pt.astype(MM), do_ref[:, hs], preferred_element_type=F32)
                dpt = lax.dot_general(v_ref[:, hs], do_ref[:, hs], NT, preferred_element_type=F32)
                dst = pt * (dpt - dr[h:h + 1, :])
                dk_acc[h] += jnp.dot(dst.astype(MM), q_ref[:, hs], preferred_element_type=F32)
                lane = lax.broadcasted_iota(jnp.int32, dck.shape, 1)
                dck = dck + jnp.where(lane == h, jnp.sum(dst, axis=1, keepdims=True), 0.0)
            dck_acc[...] = dck

        @pl.when(i == nb - 1)
        def _():
            dk_ref[...] = jnp.concatenate([dk_acc[h] for h in range(H)], axis=1).astype(MM)
            dv_ref[...] = jnp.concatenate([dv_acc[h] for h in range(H)], axis=1).astype(MM)
            dck_ref[...] = dck_acc[...]

    kj = lambda n: pl.BlockSpec((TM, n), lambda j, i: (j, 0))
    qrow = pl.BlockSpec((H, TM), lambda j, i: (0, jnp.maximum(i, j)))
    return pl.pallas_call(
        body, name=name, grid=(nb, nb),
        in_specs=[pl.BlockSpec((TM, DA), lambda j, i: (jnp.maximum(i, j), 0)),
                  pl.BlockSpec((TM, DA), lambda j, i: (j, 1)), pl.BlockSpec((TM, DA), lambda j, i: (j, 2)),
                  pl.BlockSpec((TM, DA), lambda j, i: (jnp.maximum(i, j), 0)),
                  qrow, qrow, kj(128), qrow],
        out_specs=[kj(DA), kj(DA), kj(128)],
        out_shape=[S((tp, DA), MM), S((tp, DA), MM), S((tp, 128), F32)],
        scratch_shapes=[pltpu.VMEM((H, TM, DH), F32), pltpu.VMEM((H, TM, DH), F32), pltpu.VMEM((TM, 128), F32)],
        compiler_params=_params(2),
    )(qkv, qkv, qkv, do, lse_row, dd_row, ccol, crow)


def fgate_bwd(dcq, dck, sg, name):
    tp = dcq.shape[0]
    nb = tp // TM

    def body(dcq_ref, dck_ref, sg_ref, dz_ref, db_ref, carry):
        @pl.when(pl.program_id(0) == 0)
        def _():
            carry[...] = jnp.zeros_like(carry)
            db_ref[...] = jnp.zeros_like(db_ref)

        row = lax.broadcasted_iota(jnp.int32, (TM, TM), 0)
        col = lax.broadcasted_iota(jnp.int32, (TM, TM), 1)
        tri = (row <= col).astype(F32)
        dl = jnp.dot(tri, dcq_ref[...] - dck_ref[...], precision=lax.Precision.HIGHEST,
                     preferred_element_type=F32) + carry[...]
        carry[...] = dl[0:1, :]
        dz = dl * sg_ref[...]
        dz_ref[...] = dz.astype(MM)
        db_ref[...] += jnp.sum(dz, axis=0, keepdims=True)

    rev = pl.BlockSpec((TM, 128), lambda i: (nb - 1 - i, 0))
    return pl.pallas_call(
        body, name=name, grid=(nb,),
        in_specs=[rev, rev, rev], out_specs=[rev, _full((1, 128))],
        out_shape=[S((tp, 128), MM), S((1, 128), F32)],
        scratch_shapes=[pltpu.VMEM((1, 128), F32)],
        compiler_params=_params(1),
    )(dcq, dck, sg)


def conv_bwd_pointwise(dcc, sv, pc, lng, lnb, wpw, name):
    tp = dcc.shape[0]

    def body(dcc_ref, sv_ref, b_ref, lng_ref, lnb_ref, wpw_ref, gc_ref, act_ref, pg_ref):
        @pl.when(pl.program_id(0) == 0)
        def _():
            pg_ref[...] = jnp.zeros_like(pg_ref)

        dconf = dcc_ref[:, 0:DC]
        g = lng_ref[...]
        xh, rs, ln = _layernorm_parts(sv_ref[:, 0:DC], g, lnb_ref[...])
        sig = _sigmoid(ln)
        act_ref[...] = (ln * sig).astype(MM)
        dact = lax.dot_general(dconf.astype(MM), wpw_ref[...], NT, preferred_element_type=F32)
        dln = dact * (sig * (1.0 + ln * (1.0 - sig)))
        dxh = dln * g
        ddw = rs * (dxh - jnp.mean(dxh, axis=-1, keepdims=True) - xh * jnp.mean(dxh * xh, axis=-1, keepdims=True))
        gc_ref[:, 0:DC] = ddw
        gc_ref[:, DC:2 * DC] = dcc_ref[:, DC:2 * DC] * b_ref[...]
        cs = lambda t: jnp.sum(t, axis=0, keepdims=True)
        pg_ref[0:1, :] += cs(dconf)
        pg_ref[1:2, :] += cs(dln * xh)
        pg_ref[2:3, :] += cs(dln)
        pg_ref[3:4, :] += cs(ddw)

    rows = lambda n: pl.BlockSpec((TM, n), lambda i: (i, 0))
    return pl.pallas_call(
        body, name=name, grid=(tp // TM,),
        in_specs=[rows(2 * DC), rows(2 * DC), pl.BlockSpec((TM, DC), lambda i: (i, 2)),
                  _full((1, DC)), _full((1, DC)), _full((DC, DC))],
        out_specs=[rows(2 * DC), rows(DC), _full((8, DC))],
        out_shape=[S((tp, 2 * DC), F32), S((tp, DC), MM), S((8, DC), F32)],
        compiler_params=_params(1),
    )(dcc, sv, pc, lng, lnb, wpw)


def conv_bwd_taps(gc, pc, dcc, sv, wdw, wsc, name):
    tp = gc.shape[0]
    nb = tp // TM
    hb = TM // HALO

    def body(gc_ref, gn_ref, pc_ref, hl_ref, dcc_ref, sv_ref, wdw_ref, wsc_ref, dpc_ref, wg_ref, ge, xe, ce):
        i = pl.program_id(0)

        @pl.when(i == 0)
        def _():
            wg_ref[...] = jnp.zeros_like(wg_ref)

        a, gt = pc_ref[:, 0:DC], pc_ref[:, DC:2 * DC]
        c, u = pc_ref[:, 3 * DC:4 * DC], pc_ref[:, 4 * DC:5 * DC]
        sig = _sigmoid(gt)
        _fill_halo(xe, hl_ref[:, 0:DC] * _sigmoid(hl_ref[:, DC:2 * DC]), a * sig, i == 0)
        _fill_halo(ce, hl_ref[:, 3 * DC:4 * DC] * hl_ref[:, 4 * DC:5 * DC], c * u, i == 0)
        ge[0:TM, :] = gc_ref[...]
        ge[TM:TM + HALO, :] = jnp.where(i == nb - 1, 0.0, gn_ref[...])
        ddw, dcv = gc_ref[:, 0:DC], gc_ref[:, DC:2 * DC]
        dglu = jnp.zeros((TM, DC), F32)
        for k in range(CK):
            dglu = dglu + wdw_ref[k:k + 1, :] * ge[pl.ds(CK - 1 - k, TM), 0:DC]
            wg_ref[k:k + 1, :] += jnp.sum(ddw * xe[pl.ds(HALO - (CK - 1) + k, TM), :], axis=0, keepdims=True)
        dcu = jnp.zeros((TM, DC), F32)
        for k in range(SK):
            dcu = dcu + wsc_ref[k:k + 1, :] * ge[pl.ds(SK - 1 - k, TM), DC:2 * DC]
            wg_ref[32 + k:33 + k, :] += jnp.sum(dcv * ce[pl.ds(HALO - (SK - 1) + k, TM), :], axis=0, keepdims=True)
        dpc_ref[:, 0:DC] = (dglu * sig).astype(MM)
        dpc_ref[:, DC:2 * DC] = (dglu * a * sig * (1.0 - sig)).astype(MM)
        dpc_ref[:, 2 * DC:3 * DC] = (dcc_ref[:, DC:2 * DC] * sv_ref[:, DC:2 * DC]).astype(MM)
        dpc_ref[:, 3 * DC:4 * DC] = (dcu * u).astype(MM)
        dpc_ref[:, 4 * DC:5 * DC] = (dcu * c).astype(MM)

    rows = lambda n: pl.BlockSpec((TM, n), lambda i: (i, 0))
    return pl.pallas_call(
        body, name=name, grid=(nb,),
        in_specs=[rows(2 * DC), pl.BlockSpec((HALO, 2 * DC), lambda i: (jnp.minimum((i + 1) * hb, nb * hb - 1), 0)),
                  rows(5 * DC), pl.BlockSpec((HALO, 5 * DC), lambda i: (jnp.maximum(i * hb - 1, 0), 0)),
                  rows(2 * DC), rows(2 * DC), _full((32, DC)), _full((8, DC))],
        out_specs=[rows(5 * DC), _full((40, DC))],
        out_shape=[S((tp, 5 * DC), MM), S((40, DC), F32)],
        scratch_shapes=[pltpu.VMEM((TM + HALO, 2 * DC), F32), pltpu.VMEM((HALO + TM, DC), F32),
                        pltpu.VMEM((HALO + TM, DC), F32)],
        compiler_params=_params(1),
    )(gc, gc, pc, pc, dcc, sv, wdw, wsc)


def in_proj_bwd(dproj, w, h, g, dh_in, name):
    tp = h.shape[0]

    def body(dp_ref, w_ref, h_ref, g_ref, di_ref, dh_ref, dg_ref):
        @pl.when(pl.program_id(0) == 0)
        def _():
            dg_ref[...] = jnp.zeros_like(dg_ref)

        dhn = lax.dot_general(dp_ref[...], w_ref[...], NT, preferred_element_type=F32)
        dx, dg = _rmsnorm_bwd(h_ref[...], g_ref[...], dhn)
        dh_ref[...] = di_ref[...] + dx
        dg_ref[...] += dg

    rows = lambda n: pl.BlockSpec((TM, n), lambda i: (i, 0))
    return pl.pallas_call(
        body, name=name, grid=(tp // TM,),
        in_specs=[rows(NP_IN), _full((D, NP_IN)), rows(D), _full((1, D)), rows(D)],
        out_specs=[rows(D), _full((1, D))],
        out_shape=[S((tp, D), F32), S((1, D), F32)],
        compiler_params=_params(1),
    )(dproj, w, h, g, dh_in)


def adamw(recv, w, m, v, rb, name):
    l_n, r_n, c_n = w.shape

    def body(p_ref, w_ref, m_ref, v_ref, g_ref, d_ref, m2_ref, v2_ref):
        g = p_ref[0, 0]
        for s in range(1, NDEV):
            g = g + p_ref[s, 0]
        m2 = ADAM_B1 * m_ref[0] + (1.0 - ADAM_B1) * g
        v2 = ADAM_B2 * v_ref[0] + (1.0 - ADAM_B2) * (g * g)
        m_hat = m2 / (1.0 - ADAM_B1 ** ADAM_STEP)
        v_hat = v2 / (1.0 - ADAM_B2 ** ADAM_STEP)
        g_ref[0] = g
        d_ref[0] = -ADAM_LR * (m_hat / (jnp.sqrt(v_hat) + ADAM_EPS) + ADAM_WD * w_ref[0])
        m2_ref[0] = m2
        v2_ref[0] = v2

    blk = pl.BlockSpec((1, rb, c_n), lambda l, r: (l, r, 0))
    return pl.pallas_call(
        body, name=name, grid=(l_n, r_n // rb),
        in_specs=[pl.BlockSpec((NDEV, 1, rb, c_n), lambda l, r: (0, l, r, 0)), blk, blk, blk],
        out_specs=[blk] * 4, out_shape=[S(w.shape, F32)] * 4,
        compiler_params=_params(2),
    )(recv, w, m, v)


TINY_ROWS = 168
REP_ROWS = 64


def _pack(parts, rows):
    flat = jnp.concatenate([p.reshape(-1) for p in parts])
    return jnp.pad(flat, (0, rows * 128 - flat.shape[0])).reshape(rows, 128)


def _unpack(buf, shapes):
    flat, out, o = buf.reshape(-1), [], 0
    for s in shapes:
        n = 1
        for d in s:
            n *= d
        out.append(flat[o:o + n].reshape(s))
        o += n
    return out


TINY_SHAPES = [(DEPTH, CK, DC // NDEV), (DEPTH, SK, DC // NDEV), (NM, D // NDEV), (DEPTH, DC // NDEV, DC)]
REP_SHAPES = [(DEPTH, D), (DEPTH, H), (DEPTH, DC), (DEPTH, DC), (DEPTH, DC), (DEPTH, DC), (DEPTH, D), (D,)]


def _to_padded_cols(w):
    pad = jnp.zeros(w.shape[:-1] + (NP_IN - N_IN,), w.dtype)
    return jnp.concatenate([w[..., :C0], w[..., C0 + H:], w[..., C0:C0 + H], pad], axis=-1)


def _from_padded_cols(w):
    return jnp.concatenate([w[..., :C0], w[..., F0:F0 + H], w[..., C0:F0]], axis=-1)


def kernel(x, meta_tokens, mix_norm_g, w_in, b_forget, w_conf_dw, b_conf_dw, conf_ln_g, conf_ln_b, w_conf_pw, b_conf_pw, w_sc_conv, w_out, mlp_norm_g, w_mlp1, w_mlp2, final_norm_g, loss_target, m_meta_tokens, m_mix_norm_g, m_w_in, m_b_forget, m_w_conf_dw, m_b_conf_dw, m_conf_ln_g, m_conf_ln_b, m_w_conf_pw, m_b_conf_pw, m_w_sc_conv, m_w_out, m_mlp_norm_g, m_w_mlp1, m_w_mlp2, m_final_norm_g, v_meta_tokens, v_mix_norm_g, v_w_in, v_b_forget, v_w_conf_dw, v_b_conf_dw, v_conf_ln_g, v_conf_ln_b, v_w_conf_pw, v_b_conf_pw, v_w_sc_conv, v_w_out, v_mlp_norm_g, v_w_mlp1, v_w_mlp2, v_final_norm_g):
    seq = x.shape[1]
    t_real = NM + seq
    tp = -(-t_real // TM) * TM

    tiny_w = _pack([w_conf_dw, w_sc_conv, meta_tokens, w_conf_pw], TINY_ROWS)
    g_in, g_out, g_w1, g_w2, g_tiny = exchange(
        [w_in.astype(MM), w_out.astype(MM), w_mlp1.astype(MM), w_mlp2.astype(MM), tiny_w],
        [False] * 5, "gather_weights")
    n_sh = w_in.shape[-1]
    win = _to_padded_cols(g_in.transpose(1, 2, 0, 3).reshape(DEPTH, D, NDEV * n_sh))
    wout = g_out.transpose(1, 0, 2, 3).reshape(DEPTH, D, D)
    w1 = g_w1.transpose(1, 2, 0, 3).reshape(DEPTH, D, DFF)
    w2 = g_w2.transpose(1, 0, 2, 3).reshape(DEPTH, DFF, D)
    tiny = [_unpack(g_tiny[s], TINY_SHAPES) for s in range(NDEV)]
    wdw = jnp.concatenate([t[0] for t in tiny], axis=-1)
    wsc = jnp.concatenate([t[1] for t in tiny], axis=-1)
    meta = jnp.concatenate([t[2] for t in tiny], axis=-1)
    wpw = jnp.concatenate([t[3] for t in tiny], axis=1).astype(MM)
    wdw = jnp.pad(wdw, ((0, 0), (0, 32 - CK), (0, 0)))
    wsc = jnp.pad(wsc, ((0, 0), (0, 8 - SK), (0, 0)))
    bfp = jnp.pad(b_forget, ((0, 0), (0, 128 - H)))

    row = lambda a: a.reshape(1, -1)

    h = jnp.concatenate([meta, x[0], jnp.zeros((tp - t_real, D), F32)], axis=0)
    tgt = jnp.pad(loss_target[0], ((NM, tp - t_real), (0, 0)))
    saved = []
    for l in range(DEPTH):
        hn, qkv, pc, ccol, sg = in_proj(h, row(mix_norm_g[l]), win[l], row(bfp[l]), f"in_proj{l}")
        crow = ccol[:, :H].T
        o, lse = attn_fwd(qkv, ccol, crow, f"attn_fwd{l}")
        cc, sv = conv_fwd(pc, wdw[l], row(b_conf_dw[l]), row(conf_ln_g[l]), row(conf_ln_b[l]), wpw[l],
                          row(b_conf_pw[l]), wsc[l], f"conv_fwd{l}")
        h2, hn2 = out_proj(h, o, cc, wout[l], row(mlp_norm_g[l]), f"out_proj{l}")
        r, z, h3 = mlp_fwd(hn2, h2, w1[l], w2[l], f"mlp_fwd{l}")
        saved.append((h, hn, qkv, pc, ccol, crow, sg, o, lse, cc, sv, h2, hn2, r, z))
        h = h3

    dh, loss_part, d_gf = loss_head(h, tgt, row(final_norm_g), t_real, "loss_head")
    loss = lax.psum(loss_part[0, 0], ("x", "y", "c"))

    gw = {}
    for l in reversed(range(DEPTH)):
        h0, hn, qkv, pc, ccol, crow, sg, o, lse, cc, sv, h2, hn2, r, z = saved[l]
        da, dh2, gw["mlp_g", l] = mlp_bwd(dh, r, w1[l], w2[l], h2, row(mlp_norm_g[l]), f"mlp_bwd{l}")
        gw["w1", l] = matmul_tn(hn2, da, D, FC, f"dw_mlp1_{l}")
        gw["w2", l] = matmul_tn(z, dh, FC, D, f"dw_mlp2_{l}")
        do, dcc = out_proj_bwd(dh2, wout[l], f"out_proj_bwd{l}")
        gw["wo", l] = jnp.concatenate([matmul_tn(o, dh2, DA, D, f"dw_out_a{l}"),
                                       matmul_tn(cc, dh2, 2 * DC, D, f"dw_out_c{l}")], axis=0)
        dq, dcq, dd = attn_bwd_dq(qkv, do, o, lse, ccol, crow, f"attn_bwd_dq{l}")
        dk, dv, dck = attn_bwd_dkv(qkv, do, lse[:, :H].T, dd[:, :H].T, ccol, crow, f"attn_bwd_dkv{l}")
        dz, gw["bf", l] = fgate_bwd(dcq, dck, sg, f"fgate_bwd{l}")
        gc, act, pg = conv_bwd_pointwise(dcc, sv, pc, row(conf_ln_g[l]), row(conf_ln_b[l]), wpw[l], f"conv_bwd_pw{l}")
        gw["wpw", l] = matmul_tn(act, dcc, DC, DC, f"dw_conf_pw{l}", n=DC)
        dpc, wg = conv_bwd_taps(gc, pc, dcc, sv, wdw[l], wsc[l], f"conv_bwd_taps{l}")
        gw["pg", l], gw["wg", l] = pg, wg
        dproj = jnp.concatenate([dq, dk, dv, dpc, dz], axis=1)
        gw["win", l] = matmul_tn(hn, dproj, 512, NP_IN, f"dw_in{l}")
        dh, gw["mix_g", l] = in_proj_bwd(dproj, win[l], h0, row(mix_norm_g[l]), dh2, f"in_proj_bwd{l}")

    grad_x = dh[NM:t_real][None]
    stack = lambda k: jnp.stack([gw[k, l] for l in range(DEPTH)])

    d_win = _from_padded_cols(stack("win")).reshape(DEPTH, D, NDEV, n_sh).transpose(2, 0, 1, 3)
    d_wout = stack("wo").reshape(DEPTH, NDEV, D // NDEV, D).transpose(1, 0, 2, 3)
    d_w1 = stack("w1").reshape(DEPTH, D, NDEV, DFF // NDEV).transpose(2, 0, 1, 3)
    d_w2 = stack("w2").reshape(DEPTH, NDEV, DFF // NDEV, D).transpose(1, 0, 2, 3)
    d_wdw = stack("wg")[:, 0:CK].reshape(DEPTH, CK, NDEV, DC // NDEV).transpose(2, 0, 1, 3)
    d_wsc = stack("wg")[:, 32:32 + SK].reshape(DEPTH, SK, NDEV, DC // NDEV).transpose(2, 0, 1, 3)
    d_meta = dh[0:NM].reshape(NM, NDEV, D // NDEV).transpose(1, 0, 2)
    d_wpw = stack("wpw").reshape(DEPTH, NDEV, DC // NDEV, DC).transpose(1, 0, 2, 3)
    d_tiny = jnp.stack([_pack([d_wdw[p], d_wsc[p], d_meta[p], d_wpw[p]], TINY_ROWS) for p in range(NDEV)])
    pgs = stack("pg")
    d_rep = _pack([stack("mix_g").reshape(DEPTH, D), stack("bf")[:, 0, :H], pgs[:, 3], pgs[:, 1], pgs[:, 2], pgs[:, 0],
                   stack("mlp_g").reshape(DEPTH, D), d_gf.reshape(D)], REP_ROWS)
    r_win, r_wout, r_w1, r_w2, r_tiny, r_rep = exchange(
        [d_win, d_wout, d_w1, d_w2, d_tiny, d_rep], [True, True, True, True, True, False], "exchange_grads")

    def upd(recv, w, m, v, rb, name):
        shp = w.shape
        w3 = lambda a: a.reshape((-1,) + shp[-2:])
        outs = adamw(recv.reshape((NDEV,) + w3(w).shape), w3(w), w3(m), w3(v), rb, name)
        return [a.reshape(shp) for a in outs]

    res = {}
    res["w_in"] = upd(r_win, w_in, m_w_in, v_w_in, 256, "adamw_w_in")
    res["w_out"] = upd(r_wout, w_out, m_w_out, v_w_out, D // NDEV, "adamw_w_out")
    res["w_mlp1"] = upd(r_w1, w_mlp1, m_w_mlp1, v_w_mlp1, 256, "adamw_w_mlp1")
    res["w_mlp2"] = upd(r_w2, w_mlp2, m_w_mlp2, v_w_mlp2, 128, "adamw_w_mlp2")
    tiny_names = ["w_conf_dw", "w_sc_conv", "meta_tokens", "w_conf_pw"]
    tiny_wmv = [[w_conf_dw, w_sc_conv, meta_tokens, w_conf_pw], [m_w_conf_dw, m_w_sc_conv, m_meta_tokens, m_w_conf_pw],
                [v_w_conf_dw, v_w_sc_conv, v_meta_tokens, v_w_conf_pw]]
    rep_names = ["mix_norm_g", "b_forget", "b_conf_dw", "conf_ln_g", "conf_ln_b", "b_conf_pw", "mlp_norm_g", "final_norm_g"]
    rep_wmv = [[mix_norm_g, b_forget, b_conf_dw, conf_ln_g, conf_ln_b, b_conf_pw, mlp_norm_g, final_norm_g],
               [m_mix_norm_g, m_b_forget, m_b_conf_dw, m_conf_ln_g, m_conf_ln_b, m_b_conf_pw, m_mlp_norm_g, m_final_norm_g],
               [v_mix_norm_g, v_b_forget, v_b_conf_dw, v_conf_ln_g, v_conf_ln_b, v_b_conf_pw, v_mlp_norm_g, v_final_norm_g]]
    for names, wmv, shapes, recv, rows_n, nm in ((tiny_names, tiny_wmv, TINY_SHAPES, r_tiny, TINY_ROWS, "adamw_tiny"),
                                                 (rep_names, rep_wmv, REP_SHAPES, r_rep, REP_ROWS, "adamw_rep")):
        packed = [_pack(group, rows_n)[None] for group in wmv]
        outs = adamw(recv[:, None], *packed, rows_n, nm)
        parts = [_unpack(a[0], shapes) for a in outs]
        for k, n in enumerate(names):
            res[n] = [parts[q][k] for q in range(4)]

    order = ["meta_tokens", "mix_norm_g", "w_in", "b_forget", "w_conf_dw", "b_conf_dw", "conf_ln_g", "conf_ln_b",
             "w_conf_pw", "b_conf_pw", "w_sc_conv", "w_out", "mlp_norm_g", "w_mlp1", "w_mlp2", "final_norm_g"]
    return (loss, grad_x, *[res[n][0] for n in order], *[res[n][1] for n in order],
            *[res[n][2] for n in order], *[res[n][3] for n in order])
```

```python
import functools

import jax
import jax.numpy as jnp
from jax import lax
from jax.experimental import pallas as pl
from jax.experimental.pallas import tpu as pltpu

F32 = jnp.float32
MM = jnp.bfloat16

D = 1024
H = 8
DH = 64
DA = H * DH
HT = 128
DC = 256
NM = 16
CK = 31
SK = 3
DFF = 4096
DEPTH = 2
N_IN = 3 * DA + H + 2 * DC + 3 * DC
NP_IN = 3 * DA + 5 * DC + 128
C0 = 3 * DA
F0 = 3 * DA + 5 * DC
EPS = 1e-6
TM = 640
HALO = 32
FC = 512
NDEV = 8
SCALE = DH ** -0.5
NEG = -1e30

ADAM_LR, ADAM_B1, ADAM_B2, ADAM_EPS, ADAM_WD, ADAM_STEP = 0.001, 0.9, 0.999, 1e-08, 0.01, 10

VMEM_LIMIT = 56 * 1024 * 1024

S = jax.ShapeDtypeStruct
NT = (((1,), (1,)), ((), ()))
TN = (((0,), (0,)), ((), ()))


def _params(n_grid):
    return pltpu.CompilerParams(dimension_semantics=("arbitrary",) * n_grid, vmem_limit_bytes=VMEM_LIMIT)


def _sigmoid(x):
    return 1.0 / (1.0 + jnp.exp(-x))


def _full(shape):
    n = len(shape)
    return pl.BlockSpec(shape, lambda *_: (0,) * n)


def _lane_put(dst, col, h):
    lane = lax.broadcasted_iota(jnp.int32, dst.shape, 1)
    return jnp.where(lane == h, col, dst)


def exchange(srcs, per_peer, name):
    k_n = len(srcs)
    out_shape = [S((NDEV,) + (a.shape[1:] if pp else a.shape), a.dtype) for a, pp in zip(srcs, per_peer)]

    def body(*refs):
        src, out = refs[:k_n], refs[k_n:2 * k_n]
        send_sems, recv_sems, local_sems = refs[2 * k_n:]
        x, y, c = lax.axis_index("x"), lax.axis_index("y"), lax.axis_index("c")
        me = 4 * x + 2 * y + c

        def piece(k, p):
            return src[k].at[p] if per_peer[k] else src[k]

        local = [pltpu.make_async_copy(piece(k, me), out[k].at[me], local_sems.at[k]) for k in range(k_n)]
        for cp in local:
            cp.start()
        sends, recvs = [], []
        for r in range(1, NDEV):
            rx, ry, rc = (r >> 2) & 1, (r >> 1) & 1, r & 1
            px = 1 - x if rx else x
            py = 1 - y if ry else y
            pc = 1 - c if rc else c
            pidx = 4 * px + 2 * py + pc
            for k in range(k_n):
                sends.append(pltpu.make_async_remote_copy(
                    src_ref=piece(k, pidx), dst_ref=out[k].at[me],
                    send_sem=send_sems.at[k, r - 1], recv_sem=recv_sems.at[k, r - 1],
                    device_id=(px, py, pc), device_id_type=pl.DeviceIdType.MESH))
                recvs.append(pltpu.make_async_remote_copy(
                    src_ref=piece(k, pidx), dst_ref=out[k].at[pidx],
                    send_sem=send_sems.at[k, r - 1], recv_sem=recv_sems.at[k, r - 1],
                    device_id=(px, py, pc), device_id_type=pl.DeviceIdType.MESH))
        for cp in sends:
            cp.start()
        for cp in recvs:
            cp.wait_recv()
        for cp in sends:
            cp.wait_send()
        for cp in local:
            cp.wait()

    any_spec = pl.BlockSpec(memory_space=pl.ANY)
    return pl.pallas_call(
        body, name=name, out_shape=out_shape,
        in_specs=[any_spec] * k_n, out_specs=[any_spec] * k_n,
        scratch_shapes=[pltpu.SemaphoreType.DMA((k_n, NDEV - 1)), pltpu.SemaphoreType.DMA((k_n, NDEV - 1)),
                        pltpu.SemaphoreType.DMA((k_n,))],
    )(*srcs)


def gather(srcs, name):
    k_n = len(srcs)
    out_shape = [S((NDEV,) + a.shape, a.dtype) for a in srcs]

    def body(*refs):
        src, out = refs[:k_n], refs[k_n:2 * k_n]
        send_sems, recv_sems, local_sems = refs[2 * k_n:]
        x, y, c = lax.axis_index("x"), lax.axis_index("y"), lax.axis_index("c")
        me, sibling = (x, y, c), (x, y, 1 - c)
        chips = [(1 - x, y), (x, 1 - y), (1 - x, 1 - y)]

        def slot(k, dev):
            return out[k].at[4 * dev[0] + 2 * dev[1] + dev[2]]

        def copy(k, r, block, to, from_src=False):
            return pltpu.make_async_remote_copy(
                src_ref=src[k] if from_src else slot(k, block), dst_ref=slot(k, block),
                send_sem=send_sems.at[k, r], recv_sem=recv_sems.at[k, r],
                device_id=to, device_id_type=pl.DeviceIdType.MESH)

        local = [pltpu.make_async_copy(src[k], slot(k, me), local_sems.at[k]) for k in range(k_n)]
        first = [copy(k, 0, me, sibling, True) for k in range(k_n)]
        first += [copy(k, 1 + n, me, (*chip, c), True) for n, chip in enumerate(chips) for k in range(k_n)]
        for cp in local + first:
            cp.start()
        passed = []
        for n, chip in enumerate(chips):
            for k in range(k_n):
                copy(k, 1 + n, (*chip, c), me).wait_recv()
                passed.append(copy(k, 4 + n, (*chip, c), sibling))
                passed[-1].start()
        for k in range(k_n):
            copy(k, 0, sibling, me).wait_recv()
            for n, chip in enumerate(chips):
                copy(k, 4 + n, (*chip, 1 - c), me).wait_recv()
        for cp in first + passed:
            cp.wait_send()
        for cp in local:
            cp.wait()

    any_spec = pl.BlockSpec(memory_space=pl.ANY)
    return pl.pallas_call(
        body, name=name, out_shape=out_shape,
        in_specs=[any_spec] * k_n, out_specs=[any_spec] * k_n,
        scratch_shapes=[pltpu.SemaphoreType.DMA((k_n, NDEV - 1)), pltpu.SemaphoreType.DMA((k_n, NDEV - 1)),
                        pltpu.SemaphoreType.DMA((k_n,))],
    )(*srcs)


def in_proj(h, g, w, bf, name):
    tp = h.shape[0]

    def body(h_ref, g_ref, w_ref, bf_ref, hn_ref, qa_ref, ka_ref, va_ref, pc_ref, sg_ref, carry):
        i = pl.program_id(0)

        @pl.when(i == 0)
        def _():
            carry[...] = jnp.zeros_like(carry)

        x = h_ref[...]
        r = lax.rsqrt(jnp.mean(x * x, axis=-1, keepdims=True) + EPS)
        hn = (x * r * g_ref[...]).astype(MM)
        hn_ref[...] = hn
        pc_ref[...] = jnp.dot(hn, w_ref[:, C0:F0], preferred_element_type=F32)
        z = jnp.dot(hn, w_ref[:, F0:NP_IN], preferred_element_type=F32) + bf_ref[...]
        lane = lax.broadcasted_iota(jnp.int32, z.shape, 1)
        logf = jnp.where(lane < H, jnp.minimum(z, 0.0) - jnp.log(1.0 + jnp.exp(-jnp.abs(z))), 0.0)
        sg_ref[...] = 1.0 / (1.0 + jnp.exp(z))
        row = lax.broadcasted_iota(jnp.int32, (TM, TM), 0)
        col = lax.broadcasted_iota(jnp.int32, (TM, TM), 1)
        tri = (row >= col).astype(F32)
        c = jnp.dot(tri, logf, precision=lax.Precision.HIGHEST, preferred_element_type=F32) + carry[...]
        carry[...] = c[TM - 1:TM, :]
        qkv = jnp.dot(hn, w_ref[:, 0:C0], preferred_element_type=F32)
        one = [1.0, 1.0, 1.0]
        for hd in range(H):
            t = slice(hd * HT, (hd + 1) * HT)
            cs = list(_split3(c[:, hd:hd + 1]))
            qa_ref[:, t] = _aug_tile(qkv[:, hd * DH:(hd + 1) * DH] * SCALE, cs + one).astype(MM)
            ka_ref[:, t] = _aug_tile(qkv[:, DA + hd * DH:DA + (hd + 1) * DH], one + [-v for v in cs] + one).astype(MM)
            va_ref[:, t] = _aug_tile(qkv[:, 2 * DA + hd * DH:2 * DA + (hd + 1) * DH], [1.0] + one).astype(MM)

    rows = lambda n: pl.BlockSpec((TM, n), lambda i: (i, 0))
    return pl.pallas_call(
        body, name=name, grid=(tp // TM,),
        in_specs=[rows(D), _full((1, D)), _full((D, NP_IN)), _full((1, 128))],
        out_specs=[rows(D), rows(H * HT), rows(H * HT), rows(H * HT), rows(5 * DC), rows(128)],
        out_shape=[S((tp, D), MM), S((tp, H * HT), MM), S((tp, H * HT), MM), S((tp, H * HT), MM),
                   S((tp, 5 * DC), F32), S((tp, 128), F32)],
        scratch_shapes=[pltpu.VMEM((1, 128), F32)],
        compiler_params=_params(1),
    )(h, g, w, bf)


def _split3(c):
    hi = c.astype(MM).astype(F32)
    mid = (c - hi).astype(MM).astype(F32)
    lo = (c - hi - mid).astype(MM).astype(F32)
    return hi, mid, lo


def _aug_tile(main, cols):
    lane = lax.broadcasted_iota(jnp.int32, main.shape, 1)
    aug = jnp.zeros(main.shape, F32)
    for pos, val in enumerate(cols):
        aug = jnp.where(lane == pos, val, aug)
    return jnp.concatenate([main, aug], axis=1)


def _causal_mask(transposed=False):
    a = lax.broadcasted_iota(jnp.int32, (TM, TM), 0)
    b = lax.broadcasted_iota(jnp.int32, (TM, TM), 1)
    return (b >= a) if transposed else (a >= b)


def attn_fwd(qa, ka, va, name):
    tp = qa.shape[0]
    nb = tp // TM

    def body(q_ref, k_ref, v_ref, o_ref, lse_ref, *scr):
        m_scr, acc_scr = scr[:H], scr[H:]
        i, j = pl.program_id(0), pl.program_id(1)

        @pl.when(j == 0)
        def _():
            for hd in range(H):
                m_scr[hd][...] = jnp.full((TM, 1), NEG, F32)
                acc_scr[hd][...] = jnp.zeros((TM, HT), F32)

        def logits(hd):
            t = slice(hd * HT, (hd + 1) * HT)
            return lax.dot_general(q_ref[:, t], k_ref[:, t], NT, preferred_element_type=F32)

        def step(diag):
            s_next = logits(0)
            for hd in range(H):
                t = slice(hd * HT, (hd + 1) * HT)
                s = s_next
                if hd + 1 < H:
                    s_next = logits(hd + 1)
                if diag:
                    s = jnp.where(_causal_mask(), s, NEG)
                m_prev = m_scr[hd][...]
                m_new = jnp.maximum(m_prev, jnp.max(s, axis=1, keepdims=True))
                p = jnp.exp(s - m_new).astype(MM)
                acc_scr[hd][...] = (jnp.exp(m_prev - m_new) * acc_scr[hd][...]
                                    + jnp.dot(p, v_ref[:, t], preferred_element_type=F32))
                m_scr[hd][...] = m_new

        @pl.when(j < i)
        def _():
            step(False)

        @pl.when(j == i)
        def _():
            step(True)
            lse = jnp.zeros((TM, 128), F32)
            outs = []
            for hd in range(H):
                acc = acc_scr[hd][...]
                l = acc[:, DH:DH + 1]
                outs.append(acc[:, 0:DH] / l)
                lse = _lane_put(lse, m_scr[hd][...] + jnp.log(l), hd)
            o_ref[...] = jnp.concatenate(outs, axis=1).astype(MM)
            lse_ref[...] = lse

    kv = pl.BlockSpec((TM, H * HT), lambda i, j: (jnp.minimum(j, i), 0))
    return pl.pallas_call(
        body, name=name, grid=(nb, nb),
        in_specs=[pl.BlockSpec((TM, H * HT), lambda i, j: (i, 0)), kv, kv],
        out_specs=[pl.BlockSpec((TM, DA), lambda i, j: (i, 0)), pl.BlockSpec((TM, 128), lambda i, j: (i, 0))],
        out_shape=[S((tp, DA), MM), S((tp, 128), F32)],
        scratch_shapes=[pltpu.VMEM((TM, 1), F32)] * H + [pltpu.VMEM((TM, HT), F32)] * H,
        compiler_params=_params(2),
    )(qa, ka, va)


def _layernorm_parts(dw, g, b):
    mu = jnp.mean(dw, axis=-1, keepdims=True)
    xc = dw - mu
    rs = lax.rsqrt(jnp.mean(xc * xc, axis=-1, keepdims=True) + EPS)
    xh = xc * rs
    return xh, rs, xh * g + b


def _fill_halo(ext, halo, cur, first):
    ext[0:HALO, :] = jnp.where(first, 0.0, halo)
    ext[HALO:HALO + TM, :] = cur


def conv_fwd(pc, wdw, bdw, lng, lnb, wpw, bpw, wsc, name):
    tp = pc.shape[0]

    def body(pc_ref, hl_ref, wdw_ref, bdw_ref, lng_ref, lnb_ref, wpw_ref, bpw_ref, wsc_ref, cc_ref, sv_ref, xe, ce):
        first = pl.program_id(0) == 0
        glu = pc_ref[:, 0:DC] * _sigmoid(pc_ref[:, DC:2 * DC])
        cu = pc_ref[:, 3 * DC:4 * DC] * pc_ref[:, 4 * DC:5 * DC]
        _fill_halo(xe, hl_ref[:, 0:DC] * _sigmoid(hl_ref[:, DC:2 * DC]), glu, first)
        _fill_halo(ce, hl_ref[:, 3 * DC:4 * DC] * hl_ref[:, 4 * DC:5 * DC], cu, first)
        dw = jnp.zeros((TM, DC), F32) + bdw_ref[...]
        for k in range(CK):
            dw = dw + wdw_ref[k:k + 1, :] * xe[pl.ds(HALO - (CK - 1) + k, TM), :]
        cv = jnp.zeros((TM, DC), F32)
        for k in range(SK):
            cv = cv + wsc_ref[k:k + 1, :] * ce[pl.ds(HALO - (SK - 1) + k, TM), :]
        _, _, ln = _layernorm_parts(dw, lng_ref[...], lnb_ref[...])
        act = ln * _sigmoid(ln)
        conf = jnp.dot(act.astype(MM), wpw_ref[...], preferred_element_type=F32) + bpw_ref[...]
        cc_ref[:, 0:DC] = conf.astype(MM)
        cc_ref[:, DC:2 * DC] = (pc_ref[:, 2 * DC:3 * DC] * cv).astype(MM)
        sv_ref[:, 0:DC] = dw
        sv_ref[:, DC:2 * DC] = cv

    hb = TM // HALO
    rows = lambda n: pl.BlockSpec((TM, n), lambda i: (i, 0))
    return pl.pallas_call(
        body, name=name, grid=(tp // TM,),
        in_specs=[rows(5 * DC), pl.BlockSpec((HALO, 5 * DC), lambda i: (jnp.maximum(i * hb - 1, 0), 0)),
                  _full((32, DC)), _full((1, DC)), _full((1, DC)), _full((1, DC)), _full((DC, DC)), _full((1, DC)),
                  _full((8, DC))],
        out_specs=[rows(2 * DC), rows(2 * DC)],
        out_shape=[S((tp, 2 * DC), MM), S((tp, 2 * DC), F32)],
        scratch_shapes=[pltpu.VMEM((HALO + TM, DC), F32), pltpu.VMEM((HALO + TM, DC), F32)],
        compiler_params=_params(1),
    )(pc, pc, wdw, bdw, lng, lnb, wpw, bpw, wsc)


def out_proj(h, o, cc, wo, g2, name):
    tp = h.shape[0]

    def body(h_ref, o_ref, cc_ref, wo_ref, g_ref, h2_ref, hn_ref):
        h2 = (h_ref[...] + jnp.dot(o_ref[...], wo_ref[0:DA, :], preferred_element_type=F32)
              + jnp.dot(cc_ref[...], wo_ref[DA:D, :], preferred_element_type=F32))
        h2_ref[...] = h2
        r = lax.rsqrt(jnp.mean(h2 * h2, axis=-1, keepdims=True) + EPS)
        hn_ref[...] = (h2 * r * g_ref[...]).astype(MM)

    rows = lambda n: pl.BlockSpec((TM, n), lambda i: (i, 0))
    return pl.pallas_call(
        body, name=name, grid=(tp // TM,),
        in_specs=[rows(D), rows(DA), rows(2 * DC), _full((D, D)), _full((1, D))],
        out_specs=[rows(D), rows(D)],
        out_shape=[S((tp, D), F32), S((tp, D), MM)],
        compiler_params=_params(1),
    )(h, o, cc, wo, g2)


def mlp_fwd(hn, h2, w1, w2, name):
    tp = hn.shape[0]
    nf = DFF // FC

    def body(hn_ref, h2_ref, w1_ref, w2_ref, r_ref, z_ref, h3_ref, acc):
        j = pl.program_id(1)

        @pl.when(j == 0)
        def _():
            acc[...] = jnp.zeros_like(acc)

        r = jnp.maximum(jnp.dot(hn_ref[...], w1_ref[...], preferred_element_type=F32), 0.0)
        zb = (r * r).astype(MM)
        r_ref[...] = r.astype(MM)
        z_ref[...] = zb
        acc[...] += jnp.dot(zb, w2_ref[...], preferred_element_type=F32)

        @pl.when(j == nf - 1)
        def _():
            h3_ref[...] = h2_ref[...] + acc[...]

    return pl.pallas_call(
        body, name=name, grid=(tp // TM, nf),
        in_specs=[pl.BlockSpec((TM, D), lambda i, j: (i, 0)), pl.BlockSpec((TM, D), lambda i, j: (i, 0)),
                  pl.BlockSpec((D, FC), lambda i, j: (0, j)), pl.BlockSpec((FC, D), lambda i, j: (j, 0))],
        out_specs=[pl.BlockSpec((TM, FC), lambda i, j: (i, j)), pl.BlockSpec((TM, FC), lambda i, j: (i, j)),
                   pl.BlockSpec((TM, D), lambda i, j: (i, 0))],
        out_shape=[S((tp, DFF), MM), S((tp, DFF), MM), S((tp, D), F32)],
        scratch_shapes=[pltpu.VMEM((TM, D), F32)],
        compiler_params=_params(2),
    )(hn, h2, w1, w2)


def loss_head(h, tgt, g, t_real, name):
    tp = h.shape[0]

    def body(h_ref, t_ref, g_ref, dh_ref, loss_ref, dg_ref):
        i = pl.program_id(0)

        @pl.when(i == 0)
        def _():
            loss_ref[...] = jnp.zeros_like(loss_ref)
            dg_ref[...] = jnp.zeros_like(dg_ref)

        x = h_ref[...]
        gg = g_ref[...]
        r = lax.rsqrt(jnp.mean(x * x, axis=-1, keepdims=True) + EPS)
        xn = x * r
        row = i * TM + lax.broadcasted_iota(jnp.int32, (TM, 1), 0)
        e = jnp.where((row >= NM) & (row < t_real), xn * gg - t_ref[...], 0.0)
        loss_ref[...] += jnp.sum(e * e) * (0.5 / D)
        dy = e * (1.0 / D)
        dg_ref[...] += jnp.sum(dy * xn, axis=0, keepdims=True)
        u = dy * gg
        dh_ref[...] = r * (u - xn * jnp.mean(u * xn, axis=-1, keepdims=True))

    rows = lambda n: pl.BlockSpec((TM, n), lambda i: (i, 0))
    return pl.pallas_call(
        body, name=name, grid=(tp // TM,),
        in_specs=[rows(D), rows(D), _full((1, D))],
        out_specs=[rows(D), _full((8, 128)), _full((1, D))],
        out_shape=[S((tp, D), F32), S((8, 128), F32), S((1, D), F32)],
        compiler_params=_params(1),
    )(h, tgt, g)


def _rmsnorm_bwd(x, g, dy):
    r = lax.rsqrt(jnp.mean(x * x, axis=-1, keepdims=True) + EPS)
    xn = x * r
    u = dy * g
    dx = r * (u - xn * jnp.mean(u * xn, axis=-1, keepdims=True))
    return dx, jnp.sum(dy * xn, axis=0, keepdims=True)


def matmul_tn(a, b, tm, tn, name, a_off=0, b_off=0, m=None, n=None):
    tp = a.shape[0]
    m = a.shape[1] if m is None else m
    n = b.shape[1] if n is None else n

    def body(a_ref, b_ref, o_ref):
        @pl.when(pl.program_id(2) == 0)
        def _():
            o_ref[...] = jnp.zeros_like(o_ref)

        o_ref[...] += lax.dot_general(a_ref[...].astype(MM), b_ref[...].astype(MM), TN, preferred_element_type=F32)

    return pl.pallas_call(
        body, name=name, grid=(m // tm, n // tn, tp // TM),
        in_specs=[pl.BlockSpec((TM, tm), lambda mi, ni, k: (k, mi + a_off)),
                  pl.BlockSpec((TM, tn), lambda mi, ni, k: (k, ni + b_off))],
        out_specs=pl.BlockSpec((tm, tn), lambda mi, ni, k: (mi, ni)),
        out_shape=S((m, n), F32),
        compiler_params=_params(3),
    )(a, b)


def mlp_bwd(dh3, r, w1, w2, h2, g, name):
    tp = dh3.shape[0]
    nf = DFF // FC

    def body(dh3_ref, r_ref, w1_ref, w2_ref, h2_ref, g_ref, da_ref, dh2_ref, dg_ref, acc, dhb):
        i, j = pl.program_id(0), pl.program_id(1)

        @pl.when((i == 0) & (j == 0))
        def _():
            dg_ref[...] = jnp.zeros_like(dg_ref)

        @pl.when(j == 0)
        def _():
            acc[...] = jnp.zeros_like(acc)
            dhb[...] = dh3_ref[...].astype(MM)

        dz = lax.dot_general(dhb[...], w2_ref[...], NT, preferred_element_type=F32)
        da = (dz * (2.0 * r_ref[...].astype(F32))).astype(MM)
        da_ref[...] = da
        acc[...] += lax.dot_general(da, w1_ref[...], NT, preferred_element_type=F32)

        @pl.when(j == nf - 1)
        def _():
            dx, dg = _rmsnorm_bwd(h2_ref[...], g_ref[...], acc[...])
            dh2_ref[...] = dh3_ref[...] + dx
            dg_ref[...] += dg

    return pl.pallas_call(
        body, name=name, grid=(tp // TM, nf),
        in_specs=[pl.BlockSpec((TM, D), lambda i, j: (i, 0)), pl.BlockSpec((TM, FC), lambda i, j: (i, j)),
                  pl.BlockSpec((D, FC), lambda i, j: (0, j)), pl.BlockSpec((FC, D), lambda i, j: (j, 0)),
                  pl.BlockSpec((TM, D), lambda i, j: (i, 0)), _full((1, D))],
        out_specs=[pl.BlockSpec((TM, FC), lambda i, j: (i, j)), pl.BlockSpec((TM, D), lambda i, j: (i, 0)),
                   _full((1, D))],
        out_shape=[S((tp, DFF), MM), S((tp, D), F32), S((1, D), F32)],
        scratch_shapes=[pltpu.VMEM((TM, D), F32), pltpu.VMEM((TM, D), MM)],
        compiler_params=_params(2),
    )(dh3, r, w1, w2, h2, g)


def out_proj_bwd(dh2, wo, name):
    tp = dh2.shape[0]

    def body(dh_ref, wo_ref, do_ref, dcc_ref):
        dcat = lax.dot_general(dh_ref[...].astype(MM), wo_ref[...], NT, preferred_element_type=F32)
        do_ref[...] = dcat[:, 0:DA].astype(MM)
        dcc_ref[...] = dcat[:, DA:D]

    rows = lambda n: pl.BlockSpec((TM, n), lambda i: (i, 0))
    return pl.pallas_call(
        body, name=name, grid=(tp // TM,),
        in_specs=[rows(D), _full((D, D))],
        out_specs=[rows(DA), rows(2 * DC)],
        out_shape=[S((tp, DA), MM), S((tp, 2 * DC), F32)],
        compiler_params=_params(1),
    )(dh2, wo)


def attn_bwd_dq(qa, ka, va, do, o, lse, name):
    tp = qa.shape[0]
    nb = tp // TM

    def body(q_ref, k_ref, v_ref, do_ref, o_ref, lse_ref, dq_ref, dcq_ref, qb_ref, dob_ref, dcq_acc, *dq_acc):
        i, j = pl.program_id(0), pl.program_id(1)

        @pl.when(j == 0)
        def _():
            for hd in range(H):
                dq_acc[hd][...] = jnp.zeros((TM, HT), F32)
            dcq_acc[...] = jnp.zeros_like(dcq_acc)
            lse = lse_ref[...]
            lane = lax.broadcasted_iota(jnp.int32, (TM, HT), 1)
            for hd in range(H):
                t, hs = slice(hd * HT, (hd + 1) * HT), slice(hd * DH, (hd + 1) * DH)
                dof = do_ref[:, hs].astype(F32)
                dd = jnp.sum(dof * o_ref[:, hs].astype(F32), axis=1, keepdims=True)
                dob_ref[:, t] = _aug_tile(dof, [0.0] + [-v for v in _split3(dd)]).astype(MM)
                qb = q_ref[:, t].astype(F32)
                for pos, v in enumerate(_split3(lse[:, hd:hd + 1])):
                    qb = jnp.where(lane == DH + 6 + pos, -v, qb)
                qb_ref[:, t] = qb.astype(MM)

        def products(hd):
            t = slice(hd * HT, (hd + 1) * HT)
            return (lax.dot_general(qb_ref[:, t], k_ref[:, t], NT, preferred_element_type=F32),
                    lax.dot_general(dob_ref[:, t], v_ref[:, t], NT, preferred_element_type=F32))

        def step(diag):
            dcq = dcq_acc[...]
            lane = lax.broadcasted_iota(jnp.int32, dcq.shape, 1)
            for hd in range(H):
                t = slice(hd * HT, (hd + 1) * HT)
                s, dp = products(hd)
                if diag:
                    s = jnp.where(_causal_mask(), s, NEG)
                ds = jnp.exp(s) * dp
                dq_acc[hd][...] += jnp.dot(ds.astype(MM), k_ref[:, t], preferred_element_type=F32)
                dcq = dcq + jnp.where(lane == hd, jnp.sum(ds, axis=1, keepdims=True), 0.0)
            dcq_acc[...] = dcq

        @pl.when(j < i)
        def _():
            step(False)

        @pl.when(j == i)
        def _():
            step(True)
            dq_ref[...] = (jnp.concatenate([dq_acc[hd][:, 0:DH] for hd in range(H)], axis=1) * SCALE).astype(MM)
            dcq_ref[...] = dcq_acc[...]

    qi = lambda n: pl.BlockSpec((TM, n), lambda i, j: (i, 0))
    kv = pl.BlockSpec((TM, H * HT), lambda i, j: (jnp.minimum(j, i), 0))
    return pl.pallas_call(
        body, name=name, grid=(nb, nb),
        in_specs=[qi(H * HT), kv, kv, qi(DA), qi(DA), qi(128)],
        out_specs=[qi(DA), qi(128), qi(H * HT), qi(H * HT)],
        out_shape=[S((tp, DA), MM), S((tp, 128), F32), S((tp, H * HT), MM), S((tp, H * HT), MM)],
        scratch_shapes=[pltpu.VMEM((TM, 128), F32)] + [pltpu.VMEM((TM, HT), F32)] * H,
        compiler_params=_params(2),
    )(qa, ka, va, do, o, lse)


def attn_bwd_dkv(qb, ka, va, dob, name):
    tp = qb.shape[0]
    nb = tp // TM

    def body(q_ref, k_ref, v_ref, do_ref, dk_ref, dv_ref, dck_ref, dck_acc, *scr):
        dk_acc, dv_acc = scr[:H], scr[H:]
        j, i = pl.program_id(0), pl.program_id(1)

        @pl.when(i == 0)
        def _():
            for hd in range(H):
                dk_acc[hd][...] = jnp.zeros((TM, HT), F32)
                dv_acc[hd][...] = jnp.zeros((TM, HT), F32)
            dck_acc[...] = jnp.zeros_like(dck_acc)

        def products(hd):
            t = slice(hd * HT, (hd + 1) * HT)
            return (lax.dot_general(k_ref[:, t], q_ref[:, t], NT, preferred_element_type=F32),
                    lax.dot_general(v_ref[:, t], do_ref[:, t], NT, preferred_element_type=F32))

        def step(diag):
            dck = dck_acc[...]
            lane = lax.broadcasted_iota(jnp.int32, dck.shape, 1)
            for hd in range(H):
                t = slice(hd * HT, (hd + 1) * HT)
                st, dpt = products(hd)
                if diag:
                    st = jnp.where(_causal_mask(transposed=True), st, NEG)
                pt = jnp.exp(st)
                dv_acc[hd][...] += jnp.dot(pt.astype(MM), do_ref[:, t], preferred_element_type=F32)
                dst = pt * dpt
                dk_acc[hd][...] += jnp.dot(dst.astype(MM), q_ref[:, t], preferred_element_type=F32)
                dck = dck + jnp.where(lane == hd, jnp.sum(dst, axis=1, keepdims=True), 0.0)
            dck_acc[...] = dck

        @pl.when(i > j)
        def _():
            step(False)

        @pl.when(i == j)
        def _():
            step(True)

        @pl.when(i == nb - 1)
        def _():
            dk_ref[...] = jnp.concatenate([dk_acc[hd][:, 0:DH] for hd in range(H)], axis=1).astype(MM)
            dv_ref[...] = jnp.concatenate([dv_acc[hd][:, 0:DH] for hd in range(H)], axis=1).astype(MM)
            dck_ref[...] = dck_acc[...]

    kj = lambda n: pl.BlockSpec((TM, n), lambda j, i: (j, 0))
    qi = pl.BlockSpec((TM, H * HT), lambda j, i: (jnp.maximum(i, j), 0))
    return pl.pallas_call(
        body, name=name, grid=(nb, nb),
        in_specs=[qi, kj(H * HT), kj(H * HT), qi],
        out_specs=[kj(DA), kj(DA), kj(128)],
        out_shape=[S((tp, DA), MM), S((tp, DA), MM), S((tp, 128), F32)],
        scratch_shapes=[pltpu.VMEM((TM, 128), F32)] + [pltpu.VMEM((TM, HT), F32)] * (2 * H),
        compiler_params=_params(2),
    )(qb, ka, va, dob)


def fgate_bwd(dcq, dck, sg, name):
    tp = dcq.shape[0]
    nb = tp // TM

    def body(dcq_ref, dck_ref, sg_ref, dz_ref, db_ref, carry):
        @pl.when(pl.program_id(0) == 0)
        def _():
            carry[...] = jnp.zeros_like(carry)
            db_ref[...] = jnp.zeros_like(db_ref)

        row = lax.broadcasted_iota(jnp.int32, (TM, TM), 0)
        col = lax.broadcasted_iota(jnp.int32, (TM, TM), 1)
        tri = (row <= col).astype(F32)
        dl = jnp.dot(tri, dcq_ref[...] - dck_ref[...], precision=lax.Precision.HIGHEST,
                     preferred_element_type=F32) + carry[...]
        carry[...] = dl[0:1, :]
        dz = dl * sg_ref[...]
        dz_ref[...] = dz.astype(MM)
        db_ref[...] += jnp.sum(dz, axis=0, keepdims=True)

    rev = pl.BlockSpec((TM, 128), lambda i: (nb - 1 - i, 0))
    return pl.pallas_call(
        body, name=name, grid=(nb,),
        in_specs=[rev, rev, rev], out_specs=[rev, _full((1, 128))],
        out_shape=[S((tp, 128), MM), S((1, 128), F32)],
        scratch_shapes=[pltpu.VMEM((1, 128), F32)],
        compiler_params=_params(1),
    )(dcq, dck, sg)


def conv_bwd_pointwise(dcc, sv, pc, lng, lnb, wpw, name):
    tp = dcc.shape[0]

    def body(dcc_ref, sv_ref, b_ref, lng_ref, lnb_ref, wpw_ref, gc_ref, act_ref, pg_ref):
        @pl.when(pl.program_id(0) == 0)
        def _():
            pg_ref[...] = jnp.zeros_like(pg_ref)

        dconf = dcc_ref[:, 0:DC]
        g = lng_ref[...]
        xh, rs, ln = _layernorm_parts(sv_ref[:, 0:DC], g, lnb_ref[...])
        sig = _sigmoid(ln)
        act_ref[...] = (ln * sig).astype(MM)
        dact = lax.dot_general(dconf.astype(MM), wpw_ref[...], NT, preferred_element_type=F32)
        dln = dact * (sig * (1.0 + ln * (1.0 - sig)))
        dxh = dln * g
        ddw = rs * (dxh - jnp.mean(dxh, axis=-1, keepdims=True) - xh * jnp.mean(dxh * xh, axis=-1, keepdims=True))
        gc_ref[:, 0:DC] = ddw
        gc_ref[:, DC:2 * DC] = dcc_ref[:, DC:2 * DC] * b_ref[...]
        cs = lambda t: jnp.sum(t, axis=0, keepdims=True)
        pg_ref[0:1, :] += cs(dconf)
        pg_ref[1:2, :] += cs(dln * xh)
        pg_ref[2:3, :] += cs(dln)
        pg_ref[3:4, :] += cs(ddw)

    rows = lambda n: pl.BlockSpec((TM, n), lambda i: (i, 0))
    return pl.pallas_call(
        body, name=name, grid=(tp // TM,),
        in_specs=[rows(2 * DC), rows(2 * DC), pl.BlockSpec((TM, DC), lambda i: (i, 2)),
                  _full((1, DC)), _full((1, DC)), _full((DC, DC))],
        out_specs=[rows(2 * DC), rows(DC), _full((8, DC))],
        out_shape=[S((tp, 2 * DC), F32), S((tp, DC), MM), S((8, DC), F32)],
        compiler_params=_params(1),
    )(dcc, sv, pc, lng, lnb, wpw)


def conv_bwd_taps(gc, pc, dcc, sv, wdw, wsc, name):
    tp = gc.shape[0]
    nb = tp // TM
    hb = TM // HALO

    def body(gc_ref, gn_ref, pc_ref, hl_ref, dcc_ref, sv_ref, wdw_ref, wsc_ref, dpc_ref, wg_ref, ge, xe, ce):
        i = pl.program_id(0)

        @pl.when(i == 0)
        def _():
            wg_ref[...] = jnp.zeros_like(wg_ref)

        a, gt = pc_ref[:, 0:DC], pc_ref[:, DC:2 * DC]
        c, u = pc_ref[:, 3 * DC:4 * DC], pc_ref[:, 4 * DC:5 * DC]
        sig = _sigmoid(gt)
        _fill_halo(xe, hl_ref[:, 0:DC] * _sigmoid(hl_ref[:, DC:2 * DC]), a * sig, i == 0)
        _fill_halo(ce, hl_ref[:, 3 * DC:4 * DC] * hl_ref[:, 4 * DC:5 * DC], c * u, i == 0)
        ge[0:TM, :] = gc_ref[...]
        ge[TM:TM + HALO, :] = jnp.where(i == nb - 1, 0.0, gn_ref[...])
        ddw, dcv = gc_ref[:, 0:DC], gc_ref[:, DC:2 * DC]
        dglu = jnp.zeros((TM, DC), F32)
        for k in range(CK):
            dglu = dglu + wdw_ref[k:k + 1, :] * ge[pl.ds(CK - 1 - k, TM), 0:DC]
            wg_ref[k:k + 1, :] += jnp.sum(ddw * xe[pl.ds(HALO - (CK - 1) + k, TM), :], axis=0, keepdims=True)
        dcu = jnp.zeros((TM, DC), F32)
        for k in range(SK):
            dcu = dcu + wsc_ref[k:k + 1, :] * ge[pl.ds(SK - 1 - k, TM), DC:2 * DC]
            wg_ref[32 + k:33 + k, :] += jnp.sum(dcv * ce[pl.ds(HALO - (SK - 1) + k, TM), :], axis=0, keepdims=True)
        dpc_ref[:, 0:DC] = (dglu * sig).astype(MM)
        dpc_ref[:, DC:2 * DC] = (dglu * a * sig * (1.0 - sig)).astype(MM)
        dpc_ref[:, 2 * DC:3 * DC] = (dcc_ref[:, DC:2 * DC] * sv_ref[:, DC:2 * DC]).astype(MM)
        dpc_ref[:, 3 * DC:4 * DC] = (dcu * u).astype(MM)
        dpc_ref[:, 4 * DC:5 * DC] = (dcu * c).astype(MM)

    rows = lambda n: pl.BlockSpec((TM, n), lambda i: (i, 0))
    return pl.pallas_call(
        body, name=name, grid=(nb,),
        in_specs=[rows(2 * DC), pl.BlockSpec((HALO, 2 * DC), lambda i: (jnp.minimum((i + 1) * hb, nb * hb - 1), 0)),
                  rows(5 * DC), pl.BlockSpec((HALO, 5 * DC), lambda i: (jnp.maximum(i * hb - 1, 0), 0)),
                  rows(2 * DC), rows(2 * DC), _full((32, DC)), _full((8, DC))],
        out_specs=[rows(5 * DC), _full((40, DC))],
        out_shape=[S((tp, 5 * DC), MM), S((40, DC), F32)],
        scratch_shapes=[pltpu.VMEM((TM + HALO, 2 * DC), F32), pltpu.VMEM((HALO + TM, DC), F32),
                        pltpu.VMEM((HALO + TM, DC), F32)],
        compiler_params=_params(1),
    )(gc, gc, pc, pc, dcc, sv, wdw, wsc)


def in_proj_bwd(dproj, w, h, g, dh_in, name):
    tp = h.shape[0]

    def body(dp_ref, w_ref, h_ref, g_ref, di_ref, dh_ref, dg_ref):
        @pl.when(pl.program_id(0) == 0)
        def _():
            dg_ref[...] = jnp.zeros_like(dg_ref)

        dhn = lax.dot_general(dp_ref[...], w_ref[...], NT, preferred_element_type=F32)
        dx, dg = _rmsnorm_bwd(h_ref[...], g_ref[...], dhn)
        dh_ref[...] = di_ref[...] + dx
        dg_ref[...] += dg

    rows = lambda n: pl.BlockSpec((TM, n), lambda i: (i, 0))
    return pl.pallas_call(
        body, name=name, grid=(tp // TM,),
        in_specs=[rows(NP_IN), _full((D, NP_IN)), rows(D), _full((1, D)), rows(D)],
        out_specs=[rows(D), _full((1, D))],
        out_shape=[S((tp, D), F32), S((1, D), F32)],
        compiler_params=_params(1),
    )(dproj, w, h, g, dh_in)


def adamw(recv, w, m, v, rb, name):
    l_n, r_n, c_n = w.shape

    def body(p_ref, w_ref, m_ref, v_ref, g_ref, d_ref, m2_ref, v2_ref):
        g = p_ref[0, 0].astype(F32)
        for s in range(1, NDEV):
            g = g + p_ref[s, 0].astype(F32)
        m2 = ADAM_B1 * m_ref[0] + (1.0 - ADAM_B1) * g
        v2 = ADAM_B2 * v_ref[0] + (1.0 - ADAM_B2) * (g * g)
        m_hat = m2 / (1.0 - ADAM_B1 ** ADAM_STEP)
        v_hat = v2 / (1.0 - ADAM_B2 ** ADAM_STEP)
        g_ref[0] = g
        d_ref[0] = -ADAM_LR * (m_hat / (jnp.sqrt(v_hat) + ADAM_EPS) + ADAM_WD * w_ref[0])
        m2_ref[0] = m2
        v2_ref[0] = v2

    blk = pl.BlockSpec((1, rb, c_n), lambda l, r: (l, r, 0))
    return pl.pallas_call(
        body, name=name, grid=(l_n, r_n // rb),
        in_specs=[pl.BlockSpec((NDEV, 1, rb, c_n), lambda l, r: (0, l, r, 0)), blk, blk, blk],
        out_specs=[blk] * 4, out_shape=[S(w.shape, F32)] * 4,
        compiler_params=_params(2),
    )(recv, w, m, v)


TINY_ROWS = 168
REP_ROWS = 64


def _pack(parts, rows):
    flat = jnp.concatenate([p.reshape(-1) for p in parts])
    return jnp.pad(flat, (0, rows * 128 - flat.shape[0])).reshape(rows, 128)


def _unpack(buf, shapes):
    flat, out, o = buf.reshape(-1), [], 0
    for s in shapes:
        n = 1
        for d in s:
            n *= d
        out.append(flat[o:o + n].reshape(s))
        o += n
    return out


TINY_SHAPES = [(DEPTH, CK, DC // NDEV), (DEPTH, SK, DC // NDEV), (NM, D // NDEV), (DEPTH, DC // NDEV, DC)]
REP_SHAPES = [(DEPTH, D), (DEPTH, H), (DEPTH, DC), (DEPTH, DC), (DEPTH, DC), (DEPTH, DC), (DEPTH, D), (D,)]


def _to_padded_cols(w):
    pad = jnp.zeros(w.shape[:-1] + (NP_IN - N_IN,), w.dtype)
    return jnp.concatenate([w[..., :C0], w[..., C0 + H:], w[..., C0:C0 + H], pad], axis=-1)


def _from_padded_cols(w):
    return jnp.concatenate([w[..., :C0], w[..., F0:F0 + H], w[..., C0:F0]], axis=-1)


def kernel(x, meta_tokens, mix_norm_g, w_in, b_forget, w_conf_dw, b_conf_dw, conf_ln_g, conf_ln_b, w_conf_pw, b_conf_pw, w_sc_conv, w_out, mlp_norm_g, w_mlp1, w_mlp2, final_norm_g, loss_target, m_meta_tokens, m_mix_norm_g, m_w_in, m_b_forget, m_w_conf_dw, m_b_conf_dw, m_conf_ln_g, m_conf_ln_b, m_w_conf_pw, m_b_conf_pw, m_w_sc_conv, m_w_out, m_mlp_norm_g, m_w_mlp1, m_w_mlp2, m_final_norm_g, v_meta_tokens, v_mix_norm_g, v_w_in, v_b_forget, v_w_conf_dw, v_b_conf_dw, v_conf_ln_g, v_conf_ln_b, v_w_conf_pw, v_b_conf_pw, v_w_sc_conv, v_w_out, v_mlp_norm_g, v_w_mlp1, v_w_mlp2, v_final_norm_g):
    seq = x.shape[1]
    t_real = NM + seq
    tp = -(-t_real // TM) * TM

    tiny_w = _pack([w_conf_dw, w_sc_conv, meta_tokens, w_conf_pw], TINY_ROWS)
    g_in, g_out, g_w1, g_w2, g_tiny = gather(
        [w_in.astype(MM), w_out.astype(MM), w_mlp1.astype(MM), w_mlp2.astype(MM), tiny_w], "gather_weights")
    n_sh = w_in.shape[-1]
    win = _to_padded_cols(g_in.transpose(1, 2, 0, 3).reshape(DEPTH, D, NDEV * n_sh))
    wout = g_out.transpose(1, 0, 2, 3).reshape(DEPTH, D, D)
    w1 = g_w1.transpose(1, 2, 0, 3).reshape(DEPTH, D, DFF)
    w2 = g_w2.transpose(1, 0, 2, 3).reshape(DEPTH, DFF, D)
    tiny = [_unpack(g_tiny[s], TINY_SHAPES) for s in range(NDEV)]
    wdw = jnp.concatenate([t[0] for t in tiny], axis=-1)
    wsc = jnp.concatenate([t[1] for t in tiny], axis=-1)
    meta = jnp.concatenate([t[2] for t in tiny], axis=-1)
    wpw = jnp.concatenate([t[3] for t in tiny], axis=1).astype(MM)
    wdw = jnp.pad(wdw, ((0, 0), (0, 32 - CK), (0, 0)))
    wsc = jnp.pad(wsc, ((0, 0), (0, 8 - SK), (0, 0)))
    bfp = jnp.pad(b_forget, ((0, 0), (0, 128 - H)))

    row = lambda a: a.reshape(1, -1)

    h = jnp.concatenate([meta, x[0], jnp.zeros((tp - t_real, D), F32)], axis=0)
    tgt = jnp.pad(loss_target[0], ((NM, tp - t_real), (0, 0)))
    saved = []
    for l in range(DEPTH):
        hn, qa, ka, va, pc, sg = in_proj(h, row(mix_norm_g[l]), win[l], row(bfp[l]), f"in_proj{l}")
        o, lse = attn_fwd(qa, ka, va, f"attn_fwd{l}")
        cc, sv = conv_fwd(pc, wdw[l], row(b_conf_dw[l]), row(conf_ln_g[l]), row(conf_ln_b[l]), wpw[l],
                          row(b_conf_pw[l]), wsc[l], f"conv_fwd{l}")
        h2, hn2 = out_proj(h, o, cc, wout[l], row(mlp_norm_g[l]), f"out_proj{l}")
        r, z, h3 = mlp_fwd(hn2, h2, w1[l], w2[l], f"mlp_fwd{l}")
        saved.append((h, hn, qa, ka, va, pc, sg, o, lse, cc, sv, h2, hn2, r, z))
        h = h3

    dh, loss_part, d_gf = loss_head(h, tgt, row(final_norm_g), t_real, "loss_head")
    loss = lax.psum(loss_part[0, 0], ("x", "y", "c"))

    gw = {}
    for l in reversed(range(DEPTH)):
        h0, hn, qa, ka, va, pc, sg, o, lse, cc, sv, h2, hn2, r, z = saved[l]
        da, dh2, gw["mlp_g", l] = mlp_bwd(dh, r, w1[l], w2[l], h2, row(mlp_norm_g[l]), f"mlp_bwd{l}")
        gw["w1", l] = matmul_tn(hn2, da, D, FC, f"dw_mlp1_{l}")
        gw["w2", l] = matmul_tn(z, dh, FC, D, f"dw_mlp2_{l}")
        do, dcc = out_proj_bwd(dh2, wout[l], f"out_proj_bwd{l}")
        gw["wo", l] = jnp.concatenate([matmul_tn(o, dh2, DA, D, f"dw_out_a{l}"),
                                       matmul_tn(cc, dh2, 2 * DC, D, f"dw_out_c{l}")], axis=0)
        dq, dcq, qb, dob = attn_bwd_dq(qa, ka, va, do, o, lse, f"attn_bwd_dq{l}")
        dk, dv, dck = attn_bwd_dkv(qb, ka, va, dob, f"attn_bwd_dkv{l}")
        dz, gw["bf", l] = fgate_bwd(dcq, dck, sg, f"fgate_bwd{l}")
        gc, act, pg = conv_bwd_pointwise(dcc, sv, pc, row(conf_ln_g[l]), row(conf_ln_b[l]), wpw[l], f"conv_bwd_pw{l}")
        gw["wpw", l] = matmul_tn(act, dcc, DC, DC, f"dw_conf_pw{l}", n=DC)
        dpc, wg = conv_bwd_taps(gc, pc, dcc, sv, wdw[l], wsc[l], f"conv_bwd_taps{l}")
        gw["pg", l], gw["wg", l] = pg, wg
        dproj = jnp.concatenate([dq, dk, dv, dpc, dz], axis=1)
        gw["win", l] = matmul_tn(hn, dproj, 512, NP_IN, f"dw_in{l}")
        dh, gw["mix_g", l] = in_proj_bwd(dproj, win[l], h0, row(mix_norm_g[l]), dh2, f"in_proj_bwd{l}")

    grad_x = dh[NM:t_real][None]
    stack = lambda k: jnp.stack([gw[k, l] for l in range(DEPTH)])

    d_win = _from_padded_cols(stack("win")).reshape(DEPTH, D, NDEV, n_sh).transpose(2, 0, 1, 3)
    d_wout = stack("wo").reshape(DEPTH, NDEV, D // NDEV, D).transpose(1, 0, 2, 3)
    d_w1 = stack("w1").reshape(DEPTH, D, NDEV, DFF // NDEV).transpose(2, 0, 1, 3)
    d_w2 = stack("w2").reshape(DEPTH, NDEV, DFF // NDEV, D).transpose(1, 0, 2, 3)
    d_wdw = stack("wg")[:, 0:CK].reshape(DEPTH, CK, NDEV, DC // NDEV).transpose(2, 0, 1, 3)
    d_wsc = stack("wg")[:, 32:32 + SK].reshape(DEPTH, SK, NDEV, DC // NDEV).transpose(2, 0, 1, 3)
    d_meta = dh[0:NM].reshape(NM, NDEV, D // NDEV).transpose(1, 0, 2)
    d_wpw = stack("wpw").reshape(DEPTH, NDEV, DC // NDEV, DC).transpose(1, 0, 2, 3)
    d_tiny = jnp.stack([_pack([d_wdw[p], d_wsc[p], d_meta[p], d_wpw[p]], TINY_ROWS) for p in range(NDEV)])
    pgs = stack("pg")
    d_rep = _pack([stack("mix_g").reshape(DEPTH, D), stack("bf")[:, 0, :H], pgs[:, 3], pgs[:, 1], pgs[:, 2], pgs[:, 0],
                   stack("mlp_g").reshape(DEPTH, D), d_gf.reshape(D)], REP_ROWS)
    r_win, r_wout, r_w1, r_w2, r_tiny, r_rep = exchange(
        [d_win.astype(MM), d_wout.astype(MM), d_w1.astype(MM), d_w2.astype(MM), d_tiny, d_rep],
        [True, True, True, True, True, False], "exchange_grads")

    def upd(recv, w, m, v, rb, name):
        shp = w.shape
        w3 = lambda a: a.reshape((-1,) + shp[-2:])
        outs = adamw(recv.reshape((NDEV,) + w3(w).shape), w3(w), w3(m), w3(v), rb, name)
        return [a.reshape(shp) for a in outs]

    res = {}
    res["w_in"] = upd(r_win, w_in, m_w_in, v_w_in, 256, "adamw_w_in")
    res["w_out"] = upd(r_wout, w_out, m_w_out, v_w_out, D // NDEV, "adamw_w_out")
    res["w_mlp1"] = upd(r_w1, w_mlp1, m_w_mlp1, v_w_mlp1, 256, "adamw_w_mlp1")
    res["w_mlp2"] = upd(r_w2, w_mlp2, m_w_mlp2, v_w_mlp2, 128, "adamw_w_mlp2")
    tiny_names = ["w_conf_dw", "w_sc_conv", "meta_tokens", "w_conf_pw"]
    tiny_wmv = [[w_conf_dw, w_sc_conv, meta_tokens, w_conf_pw], [m_w_conf_dw, m_w_sc_conv, m_meta_tokens, m_w_conf_pw],
                [v_w_conf_dw, v_w_sc_conv, v_meta_tokens, v_w_conf_pw]]
    rep_names = ["mix_norm_g", "b_forget", "b_conf_dw", "conf_ln_g", "conf_ln_b", "b_conf_pw", "mlp_norm_g", "final_norm_g"]
    rep_wmv = [[mix_norm_g, b_forget, b_conf_dw, conf_ln_g, conf_ln_b, b_conf_pw, mlp_norm_g, final_norm_g],
               [m_mix_norm_g, m_b_forget, m_b_conf_dw, m_conf_ln_g, m_conf_ln_b, m_b_conf_pw, m_mlp_norm_g, m_final_norm_g],
               [v_mix_norm_g, v_b_forget, v_b_conf_dw, v_conf_ln_g, v_conf_ln_b, v_b_conf_pw, v_mlp_norm_g, v_final_norm_g]]
    for names, wmv, shapes, recv, rows_n, nm in ((tiny_names, tiny_wmv, TINY_SHAPES, r_tiny, TINY_ROWS, "adamw_tiny"),
                                                 (rep_names, rep_wmv, REP_SHAPES, r_rep, REP_ROWS, "adamw_rep")):
        packed = [_pack(group, rows_n)[None] for group in wmv]
        outs = adamw(recv[:, None], *packed, rows_n, nm)
        parts = [_unpack(a[0], shapes) for a in outs]
        for k, n in enumerate(names):
            res[n] = [parts[q][k] for q in range(4)]

    order = ["meta_tokens", "mix_norm_g", "w_in", "b_forget", "w_conf_dw", "b_conf_dw", "conf_ln_g", "conf_ln_b",
             "w_conf_pw", "b_conf_pw", "w_sc_conv", "w_out", "mlp_norm_g", "w_mlp1", "w_mlp2", "final_norm_g"]
    return (loss, grad_x, *[res[n][0] for n in order], *[res[n][1] for n in order],
            *[res[n][2] for n in order], *[res[n][3] for n in order])
```

```python
import functools

import jax
import jax.numpy as jnp
from jax import lax
from jax.experimental import pallas as pl
from jax.experimental.pallas import tpu as pltpu

F32 = jnp.float32
MM = jnp.bfloat16

D = 1024
H = 8
DH = 64
DA = H * DH
HT = 128
DC = 256
NM = 16
CK = 31
SK = 3
DFF = 4096
DEPTH = 2
N_IN = 3 * DA + H + 2 * DC + 3 * DC
NP_IN = 3 * DA + 5 * DC + 128
C0 = 3 * DA
F0 = 3 * DA + 5 * DC
EPS = 1e-6
TM = 640
HALO = 32
FC = 512
NDEV = 8
SCALE = DH ** -0.5
NEG = -1e30

ADAM_LR, ADAM_B1, ADAM_B2, ADAM_EPS, ADAM_WD, ADAM_STEP = 0.001, 0.9, 0.999, 1e-08, 0.01, 10

VMEM_LIMIT = 56 * 1024 * 1024

S = jax.ShapeDtypeStruct
NT = (((1,), (1,)), ((), ()))
TN = (((0,), (0,)), ((), ()))


def _params(n_grid):
    return pltpu.CompilerParams(dimension_semantics=("arbitrary",) * n_grid, vmem_limit_bytes=VMEM_LIMIT)


def _sigmoid(x):
    return 1.0 / (1.0 + jnp.exp(-x))


def _full(shape):
    n = len(shape)
    return pl.BlockSpec(shape, lambda *_: (0,) * n)


def _lane_put(dst, col, h):
    lane = lax.broadcasted_iota(jnp.int32, dst.shape, 1)
    return jnp.where(lane == h, col, dst)


def exchange(srcs, per_peer, name):
    k_n = len(srcs)
    out_shape = [S((NDEV,) + (a.shape[1:] if pp else a.shape), a.dtype) for a, pp in zip(srcs, per_peer)]

    def body(*refs):
        src, out = refs[:k_n], refs[k_n:2 * k_n]
        send_sems, recv_sems, local_sems = refs[2 * k_n:]
        x, y, c = lax.axis_index("x"), lax.axis_index("y"), lax.axis_index("c")
        me = 4 * x + 2 * y + c

        def piece(k, p):
            return src[k].at[p] if per_peer[k] else src[k]

        local = [pltpu.make_async_copy(piece(k, me), out[k].at[me], local_sems.at[k]) for k in range(k_n)]
        for cp in local:
            cp.start()
        sends, recvs = [], []
        for r in range(1, NDEV):
            rx, ry, rc = (r >> 2) & 1, (r >> 1) & 1, r & 1
            px = 1 - x if rx else x
            py = 1 - y if ry else y
            pc = 1 - c if rc else c
            pidx = 4 * px + 2 * py + pc
            for k in range(k_n):
                sends.append(pltpu.make_async_remote_copy(
                    src_ref=piece(k, pidx), dst_ref=out[k].at[me],
                    send_sem=send_sems.at[k, r - 1], recv_sem=recv_sems.at[k, r - 1],
                    device_id=(px, py, pc), device_id_type=pl.DeviceIdType.MESH))
                recvs.append(pltpu.make_async_remote_copy(
                    src_ref=piece(k, pidx), dst_ref=out[k].at[pidx],
                    send_sem=send_sems.at[k, r - 1], recv_sem=recv_sems.at[k, r - 1],
                    device_id=(px, py, pc), device_id_type=pl.DeviceIdType.MESH))
        for cp in sends:
            cp.start()
        for cp in recvs:
            cp.wait_recv()
        for cp in sends:
            cp.wait_send()
        for cp in local:
            cp.wait()

    any_spec = pl.BlockSpec(memory_space=pl.ANY)
    return pl.pallas_call(
        body, name=name, out_shape=out_shape,
        in_specs=[any_spec] * k_n, out_specs=[any_spec] * k_n,
        scratch_shapes=[pltpu.SemaphoreType.DMA((k_n, NDEV - 1)), pltpu.SemaphoreType.DMA((k_n, NDEV - 1)),
                        pltpu.SemaphoreType.DMA((k_n,))],
    )(*srcs)


def gather(srcs, name):
    k_n = len(srcs)
    out_shape = [S((NDEV,) + a.shape, a.dtype) for a in srcs]

    def body(*refs):
        src, out = refs[:k_n], refs[k_n:2 * k_n]
        send_sems, recv_sems, local_sems = refs[2 * k_n:]
        x, y, c = lax.axis_index("x"), lax.axis_index("y"), lax.axis_index("c")
        me, sibling = (x, y, c), (x, y, 1 - c)
        chips = [(1 - x, y), (x, 1 - y), (1 - x, 1 - y)]

        def slot(k, dev):
            return out[k].at[4 * dev[0] + 2 * dev[1] + dev[2]]

        def copy(k, r, block, to, from_src=False):
            return pltpu.make_async_remote_copy(
                src_ref=src[k] if from_src else slot(k, block), dst_ref=slot(k, block),
                send_sem=send_sems.at[k, r], recv_sem=recv_sems.at[k, r],
                device_id=to, device_id_type=pl.DeviceIdType.MESH)

        local = [pltpu.make_async_copy(src[k], slot(k, me), local_sems.at[k]) for k in range(k_n)]
        first = [copy(k, 0, me, sibling, True) for k in range(k_n)]
        first += [copy(k, 1 + n, me, (*chip, c), True) for n, chip in enumerate(chips) for k in range(k_n)]
        for cp in local + first:
            cp.start()
        passed = []
        for n, chip in enumerate(chips):
            for k in range(k_n):
                copy(k, 1 + n, (*chip, c), me).wait_recv()
                passed.append(copy(k, 4 + n, (*chip, c), sibling))
                passed[-1].start()
        for k in range(k_n):
            copy(k, 0, sibling, me).wait_recv()
            for n, chip in enumerate(chips):
                copy(k, 4 + n, (*chip, 1 - c), me).wait_recv()
        for cp in first + passed:
            cp.wait_send()
        for cp in local:
            cp.wait()

    any_spec = pl.BlockSpec(memory_space=pl.ANY)
    return pl.pallas_call(
        body, name=name, out_shape=out_shape,
        in_specs=[any_spec] * k_n, out_specs=[any_spec] * k_n,
        scratch_shapes=[pltpu.SemaphoreType.DMA((k_n, NDEV - 1)), pltpu.SemaphoreType.DMA((k_n, NDEV - 1)),
                        pltpu.SemaphoreType.DMA((k_n,))],
    )(*srcs)


def in_proj(h, g, w, bf, name):
    tp = h.shape[0]

    def body(h_ref, g_ref, w_ref, bf_ref, hn_ref, qa_ref, ka_ref, va_ref, pc_ref, sg_ref, carry):
        i = pl.program_id(0)

        @pl.when(i == 0)
        def _():
            carry[...] = jnp.zeros_like(carry)

        x = h_ref[...]
        r = lax.rsqrt(jnp.mean(x * x, axis=-1, keepdims=True) + EPS)
        hn = (x * r * g_ref[...]).astype(MM)
        hn_ref[...] = hn
        pc_ref[...] = jnp.dot(hn, w_ref[:, C0:F0], preferred_element_type=F32)
        z = jnp.dot(hn, w_ref[:, F0:NP_IN], preferred_element_type=F32) + bf_ref[...]
        lane = lax.broadcasted_iota(jnp.int32, z.shape, 1)
        logf = jnp.where(lane < H, jnp.minimum(z, 0.0) - jnp.log(1.0 + jnp.exp(-jnp.abs(z))), 0.0)
        sg_ref[...] = 1.0 / (1.0 + jnp.exp(z))
        row = lax.broadcasted_iota(jnp.int32, (TM, TM), 0)
        col = lax.broadcasted_iota(jnp.int32, (TM, TM), 1)
        tri = (row >= col).astype(F32)
        c = jnp.dot(tri, logf, precision=lax.Precision.HIGHEST, preferred_element_type=F32) + carry[...]
        carry[...] = c[TM - 1:TM, :]
        qkv = jnp.dot(hn, w_ref[:, 0:C0], preferred_element_type=F32)
        one = [1.0, 1.0, 1.0]
        for hd in range(H):
            t = slice(hd * HT, (hd + 1) * HT)
            cs = list(_split3(c[:, hd:hd + 1]))
            qa_ref[:, t] = _aug_tile(hd, qkv[:, hd * DH:(hd + 1) * DH] * SCALE, cs + one).astype(MM)
            ka_ref[:, t] = _aug_tile(hd, qkv[:, DA + hd * DH:DA + (hd + 1) * DH], one + [-v for v in cs] + one).astype(MM)
            va_ref[:, t] = _aug_tile(hd, qkv[:, 2 * DA + hd * DH:2 * DA + (hd + 1) * DH], [1.0] + one).astype(MM)

    rows = lambda n: pl.BlockSpec((TM, n), lambda i: (i, 0))
    return pl.pallas_call(
        body, name=name, grid=(tp // TM,),
        in_specs=[rows(D), _full((1, D)), _full((D, NP_IN)), _full((1, 128))],
        out_specs=[rows(D), rows(H * HT), rows(H * HT), rows(H * HT), rows(5 * DC), rows(128)],
        out_shape=[S((tp, D), MM), S((tp, H * HT), MM), S((tp, H * HT), MM), S((tp, H * HT), MM),
                   S((tp, 5 * DC), F32), S((tp, 128), F32)],
        scratch_shapes=[pltpu.VMEM((1, 128), F32)],
        compiler_params=_params(1),
    )(h, g, w, bf)


def _split3(c):
    hi = c.astype(MM).astype(F32)
    mid = (c - hi).astype(MM).astype(F32)
    lo = (c - hi - mid).astype(MM).astype(F32)
    return hi, mid, lo


def _main(hd):
    return slice(0, DH) if hd % 2 == 0 else slice(DH, HT)


def _aug_lane(hd, pos):
    return pos + (DH if hd % 2 == 0 else 0)


def _aug_tile(hd, main, cols):
    lane = lax.broadcasted_iota(jnp.int32, main.shape, 1)
    aug = jnp.zeros(main.shape, F32)
    for pos, val in enumerate(cols):
        aug = jnp.where(lane == pos, val, aug)
    return jnp.concatenate([main, aug] if hd % 2 == 0 else [aug, main], axis=1)


def _merge_pairs(tiles):
    lane = lax.broadcasted_iota(jnp.int32, tiles[0].shape, 1)
    return jnp.concatenate([jnp.where(lane < DH, tiles[2 * m], tiles[2 * m + 1]) for m in range(H // 2)], axis=1)


def _causal_mask():
    return lax.broadcasted_iota(jnp.int32, (TM, TM), 0) >= lax.broadcasted_iota(jnp.int32, (TM, TM), 1)


def attn_fwd(qa, ka, va, name):
    tp = qa.shape[0]
    nb = tp // TM

    def body(q_ref, k_ref, v_ref, o_ref, lse_ref, *scr):
        m_scr, acc_scr = scr[:H], scr[H:]
        i, j = pl.program_id(0), pl.program_id(1)

        @pl.when(j == 0)
        def _():
            for hd in range(H):
                m_scr[hd][...] = jnp.full((TM, 1), NEG, F32)
                acc_scr[hd][...] = jnp.zeros((TM, HT), F32)

        def logits(hd):
            t = slice(hd * HT, (hd + 1) * HT)
            return lax.dot_general(q_ref[:, t], k_ref[:, t], NT, preferred_element_type=F32)

        def step(diag):
            s_next = logits(0)
            for hd in range(H):
                t = slice(hd * HT, (hd + 1) * HT)
                s = s_next
                if hd + 1 < H:
                    s_next = logits(hd + 1)
                if diag:
                    s = jnp.where(_causal_mask(), s, NEG)
                m_prev = m_scr[hd][...]
                m_new = jnp.maximum(m_prev, jnp.max(s, axis=1, keepdims=True))
                p = jnp.exp(s - m_new).astype(MM)
                acc_scr[hd][...] = (jnp.exp(m_prev - m_new) * acc_scr[hd][...]
                                    + jnp.dot(p, v_ref[:, t], preferred_element_type=F32))
                m_scr[hd][...] = m_new

        @pl.when(j < i)
        def _():
            step(False)

        @pl.when(j == i)
        def _():
            step(True)
            lse = jnp.zeros((TM, 128), F32)
            outs = []
            for hd in range(H):
                acc = acc_scr[hd][...]
                l = acc[:, _aug_lane(hd, 0):_aug_lane(hd, 0) + 1]
                outs.append(acc / l)
                lse = _lane_put(lse, m_scr[hd][...] + jnp.log(l), hd)
            o_ref[...] = _merge_pairs(outs).astype(MM)
            lse_ref[...] = lse

    kv = pl.BlockSpec((TM, H * HT), lambda i, j: (jnp.minimum(j, i), 0))
    return pl.pallas_call(
        body, name=name, grid=(nb, nb),
        in_specs=[pl.BlockSpec((TM, H * HT), lambda i, j: (i, 0)), kv, kv],
        out_specs=[pl.BlockSpec((TM, DA), lambda i, j: (i, 0)), pl.BlockSpec((TM, 128), lambda i, j: (i, 0))],
        out_shape=[S((tp, DA), MM), S((tp, 128), F32)],
        scratch_shapes=[pltpu.VMEM((TM, 1), F32)] * H + [pltpu.VMEM((TM, HT), F32)] * H,
        compiler_params=_params(2),
    )(qa, ka, va)


def _layernorm_parts(dw, g, b):
    mu = jnp.mean(dw, axis=-1, keepdims=True)
    xc = dw - mu
    rs = lax.rsqrt(jnp.mean(xc * xc, axis=-1, keepdims=True) + EPS)
    xh = xc * rs
    return xh, rs, xh * g + b


def _fill_halo(ext, halo, cur, first):
    ext[0:HALO, :] = jnp.where(first, 0.0, halo)
    ext[HALO:HALO + TM, :] = cur


def conv_fwd(pc, wdw, bdw, lng, lnb, wpw, bpw, wsc, name):
    tp = pc.shape[0]

    def body(pc_ref, hl_ref, wdw_ref, bdw_ref, lng_ref, lnb_ref, wpw_ref, bpw_ref, wsc_ref, cc_ref, sv_ref, xe, ce):
        first = pl.program_id(0) == 0
        glu = pc_ref[:, 0:DC] * _sigmoid(pc_ref[:, DC:2 * DC])
        cu = pc_ref[:, 3 * DC:4 * DC] * pc_ref[:, 4 * DC:5 * DC]
        _fill_halo(xe, hl_ref[:, 0:DC] * _sigmoid(hl_ref[:, DC:2 * DC]), glu, first)
        _fill_halo(ce, hl_ref[:, 3 * DC:4 * DC] * hl_ref[:, 4 * DC:5 * DC], cu, first)
        dw = jnp.zeros((TM, DC), F32) + bdw_ref[...]
        for k in range(CK):
            dw = dw + wdw_ref[k:k + 1, :] * xe[pl.ds(HALO - (CK - 1) + k, TM), :]
        cv = jnp.zeros((TM, DC), F32)
        for k in range(SK):
            cv = cv + wsc_ref[k:k + 1, :] * ce[pl.ds(HALO - (SK - 1) + k, TM), :]
        _, _, ln = _layernorm_parts(dw, lng_ref[...], lnb_ref[...])
        act = ln * _sigmoid(ln)
        conf = jnp.dot(act.astype(MM), wpw_ref[...], preferred_element_type=F32) + bpw_ref[...]
        cc_ref[:, 0:DC] = conf.astype(MM)
        cc_ref[:, DC:2 * DC] = (pc_ref[:, 2 * DC:3 * DC] * cv).astype(MM)
        sv_ref[:, 0:DC] = dw
        sv_ref[:, DC:2 * DC] = cv

    hb = TM // HALO
    rows = lambda n: pl.BlockSpec((TM, n), lambda i: (i, 0))
    return pl.pallas_call(
        body, name=name, grid=(tp // TM,),
        in_specs=[rows(5 * DC), pl.BlockSpec((HALO, 5 * DC), lambda i: (jnp.maximum(i * hb - 1, 0), 0)),
                  _full((32, DC)), _full((1, DC)), _full((1, DC)), _full((1, DC)), _full((DC, DC)), _full((1, DC)),
                  _full((8, DC))],
        out_specs=[rows(2 * DC), rows(2 * DC)],
        out_shape=[S((tp, 2 * DC), MM), S((tp, 2 * DC), F32)],
        scratch_shapes=[pltpu.VMEM((HALO + TM, DC), F32), pltpu.VMEM((HALO + TM, DC), F32)],
        compiler_params=_params(1),
    )(pc, pc, wdw, bdw, lng, lnb, wpw, bpw, wsc)


def out_proj(h, o, cc, wo, g2, name):
    tp = h.shape[0]

    def body(h_ref, o_ref, cc_ref, wo_ref, g_ref, h2_ref, hn_ref):
        h2 = (h_ref[...] + jnp.dot(o_ref[...], wo_ref[0:DA, :], preferred_element_type=F32)
              + jnp.dot(cc_ref[...], wo_ref[DA:D, :], preferred_element_type=F32))
        h2_ref[...] = h2
        r = lax.rsqrt(jnp.mean(h2 * h2, axis=-1, keepdims=True) + EPS)
        hn_ref[...] = (h2 * r * g_ref[...]).astype(MM)

    rows = lambda n: pl.BlockSpec((TM, n), lambda i: (i, 0))
    return pl.pallas_call(
        body, name=name, grid=(tp // TM,),
        in_specs=[rows(D), rows(DA), rows(2 * DC), _full((D, D)), _full((1, D))],
        out_specs=[rows(D), rows(D)],
        out_shape=[S((tp, D), F32), S((tp, D), MM)],
        compiler_params=_params(1),
    )(h, o, cc, wo, g2)


def mlp_fwd(hn, h2, w1, w2, name):
    tp = hn.shape[0]
    nf = DFF // FC

    def body(hn_ref, h2_ref, w1_ref, w2_ref, r_ref, z_ref, h3_ref, acc):
        j = pl.program_id(1)

        @pl.when(j == 0)
        def _():
            acc[...] = jnp.zeros_like(acc)

        r = jnp.maximum(jnp.dot(hn_ref[...], w1_ref[...], preferred_element_type=F32), 0.0)
        zb = (r * r).astype(MM)
        r_ref[...] = r.astype(MM)
        z_ref[...] = zb
        acc[...] += jnp.dot(zb, w2_ref[...], preferred_element_type=F32)

        @pl.when(j == nf - 1)
        def _():
            h3_ref[...] = h2_ref[...] + acc[...]

    return pl.pallas_call(
        body, name=name, grid=(tp // TM, nf),
        in_specs=[pl.BlockSpec((TM, D), lambda i, j: (i, 0)), pl.BlockSpec((TM, D), lambda i, j: (i, 0)),
                  pl.BlockSpec((D, FC), lambda i, j: (0, j)), pl.BlockSpec((FC, D), lambda i, j: (j, 0))],
        out_specs=[pl.BlockSpec((TM, FC), lambda i, j: (i, j)), pl.BlockSpec((TM, FC), lambda i, j: (i, j)),
                   pl.BlockSpec((TM, D), lambda i, j: (i, 0))],
        out_shape=[S((tp, DFF), MM), S((tp, DFF), MM), S((tp, D), F32)],
        scratch_shapes=[pltpu.VMEM((TM, D), F32)],
        compiler_params=_params(2),
    )(hn, h2, w1, w2)


def loss_head(h, tgt, g, t_real, name):
    tp = h.shape[0]

    def body(h_ref, t_ref, g_ref, dh_ref, loss_ref, dg_ref):
        i = pl.program_id(0)

        @pl.when(i == 0)
        def _():
            loss_ref[...] = jnp.zeros_like(loss_ref)
            dg_ref[...] = jnp.zeros_like(dg_ref)

        x = h_ref[...]
        gg = g_ref[...]
        r = lax.rsqrt(jnp.mean(x * x, axis=-1, keepdims=True) + EPS)
        xn = x * r
        row = i * TM + lax.broadcasted_iota(jnp.int32, (TM, 1), 0)
        e = jnp.where((row >= NM) & (row < t_real), xn * gg - t_ref[...], 0.0)
        loss_ref[...] += jnp.sum(e * e) * (0.5 / D)
        dy = e * (1.0 / D)
        dg_ref[...] += jnp.sum(dy * xn, axis=0, keepdims=True)
        u = dy * gg
        dh_ref[...] = r * (u - xn * jnp.mean(u * xn, axis=-1, keepdims=True))

    rows = lambda n: pl.BlockSpec((TM, n), lambda i: (i, 0))
    return pl.pallas_call(
        body, name=name, grid=(tp // TM,),
        in_specs=[rows(D), rows(D), _full((1, D))],
        out_specs=[rows(D), _full((8, 128)), _full((1, D))],
        out_shape=[S((tp, D), F32), S((8, 128), F32), S((1, D), F32)],
        compiler_params=_params(1),
    )(h, tgt, g)


def _rmsnorm_bwd(x, g, dy):
    r = lax.rsqrt(jnp.mean(x * x, axis=-1, keepdims=True) + EPS)
    xn = x * r
    u = dy * g
    dx = r * (u - xn * jnp.mean(u * xn, axis=-1, keepdims=True))
    return dx, jnp.sum(dy * xn, axis=0, keepdims=True)


def matmul_tn(a, b, tm, tn, name, a_off=0, b_off=0, m=None, n=None):
    tp = a.shape[0]
    m = a.shape[1] if m is None else m
    n = b.shape[1] if n is None else n

    def body(a_ref, b_ref, o_ref):
        @pl.when(pl.program_id(2) == 0)
        def _():
            o_ref[...] = jnp.zeros_like(o_ref)

        o_ref[...] += lax.dot_general(a_ref[...].astype(MM), b_ref[...].astype(MM), TN, preferred_element_type=F32)

    return pl.pallas_call(
        body, name=name, grid=(m // tm, n // tn, tp // TM),
        in_specs=[pl.BlockSpec((TM, tm), lambda mi, ni, k: (k, mi + a_off)),
                  pl.BlockSpec((TM, tn), lambda mi, ni, k: (k, ni + b_off))],
        out_specs=pl.BlockSpec((tm, tn), lambda mi, ni, k: (mi, ni)),
        out_shape=S((m, n), F32),
        compiler_params=_params(3),
    )(a, b)


def mlp_bwd(dh3, r, w1, w2, h2, g, name):
    tp = dh3.shape[0]
    nf = DFF // FC

    def body(dh3_ref, r_ref, w1_ref, w2_ref, h2_ref, g_ref, da_ref, dh2_ref, dg_ref, acc, dhb):
        i, j = pl.program_id(0), pl.program_id(1)

        @pl.when((i == 0) & (j == 0))
        def _():
            dg_ref[...] = jnp.zeros_like(dg_ref)

        @pl.when(j == 0)
        def _():
            acc[...] = jnp.zeros_like(acc)
            dhb[...] = dh3_ref[...].astype(MM)

        dz = lax.dot_general(dhb[...], w2_ref[...], NT, preferred_element_type=F32)
        da = (dz * (2.0 * r_ref[...].astype(F32))).astype(MM)
        da_ref[...] = da
        acc[...] += lax.dot_general(da, w1_ref[...], NT, preferred_element_type=F32)

        @pl.when(j == nf - 1)
        def _():
            dx, dg = _rmsnorm_bwd(h2_ref[...], g_ref[...], acc[...])
            dh2_ref[...] = dh3_ref[...] + dx
            dg_ref[...] += dg

    return pl.pallas_call(
        body, name=name, grid=(tp // TM, nf),
        in_specs=[pl.BlockSpec((TM, D), lambda i, j: (i, 0)), pl.BlockSpec((TM, FC), lambda i, j: (i, j)),
                  pl.BlockSpec((D, FC), lambda i, j: (0, j)), pl.BlockSpec((FC, D), lambda i, j: (j, 0)),
                  pl.BlockSpec((TM, D), lambda i, j: (i, 0)), _full((1, D))],
        out_specs=[pl.BlockSpec((TM, FC), lambda i, j: (i, j)), pl.BlockSpec((TM, D), lambda i, j: (i, 0)),
                   _full((1, D))],
        out_shape=[S((tp, DFF), MM), S((tp, D), F32), S((1, D), F32)],
        scratch_shapes=[pltpu.VMEM((TM, D), F32), pltpu.VMEM((TM, D), MM)],
        compiler_params=_params(2),
    )(dh3, r, w1, w2, h2, g)


def out_proj_bwd(dh2, wo, name):
    tp = dh2.shape[0]

    def body(dh_ref, wo_ref, do_ref, dcc_ref):
        dcat = lax.dot_general(dh_ref[...].astype(MM), wo_ref[...], NT, preferred_element_type=F32)
        do_ref[...] = dcat[:, 0:DA].astype(MM)
        dcc_ref[...] = dcat[:, DA:D]

    rows = lambda n: pl.BlockSpec((TM, n), lambda i: (i, 0))
    return pl.pallas_call(
        body, name=name, grid=(tp // TM,),
        in_specs=[rows(D), _full((D, D))],
        out_specs=[rows(DA), rows(2 * DC)],
        out_shape=[S((tp, DA), MM), S((tp, 2 * DC), F32)],
        compiler_params=_params(1),
    )(dh2, wo)


def attn_bwd_prep(qa, do, o, lse, name):
    tp = qa.shape[0]

    def body(q_ref, do_ref, o_ref, lse_ref, qb_ref, dob_ref, qbt_ref, dobt_ref):
        lse = lse_ref[...]
        lane = lax.broadcasted_iota(jnp.int32, (TM, HT), 1)
        for hd in range(H):
            t, hs = slice(hd * HT, (hd + 1) * HT), slice(hd * DH, (hd + 1) * DH)
            dof = do_ref[:, hs].astype(F32)
            dd = jnp.sum(dof * o_ref[:, hs].astype(F32), axis=1, keepdims=True)
            dob = _aug_tile(hd, dof, [0.0] + [-v for v in _split3(dd)])
            qb = q_ref[:, t].astype(F32)
            for pos, v in enumerate(_split3(lse[:, hd:hd + 1])):
                qb = jnp.where(lane == _aug_lane(hd, 6 + pos), -v, qb)
            qb_ref[:, t] = qb.astype(MM)
            dob_ref[:, t] = dob.astype(MM)
            qbt_ref[t, :] = qb.T.astype(MM)
            dobt_ref[t, :] = dob.T.astype(MM)

    rows = lambda n: pl.BlockSpec((TM, n), lambda i: (i, 0))
    cols = pl.BlockSpec((H * HT, TM), lambda i: (0, i))
    return pl.pallas_call(
        body, name=name, grid=(tp // TM,),
        in_specs=[rows(H * HT), rows(DA), rows(DA), rows(128)],
        out_specs=[rows(H * HT), rows(H * HT), cols, cols],
        out_shape=[S((tp, H * HT), MM), S((tp, H * HT), MM), S((H * HT, tp), MM), S((H * HT, tp), MM)],
        compiler_params=_params(1),
    )(qa, do, o, lse)


def attn_bwd(qb, dob, qbt, dobt, ka, va, name):
    tp = qb.shape[0]
    nb = tp // TM

    def body(qb_ref, dob_ref, qbt_ref, dobt_ref, k_ref, v_ref, dq_ref, dk_ref, dv_ref, dcq_ref, dck_ref,
             dq_acc, dcq_acc, dkt_acc, dvt_acc, dck_acc, stage, sems):
        j, i = pl.program_id(0), pl.program_id(1)

        @pl.when((j == 0) & (i == 0))
        def _():
            dq_acc[...] = jnp.zeros_like(dq_acc)
            dcq_acc[...] = jnp.zeros_like(dcq_acc)

        @pl.when(i == 0)
        def _():
            dkt_acc[...] = jnp.zeros_like(dkt_acc)
            dvt_acc[...] = jnp.zeros_like(dvt_acc)
            dck_acc[...] = jnp.zeros_like(dck_acc)

        def step(diag):
            rows = pl.ds(pl.multiple_of(i * TM, TM), TM)
            dcq = dcq_acc[rows, :]
            lane = lax.broadcasted_iota(jnp.int32, (TM, HT), 1)
            for hd in range(H):
                t = slice(hd * HT, (hd + 1) * HT)
                s = lax.dot_general(qb_ref[:, t], k_ref[:, t], NT, preferred_element_type=F32)
                dp = lax.dot_general(dob_ref[:, t], v_ref[:, t], NT, preferred_element_type=F32)
                if diag:
                    s = jnp.where(_causal_mask(), s, NEG)
                p = jnp.exp(s)
                ds = p * dp
                dsb = ds.astype(MM)
                dvt_acc[t, :] += jnp.dot(dobt_ref[t, :], p.astype(MM), preferred_element_type=F32)
                dkt_acc[t, :] += jnp.dot(qbt_ref[t, :], dsb, preferred_element_type=F32)
                dqh = jnp.dot(dsb, k_ref[:, t], preferred_element_type=F32)
                if hd % 2 == 0:
                    dq_even = dqh
                else:
                    pair = slice((hd // 2) * HT, (hd // 2 + 1) * HT)
                    dq_acc[rows, pair] += jnp.where(lane < DH, dq_even, dqh)
                dcq = dcq + jnp.where(lane == hd, jnp.sum(ds, axis=1, keepdims=True), 0.0)
                dck_acc[hd:hd + 1, :] += jnp.sum(ds, axis=0, keepdims=True)
            dcq_acc[rows, :] = dcq
            return rows

        @pl.when(i > j)
        def _():
            step(False)

        @pl.when(i == j)
        def _():
            rows = step(True)
            stage[...] = (dq_acc[rows, :] * SCALE).astype(MM)
            out = [pltpu.make_async_copy(stage, dq_ref.at[rows, :], sems.at[0]),
                   pltpu.make_async_copy(dcq_acc.at[rows, :], dcq_ref.at[rows, :], sems.at[1])]
            for cp in out:
                cp.start()
            for cp in out:
                cp.wait()

        @pl.when(i == nb - 1)
        def _():
            dk_ref[...] = _merge_pairs([dkt_acc[hd * HT:(hd + 1) * HT, :].T for hd in range(H)]).astype(MM)
            dv_ref[...] = _merge_pairs([dvt_acc[hd * HT:(hd + 1) * HT, :].T for hd in range(H)]).astype(MM)
            dck_ref[...] = dck_acc[...]

    qi = lambda n: pl.BlockSpec((TM, n), lambda j, i: (jnp.maximum(i, j), 0))
    qt = pl.BlockSpec((H * HT, TM), lambda j, i: (0, jnp.maximum(i, j)))
    kj = lambda n: pl.BlockSpec((TM, n), lambda j, i: (j, 0))
    return pl.pallas_call(
        body, name=name, grid=(nb, nb),
        in_specs=[qi(H * HT), qi(H * HT), qt, qt, kj(H * HT), kj(H * HT)],
        out_specs=[pl.BlockSpec(memory_space=pl.ANY), kj(DA), kj(DA), pl.BlockSpec(memory_space=pl.ANY),
                   pl.BlockSpec((H, TM), lambda j, i: (0, j))],
        out_shape=[S((tp, DA), MM), S((tp, DA), MM), S((tp, DA), MM), S((tp, 128), F32), S((H, tp), F32)],
        scratch_shapes=[pltpu.VMEM((tp, DA), F32), pltpu.VMEM((tp, 128), F32), pltpu.VMEM((H * HT, TM), F32),
                        pltpu.VMEM((H * HT, TM), F32), pltpu.VMEM((H, TM), F32), pltpu.VMEM((TM, DA), MM),
                        pltpu.SemaphoreType.DMA((2,))],
        compiler_params=_params(2),
    )(qb, dob, qbt, dobt, ka, va)


def fgate_bwd(dcq, dck, sg, name):
    tp = dcq.shape[0]
    nb = tp // TM

    def body(dcq_ref, dck_ref, sg_ref, dz_ref, db_ref, carry):
        @pl.when(pl.program_id(0) == 0)
        def _():
            carry[...] = jnp.zeros_like(carry)
            db_ref[...] = jnp.zeros_like(db_ref)

        row = lax.broadcasted_iota(jnp.int32, (TM, TM), 0)
        col = lax.broadcasted_iota(jnp.int32, (TM, TM), 1)
        tri = (row <= col).astype(F32)
        dl = jnp.dot(tri, dcq_ref[...] - dck_ref[...], precision=lax.Precision.HIGHEST,
                     preferred_element_type=F32) + carry[...]
        carry[...] = dl[0:1, :]
        dz = dl * sg_ref[...]
        dz_ref[...] = dz.astype(MM)
        db_ref[...] += jnp.sum(dz, axis=0, keepdims=True)

    rev = pl.BlockSpec((TM, 128), lambda i: (nb - 1 - i, 0))
    return pl.pallas_call(
        body, name=name, grid=(nb,),
        in_specs=[rev, rev, rev], out_specs=[rev, _full((1, 128))],
        out_shape=[S((tp, 128), MM), S((1, 128), F32)],
        scratch_shapes=[pltpu.VMEM((1, 128), F32)],
        compiler_params=_params(1),
    )(dcq, dck, sg)


def conv_bwd_pointwise(dcc, sv, pc, lng, lnb, wpw, name):
    tp = dcc.shape[0]

    def body(dcc_ref, sv_ref, b_ref, lng_ref, lnb_ref, wpw_ref, gc_ref, act_ref, pg_ref):
        @pl.when(pl.program_id(0) == 0)
        def _():
            pg_ref[...] = jnp.zeros_like(pg_ref)

        dconf = dcc_ref[:, 0:DC]
        g = lng_ref[...]
        xh, rs, ln = _layernorm_parts(sv_ref[:, 0:DC], g, lnb_ref[...])
        sig = _sigmoid(ln)
        act_ref[...] = (ln * sig).astype(MM)
        dact = lax.dot_general(dconf.astype(MM), wpw_ref[...], NT, preferred_element_type=F32)
        dln = dact * (sig * (1.0 + ln * (1.0 - sig)))
        dxh = dln * g
        ddw = rs * (dxh - jnp.mean(dxh, axis=-1, keepdims=True) - xh * jnp.mean(dxh * xh, axis=-1, keepdims=True))
        gc_ref[:, 0:DC] = ddw
        gc_ref[:, DC:2 * DC] = dcc_ref[:, DC:2 * DC] * b_ref[...]
        cs = lambda t: jnp.sum(t, axis=0, keepdims=True)
        pg_ref[0:1, :] += cs(dconf)
        pg_ref[1:2, :] += cs(dln * xh)
        pg_ref[2:3, :] += cs(dln)
        pg_ref[3:4, :] += cs(ddw)

    rows = lambda n: pl.BlockSpec((TM, n), lambda i: (i, 0))
    return pl.pallas_call(
        body, name=name, grid=(tp // TM,),
        in_specs=[rows(2 * DC), rows(2 * DC), pl.BlockSpec((TM, DC), lambda i: (i, 2)),
                  _full((1, DC)), _full((1, DC)), _full((DC, DC))],
        out_specs=[rows(2 * DC), rows(DC), _full((8, DC))],
        out_shape=[S((tp, 2 * DC), F32), S((tp, DC), MM), S((8, DC), F32)],
        compiler_params=_params(1),
    )(dcc, sv, pc, lng, lnb, wpw)


def conv_bwd_taps(gc, pc, dcc, sv, wdw, wsc, name):
    tp = gc.shape[0]
    nb = tp // TM
    hb = TM // HALO

    def body(gc_ref, gn_ref, pc_ref, hl_ref, dcc_ref, sv_ref, wdw_ref, wsc_ref, dpc_ref, wg_ref, ge, xe, ce):
        i = pl.program_id(0)

        @pl.when(i == 0)
        def _():
            wg_ref[...] = jnp.zeros_like(wg_ref)

        a, gt = pc_ref[:, 0:DC], pc_ref[:, DC:2 * DC]
        c, u = pc_ref[:, 3 * DC:4 * DC], pc_ref[:, 4 * DC:5 * DC]
        sig = _sigmoid(gt)
        _fill_halo(xe, hl_ref[:, 0:DC] * _sigmoid(hl_ref[:, DC:2 * DC]), a * sig, i == 0)
        _fill_halo(ce, hl_ref[:, 3 * DC:4 * DC] * hl_ref[:, 4 * DC:5 * DC], c * u, i == 0)
        ge[0:TM, :] = gc_ref[...]
        ge[TM:TM + HALO, :] = jnp.where(i == nb - 1, 0.0, gn_ref[...])
        ddw, dcv = gc_ref[:, 0:DC], gc_ref[:, DC:2 * DC]
        dglu = jnp.zeros((TM, DC), F32)
        for k in range(CK):
            dglu = dglu + wdw_ref[k:k + 1, :] * ge[pl.ds(CK - 1 - k, TM), 0:DC]
            wg_ref[k:k + 1, :] += jnp.sum(ddw * xe[pl.ds(HALO - (CK - 1) + k, TM), :], axis=0, keepdims=True)
        dcu = jnp.zeros((TM, DC), F32)
        for k in range(SK):
            dcu = dcu + wsc_ref[k:k + 1, :] * ge[pl.ds(SK - 1 - k, TM), DC:2 * DC]
            wg_ref[32 + k:33 + k, :] += jnp.sum(dcv * ce[pl.ds(HALO - (SK - 1) + k, TM), :], axis=0, keepdims=True)
        dpc_ref[:, 0:DC] = (dglu * sig).astype(MM)
        dpc_ref[:, DC:2 * DC] = (dglu * a * sig * (1.0 - sig)).astype(MM)
        dpc_ref[:, 2 * DC:3 * DC] = (dcc_ref[:, DC:2 * DC] * sv_ref[:, DC:2 * DC]).astype(MM)
        dpc_ref[:, 3 * DC:4 * DC] = (dcu * u).astype(MM)
        dpc_ref[:, 4 * DC:5 * DC] = (dcu * c).astype(MM)

    rows = lambda n: pl.BlockSpec((TM, n), lambda i: (i, 0))
    return pl.pallas_call(
        body, name=name, grid=(nb,),
        in_specs=[rows(2 * DC), pl.BlockSpec((HALO, 2 * DC), lambda i: (jnp.minimum((i + 1) * hb, nb * hb - 1), 0)),
                  rows(5 * DC), pl.BlockSpec((HALO, 5 * DC), lambda i: (jnp.maximum(i * hb - 1, 0), 0)),
                  rows(2 * DC), rows(2 * DC), _full((32, DC)), _full((8, DC))],
        out_specs=[rows(5 * DC), _full((40, DC))],
        out_shape=[S((tp, 5 * DC), MM), S((40, DC), F32)],
        scratch_shapes=[pltpu.VMEM((TM + HALO, 2 * DC), F32), pltpu.VMEM((HALO + TM, DC), F32),
                        pltpu.VMEM((HALO + TM, DC), F32)],
        compiler_params=_params(1),
    )(gc, gc, pc, pc, dcc, sv, wdw, wsc)


def in_proj_bwd(dproj, w, h, g, dh_in, name):
    tp = h.shape[0]

    def body(dp_ref, w_ref, h_ref, g_ref, di_ref, dh_ref, dg_ref):
        @pl.when(pl.program_id(0) == 0)
        def _():
            dg_ref[...] = jnp.zeros_like(dg_ref)

        dhn = lax.dot_general(dp_ref[...], w_ref[...], NT, preferred_element_type=F32)
        dx, dg = _rmsnorm_bwd(h_ref[...], g_ref[...], dhn)
        dh_ref[...] = di_ref[...] + dx
        dg_ref[...] += dg

    rows = lambda n: pl.BlockSpec((TM, n), lambda i: (i, 0))
    return pl.pallas_call(
        body, name=name, grid=(tp // TM,),
        in_specs=[rows(NP_IN), _full((D, NP_IN)), rows(D), _full((1, D)), rows(D)],
        out_specs=[rows(D), _full((1, D))],
        out_shape=[S((tp, D), F32), S((1, D), F32)],
        compiler_params=_params(1),
    )(dproj, w, h, g, dh_in)


def adamw(recv, w, m, v, rb, name):
    l_n, r_n, c_n = w.shape

    def body(p_ref, w_ref, m_ref, v_ref, g_ref, d_ref, m2_ref, v2_ref):
        g = p_ref[0, 0].astype(F32)
        for s in range(1, NDEV):
            g = g + p_ref[s, 0].astype(F32)
        m2 = ADAM_B1 * m_ref[0] + (1.0 - ADAM_B1) * g
        v2 = ADAM_B2 * v_ref[0] + (1.0 - ADAM_B2) * (g * g)
        m_hat = m2 / (1.0 - ADAM_B1 ** ADAM_STEP)
        v_hat = v2 / (1.0 - ADAM_B2 ** ADAM_STEP)
        g_ref[0] = g
        d_ref[0] = -ADAM_LR * (m_hat / (jnp.sqrt(v_hat) + ADAM_EPS) + ADAM_WD * w_ref[0])
        m2_ref[0] = m2
        v2_ref[0] = v2

    blk = pl.BlockSpec((1, rb, c_n), lambda l, r: (l, r, 0))
    return pl.pallas_call(
        body, name=name, grid=(l_n, r_n // rb),
        in_specs=[pl.BlockSpec((NDEV, 1, rb, c_n), lambda l, r: (0, l, r, 0)), blk, blk, blk],
        out_specs=[blk] * 4, out_shape=[S(w.shape, F32)] * 4,
        compiler_params=_params(2),
    )(recv, w, m, v)


TINY_ROWS = 168
REP_ROWS = 64


def _pack(parts, rows):
    flat = jnp.concatenate([p.reshape(-1) for p in parts])
    return jnp.pad(flat, (0, rows * 128 - flat.shape[0])).reshape(rows, 128)


def _unpack(buf, shapes):
    flat, out, o = buf.reshape(-1), [], 0
    for s in shapes:
        n = 1
        for d in s:
            n *= d
        out.append(flat[o:o + n].reshape(s))
        o += n
    return out


TINY_SHAPES = [(DEPTH, CK, DC // NDEV), (DEPTH, SK, DC // NDEV), (NM, D // NDEV), (DEPTH, DC // NDEV, DC)]
REP_SHAPES = [(DEPTH, D), (DEPTH, H), (DEPTH, DC), (DEPTH, DC), (DEPTH, DC), (DEPTH, DC), (DEPTH, D), (D,)]


def _to_padded_cols(w):
    pad = jnp.zeros(w.shape[:-1] + (NP_IN - N_IN,), w.dtype)
    return jnp.concatenate([w[..., :C0], w[..., C0 + H:], w[..., C0:C0 + H], pad], axis=-1)


def _from_padded_cols(w):
    return jnp.concatenate([w[..., :C0], w[..., F0:F0 + H], w[..., C0:F0]], axis=-1)


def kernel(x, meta_tokens, mix_norm_g, w_in, b_forget, w_conf_dw, b_conf_dw, conf_ln_g, conf_ln_b, w_conf_pw, b_conf_pw, w_sc_conv, w_out, mlp_norm_g, w_mlp1, w_mlp2, final_norm_g, loss_target, m_meta_tokens, m_mix_norm_g, m_w_in, m_b_forget, m_w_conf_dw, m_b_conf_dw, m_conf_ln_g, m_conf_ln_b, m_w_conf_pw, m_b_conf_pw, m_w_sc_conv, m_w_out, m_mlp_norm_g, m_w_mlp1, m_w_mlp2, m_final_norm_g, v_meta_tokens, v_mix_norm_g, v_w_in, v_b_forget, v_w_conf_dw, v_b_conf_dw, v_conf_ln_g, v_conf_ln_b, v_w_conf_pw, v_b_conf_pw, v_w_sc_conv, v_w_out, v_mlp_norm_g, v_w_mlp1, v_w_mlp2, v_final_norm_g):
    seq = x.shape[1]
    t_real = NM + seq
    tp = -(-t_real // TM) * TM

    tiny_w = _pack([w_conf_dw, w_sc_conv, meta_tokens, w_conf_pw], TINY_ROWS)
    g_in, g_out, g_w1, g_w2, g_tiny = gather(
        [w_in.astype(MM), w_out.astype(MM), w_mlp1.astype(MM), w_mlp2.astype(MM), tiny_w], "gather_weights")
    n_sh = w_in.shape[-1]
    win = _to_padded_cols(g_in.transpose(1, 2, 0, 3).reshape(DEPTH, D, NDEV * n_sh))
    wout = g_out.transpose(1, 0, 2, 3).reshape(DEPTH, D, D)
    w1 = g_w1.transpose(1, 2, 0, 3).reshape(DEPTH, D, DFF)
    w2 = g_w2.transpose(1, 0, 2, 3).reshape(DEPTH, DFF, D)
    tiny = [_unpack(g_tiny[s], TINY_SHAPES) for s in range(NDEV)]
    wdw = jnp.concatenate([t[0] for t in tiny], axis=-1)
    wsc = jnp.concatenate([t[1] for t in tiny], axis=-1)
    meta = jnp.concatenate([t[2] for t in tiny], axis=-1)
    wpw = jnp.concatenate([t[3] for t in tiny], axis=1).astype(MM)
    wdw = jnp.pad(wdw, ((0, 0), (0, 32 - CK), (0, 0)))
    wsc = jnp.pad(wsc, ((0, 0), (0, 8 - SK), (0, 0)))
    bfp = jnp.pad(b_forget, ((0, 0), (0, 128 - H)))

    row = lambda a: a.reshape(1, -1)

    h = jnp.concatenate([meta, x[0], jnp.zeros((tp - t_real, D), F32)], axis=0)
    tgt = jnp.pad(loss_target[0], ((NM, tp - t_real), (0, 0)))
    saved = []
    for l in range(DEPTH):
        hn, qa, ka, va, pc, sg = in_proj(h, row(mix_norm_g[l]), win[l], row(bfp[l]), f"in_proj{l}")
        o, lse = attn_fwd(qa, ka, va, f"attn_fwd{l}")
        cc, sv = conv_fwd(pc, wdw[l], row(b_conf_dw[l]), row(conf_ln_g[l]), row(conf_ln_b[l]), wpw[l],
                          row(b_conf_pw[l]), wsc[l], f"conv_fwd{l}")
        h2, hn2 = out_proj(h, o, cc, wout[l], row(mlp_norm_g[l]), f"out_proj{l}")
        r, z, h3 = mlp_fwd(hn2, h2, w1[l], w2[l], f"mlp_fwd{l}")
        saved.append((h, hn, qa, ka, va, pc, sg, o, lse, cc, sv, h2, hn2, r, z))
        h = h3

    dh, loss_part, d_gf = loss_head(h, tgt, row(final_norm_g), t_real, "loss_head")
    loss = lax.psum(loss_part[0, 0], ("x", "y", "c"))

    gw = {}
    for l in reversed(range(DEPTH)):
        h0, hn, qa, ka, va, pc, sg, o, lse, cc, sv, h2, hn2, r, z = saved[l]
        da, dh2, gw["mlp_g", l] = mlp_bwd(dh, r, w1[l], w2[l], h2, row(mlp_norm_g[l]), f"mlp_bwd{l}")
        gw["w1", l] = matmul_tn(hn2, da, D, FC, f"dw_mlp1_{l}")
        gw["w2", l] = matmul_tn(z, dh, FC, D, f"dw_mlp2_{l}")
        do, dcc = out_proj_bwd(dh2, wout[l], f"out_proj_bwd{l}")
        gw["wo", l] = jnp.concatenate([matmul_tn(o, dh2, DA, D, f"dw_out_a{l}"),
                                       matmul_tn(cc, dh2, 2 * DC, D, f"dw_out_c{l}")], axis=0)
        qb, dob, qbt, dobt = attn_bwd_prep(qa, do, o, lse, f"attn_bwd_prep{l}")
        dq, dk, dv, dcq, dck = attn_bwd(qb, dob, qbt, dobt, ka, va, f"attn_bwd{l}")
        dz, gw["bf", l] = fgate_bwd(dcq, jnp.pad(dck.T, ((0, 0), (0, 128 - H))), sg, f"fgate_bwd{l}")
        gc, act, pg = conv_bwd_pointwise(dcc, sv, pc, row(conf_ln_g[l]), row(conf_ln_b[l]), wpw[l], f"conv_bwd_pw{l}")
        gw["wpw", l] = matmul_tn(act, dcc, DC, DC, f"dw_conf_pw{l}", n=DC)
        dpc, wg = conv_bwd_taps(gc, pc, dcc, sv, wdw[l], wsc[l], f"conv_bwd_taps{l}")
        gw["pg", l], gw["wg", l] = pg, wg
        dproj = jnp.concatenate([dq, dk, dv, dpc, dz], axis=1)
        gw["win", l] = matmul_tn(hn, dproj, 512, NP_IN, f"dw_in{l}")
        dh, gw["mix_g", l] = in_proj_bwd(dproj, win[l], h0, row(mix_norm_g[l]), dh2, f"in_proj_bwd{l}")

    grad_x = dh[NM:t_real][None]
    stack = lambda k: jnp.stack([gw[k, l] for l in range(DEPTH)])

    d_win = _from_padded_cols(stack("win")).reshape(DEPTH, D, NDEV, n_sh).transpose(2, 0, 1, 3)
    d_wout = stack("wo").reshape(DEPTH, NDEV, D // NDEV, D).transpose(1, 0, 2, 3)
    d_w1 = stack("w1").reshape(DEPTH, D, NDEV, DFF // NDEV).transpose(2, 0, 1, 3)
    d_w2 = stack("w2").reshape(DEPTH, NDEV, DFF // NDEV, D).transpose(1, 0, 2, 3)
    d_wdw = stack("wg")[:, 0:CK].reshape(DEPTH, CK, NDEV, DC // NDEV).transpose(2, 0, 1, 3)
    d_wsc = stack("wg")[:, 32:32 + SK].reshape(DEPTH, SK, NDEV, DC // NDEV).transpose(2, 0, 1, 3)
    d_meta = dh[0:NM].reshape(NM, NDEV, D // NDEV).transpose(1, 0, 2)
    d_wpw = stack("wpw").reshape(DEPTH, NDEV, DC // NDEV, DC).transpose(1, 0, 2, 3)
    d_tiny = jnp.stack([_pack([d_wdw[p], d_wsc[p], d_meta[p], d_wpw[p]], TINY_ROWS) for p in range(NDEV)])
    pgs = stack("pg")
    d_rep = _pack([stack("mix_g").reshape(DEPTH, D), stack("bf")[:, 0, :H], pgs[:, 3], pgs[:, 1], pgs[:, 2], pgs[:, 0],
                   stack("mlp_g").reshape(DEPTH, D), d_gf.reshape(D)], REP_ROWS)
    r_win, r_wout, r_w1, r_w2, r_tiny, r_rep = exchange(
        [d_win.astype(MM), d_wout.astype(MM), d_w1.astype(MM), d_w2.astype(MM), d_tiny, d_rep],
        [True, True, True, True, True, False], "exchange_grads")

    def upd(recv, w, m, v, rb, name):
        shp = w.shape
        w3 = lambda a: a.reshape((-1,) + shp[-2:])
        outs = adamw(recv.reshape((NDEV,) + w3(w).shape), w3(w), w3(m), w3(v), rb, name)
        return [a.reshape(shp) for a in outs]

    res = {}
    res["w_in"] = upd(r_win, w_in, m_w_in, v_w_in, 256, "adamw_w_in")
    res["w_out"] = upd(r_wout, w_out, m_w_out, v_w_out, D // NDEV, "adamw_w_out")
    res["w_mlp1"] = upd(r_w1, w_mlp1, m_w_mlp1, v_w_mlp1, 256, "adamw_w_mlp1")
    res["w_mlp2"] = upd(r_w2, w_mlp2, m_w_mlp2, v_w_mlp2, 128, "adamw_w_mlp2")
    tiny_names = ["w_conf_dw", "w_sc_conv", "meta_tokens", "w_conf_pw"]
    tiny_wmv = [[w_conf_dw, w_sc_conv, meta_tokens, w_conf_pw], [m_w_conf_dw, m_w_sc_conv, m_meta_tokens, m_w_conf_pw],
                [v_w_conf_dw, v_w_sc_conv, v_meta_tokens, v_w_conf_pw]]
    rep_names = ["mix_norm_g", "b_forget", "b_conf_dw", "conf_ln_g", "conf_ln_b", "b_conf_pw", "mlp_norm_g", "final_norm_g"]
    rep_wmv = [[mix_norm_g, b_forget, b_conf_dw, conf_ln_g, conf_ln_b, b_conf_pw, mlp_norm_g, final_norm_g],
               [m_mix_norm_g, m_b_forget, m_b_conf_dw, m_conf_ln_g, m_conf_ln_b, m_b_conf_pw, m_mlp_norm_g, m_final_norm_g],
               [v_mix_norm_g, v_b_forget, v_b_conf_dw, v_conf_ln_g, v_conf_ln_b, v_b_conf_pw, v_mlp_norm_g, v_final_norm_g]]
    for names, wmv, shapes, recv, rows_n, nm in ((tiny_names, tiny_wmv, TINY_SHAPES, r_tiny, TINY_ROWS, "adamw_tiny"),
                                                 (rep_names, rep_wmv, REP_SHAPES, r_rep, REP_ROWS, "adamw_rep")):
        packed = [_pack(group, rows_n)[None] for group in wmv]
        outs = adamw(recv[:, None], *packed, rows_n, nm)
        parts = [_unpack(a[0], shapes) for a in outs]
        for k, n in enumerate(names):
            res[n] = [parts[q][k] for q in range(4)]

    order = ["meta_tokens", "mix_norm_g", "w_in", "b_forget", "w_conf_dw", "b_conf_dw", "conf_ln_g", "conf_ln_b",
             "w_conf_pw", "b_conf_pw", "w_sc_conv", "w_out", "mlp_norm_g", "w_mlp1", "w_mlp2", "final_norm_g"]
    return (loss, grad_x, *[res[n][0] for n in order], *[res[n][1] for n in order],
            *[res[n][2] for n in order], *[res[n][3] for n in order])
```

```python
import functools

import jax
import jax.numpy as jnp
from jax import lax
from jax.experimental import pallas as pl
from jax.experimental.pallas import tpu as pltpu

F32 = jnp.float32
MM = jnp.bfloat16

D = 1024
H = 8
DH = 64
DA = H * DH
HT = 128
DC = 256
NM = 16
CK = 31
SK = 3
DFF = 4096
DEPTH = 2
N_IN = 3 * DA + H + 2 * DC + 3 * DC
NP_IN = 3 * DA + 5 * DC + 128
C0 = 3 * DA
F0 = 3 * DA + 5 * DC
EPS = 1e-6
TM = 640
HALO = 32
FC = 512
NDEV = 8
SCALE = DH ** -0.5
NEG = -1e30

ADAM_LR, ADAM_B1, ADAM_B2, ADAM_EPS, ADAM_WD, ADAM_STEP = 0.001, 0.9, 0.999, 1e-08, 0.01, 10

VMEM_LIMIT = 56 * 1024 * 1024

S = jax.ShapeDtypeStruct
NT = (((1,), (1,)), ((), ()))
TN = (((0,), (0,)), ((), ()))


def _params(n_grid):
    return pltpu.CompilerParams(dimension_semantics=("arbitrary",) * n_grid, vmem_limit_bytes=VMEM_LIMIT)


def _sigmoid(x):
    return 1.0 / (1.0 + jnp.exp(-x))


def _full(shape):
    n = len(shape)
    return pl.BlockSpec(shape, lambda *_: (0,) * n)


def _lane_put(dst, col, h):
    lane = lax.broadcasted_iota(jnp.int32, dst.shape, 1)
    return jnp.where(lane == h, col, dst)


ANY = pl.BlockSpec(memory_space=pl.ANY)


class AllToAll:
    def __init__(self, srcs, per_peer):
        self.srcs, self.per_peer, self.k_n = list(srcs), list(per_peer), len(srcs)
        self.out_shape = [S((NDEV,) + (a.shape[1:] if pp else a.shape), a.dtype) for a, pp in zip(srcs, per_peer)]
        self.scratch = [pltpu.SemaphoreType.DMA((self.k_n, NDEV - 1)), pltpu.SemaphoreType.DMA((self.k_n, NDEV - 1)),
                        pltpu.SemaphoreType.DMA((self.k_n,))]

    def copies(self, src, out, send_sems, recv_sems, local_sems):
        x, y, c = lax.axis_index("x"), lax.axis_index("y"), lax.axis_index("c")
        me = 4 * x + 2 * y + c

        def piece(k, p):
            return src[k].at[p] if self.per_peer[k] else src[k]

        local = [pltpu.make_async_copy(piece(k, me), out[k].at[me], local_sems.at[k]) for k in range(self.k_n)]
        sends, recvs = [], []
        for r in range(1, NDEV):
            px = 1 - x if (r >> 2) & 1 else x
            py = 1 - y if (r >> 1) & 1 else y
            pc = 1 - c if r & 1 else c
            pidx = 4 * px + 2 * py + pc
            for k in range(self.k_n):
                sends.append(pltpu.make_async_remote_copy(
                    src_ref=piece(k, pidx), dst_ref=out[k].at[me],
                    send_sem=send_sems.at[k, r - 1], recv_sem=recv_sems.at[k, r - 1],
                    device_id=(px, py, pc), device_id_type=pl.DeviceIdType.MESH))
                recvs.append(pltpu.make_async_remote_copy(
                    src_ref=piece(k, pidx), dst_ref=out[k].at[pidx],
                    send_sem=send_sems.at[k, r - 1], recv_sem=recv_sems.at[k, r - 1],
                    device_id=(px, py, pc), device_id_type=pl.DeviceIdType.MESH))

        def start():
            for cp in local + sends:
                cp.start()

        def wait():
            for cp in recvs:
                cp.wait_recv()
            for cp in sends:
                cp.wait_send()
            for cp in local:
                cp.wait()

        return start, wait


def exchange(srcs, per_peer, name):
    plan = AllToAll(srcs, per_peer)

    def body(*refs):
        start, wait = plan.copies(refs[:plan.k_n], refs[plan.k_n:2 * plan.k_n], *refs[2 * plan.k_n:])
        start()
        wait()

    return pl.pallas_call(
        body, name=name, out_shape=plan.out_shape, in_specs=[ANY] * plan.k_n, out_specs=[ANY] * plan.k_n,
        scratch_shapes=plan.scratch,
    )(*srcs)


def gather(srcs, name):
    k_n = len(srcs)
    out_shape = [S((NDEV,) + a.shape, a.dtype) for a in srcs]

    def body(*refs):
        src, out = refs[:k_n], refs[k_n:2 * k_n]
        send_sems, recv_sems, local_sems = refs[2 * k_n:]
        x, y, c = lax.axis_index("x"), lax.axis_index("y"), lax.axis_index("c")
        me, sibling = (x, y, c), (x, y, 1 - c)
        chips = [(1 - x, y), (x, 1 - y), (1 - x, 1 - y)]

        def slot(k, dev):
            return out[k].at[4 * dev[0] + 2 * dev[1] + dev[2]]

        def copy(k, r, block, to, from_src=False):
            return pltpu.make_async_remote_copy(
                src_ref=src[k] if from_src else slot(k, block), dst_ref=slot(k, block),
                send_sem=send_sems.at[k, r], recv_sem=recv_sems.at[k, r],
                device_id=to, device_id_type=pl.DeviceIdType.MESH)

        local = [pltpu.make_async_copy(src[k], slot(k, me), local_sems.at[k]) for k in range(k_n)]
        first = [copy(k, 0, me, sibling, True) for k in range(k_n)]
        first += [copy(k, 1 + n, me, (*chip, c), True) for n, chip in enumerate(chips) for k in range(k_n)]
        for cp in local + first:
            cp.start()
        passed = []
        for n, chip in enumerate(chips):
            for k in range(k_n):
                copy(k, 1 + n, (*chip, c), me).wait_recv()
                passed.append(copy(k, 4 + n, (*chip, c), sibling))
                passed[-1].start()
        for k in range(k_n):
            copy(k, 0, sibling, me).wait_recv()
            for n, chip in enumerate(chips):
                copy(k, 4 + n, (*chip, 1 - c), me).wait_recv()
        for cp in first + passed:
            cp.wait_send()
        for cp in local:
            cp.wait()

    any_spec = pl.BlockSpec(memory_space=pl.ANY)
    return pl.pallas_call(
        body, name=name, out_shape=out_shape,
        in_specs=[any_spec] * k_n, out_specs=[any_spec] * k_n,
        scratch_shapes=[pltpu.SemaphoreType.DMA((k_n, NDEV - 1)), pltpu.SemaphoreType.DMA((k_n, NDEV - 1)),
                        pltpu.SemaphoreType.DMA((k_n,))],
    )(*srcs)


def in_proj(h, g, w, bf, name):
    tp = h.shape[0]

    def body(h_ref, g_ref, w_ref, bf_ref, hn_ref, qa_ref, ka_ref, va_ref, pc_ref, sg_ref, carry):
        i = pl.program_id(0)

        @pl.when(i == 0)
        def _():
            carry[...] = jnp.zeros_like(carry)

        x = h_ref[...]
        r = lax.rsqrt(jnp.mean(x * x, axis=-1, keepdims=True) + EPS)
        hn = (x * r * g_ref[...]).astype(MM)
        hn_ref[...] = hn
        pc_ref[...] = jnp.dot(hn, w_ref[:, C0:F0], preferred_element_type=F32)
        z = jnp.dot(hn, w_ref[:, F0:NP_IN], preferred_element_type=F32) + bf_ref[...]
        lane = lax.broadcasted_iota(jnp.int32, z.shape, 1)
        logf = jnp.where(lane < H, jnp.minimum(z, 0.0) - jnp.log(1.0 + jnp.exp(-jnp.abs(z))), 0.0)
        sg_ref[...] = 1.0 / (1.0 + jnp.exp(z))
        row = lax.broadcasted_iota(jnp.int32, (TM, TM), 0)
        col = lax.broadcasted_iota(jnp.int32, (TM, TM), 1)
        tri = (row >= col).astype(F32)
        c = jnp.dot(tri, logf, precision=lax.Precision.HIGHEST, preferred_element_type=F32) + carry[...]
        carry[...] = c[TM - 1:TM, :]
        qkv = jnp.dot(hn, w_ref[:, 0:C0], preferred_element_type=F32)
        one = [1.0, 1.0, 1.0]
        for hd in range(H):
            t = slice(hd * HT, (hd + 1) * HT)
            cs = list(_split3(c[:, hd:hd + 1]))
            qa_ref[:, t] = _aug_tile(hd, qkv[:, hd * DH:(hd + 1) * DH] * SCALE, cs + one).astype(MM)
            ka_ref[:, t] = _aug_tile(hd, qkv[:, DA + hd * DH:DA + (hd + 1) * DH], one + [-v for v in cs] + one).astype(MM)
            va_ref[:, t] = _aug_tile(hd, qkv[:, 2 * DA + hd * DH:2 * DA + (hd + 1) * DH], [1.0] + one).astype(MM)

    rows = lambda n: pl.BlockSpec((TM, n), lambda i: (i, 0))
    return pl.pallas_call(
        body, name=name, grid=(tp // TM,),
        in_specs=[rows(D), _full((1, D)), _full((D, NP_IN)), _full((1, 128))],
        out_specs=[rows(D), rows(H * HT), rows(H * HT), rows(H * HT), rows(5 * DC), rows(128)],
        out_shape=[S((tp, D), MM), S((tp, H * HT), MM), S((tp, H * HT), MM), S((tp, H * HT), MM),
                   S((tp, 5 * DC), F32), S((tp, 128), F32)],
        scratch_shapes=[pltpu.VMEM((1, 128), F32)],
        compiler_params=_params(1),
    )(h, g, w, bf)


def _split3(c):
    hi = c.astype(MM).astype(F32)
    mid = (c - hi).astype(MM).astype(F32)
    lo = (c - hi - mid).astype(MM).astype(F32)
    return hi, mid, lo


def _main(hd):
    return slice(0, DH) if hd % 2 == 0 else slice(DH, HT)


def _aug_lane(hd, pos):
    return pos + (DH if hd % 2 == 0 else 0)


def _aug_tile(hd, main, cols):
    lane = lax.broadcasted_iota(jnp.int32, main.shape, 1)
    aug = jnp.zeros(main.shape, F32)
    for pos, val in enumerate(cols):
        aug = jnp.where(lane == pos, val, aug)
    return jnp.concatenate([main, aug] if hd % 2 == 0 else [aug, main], axis=1)


def _merge_pairs(tiles):
    lane = lax.broadcasted_iota(jnp.int32, tiles[0].shape, 1)
    return jnp.concatenate([jnp.where(lane < DH, tiles[2 * m], tiles[2 * m + 1]) for m in range(H // 2)], axis=1)


def _causal_mask():
    return lax.broadcasted_iota(jnp.int32, (TM, TM), 0) >= lax.broadcasted_iota(jnp.int32, (TM, TM), 1)


def _split_rider(refs, n_in, n_out, rider):
    if rider is None:
        return refs, None
    k = rider.k_n
    own = refs[:n_in] + refs[n_in + k:n_in + k + n_out] + refs[n_in + 2 * k + n_out:-3]
    return own, rider.copies(refs[n_in:n_in + k], refs[n_in + k + n_out:n_in + 2 * k + n_out], *refs[-3:])


def attn_fwd(qa, ka, va, name, rider=None):
    tp = qa.shape[0]
    nb = tp // TM

    def body(*refs):
        (q_ref, k_ref, v_ref, o_ref, lse_ref, *scr), copies = _split_rider(refs, 3, 2, rider)
        m_scr, acc_scr = scr[:H], scr[H:]
        i, j = pl.program_id(0), pl.program_id(1)

        if copies:
            @pl.when((i == 0) & (j == 0))
            def _():
                copies[0]()

        @pl.when(j == 0)
        def _():
            for hd in range(H):
                m_scr[hd][...] = jnp.full((TM, 1), NEG, F32)
                acc_scr[hd][...] = jnp.zeros((TM, HT), F32)

        def logits(hd):
            t = slice(hd * HT, (hd + 1) * HT)
            return lax.dot_general(q_ref[:, t], k_ref[:, t], NT, preferred_element_type=F32)

        def step(diag):
            s_next = logits(0)
            for hd in range(H):
                t = slice(hd * HT, (hd + 1) * HT)
                s = s_next
                if hd + 1 < H:
                    s_next = logits(hd + 1)
                if diag:
                    s = jnp.where(_causal_mask(), s, NEG)
                m_prev = m_scr[hd][...]
                m_new = jnp.maximum(m_prev, jnp.max(s, axis=1, keepdims=True))
                p = jnp.exp(s - m_new).astype(MM)
                acc_scr[hd][...] = (jnp.exp(m_prev - m_new) * acc_scr[hd][...]
                                    + jnp.dot(p, v_ref[:, t], preferred_element_type=F32))
                m_scr[hd][...] = m_new

        @pl.when(j < i)
        def _():
            step(False)

        @pl.when(j == i)
        def _():
            step(True)
            lse = jnp.zeros((TM, 128), F32)
            outs = []
            for hd in range(H):
                acc = acc_scr[hd][...]
                l = acc[:, _aug_lane(hd, 0):_aug_lane(hd, 0) + 1]
                outs.append(acc / l)
                lse = _lane_put(lse, m_scr[hd][...] + jnp.log(l), hd)
            o_ref[...] = _merge_pairs(outs).astype(MM)
            lse_ref[...] = lse

        if copies:
            @pl.when((i == nb - 1) & (j == nb - 1))
            def _():
                copies[1]()

    kv = pl.BlockSpec((TM, H * HT), lambda i, j: (jnp.minimum(j, i), 0))
    r_n = rider.k_n if rider else 0
    return pl.pallas_call(
        body, name=name, grid=(nb, nb),
        in_specs=[pl.BlockSpec((TM, H * HT), lambda i, j: (i, 0)), kv, kv] + [ANY] * r_n,
        out_specs=[pl.BlockSpec((TM, DA), lambda i, j: (i, 0)), pl.BlockSpec((TM, 128), lambda i, j: (i, 0))] + [ANY] * r_n,
        out_shape=[S((tp, DA), MM), S((tp, 128), F32)] + (rider.out_shape if rider else []),
        scratch_shapes=[pltpu.VMEM((TM, 1), F32)] * H + [pltpu.VMEM((TM, HT), F32)] * H + (rider.scratch if rider else []),
        compiler_params=_params(2),
    )(qa, ka, va, *(rider.srcs if rider else []))


def _layernorm_parts(dw, g, b):
    mu = jnp.mean(dw, axis=-1, keepdims=True)
    xc = dw - mu
    rs = lax.rsqrt(jnp.mean(xc * xc, axis=-1, keepdims=True) + EPS)
    xh = xc * rs
    return xh, rs, xh * g + b


def _fill_halo(ext, halo, cur, first):
    ext[0:HALO, :] = jnp.where(first, 0.0, halo)
    ext[HALO:HALO + TM, :] = cur


def conv_fwd(pc, wdw, bdw, lng, lnb, wpw, bpw, wsc, name):
    tp = pc.shape[0]

    def body(pc_ref, hl_ref, wdw_ref, bdw_ref, lng_ref, lnb_ref, wpw_ref, bpw_ref, wsc_ref, cc_ref, sv_ref, xe, ce):
        first = pl.program_id(0) == 0
        glu = pc_ref[:, 0:DC] * _sigmoid(pc_ref[:, DC:2 * DC])
        cu = pc_ref[:, 3 * DC:4 * DC] * pc_ref[:, 4 * DC:5 * DC]
        _fill_halo(xe, hl_ref[:, 0:DC] * _sigmoid(hl_ref[:, DC:2 * DC]), glu, first)
        _fill_halo(ce, hl_ref[:, 3 * DC:4 * DC] * hl_ref[:, 4 * DC:5 * DC], cu, first)
        dw = jnp.zeros((TM, DC), F32) + bdw_ref[...]
        for k in range(CK):
            dw = dw + wdw_ref[k:k + 1, :] * xe[pl.ds(HALO - (CK - 1) + k, TM), :]
        cv = jnp.zeros((TM, DC), F32)
        for k in range(SK):
            cv = cv + wsc_ref[k:k + 1, :] * ce[pl.ds(HALO - (SK - 1) + k, TM), :]
        _, _, ln = _layernorm_parts(dw, lng_ref[...], lnb_ref[...])
        act = ln * _sigmoid(ln)
        conf = jnp.dot(act.astype(MM), wpw_ref[...], preferred_element_type=F32) + bpw_ref[...]
        cc_ref[:, 0:DC] = conf.astype(MM)
        cc_ref[:, DC:2 * DC] = (pc_ref[:, 2 * DC:3 * DC] * cv).astype(MM)
        sv_ref[:, 0:DC] = dw
        sv_ref[:, DC:2 * DC] = cv

    hb = TM // HALO
    rows = lambda n: pl.BlockSpec((TM, n), lambda i: (i, 0))
    return pl.pallas_call(
        body, name=name, grid=(tp // TM,),
        in_specs=[rows(5 * DC), pl.BlockSpec((HALO, 5 * DC), lambda i: (jnp.maximum(i * hb - 1, 0), 0)),
                  _full((32, DC)), _full((1, DC)), _full((1, DC)), _full((1, DC)), _full((DC, DC)), _full((1, DC)),
                  _full((8, DC))],
        out_specs=[rows(2 * DC), rows(2 * DC)],
        out_shape=[S((tp, 2 * DC), MM), S((tp, 2 * DC), F32)],
        scratch_shapes=[pltpu.VMEM((HALO + TM, DC), F32), pltpu.VMEM((HALO + TM, DC), F32)],
        compiler_params=_params(1),
    )(pc, pc, wdw, bdw, lng, lnb, wpw, bpw, wsc)


def out_proj(h, o, cc, wo, g2, name):
    tp = h.shape[0]

    def body(h_ref, o_ref, cc_ref, wo_ref, g_ref, h2_ref, hn_ref):
        wo = wo_ref[...].reshape(D, D)
        h2 = (h_ref[...] + jnp.dot(o_ref[...], wo[0:DA, :], preferred_element_type=F32)
              + jnp.dot(cc_ref[...], wo[DA:D, :], preferred_element_type=F32))
        h2_ref[...] = h2
        r = lax.rsqrt(jnp.mean(h2 * h2, axis=-1, keepdims=True) + EPS)
        hn_ref[...] = (h2 * r * g_ref[...]).astype(MM)

    rows = lambda n: pl.BlockSpec((TM, n), lambda i: (i, 0))
    return pl.pallas_call(
        body, name=name, grid=(tp // TM,),
        in_specs=[rows(D), rows(DA), rows(2 * DC), _full((NDEV, D // NDEV, D)), _full((1, D))],
        out_specs=[rows(D), rows(D)],
        out_shape=[S((tp, D), F32), S((tp, D), MM)],
        compiler_params=_params(1),
    )(h, o, cc, wo, g2)


def mlp_fwd(hn, h2, w1, w2, name):
    tp = hn.shape[0]
    nf = DFF // FC

    def body(hn_ref, h2_ref, w1_ref, w2_ref, r_ref, z_ref, h3_ref, acc):
        j = pl.program_id(1)

        @pl.when(j == 0)
        def _():
            acc[...] = jnp.zeros_like(acc)

        r = jnp.maximum(jnp.dot(hn_ref[...], w1_ref[...], preferred_element_type=F32), 0.0)
        zb = (r * r).astype(MM)
        r_ref[...] = r.astype(MM)
        z_ref[...] = zb
        acc[...] += jnp.dot(zb, w2_ref[...], preferred_element_type=F32)

        @pl.when(j == nf - 1)
        def _():
            h3_ref[...] = h2_ref[...] + acc[...]

    return pl.pallas_call(
        body, name=name, grid=(tp // TM, nf),
        in_specs=[pl.BlockSpec((TM, D), lambda i, j: (i, 0)), pl.BlockSpec((TM, D), lambda i, j: (i, 0)),
                  pl.BlockSpec((None, D, FC), lambda i, j: (j, 0, 0)), pl.BlockSpec((None, FC, D), lambda i, j: (j, 0, 0))],
        out_specs=[pl.BlockSpec((TM, FC), lambda i, j: (i, j)), pl.BlockSpec((TM, FC), lambda i, j: (i, j)),
                   pl.BlockSpec((TM, D), lambda i, j: (i, 0))],
        out_shape=[S((tp, DFF), MM), S((tp, DFF), MM), S((tp, D), F32)],
        scratch_shapes=[pltpu.VMEM((TM, D), F32)],
        compiler_params=_params(2),
    )(hn, h2, w1, w2)


def loss_head(h, tgt, g, t_real, name):
    tp = h.shape[0]

    def body(h_ref, t_ref, g_ref, dh_ref, loss_ref, dg_ref):
        i = pl.program_id(0)

        @pl.when(i == 0)
        def _():
            loss_ref[...] = jnp.zeros_like(loss_ref)
            dg_ref[...] = jnp.zeros_like(dg_ref)

        x = h_ref[...]
        gg = g_ref[...]
        r = lax.rsqrt(jnp.mean(x * x, axis=-1, keepdims=True) + EPS)
        xn = x * r
        row = i * TM + lax.broadcasted_iota(jnp.int32, (TM, 1), 0)
        e = jnp.where((row >= NM) & (row < t_real), xn * gg - t_ref[...], 0.0)
        loss_ref[...] += jnp.sum(e * e) * (0.5 / D)
        dy = e * (1.0 / D)
        dg_ref[...] += jnp.sum(dy * xn, axis=0, keepdims=True)
        u = dy * gg
        dh_ref[...] = r * (u - xn * jnp.mean(u * xn, axis=-1, keepdims=True))

    rows = lambda n: pl.BlockSpec((TM, n), lambda i: (i, 0))
    return pl.pallas_call(
        body, name=name, grid=(tp // TM,),
        in_specs=[rows(D), rows(D), _full((1, D))],
        out_specs=[rows(D), _full((8, 128)), _full((1, D))],
        out_shape=[S((tp, D), F32), S((8, 128), F32), S((1, D), F32)],
        compiler_params=_params(1),
    )(h, tgt, g)


def _rmsnorm_bwd(x, g, dy):
    r = lax.rsqrt(jnp.mean(x * x, axis=-1, keepdims=True) + EPS)
    xn = x * r
    u = dy * g
    dx = r * (u - xn * jnp.mean(u * xn, axis=-1, keepdims=True))
    return dx, jnp.sum(dy * xn, axis=0, keepdims=True)


def matmul_tn(a, b, tm, tn, name, n=None, out_dtype=F32, shard=None):
    tp, m = a.shape
    n = b.shape[1] if n is None else n
    nk = tp // TM

    def body(a_ref, b_ref, o_ref, acc):
        k = pl.program_id(2)

        @pl.when(k == 0)
        def _():
            acc[...] = jnp.zeros_like(acc)

        acc[...] += lax.dot_general(a_ref[...].astype(MM), b_ref[...].astype(MM), TN, preferred_element_type=F32)

        @pl.when(k == nk - 1)
        def _():
            o_ref[...] = acc[...].astype(out_dtype)

    if shard is None:
        out_spec, out_shape = pl.BlockSpec((tm, tn), lambda mi, ni, k: (mi, ni)), (m, n)
    elif shard == "m":
        out_spec, out_shape = pl.BlockSpec((None, tm, tn), lambda mi, ni, k: (mi, 0, ni)), (m // tm, tm, n)
    else:
        out_spec, out_shape = pl.BlockSpec((None, tm, tn), lambda mi, ni, k: (ni, mi, 0)), (n // tn, m, tn)
    return pl.pallas_call(
        body, name=name, grid=(m // tm, n // tn, nk),
        in_specs=[pl.BlockSpec((TM, tm), lambda mi, ni, k: (k, mi)), pl.BlockSpec((TM, tn), lambda mi, ni, k: (k, ni))],
        out_specs=out_spec, out_shape=S(out_shape, out_dtype),
        scratch_shapes=[pltpu.VMEM((tm, tn), F32)],
        compiler_params=_params(3),
    )(a, b)


def mlp_bwd(dh3, r, w1, w2, h2, g, name):
    tp = dh3.shape[0]
    nf = DFF // FC

    def body(dh3_ref, r_ref, w1_ref, w2_ref, h2_ref, g_ref, da_ref, dh2_ref, dg_ref, acc, dhb):
        i, j = pl.program_id(0), pl.program_id(1)

        @pl.when((i == 0) & (j == 0))
        def _():
            dg_ref[...] = jnp.zeros_like(dg_ref)

        @pl.when(j == 0)
        def _():
            acc[...] = jnp.zeros_like(acc)
            dhb[...] = dh3_ref[...].astype(MM)

        dz = lax.dot_general(dhb[...], w2_ref[...], NT, preferred_element_type=F32)
        da = (dz * (2.0 * r_ref[...].astype(F32))).astype(MM)
        da_ref[...] = da
        acc[...] += lax.dot_general(da, w1_ref[...], NT, preferred_element_type=F32)

        @pl.when(j == nf - 1)
        def _():
            dx, dg = _rmsnorm_bwd(h2_ref[...], g_ref[...], acc[...])
            dh2_ref[...] = dh3_ref[...] + dx
            dg_ref[...] += dg

    return pl.pallas_call(
        body, name=name, grid=(tp // TM, nf),
        in_specs=[pl.BlockSpec((TM, D), lambda i, j: (i, 0)), pl.BlockSpec((TM, FC), lambda i, j: (i, j)),
                  pl.BlockSpec((None, D, FC), lambda i, j: (j, 0, 0)), pl.BlockSpec((None, FC, D), lambda i, j: (j, 0, 0)),
                  pl.BlockSpec((TM, D), lambda i, j: (i, 0)), _full((1, D))],
        out_specs=[pl.BlockSpec((TM, FC), lambda i, j: (i, j)), pl.BlockSpec((TM, D), lambda i, j: (i, 0)),
                   _full((1, D))],
        out_shape=[S((tp, DFF), MM), S((tp, D), F32), S((1, D), F32)],
        scratch_shapes=[pltpu.VMEM((TM, D), F32), pltpu.VMEM((TM, D), MM)],
        compiler_params=_params(2),
    )(dh3, r, w1, w2, h2, g)


def out_proj_bwd(dh2, wo, name):
    tp = dh2.shape[0]

    def body(dh_ref, wo_ref, do_ref, dcc_ref):
        dcat = lax.dot_general(dh_ref[...].astype(MM), wo_ref[...].reshape(D, D), NT, preferred_element_type=F32)
        do_ref[...] = dcat[:, 0:DA].astype(MM)
        dcc_ref[...] = dcat[:, DA:D]

    rows = lambda n: pl.BlockSpec((TM, n), lambda i: (i, 0))
    return pl.pallas_call(
        body, name=name, grid=(tp // TM,),
        in_specs=[rows(D), _full((NDEV, D // NDEV, D))],
        out_specs=[rows(DA), rows(2 * DC)],
        out_shape=[S((tp, DA), MM), S((tp, 2 * DC), F32)],
        compiler_params=_params(1),
    )(dh2, wo)


def attn_bwd_prep(qa, do, o, lse, name):
    tp = qa.shape[0]

    def body(q_ref, do_ref, o_ref, lse_ref, qb_ref, dob_ref, qbt_ref, dobt_ref):
        lse = lse_ref[...]
        lane = lax.broadcasted_iota(jnp.int32, (TM, HT), 1)
        for hd in range(H):
            t, hs = slice(hd * HT, (hd + 1) * HT), slice(hd * DH, (hd + 1) * DH)
            dof = do_ref[:, hs].astype(F32)
            dd = jnp.sum(dof * o_ref[:, hs].astype(F32), axis=1, keepdims=True)
            dob = _aug_tile(hd, dof, [0.0] + [-v for v in _split3(dd)])
            qb = q_ref[:, t].astype(F32)
            for pos, v in enumerate(_split3(lse[:, hd:hd + 1])):
                qb = jnp.where(lane == _aug_lane(hd, 6 + pos), -v, qb)
            qb_ref[:, t] = qb.astype(MM)
            dob_ref[:, t] = dob.astype(MM)
            qbt_ref[t, :] = qb.T.astype(MM)
            dobt_ref[t, :] = dob.T.astype(MM)

    rows = lambda n: pl.BlockSpec((TM, n), lambda i: (i, 0))
    cols = pl.BlockSpec((H * HT, TM), lambda i: (0, i))
    return pl.pallas_call(
        body, name=name, grid=(tp // TM,),
        in_specs=[rows(H * HT), rows(DA), rows(DA), rows(128)],
        out_specs=[rows(H * HT), rows(H * HT), cols, cols],
        out_shape=[S((tp, H * HT), MM), S((tp, H * HT), MM), S((H * HT, tp), MM), S((H * HT, tp), MM)],
        compiler_params=_params(1),
    )(qa, do, o, lse)


def attn_bwd(qb, dob, qbt, dobt, ka, va, name, rider=None):
    tp = qb.shape[0]
    nb = tp // TM

    def body(*refs):
        own, copies = _split_rider(refs, 6, 5, rider)
        (qb_ref, dob_ref, qbt_ref, dobt_ref, k_ref, v_ref, dq_ref, dk_ref, dv_ref, dcq_ref, dck_ref,
         dq_acc, dcq_acc, dkt_acc, dvt_acc, dck_acc, stage, sems) = own
        j, i = pl.program_id(0), pl.program_id(1)

        @pl.when((j == 0) & (i == 0))
        def _():
            if copies:
                copies[0]()
            dq_acc[...] = jnp.zeros_like(dq_acc)
            dcq_acc[...] = jnp.zeros_like(dcq_acc)

        @pl.when(i == 0)
        def _():
            dkt_acc[...] = jnp.zeros_like(dkt_acc)
            dvt_acc[...] = jnp.zeros_like(dvt_acc)
            dck_acc[...] = jnp.zeros_like(dck_acc)

        def step(diag):
            rows = pl.ds(pl.multiple_of(i * TM, TM), TM)
            dcq = dcq_acc[rows, :]
            lane = lax.broadcasted_iota(jnp.int32, (TM, HT), 1)
            for hd in range(H):
                t = slice(hd * HT, (hd + 1) * HT)
                s = lax.dot_general(qb_ref[:, t], k_ref[:, t], NT, preferred_element_type=F32)
                dp = lax.dot_general(dob_ref[:, t], v_ref[:, t], NT, preferred_element_type=F32)
                if diag:
                    s = jnp.where(_causal_mask(), s, NEG)
                p = jnp.exp(s)
                ds = p * dp
                dsb = ds.astype(MM)
                dvt_acc[t, :] += jnp.dot(dobt_ref[t, :], p.astype(MM), preferred_element_type=F32)
                dkt_acc[t, :] += jnp.dot(qbt_ref[t, :], dsb, preferred_element_type=F32)
                dqh = jnp.dot(dsb, k_ref[:, t], preferred_element_type=F32)
                if hd % 2 == 0:
                    dq_even = dqh
                else:
                    pair = slice((hd // 2) * HT, (hd // 2 + 1) * HT)
                    dq_acc[rows, pair] += jnp.where(lane < DH, dq_even, dqh)
                dcq = dcq + jnp.where(lane == hd, jnp.sum(ds, axis=1, keepdims=True), 0.0)
                dck_acc[hd:hd + 1, :] += jnp.sum(ds, axis=0, keepdims=True)
            dcq_acc[rows, :] = dcq
            return rows

        @pl.when(i > j)
        def _():
            step(False)

        @pl.when(i == j)
        def _():
            rows = step(True)
            stage[...] = (dq_acc[rows, :] * SCALE).astype(MM)
            out = [pltpu.make_async_copy(stage, dq_ref.at[rows, :], sems.at[0]),
                   pltpu.make_async_copy(dcq_acc.at[rows, :], dcq_ref.at[rows, :], sems.at[1])]
            for cp in out:
                cp.start()
            for cp in out:
                cp.wait()

        @pl.when(i == nb - 1)
        def _():
            dk_ref[...] = _merge_pairs([dkt_acc[hd * HT:(hd + 1) * HT, :].T for hd in range(H)]).astype(MM)
            dv_ref[...] = _merge_pairs([dvt_acc[hd * HT:(hd + 1) * HT, :].T for hd in range(H)]).astype(MM)
            dck_ref[...] = dck_acc[...]

        if copies:
            @pl.when((j == nb - 1) & (i == nb - 1))
            def _():
                copies[1]()

    qi = lambda n: pl.BlockSpec((TM, n), lambda j, i: (jnp.maximum(i, j), 0))
    qt = pl.BlockSpec((H * HT, TM), lambda j, i: (0, jnp.maximum(i, j)))
    kj = lambda n: pl.BlockSpec((TM, n), lambda j, i: (j, 0))
    r_n = rider.k_n if rider else 0
    return pl.pallas_call(
        body, name=name, grid=(nb, nb),
        in_specs=[qi(H * HT), qi(H * HT), qt, qt, kj(H * HT), kj(H * HT)] + [ANY] * r_n,
        out_specs=[ANY, kj(DA), kj(DA), ANY, pl.BlockSpec((H, TM), lambda j, i: (0, j))] + [ANY] * r_n,
        out_shape=[S((tp, DA), MM), S((tp, DA), MM), S((tp, DA), MM), S((tp, 128), F32), S((H, tp), F32)]
        + (rider.out_shape if rider else []),
        scratch_shapes=[pltpu.VMEM((tp, DA), F32), pltpu.VMEM((tp, 128), F32), pltpu.VMEM((H * HT, TM), F32),
                        pltpu.VMEM((H * HT, TM), F32), pltpu.VMEM((H, TM), F32), pltpu.VMEM((TM, DA), MM),
                        pltpu.SemaphoreType.DMA((2,))] + (rider.scratch if rider else []),
        compiler_params=_params(2),
    )(qb, dob, qbt, dobt, ka, va, *(rider.srcs if rider else []))


def fgate_bwd(dcq, dck, sg, name):
    tp = dcq.shape[0]
    nb = tp // TM

    def body(dcq_ref, dck_ref, sg_ref, dz_ref, db_ref, carry):
        @pl.when(pl.program_id(0) == 0)
        def _():
            carry[...] = jnp.zeros_like(carry)
            db_ref[...] = jnp.zeros_like(db_ref)

        row = lax.broadcasted_iota(jnp.int32, (TM, TM), 0)
        col = lax.broadcasted_iota(jnp.int32, (TM, TM), 1)
        tri = (row <= col).astype(F32)
        dl = jnp.dot(tri, dcq_ref[...] - dck_ref[...], precision=lax.Precision.HIGHEST,
                     preferred_element_type=F32) + carry[...]
        carry[...] = dl[0:1, :]
        dz = dl * sg_ref[...]
        dz_ref[...] = dz.astype(MM)
        db_ref[...] += jnp.sum(dz, axis=0, keepdims=True)

    rev = pl.BlockSpec((TM, 128), lambda i: (nb - 1 - i, 0))
    return pl.pallas_call(
        body, name=name, grid=(nb,),
        in_specs=[rev, rev, rev], out_specs=[rev, _full((1, 128))],
        out_shape=[S((tp, 128), MM), S((1, 128), F32)],
        scratch_shapes=[pltpu.VMEM((1, 128), F32)],
        compiler_params=_params(1),
    )(dcq, dck, sg)


def conv_bwd_pointwise(dcc, sv, pc, lng, lnb, wpw, name):
    tp = dcc.shape[0]

    def body(dcc_ref, sv_ref, b_ref, lng_ref, lnb_ref, wpw_ref, gc_ref, act_ref, pg_ref):
        @pl.when(pl.program_id(0) == 0)
        def _():
            pg_ref[...] = jnp.zeros_like(pg_ref)

        dconf = dcc_ref[:, 0:DC]
        g = lng_ref[...]
        xh, rs, ln = _layernorm_parts(sv_ref[:, 0:DC], g, lnb_ref[...])
        sig = _sigmoid(ln)
        act_ref[...] = (ln * sig).astype(MM)
        dact = lax.dot_general(dconf.astype(MM), wpw_ref[...], NT, preferred_element_type=F32)
        dln = dact * (sig * (1.0 + ln * (1.0 - sig)))
        dxh = dln * g
        ddw = rs * (dxh - jnp.mean(dxh, axis=-1, keepdims=True) - xh * jnp.mean(dxh * xh, axis=-1, keepdims=True))
        gc_ref[:, 0:DC] = ddw
        gc_ref[:, DC:2 * DC] = dcc_ref[:, DC:2 * DC] * b_ref[...]
        cs = lambda t: jnp.sum(t, axis=0, keepdims=True)
        pg_ref[0:1, :] += cs(dconf)
        pg_ref[1:2, :] += cs(dln * xh)
        pg_ref[2:3, :] += cs(dln)
        pg_ref[3:4, :] += cs(ddw)

    rows = lambda n: pl.BlockSpec((TM, n), lambda i: (i, 0))
    return pl.pallas_call(
        body, name=name, grid=(tp // TM,),
        in_specs=[rows(2 * DC), rows(2 * DC), pl.BlockSpec((TM, DC), lambda i: (i, 2)),
                  _full((1, DC)), _full((1, DC)), _full((DC, DC))],
        out_specs=[rows(2 * DC), rows(DC), _full((8, DC))],
        out_shape=[S((tp, 2 * DC), F32), S((tp, DC), MM), S((8, DC), F32)],
        compiler_params=_params(1),
    )(dcc, sv, pc, lng, lnb, wpw)


def conv_bwd_taps(gc, pc, dcc, sv, wdw, wsc, name):
    tp = gc.shape[0]
    nb = tp // TM
    hb = TM // HALO

    def body(gc_ref, gn_ref, pc_ref, hl_ref, dcc_ref, sv_ref, wdw_ref, wsc_ref, dpc_ref, wg_ref, ge, xe, ce):
        i = pl.program_id(0)

        @pl.when(i == 0)
        def _():
            wg_ref[...] = jnp.zeros_like(wg_ref)

        a, gt = pc_ref[:, 0:DC], pc_ref[:, DC:2 * DC]
        c, u = pc_ref[:, 3 * DC:4 * DC], pc_ref[:, 4 * DC:5 * DC]
        sig = _sigmoid(gt)
        _fill_halo(xe, hl_ref[:, 0:DC] * _sigmoid(hl_ref[:, DC:2 * DC]), a * sig, i == 0)
        _fill_halo(ce, hl_ref[:, 3 * DC:4 * DC] * hl_ref[:, 4 * DC:5 * DC], c * u, i == 0)
        ge[0:TM, :] = gc_ref[...]
        ge[TM:TM + HALO, :] = jnp.where(i == nb - 1, 0.0, gn_ref[...])
        ddw, dcv = gc_ref[:, 0:DC], gc_ref[:, DC:2 * DC]
        dglu = jnp.zeros((TM, DC), F32)
        for k in range(CK):
            dglu = dglu + wdw_ref[k:k + 1, :] * ge[pl.ds(CK - 1 - k, TM), 0:DC]
            wg_ref[k:k + 1, :] += jnp.sum(ddw * xe[pl.ds(HALO - (CK - 1) + k, TM), :], axis=0, keepdims=True)
        dcu = jnp.zeros((TM, DC), F32)
        for k in range(SK):
            dcu = dcu + wsc_ref[k:k + 1, :] * ge[pl.ds(SK - 1 - k, TM), DC:2 * DC]
            wg_ref[32 + k:33 + k, :] += jnp.sum(dcv * ce[pl.ds(HALO - (SK - 1) + k, TM), :], axis=0, keepdims=True)
        dpc_ref[:, 0:DC] = (dglu * sig).astype(MM)
        dpc_ref[:, DC:2 * DC] = (dglu * a * sig * (1.0 - sig)).astype(MM)
        dpc_ref[:, 2 * DC:3 * DC] = (dcc_ref[:, DC:2 * DC] * sv_ref[:, DC:2 * DC]).astype(MM)
        dpc_ref[:, 3 * DC:4 * DC] = (dcu * u).astype(MM)
        dpc_ref[:, 4 * DC:5 * DC] = (dcu * c).astype(MM)

    rows = lambda n: pl.BlockSpec((TM, n), lambda i: (i, 0))
    return pl.pallas_call(
        body, name=name, grid=(nb,),
        in_specs=[rows(2 * DC), pl.BlockSpec((HALO, 2 * DC), lambda i: (jnp.minimum((i + 1) * hb, nb * hb - 1), 0)),
                  rows(5 * DC), pl.BlockSpec((HALO, 5 * DC), lambda i: (jnp.maximum(i * hb - 1, 0), 0)),
                  rows(2 * DC), rows(2 * DC), _full((32, DC)), _full((8, DC))],
        out_specs=[rows(5 * DC), _full((40, DC))],
        out_shape=[S((tp, 5 * DC), MM), S((40, DC), F32)],
        scratch_shapes=[pltpu.VMEM((TM + HALO, 2 * DC), F32), pltpu.VMEM((HALO + TM, DC), F32),
                        pltpu.VMEM((HALO + TM, DC), F32)],
        compiler_params=_params(1),
    )(gc, gc, pc, pc, dcc, sv, wdw, wsc)


def in_proj_bwd(dproj, w, h, g, dh_in, name):
    tp = h.shape[0]

    def body(dp_ref, w_ref, h_ref, g_ref, di_ref, dh_ref, dg_ref):
        @pl.when(pl.program_id(0) == 0)
        def _():
            dg_ref[...] = jnp.zeros_like(dg_ref)

        dhn = lax.dot_general(dp_ref[...], w_ref[...], NT, preferred_element_type=F32)
        dx, dg = _rmsnorm_bwd(h_ref[...], g_ref[...], dhn)
        dh_ref[...] = di_ref[...] + dx
        dg_ref[...] += dg

    rows = lambda n: pl.BlockSpec((TM, n), lambda i: (i, 0))
    return pl.pallas_call(
        body, name=name, grid=(tp // TM,),
        in_specs=[rows(NP_IN), _full((D, NP_IN)), rows(D), _full((1, D)), rows(D)],
        out_specs=[rows(D), _full((1, D))],
        out_shape=[S((tp, D), F32), S((1, D), F32)],
        compiler_params=_params(1),
    )(dproj, w, h, g, dh_in)


def adamw(recvs, w, m, v, rb, name):
    l_n, r_n, c_n = w.shape

    def body(*refs):
        p_refs = refs[:l_n]
        w_ref, m_ref, v_ref, g_ref, d_ref, m2_ref, v2_ref = refs[l_n:]
        for l in range(l_n):
            g = p_refs[l][0].astype(F32)
            for s in range(1, NDEV):
                g = g + p_refs[l][s].astype(F32)
            m2 = ADAM_B1 * m_ref[l] + (1.0 - ADAM_B1) * g
            v2 = ADAM_B2 * v_ref[l] + (1.0 - ADAM_B2) * (g * g)
            m_hat = m2 / (1.0 - ADAM_B1 ** ADAM_STEP)
            v_hat = v2 / (1.0 - ADAM_B2 ** ADAM_STEP)
            g_ref[l] = g
            d_ref[l] = -ADAM_LR * (m_hat / (jnp.sqrt(v_hat) + ADAM_EPS) + ADAM_WD * w_ref[l])
            m2_ref[l] = m2
            v2_ref[l] = v2

    blk = pl.BlockSpec((l_n, rb, c_n), lambda r: (0, r, 0))
    return pl.pallas_call(
        body, name=name, grid=(r_n // rb,),
        in_specs=[pl.BlockSpec((NDEV, rb, c_n), lambda r: (0, r, 0))] * l_n + [blk, blk, blk],
        out_specs=[blk] * 4, out_shape=[S(w.shape, F32)] * 4,
        compiler_params=_params(1),
    )(*recvs, w, m, v)


TINY_ROWS = 168
REP_ROWS = 64


def _pack(parts, rows):
    flat = jnp.concatenate([p.reshape(-1) for p in parts])
    return jnp.pad(flat, (0, rows * 128 - flat.shape[0])).reshape(rows, 128)


def _unpack(buf, shapes):
    flat, out, o = buf.reshape(-1), [], 0
    for s in shapes:
        n = 1
        for d in s:
            n *= d
        out.append(flat[o:o + n].reshape(s))
        o += n
    return out


TINY_SHAPES = [(DEPTH, CK, DC // NDEV), (DEPTH, SK, DC // NDEV), (NM, D // NDEV), (DEPTH, DC // NDEV, DC)]
REP_SHAPES = [(DEPTH, D), (DEPTH, H), (DEPTH, DC), (DEPTH, DC), (DEPTH, DC), (DEPTH, DC), (DEPTH, D), (D,)]


def _to_padded_cols(w):
    pad = jnp.zeros(w.shape[:-1] + (NP_IN - N_IN,), w.dtype)
    return jnp.concatenate([w[..., :C0], w[..., C0 + H:], w[..., C0:C0 + H], pad], axis=-1)


def _from_padded_cols(w):
    return jnp.concatenate([w[..., :C0], w[..., F0:F0 + H], w[..., C0:F0]], axis=-1)


def kernel(x, meta_tokens, mix_norm_g, w_in, b_forget, w_conf_dw, b_conf_dw, conf_ln_g, conf_ln_b, w_conf_pw, b_conf_pw, w_sc_conv, w_out, mlp_norm_g, w_mlp1, w_mlp2, final_norm_g, loss_target, m_meta_tokens, m_mix_norm_g, m_w_in, m_b_forget, m_w_conf_dw, m_b_conf_dw, m_conf_ln_g, m_conf_ln_b, m_w_conf_pw, m_b_conf_pw, m_w_sc_conv, m_w_out, m_mlp_norm_g, m_w_mlp1, m_w_mlp2, m_final_norm_g, v_meta_tokens, v_mix_norm_g, v_w_in, v_b_forget, v_w_conf_dw, v_b_conf_dw, v_conf_ln_g, v_conf_ln_b, v_w_conf_pw, v_b_conf_pw, v_w_sc_conv, v_w_out, v_mlp_norm_g, v_w_mlp1, v_w_mlp2, v_final_norm_g):
    seq = x.shape[1]
    t_real = NM + seq
    tp = -(-t_real // TM) * TM

    tiny_w = _pack([w_conf_dw, w_sc_conv, meta_tokens, w_conf_pw], TINY_ROWS)
    big = lambda l: [w_in[l].astype(MM), w_out[l].astype(MM), w_mlp1[l].astype(MM), w_mlp2[l].astype(MM)]
    *first, g_tiny = gather(big(0) + [tiny_w], "gather_weights0")
    gathered = {0: first}
    n_sh = w_in.shape[-1]
    assert w_mlp1.shape[-1] == FC and w_mlp2.shape[-2] == FC
    tiny = [_unpack(g_tiny[s], TINY_SHAPES) for s in range(NDEV)]
    wdw = jnp.concatenate([t[0] for t in tiny], axis=-1)
    wsc = jnp.concatenate([t[1] for t in tiny], axis=-1)
    meta = jnp.concatenate([t[2] for t in tiny], axis=-1)
    wpw = jnp.concatenate([t[3] for t in tiny], axis=1).astype(MM)
    wdw = jnp.pad(wdw, ((0, 0), (0, 32 - CK), (0, 0)))
    wsc = jnp.pad(wsc, ((0, 0), (0, 8 - SK), (0, 0)))
    bfp = jnp.pad(b_forget, ((0, 0), (0, 128 - H)))

    row = lambda a: a.reshape(1, -1)

    h = jnp.concatenate([meta, x[0], jnp.zeros((tp - t_real, D), F32)], axis=0)
    tgt = jnp.pad(loss_target[0], ((NM, tp - t_real), (0, 0)))
    saved = []
    for l in range(DEPTH):
        g_in, g_out, g_w1, g_w2 = gathered[l]
        win = _to_padded_cols(g_in.transpose(1, 0, 2).reshape(D, NDEV * n_sh))
        hn, qa, ka, va, pc, sg = in_proj(h, row(mix_norm_g[l]), win, row(bfp[l]), f"in_proj{l}")
        rider = AllToAll(big(l + 1), [False] * 4) if l + 1 < DEPTH else None
        o, lse, *arrived = attn_fwd(qa, ka, va, f"attn_fwd{l}", rider)
        if rider:
            gathered[l + 1] = arrived
        cc, sv = conv_fwd(pc, wdw[l], row(b_conf_dw[l]), row(conf_ln_g[l]), row(conf_ln_b[l]), wpw[l],
                          row(b_conf_pw[l]), wsc[l], f"conv_fwd{l}")
        h2, hn2 = out_proj(h, o, cc, g_out, row(mlp_norm_g[l]), f"out_proj{l}")
        r, z, h3 = mlp_fwd(hn2, h2, g_w1, g_w2, f"mlp_fwd{l}")
        saved.append((h, hn, qa, ka, va, pc, sg, o, lse, cc, sv, h2, hn2, r, z, win))
        h = h3

    dh, loss_part, d_gf = loss_head(h, tgt, row(final_norm_g), t_real, "loss_head")
    loss = lax.psum(loss_part[0, 0], ("x", "y", "c"))

    gw, recv, d_win = {}, {}, {}
    for l in reversed(range(DEPTH)):
        h0, hn, qa, ka, va, pc, sg, o, lse, cc, sv, h2, hn2, r, z, win = saved[l]
        _, g_out, g_w1, g_w2 = gathered[l]
        da, dh2, gw["mlp_g", l] = mlp_bwd(dh, r, g_w1, g_w2, h2, row(mlp_norm_g[l]), f"mlp_bwd{l}")
        d_w1 = matmul_tn(hn2, da, D, FC, f"dw_mlp1_{l}", out_dtype=MM, shard="n")
        d_w2 = matmul_tn(z, dh, FC, D, f"dw_mlp2_{l}", out_dtype=MM, shard="m")
        do, dcc = out_proj_bwd(dh2, g_out, f"out_proj_bwd{l}")
        d_wout = jnp.concatenate([matmul_tn(o, dh2, DA, D, f"dw_out_a{l}"), matmul_tn(cc, dh2, 2 * DC, D, f"dw_out_c{l}")],
                                 axis=0).reshape(NDEV, D // NDEV, D).astype(MM)
        qb, dob, qbt, dobt = attn_bwd_prep(qa, do, o, lse, f"attn_bwd_prep{l}")
        riding = [("wout", l, d_wout), ("w1", l, d_w1), ("w2", l, d_w2)] + ([("win", l + 1, d_win[l + 1])] if l + 1 < DEPTH else [])
        dq, dk, dv, dcq, dck, *arrived = attn_bwd(qb, dob, qbt, dobt, ka, va, f"attn_bwd{l}",
                                                  AllToAll([a for _, _, a in riding], [True] * len(riding)))
        for (k, kl, _), a in zip(riding, arrived):
            recv[k, kl] = a
        dz, gw["bf", l] = fgate_bwd(dcq, jnp.pad(dck.T, ((0, 0), (0, 128 - H))), sg, f"fgate_bwd{l}")
        gc, act, pg = conv_bwd_pointwise(dcc, sv, pc, row(conf_ln_g[l]), row(conf_ln_b[l]), wpw[l], f"conv_bwd_pw{l}")
        gw["wpw", l] = matmul_tn(act, dcc, DC, DC, f"dw_conf_pw{l}", n=DC)
        dpc, wg = conv_bwd_taps(gc, pc, dcc, sv, wdw[l], wsc[l], f"conv_bwd_taps{l}")
        gw["pg", l], gw["wg", l] = pg, wg
        dproj = jnp.concatenate([dq, dk, dv, dpc, dz], axis=1)
        d_win[l] = _from_padded_cols(matmul_tn(hn, dproj, 512, NP_IN, f"dw_in{l}")).reshape(D, NDEV, n_sh).transpose(
            1, 0, 2).astype(MM)
        dh, gw["mix_g", l] = in_proj_bwd(dproj, win, h0, row(mix_norm_g[l]), dh2, f"in_proj_bwd{l}")

    grad_x = dh[NM:t_real][None]
    stack = lambda k: jnp.stack([gw[k, l] for l in range(DEPTH)])

    d_wdw = stack("wg")[:, 0:CK].reshape(DEPTH, CK, NDEV, DC // NDEV).transpose(2, 0, 1, 3)
    d_wsc = stack("wg")[:, 32:32 + SK].reshape(DEPTH, SK, NDEV, DC // NDEV).transpose(2, 0, 1, 3)
    d_meta = dh[0:NM].reshape(NM, NDEV, D // NDEV).transpose(1, 0, 2)
    d_wpw = stack("wpw").reshape(DEPTH, NDEV, DC // NDEV, DC).transpose(1, 0, 2, 3)
    d_tiny = jnp.stack([_pack([d_wdw[p], d_wsc[p], d_meta[p], d_wpw[p]], TINY_ROWS) for p in range(NDEV)])
    pgs = stack("pg")
    d_rep = _pack([stack("mix_g").reshape(DEPTH, D), stack("bf")[:, 0, :H], pgs[:, 3], pgs[:, 1], pgs[:, 2], pgs[:, 0],
                   stack("mlp_g").reshape(DEPTH, D), d_gf.reshape(D)], REP_ROWS)
    recv["win", 0], r_tiny, r_rep = exchange([d_win[0], d_tiny, d_rep], [True, True, False], "exchange_grads")

    per_layer = lambda k: [recv[k, l] for l in range(DEPTH)]
    res = {}
    res["w_in"] = adamw(per_layer("win"), w_in, m_w_in, v_w_in, 256, "adamw_w_in")
    res["w_out"] = adamw(per_layer("wout"), w_out, m_w_out, v_w_out, D // NDEV, "adamw_w_out")
    res["w_mlp1"] = adamw(per_layer("w1"), w_mlp1, m_w_mlp1, v_w_mlp1, 256, "adamw_w_mlp1")
    res["w_mlp2"] = adamw(per_layer("w2"), w_mlp2, m_w_mlp2, v_w_mlp2, 128, "adamw_w_mlp2")
    tiny_names = ["w_conf_dw", "w_sc_conv", "meta_tokens", "w_conf_pw"]
    tiny_wmv = [[w_conf_dw, w_sc_conv, meta_tokens, w_conf_pw], [m_w_conf_dw, m_w_sc_conv, m_meta_tokens, m_w_conf_pw],
                [v_w_conf_dw, v_w_sc_conv, v_meta_tokens, v_w_conf_pw]]
    rep_names = ["mix_norm_g", "b_forget", "b_conf_dw", "conf_ln_g", "conf_ln_b", "b_conf_pw", "mlp_norm_g", "final_norm_g"]
    rep_wmv = [[mix_norm_g, b_forget, b_conf_dw, conf_ln_g, conf_ln_b, b_conf_pw, mlp_norm_g, final_norm_g],
               [m_mix_norm_g, m_b_forget, m_b_conf_dw, m_conf_ln_g, m_conf_ln_b, m_b_conf_pw, m_mlp_norm_g, m_final_norm_g],
               [v_mix_norm_g, v_b_forget, v_b_conf_dw, v_conf_ln_g, v_conf_ln_b, v_b_conf_pw, v_mlp_norm_g, v_final_norm_g]]
    for names, wmv, shapes, recv_pack, rows_n, nm in ((tiny_names, tiny_wmv, TINY_SHAPES, r_tiny, TINY_ROWS, "adamw_tiny"),
                                                 (rep_names, rep_wmv, REP_SHAPES, r_rep, REP_ROWS, "adamw_rep")):
        packed = [_pack(group, rows_n)[None] for group in wmv]
        outs = adamw([recv_pack], *packed, rows_n, nm)
        parts = [_unpack(a[0], shapes) for a in outs]
        for k, n in enumerate(names):
            res[n] = [parts[q][k] for q in range(4)]

    order = ["meta_tokens", "mix_norm_g", "w_in", "b_forget", "w_conf_dw", "b_conf_dw", "conf_ln_g", "conf_ln_b",
             "w_conf_pw", "b_conf_pw", "w_sc_conv", "w_out", "mlp_norm_g", "w_mlp1", "w_mlp2", "final_norm_g"]
    return (loss, grad_x, *[res[n][0] for n in order], *[res[n][1] for n in order],
            *[res[n][2] for n in order], *[res[n][3] for n in order])
```

```python
import functools

import jax
import jax.numpy as jnp
from jax import lax
from jax.experimental import pallas as pl
from jax.experimental.pallas import tpu as pltpu

F32 = jnp.float32
MM = jnp.bfloat16

D = 1024
H = 8
DH = 64
DA = H * DH
HT = 128
DC = 256
NM = 16
CK = 31
SK = 3
DFF = 4096
DEPTH = 2
N_IN = 3 * DA + H + 2 * DC + 3 * DC
NP_IN = 3 * DA + 5 * DC + 128
C0 = 3 * DA
F0 = 3 * DA + 5 * DC
EPS = 1e-6
TM = 640
HALO = 32
FC = 512
NDEV = 8
SCALE = DH ** -0.5
NEG = -1e30

ADAM_LR, ADAM_B1, ADAM_B2, ADAM_EPS, ADAM_WD, ADAM_STEP = 0.001, 0.9, 0.999, 1e-08, 0.01, 10

VMEM_LIMIT = 56 * 1024 * 1024

S = jax.ShapeDtypeStruct
NT = (((1,), (1,)), ((), ()))
TN = (((0,), (0,)), ((), ()))


def _params(n_grid):
    return pltpu.CompilerParams(dimension_semantics=("arbitrary",) * n_grid, vmem_limit_bytes=VMEM_LIMIT)


def _sigmoid(x):
    return 1.0 / (1.0 + jnp.exp(-x))


def _full(shape):
    n = len(shape)
    return pl.BlockSpec(shape, lambda *_: (0,) * n)


def _lane_put(dst, col, h):
    lane = lax.broadcasted_iota(jnp.int32, dst.shape, 1)
    return jnp.where(lane == h, col, dst)


ANY = pl.BlockSpec(memory_space=pl.ANY)


class AllToAll:
    def __init__(self, srcs, per_peer):
        self.srcs, self.per_peer, self.k_n = list(srcs), list(per_peer), len(srcs)
        self.out_shape = [S((NDEV,) + (a.shape[1:] if pp else a.shape), a.dtype) for a, pp in zip(srcs, per_peer)]
        self.scratch = [pltpu.SemaphoreType.DMA((self.k_n, NDEV - 1)), pltpu.SemaphoreType.DMA((self.k_n, NDEV - 1)),
                        pltpu.SemaphoreType.DMA((self.k_n,))]

    def copies(self, src, out, send_sems, recv_sems, local_sems):
        x, y, c = lax.axis_index("x"), lax.axis_index("y"), lax.axis_index("c")
        me = 4 * x + 2 * y + c

        def piece(k, p):
            return src[k].at[p] if self.per_peer[k] else src[k]

        local = [pltpu.make_async_copy(piece(k, me), out[k].at[me], local_sems.at[k]) for k in range(self.k_n)]
        sends, recvs = [], []
        for r in range(1, NDEV):
            px = 1 - x if (r >> 2) & 1 else x
            py = 1 - y if (r >> 1) & 1 else y
            pc = 1 - c if r & 1 else c
            pidx = 4 * px + 2 * py + pc
            for k in range(self.k_n):
                sends.append(pltpu.make_async_remote_copy(
                    src_ref=piece(k, pidx), dst_ref=out[k].at[me],
                    send_sem=send_sems.at[k, r - 1], recv_sem=recv_sems.at[k, r - 1],
                    device_id=(px, py, pc), device_id_type=pl.DeviceIdType.MESH))
                recvs.append(pltpu.make_async_remote_copy(
                    src_ref=piece(k, pidx), dst_ref=out[k].at[pidx],
                    send_sem=send_sems.at[k, r - 1], recv_sem=recv_sems.at[k, r - 1],
                    device_id=(px, py, pc), device_id_type=pl.DeviceIdType.MESH))

        def start():
            for cp in local + sends:
                cp.start()

        def wait():
            for cp in recvs:
                cp.wait_recv()
            for cp in sends:
                cp.wait_send()
            for cp in local:
                cp.wait()

        return start, wait


def exchange(srcs, per_peer, name):
    plan = AllToAll(srcs, per_peer)

    def body(*refs):
        start, wait = plan.copies(refs[:plan.k_n], refs[plan.k_n:2 * plan.k_n], *refs[2 * plan.k_n:])
        start()
        wait()

    return pl.pallas_call(
        body, name=name, out_shape=plan.out_shape, in_specs=[ANY] * plan.k_n, out_specs=[ANY] * plan.k_n,
        scratch_shapes=plan.scratch,
    )(*srcs)


def gather(srcs, name):
    k_n = len(srcs)
    out_shape = [S((NDEV,) + a.shape, a.dtype) for a in srcs]

    def body(*refs):
        src, out = refs[:k_n], refs[k_n:2 * k_n]
        send_sems, recv_sems, local_sems = refs[2 * k_n:]
        x, y, c = lax.axis_index("x"), lax.axis_index("y"), lax.axis_index("c")
        me, sibling = (x, y, c), (x, y, 1 - c)
        chips = [(1 - x, y), (x, 1 - y), (1 - x, 1 - y)]

        def slot(k, dev):
            return out[k].at[4 * dev[0] + 2 * dev[1] + dev[2]]

        def copy(k, r, block, to, from_src=False):
            return pltpu.make_async_remote_copy(
                src_ref=src[k] if from_src else slot(k, block), dst_ref=slot(k, block),
                send_sem=send_sems.at[k, r], recv_sem=recv_sems.at[k, r],
                device_id=to, device_id_type=pl.DeviceIdType.MESH)

        local = [pltpu.make_async_copy(src[k], slot(k, me), local_sems.at[k]) for k in range(k_n)]
        first = [copy(k, 0, me, sibling, True) for k in range(k_n)]
        first += [copy(k, 1 + n, me, (*chip, c), True) for n, chip in enumerate(chips) for k in range(k_n)]
        for cp in local + first:
            cp.start()
        passed = []
        for n, chip in enumerate(chips):
            for k in range(k_n):
                copy(k, 1 + n, (*chip, c), me).wait_recv()
                passed.append(copy(k, 4 + n, (*chip, c), sibling))
                passed[-1].start()
        for k in range(k_n):
            copy(k, 0, sibling, me).wait_recv()
            for n, chip in enumerate(chips):
                copy(k, 4 + n, (*chip, 1 - c), me).wait_recv()
        for cp in first + passed:
            cp.wait_send()
        for cp in local:
            cp.wait()

    any_spec = pl.BlockSpec(memory_space=pl.ANY)
    return pl.pallas_call(
        body, name=name, out_shape=out_shape,
        in_specs=[any_spec] * k_n, out_specs=[any_spec] * k_n,
        scratch_shapes=[pltpu.SemaphoreType.DMA((k_n, NDEV - 1)), pltpu.SemaphoreType.DMA((k_n, NDEV - 1)),
                        pltpu.SemaphoreType.DMA((k_n,))],
    )(*srcs)


def in_proj(h, g, w, bf, name):
    tp = h.shape[0]

    def body(h_ref, g_ref, w_ref, bf_ref, hn_ref, qa_ref, ka_ref, va_ref, pc_ref, sg_ref, carry):
        i = pl.program_id(0)

        @pl.when(i == 0)
        def _():
            carry[...] = jnp.zeros_like(carry)

        x = h_ref[...]
        r = lax.rsqrt(jnp.mean(x * x, axis=-1, keepdims=True) + EPS)
        hn = (x * r * g_ref[...]).astype(MM)
        hn_ref[...] = hn
        pc_ref[...] = jnp.dot(hn, w_ref[:, C0:F0], preferred_element_type=F32)
        z = jnp.dot(hn, w_ref[:, F0:NP_IN], preferred_element_type=F32) + bf_ref[...]
        lane = lax.broadcasted_iota(jnp.int32, z.shape, 1)
        logf = jnp.where(lane < H, jnp.minimum(z, 0.0) - jnp.log(1.0 + jnp.exp(-jnp.abs(z))), 0.0)
        sg_ref[...] = 1.0 / (1.0 + jnp.exp(z))
        row = lax.broadcasted_iota(jnp.int32, (TM, TM), 0)
        col = lax.broadcasted_iota(jnp.int32, (TM, TM), 1)
        tri = (row >= col).astype(F32)
        c = jnp.dot(tri, logf, precision=lax.Precision.HIGHEST, preferred_element_type=F32) + carry[...]
        carry[...] = c[TM - 1:TM, :]
        qkv = jnp.dot(hn, w_ref[:, 0:C0], preferred_element_type=F32)
        one = [1.0, 1.0, 1.0]
        for hd in range(H):
            t = slice(hd * HT, (hd + 1) * HT)
            cs = list(_split3(c[:, hd:hd + 1]))
            qa_ref[:, t] = _aug_tile(hd, qkv[:, hd * DH:(hd + 1) * DH] * SCALE, cs + one).astype(MM)
            ka_ref[:, t] = _aug_tile(hd, qkv[:, DA + hd * DH:DA + (hd + 1) * DH], one + [-v for v in cs] + one).astype(MM)
            va_ref[:, t] = _aug_tile(hd, qkv[:, 2 * DA + hd * DH:2 * DA + (hd + 1) * DH], [1.0] + one).astype(MM)

    rows = lambda n: pl.BlockSpec((TM, n), lambda i: (i, 0))
    return pl.pallas_call(
        body, name=name, grid=(tp // TM,),
        in_specs=[rows(D), _full((1, D)), _full((D, NP_IN)), _full((1, 128))],
        out_specs=[rows(D), rows(H * HT), rows(H * HT), rows(H * HT), rows(5 * DC), rows(128)],
        out_shape=[S((tp, D), MM), S((tp, H * HT), MM), S((tp, H * HT), MM), S((tp, H * HT), MM),
                   S((tp, 5 * DC), F32), S((tp, 128), F32)],
        scratch_shapes=[pltpu.VMEM((1, 128), F32)],
        compiler_params=_params(1),
    )(h, g, w, bf)


def _split3(c):
    hi = c.astype(MM).astype(F32)
    mid = (c - hi).astype(MM).astype(F32)
    lo = (c - hi - mid).astype(MM).astype(F32)
    return hi, mid, lo


def _main(hd):
    return slice(0, DH) if hd % 2 == 0 else slice(DH, HT)


def _aug_lane(hd, pos):
    return pos + (DH if hd % 2 == 0 else 0)


def _aug_tile(hd, main, cols):
    lane = lax.broadcasted_iota(jnp.int32, main.shape, 1)
    aug = jnp.zeros(main.shape, F32)
    for pos, val in enumerate(cols):
        aug = jnp.where(lane == pos, val, aug)
    return jnp.concatenate([main, aug] if hd % 2 == 0 else [aug, main], axis=1)


def _merge_pairs(tiles):
    lane = lax.broadcasted_iota(jnp.int32, tiles[0].shape, 1)
    return jnp.concatenate([jnp.where(lane < DH, tiles[2 * m], tiles[2 * m + 1]) for m in range(H // 2)], axis=1)


def _causal_mask():
    return lax.broadcasted_iota(jnp.int32, (TM, TM), 0) >= lax.broadcasted_iota(jnp.int32, (TM, TM), 1)


def _split_rider(refs, n_in, n_out, rider):
    if rider is None:
        return refs, None
    k = rider.k_n
    own = refs[:n_in] + refs[n_in + k:n_in + k + n_out] + refs[n_in + 2 * k + n_out:-3]
    return own, rider.copies(refs[n_in:n_in + k], refs[n_in + k + n_out:n_in + 2 * k + n_out], *refs[-3:])


def attn_fwd(qa, ka, va, name, rider=None):
    tp = qa.shape[0]
    nb = tp // TM

    def body(*refs):
        (q_ref, k_ref, v_ref, o_ref, lse_ref, *scr), copies = _split_rider(refs, 3, 2, rider)
        m_scr, acc_scr, bias = scr[:H], scr[H:2 * H], scr[2 * H]
        i, j = pl.program_id(0), pl.program_id(1)

        @pl.when((i == 0) & (j == 0))
        def _():
            if copies:
                copies[0]()
            bias[...] = jnp.where(_causal_mask(), 0.0, NEG)

        @pl.when(j == 0)
        def _():
            for hd in range(H):
                m_scr[hd][...] = jnp.full((TM, 1), NEG, F32)
                acc_scr[hd][...] = jnp.zeros((TM, HT), F32)

        def step(diag):
            def logits(hd):
                t = slice(hd * HT, (hd + 1) * HT)
                s = lax.dot_general(q_ref[:, t], k_ref[:, t], NT, preferred_element_type=F32)
                return s + bias[...] if diag else s

            s_next = logits(0)
            for hd in range(H):
                t = slice(hd * HT, (hd + 1) * HT)
                s = s_next
                if hd + 1 < H:
                    s_next = logits(hd + 1)
                m_prev = m_scr[hd][...]
                m_new = jnp.maximum(m_prev, jnp.max(s, axis=1, keepdims=True))
                p = jnp.exp(s - m_new).astype(MM)
                acc_scr[hd][...] = (jnp.exp(m_prev - m_new) * acc_scr[hd][...]
                                    + jnp.dot(p, v_ref[:, t], preferred_element_type=F32))
                m_scr[hd][...] = m_new

        @pl.when(j < i)
        def _():
            step(False)

        @pl.when(j == i)
        def _():
            step(True)

        @pl.when(j == i)
        def _():
            lse = jnp.zeros((TM, 128), F32)
            outs = []
            for hd in range(H):
                acc = acc_scr[hd][...]
                l = acc[:, _aug_lane(hd, 0):_aug_lane(hd, 0) + 1]
                outs.append(acc * (1.0 / l))
                lse = _lane_put(lse, m_scr[hd][...] + jnp.log(l), hd)
            o_ref[...] = _merge_pairs(outs).astype(MM)
            lse_ref[...] = lse

        if copies:
            @pl.when((i == nb - 1) & (j == nb - 1))
            def _():
                copies[1]()

    kv = pl.BlockSpec((TM, H * HT), lambda i, j: (jnp.minimum(j, i), 0))
    r_n = rider.k_n if rider else 0
    return pl.pallas_call(
        body, name=name, grid=(nb, nb),
        in_specs=[pl.BlockSpec((TM, H * HT), lambda i, j: (i, 0)), kv, kv] + [ANY] * r_n,
        out_specs=[pl.BlockSpec((TM, DA), lambda i, j: (i, 0)), pl.BlockSpec((TM, 128), lambda i, j: (i, 0))] + [ANY] * r_n,
        out_shape=[S((tp, DA), MM), S((tp, 128), F32)] + (rider.out_shape if rider else []),
        scratch_shapes=[pltpu.VMEM((TM, 1), F32)] * H + [pltpu.VMEM((TM, HT), F32)] * H + [pltpu.VMEM((TM, TM), F32)]
        + (rider.scratch if rider else []),
        compiler_params=_params(2),
    )(qa, ka, va, *(rider.srcs if rider else []))


def _layernorm_parts(dw, g, b):
    mu = jnp.mean(dw, axis=-1, keepdims=True)
    xc = dw - mu
    rs = lax.rsqrt(jnp.mean(xc * xc, axis=-1, keepdims=True) + EPS)
    xh = xc * rs
    return xh, rs, xh * g + b


def _fill_halo(ext, halo, cur, first):
    ext[0:HALO, :] = jnp.where(first, 0.0, halo)
    ext[HALO:HALO + TM, :] = cur


def _shift_copies(dst, src, lanes=slice(None)):
    for r in range(1, 8):
        dst[r - 1] = src[pl.ds(r, dst.shape[1]), lanes]


def _window(src, shifted, off, lanes=slice(None)):
    if off % 8 == 0:
        return src[pl.ds(off, TM), lanes]
    return shifted[off % 8 - 1, pl.ds(off - off % 8, TM), :]


def conv_fwd(pc, wdw, bdw, lng, lnb, wpw, bpw, wsc, name):
    tp = pc.shape[0]

    def body(pc_ref, hl_ref, wdw_ref, bdw_ref, lng_ref, lnb_ref, wpw_ref, bpw_ref, wsc_ref, cc_ref, sv_ref, xe, ce, xs):
        first = pl.program_id(0) == 0
        glu = pc_ref[:, 0:DC] * _sigmoid(pc_ref[:, DC:2 * DC])
        cu = pc_ref[:, 3 * DC:4 * DC] * pc_ref[:, 4 * DC:5 * DC]
        _fill_halo(xe, hl_ref[:, 0:DC] * _sigmoid(hl_ref[:, DC:2 * DC]), glu, first)
        _fill_halo(ce, hl_ref[:, 3 * DC:4 * DC] * hl_ref[:, 4 * DC:5 * DC], cu, first)
        _shift_copies(xs, xe)
        dw = jnp.zeros((TM, DC), F32) + bdw_ref[...]
        for k in range(CK):
            dw = dw + wdw_ref[k:k + 1, :] * _window(xe, xs, HALO - (CK - 1) + k)
        cv = jnp.zeros((TM, DC), F32)
        for k in range(SK):
            cv = cv + wsc_ref[k:k + 1, :] * ce[pl.ds(HALO - (SK - 1) + k, TM), :]
        _, _, ln = _layernorm_parts(dw, lng_ref[...], lnb_ref[...])
        act = ln * _sigmoid(ln)
        conf = jnp.dot(act.astype(MM), wpw_ref[...], preferred_element_type=F32) + bpw_ref[...]
        cc_ref[:, 0:DC] = conf.astype(MM)
        cc_ref[:, DC:2 * DC] = (pc_ref[:, 2 * DC:3 * DC] * cv).astype(MM)
        sv_ref[:, 0:DC] = dw
        sv_ref[:, DC:2 * DC] = cv

    hb = TM // HALO
    rows = lambda n: pl.BlockSpec((TM, n), lambda i: (i, 0))
    return pl.pallas_call(
        body, name=name, grid=(tp // TM,),
        in_specs=[rows(5 * DC), pl.BlockSpec((HALO, 5 * DC), lambda i: (jnp.maximum(i * hb - 1, 0), 0)),
                  _full((32, DC)), _full((1, DC)), _full((1, DC)), _full((1, DC)), _full((DC, DC)), _full((1, DC)),
                  _full((8, DC))],
        out_specs=[rows(2 * DC), rows(2 * DC)],
        out_shape=[S((tp, 2 * DC), MM), S((tp, 2 * DC), F32)],
        scratch_shapes=[pltpu.VMEM((HALO + TM, DC), F32), pltpu.VMEM((HALO + TM, DC), F32),
                        pltpu.VMEM((7, TM + HALO - 8, DC), F32)],
        compiler_params=_params(1),
    )(pc, pc, wdw, bdw, lng, lnb, wpw, bpw, wsc)


def out_proj(h, o, cc, wo, g2, name):
    tp = h.shape[0]

    def body(h_ref, o_ref, cc_ref, wo_ref, g_ref, h2_ref, hn_ref):
        wo = wo_ref[...].reshape(D, D)
        h2 = (h_ref[...] + jnp.dot(o_ref[...], wo[0:DA, :], preferred_element_type=F32)
              + jnp.dot(cc_ref[...], wo[DA:D, :], preferred_element_type=F32))
        h2_ref[...] = h2
        r = lax.rsqrt(jnp.mean(h2 * h2, axis=-1, keepdims=True) + EPS)
        hn_ref[...] = (h2 * r * g_ref[...]).astype(MM)

    rows = lambda n: pl.BlockSpec((TM, n), lambda i: (i, 0))
    return pl.pallas_call(
        body, name=name, grid=(tp // TM,),
        in_specs=[rows(D), rows(DA), rows(2 * DC), _full((NDEV, D // NDEV, D)), _full((1, D))],
        out_specs=[rows(D), rows(D)],
        out_shape=[S((tp, D), F32), S((tp, D), MM)],
        compiler_params=_params(1),
    )(h, o, cc, wo, g2)


def mlp_fwd(hn, h2, w1, w2, name):
    tp = hn.shape[0]
    nf = DFF // FC

    def body(hn_ref, h2_ref, w1_ref, w2_ref, r_ref, z_ref, h3_ref, acc):
        j = pl.program_id(1)

        @pl.when(j == 0)
        def _():
            acc[...] = jnp.zeros_like(acc)

        r = jnp.maximum(jnp.dot(hn_ref[...], w1_ref[...], preferred_element_type=F32), 0.0)
        zb = (r * r).astype(MM)
        r_ref[...] = r.astype(MM)
        z_ref[...] = zb
        acc[...] += jnp.dot(zb, w2_ref[...], preferred_element_type=F32)

        @pl.when(j == nf - 1)
        def _():
            h3_ref[...] = h2_ref[...] + acc[...]

    return pl.pallas_call(
        body, name=name, grid=(tp // TM, nf),
        in_specs=[pl.BlockSpec((TM, D), lambda i, j: (i, 0)), pl.BlockSpec((TM, D), lambda i, j: (i, 0)),
                  pl.BlockSpec((None, D, FC), lambda i, j: (j, 0, 0)), pl.BlockSpec((None, FC, D), lambda i, j: (j, 0, 0))],
        out_specs=[pl.BlockSpec((TM, FC), lambda i, j: (i, j)), pl.BlockSpec((TM, FC), lambda i, j: (i, j)),
                   pl.BlockSpec((TM, D), lambda i, j: (i, 0))],
        out_shape=[S((tp, DFF), MM), S((tp, DFF), MM), S((tp, D), F32)],
        scratch_shapes=[pltpu.VMEM((TM, D), F32)],
        compiler_params=_params(2),
    )(hn, h2, w1, w2)


def loss_head(h, tgt, g, t_real, name):
    tp = h.shape[0]

    def body(h_ref, t_ref, g_ref, dh_ref, loss_ref, dg_ref):
        i = pl.program_id(0)

        @pl.when(i == 0)
        def _():
            loss_ref[...] = jnp.zeros_like(loss_ref)
            dg_ref[...] = jnp.zeros_like(dg_ref)

        x = h_ref[...]
        gg = g_ref[...]
        r = lax.rsqrt(jnp.mean(x * x, axis=-1, keepdims=True) + EPS)
        xn = x * r
        row = i * TM + lax.broadcasted_iota(jnp.int32, (TM, 1), 0)
        e = jnp.where((row >= NM) & (row < t_real), xn * gg - t_ref[...], 0.0)
        loss_ref[...] += jnp.sum(e * e) * (0.5 / D)
        dy = e * (1.0 / D)
        dg_ref[...] += jnp.sum(dy * xn, axis=0, keepdims=True)
        u = dy * gg
        dh_ref[...] = r * (u - xn * jnp.mean(u * xn, axis=-1, keepdims=True))

    rows = lambda n: pl.BlockSpec((TM, n), lambda i: (i, 0))
    return pl.pallas_call(
        body, name=name, grid=(tp // TM,),
        in_specs=[rows(D), rows(D), _full((1, D))],
        out_specs=[rows(D), _full((8, 128)), _full((1, D))],
        out_shape=[S((tp, D), F32), S((8, 128), F32), S((1, D), F32)],
        compiler_params=_params(1),
    )(h, tgt, g)


def _rmsnorm_bwd(x, g, dy):
    r = lax.rsqrt(jnp.mean(x * x, axis=-1, keepdims=True) + EPS)
    xn = x * r
    u = dy * g
    dx = r * (u - xn * jnp.mean(u * xn, axis=-1, keepdims=True))
    return dx, jnp.sum(dy * xn, axis=0, keepdims=True)


def matmul_tn(a, b, tm, tn, name, n=None, out_dtype=F32, shard=None):
    tp, m = a.shape
    n = b.shape[1] if n is None else n
    nk = tp // TM

    def body(a_ref, b_ref, o_ref, acc):
        k = pl.program_id(2)

        @pl.when(k == 0)
        def _():
            acc[...] = jnp.zeros_like(acc)

        acc[...] += lax.dot_general(a_ref[...].astype(MM), b_ref[...].astype(MM), TN, preferred_element_type=F32)

        @pl.when(k == nk - 1)
        def _():
            o_ref[...] = acc[...].astype(out_dtype)

    if shard is None:
        out_spec, out_shape = pl.BlockSpec((tm, tn), lambda mi, ni, k: (mi, ni)), (m, n)
    elif shard == "m":
        out_spec, out_shape = pl.BlockSpec((None, tm, tn), lambda mi, ni, k: (mi, 0, ni)), (m // tm, tm, n)
    else:
        out_spec, out_shape = pl.BlockSpec((None, tm, tn), lambda mi, ni, k: (ni, mi, 0)), (n // tn, m, tn)
    return pl.pallas_call(
        body, name=name, grid=(m // tm, n // tn, nk),
        in_specs=[pl.BlockSpec((TM, tm), lambda mi, ni, k: (k, mi)), pl.BlockSpec((TM, tn), lambda mi, ni, k: (k, ni))],
        out_specs=out_spec, out_shape=S(out_shape, out_dtype),
        scratch_shapes=[pltpu.VMEM((tm, tn), F32)],
        compiler_params=_params(3),
    )(a, b)


def mlp_bwd(dh3, r, w1, w2, h2, g, name):
    tp = dh3.shape[0]
    nf = DFF // FC

    def body(dh3_ref, r_ref, w1_ref, w2_ref, h2_ref, g_ref, da_ref, dh2_ref, dg_ref, acc, dhb):
        i, j = pl.program_id(0), pl.program_id(1)

        @pl.when((i == 0) & (j == 0))
        def _():
            dg_ref[...] = jnp.zeros_like(dg_ref)

        @pl.when(j == 0)
        def _():
            acc[...] = jnp.zeros_like(acc)
            dhb[...] = dh3_ref[...].astype(MM)

        dz = lax.dot_general(dhb[...], w2_ref[...], NT, preferred_element_type=F32)
        da = (dz * (2.0 * r_ref[...].astype(F32))).astype(MM)
        da_ref[...] = da
        acc[...] += lax.dot_general(da, w1_ref[...], NT, preferred_element_type=F32)

        @pl.when(j == nf - 1)
        def _():
            dx, dg = _rmsnorm_bwd(h2_ref[...], g_ref[...], acc[...])
            dh2_ref[...] = dh3_ref[...] + dx
            dg_ref[...] += dg

    return pl.pallas_call(
        body, name=name, grid=(tp // TM, nf),
        in_specs=[pl.BlockSpec((TM, D), lambda i, j: (i, 0)), pl.BlockSpec((TM, FC), lambda i, j: (i, j)),
                  pl.BlockSpec((None, D, FC), lambda i, j: (j, 0, 0)), pl.BlockSpec((None, FC, D), lambda i, j: (j, 0, 0)),
                  pl.BlockSpec((TM, D), lambda i, j: (i, 0)), _full((1, D))],
        out_specs=[pl.BlockSpec((TM, FC), lambda i, j: (i, j)), pl.BlockSpec((TM, D), lambda i, j: (i, 0)),
                   _full((1, D))],
        out_shape=[S((tp, DFF), MM), S((tp, D), F32), S((1, D), F32)],
        scratch_shapes=[pltpu.VMEM((TM, D), F32), pltpu.VMEM((TM, D), MM)],
        compiler_params=_params(2),
    )(dh3, r, w1, w2, h2, g)


def out_proj_bwd(dh2, wo, name):
    tp = dh2.shape[0]

    def body(dh_ref, wo_ref, do_ref, dcc_ref):
        dcat = lax.dot_general(dh_ref[...].astype(MM), wo_ref[...].reshape(D, D), NT, preferred_element_type=F32)
        do_ref[...] = dcat[:, 0:DA].astype(MM)
        dcc_ref[...] = dcat[:, DA:D]

    rows = lambda n: pl.BlockSpec((TM, n), lambda i: (i, 0))
    return pl.pallas_call(
        body, name=name, grid=(tp // TM,),
        in_specs=[rows(D), _full((NDEV, D // NDEV, D))],
        out_specs=[rows(DA), rows(2 * DC)],
        out_shape=[S((tp, DA), MM), S((tp, 2 * DC), F32)],
        compiler_params=_params(1),
    )(dh2, wo)


def attn_bwd_prep(qa, do, o, lse, name):
    tp = qa.shape[0]

    def body(q_ref, do_ref, o_ref, lse_ref, qb_ref, dob_ref, qbt_ref, dobt_ref):
        lse = lse_ref[...]
        lane = lax.broadcasted_iota(jnp.int32, (TM, HT), 1)
        for hd in range(H):
            t, hs = slice(hd * HT, (hd + 1) * HT), slice(hd * DH, (hd + 1) * DH)
            dof = do_ref[:, hs].astype(F32)
            dd = jnp.sum(dof * o_ref[:, hs].astype(F32), axis=1, keepdims=True)
            dob = _aug_tile(hd, dof, [0.0] + [-v for v in _split3(dd)])
            qb = q_ref[:, t].astype(F32)
            for pos, v in enumerate(_split3(lse[:, hd:hd + 1])):
                qb = jnp.where(lane == _aug_lane(hd, 6 + pos), -v, qb)
            qb_ref[:, t] = qb.astype(MM)
            dob_ref[:, t] = dob.astype(MM)
            qbt_ref[t, :] = qb.T.astype(MM)
            dobt_ref[t, :] = dob.T.astype(MM)

    rows = lambda n: pl.BlockSpec((TM, n), lambda i: (i, 0))
    cols = pl.BlockSpec((H * HT, TM), lambda i: (0, i))
    return pl.pallas_call(
        body, name=name, grid=(tp // TM,),
        in_specs=[rows(H * HT), rows(DA), rows(DA), rows(128)],
        out_specs=[rows(H * HT), rows(H * HT), cols, cols],
        out_shape=[S((tp, H * HT), MM), S((tp, H * HT), MM), S((H * HT, tp), MM), S((H * HT, tp), MM)],
        compiler_params=_params(1),
    )(qa, do, o, lse)


def attn_bwd(qb, dob, qbt, dobt, ka, va, name, rider=None):
    tp = qb.shape[0]
    nb = tp // TM

    def body(*refs):
        own, copies = _split_rider(refs, 6, 5, rider)
        (qb_ref, dob_ref, qbt_ref, dobt_ref, k_ref, v_ref, dq_ref, dk_ref, dv_ref, dcq_ref, dck_ref,
         dq_acc, dcq_acc, dkt_acc, dvt_acc, dck_acc, stage, sems) = own
        j, i = pl.program_id(0), pl.program_id(1)

        @pl.when((j == 0) & (i == 0))
        def _():
            if copies:
                copies[0]()
            dq_acc[...] = jnp.zeros_like(dq_acc)
            dcq_acc[...] = jnp.zeros_like(dcq_acc)

        @pl.when(i == 0)
        def _():
            dkt_acc[...] = jnp.zeros_like(dkt_acc)
            dvt_acc[...] = jnp.zeros_like(dvt_acc)
            dck_acc[...] = jnp.zeros_like(dck_acc)

        def step(diag):
            rows = pl.ds(pl.multiple_of(i * TM, TM), TM)
            dcq = dcq_acc[rows, :]
            lane = lax.broadcasted_iota(jnp.int32, (TM, HT), 1)
            mask = _causal_mask() if diag else None
            for hd in range(H):
                t = slice(hd * HT, (hd + 1) * HT)
                s = lax.dot_general(qb_ref[:, t], k_ref[:, t], NT, preferred_element_type=F32)
                dp = lax.dot_general(dob_ref[:, t], v_ref[:, t], NT, preferred_element_type=F32)
                if diag:
                    s = jnp.where(mask, s, NEG)
                p = jnp.exp(s)
                ds = p * dp
                dsb = ds.astype(MM)
                dvt_acc[t, :] += jnp.dot(dobt_ref[t, :], p.astype(MM), preferred_element_type=F32)
                dkt_acc[t, :] += jnp.dot(qbt_ref[t, :], dsb, preferred_element_type=F32)
                dqh = jnp.dot(dsb, k_ref[:, t], preferred_element_type=F32)
                if hd % 2 == 0:
                    dq_even = dqh
                else:
                    pair = slice((hd // 2) * HT, (hd // 2 + 1) * HT)
                    dq_acc[rows, pair] += jnp.where(lane < DH, dq_even, dqh)
                dcq = dcq + jnp.where(lane == hd, jnp.sum(ds, axis=1, keepdims=True), 0.0)
                dck_acc[hd:hd + 1, :] += jnp.sum(ds, axis=0, keepdims=True)
            dcq_acc[rows, :] = dcq
            return rows

        @pl.when(i > j)
        def _():
            step(False)

        @pl.when(i == j)
        def _():
            rows = step(True)
            stage[...] = (dq_acc[rows, :] * SCALE).astype(MM)
            out = [pltpu.make_async_copy(stage, dq_ref.at[rows, :], sems.at[0]),
                   pltpu.make_async_copy(dcq_acc.at[rows, :], dcq_ref.at[rows, :], sems.at[1])]
            for cp in out:
                cp.start()
            for cp in out:
                cp.wait()

        @pl.when(i == nb - 1)
        def _():
            dk_ref[...] = _merge_pairs([dkt_acc[hd * HT:(hd + 1) * HT, :].T for hd in range(H)]).astype(MM)
            dv_ref[...] = _merge_pairs([dvt_acc[hd * HT:(hd + 1) * HT, :].T for hd in range(H)]).astype(MM)
            dck_ref[...] = dck_acc[...]

        if copies:
            @pl.when((j == nb - 1) & (i == nb - 1))
            def _():
                copies[1]()

    qi = lambda n: pl.BlockSpec((TM, n), lambda j, i: (jnp.maximum(i, j), 0))
    qt = pl.BlockSpec((H * HT, TM), lambda j, i: (0, jnp.maximum(i, j)))
    kj = lambda n: pl.BlockSpec((TM, n), lambda j, i: (j, 0))
    r_n = rider.k_n if rider else 0
    return pl.pallas_call(
        body, name=name, grid=(nb, nb),
        in_specs=[qi(H * HT), qi(H * HT), qt, qt, kj(H * HT), kj(H * HT)] + [ANY] * r_n,
        out_specs=[ANY, kj(DA), kj(DA), ANY, pl.BlockSpec((H, TM), lambda j, i: (0, j))] + [ANY] * r_n,
        out_shape=[S((tp, DA), MM), S((tp, DA), MM), S((tp, DA), MM), S((tp, 128), F32), S((H, tp), F32)]
        + (rider.out_shape if rider else []),
        scratch_shapes=[pltpu.VMEM((tp, DA), F32), pltpu.VMEM((tp, 128), F32), pltpu.VMEM((H * HT, TM), F32),
                        pltpu.VMEM((H * HT, TM), F32), pltpu.VMEM((H, TM), F32), pltpu.VMEM((TM, DA), MM),
                        pltpu.SemaphoreType.DMA((2,))] + (rider.scratch if rider else []),
        compiler_params=_params(2),
    )(qb, dob, qbt, dobt, ka, va, *(rider.srcs if rider else []))


def fgate_bwd(dcq, dck, sg, name):
    tp = dcq.shape[0]
    nb = tp // TM

    def body(dcq_ref, dck_ref, sg_ref, dz_ref, db_ref, carry):
        @pl.when(pl.program_id(0) == 0)
        def _():
            carry[...] = jnp.zeros_like(carry)
            db_ref[...] = jnp.zeros_like(db_ref)

        row = lax.broadcasted_iota(jnp.int32, (TM, TM), 0)
        col = lax.broadcasted_iota(jnp.int32, (TM, TM), 1)
        tri = (row <= col).astype(F32)
        dl = jnp.dot(tri, dcq_ref[...] - dck_ref[...], precision=lax.Precision.HIGHEST,
                     preferred_element_type=F32) + carry[...]
        carry[...] = dl[0:1, :]
        dz = dl * sg_ref[...]
        dz_ref[...] = dz.astype(MM)
        db_ref[...] += jnp.sum(dz, axis=0, keepdims=True)

    rev = pl.BlockSpec((TM, 128), lambda i: (nb - 1 - i, 0))
    return pl.pallas_call(
        body, name=name, grid=(nb,),
        in_specs=[rev, rev, rev], out_specs=[rev, _full((1, 128))],
        out_shape=[S((tp, 128), MM), S((1, 128), F32)],
        scratch_shapes=[pltpu.VMEM((1, 128), F32)],
        compiler_params=_params(1),
    )(dcq, dck, sg)


def conv_bwd_pointwise(dcc, sv, pc, lng, lnb, wpw, name):
    tp = dcc.shape[0]

    def body(dcc_ref, sv_ref, b_ref, lng_ref, lnb_ref, wpw_ref, gc_ref, act_ref, pg_ref):
        @pl.when(pl.program_id(0) == 0)
        def _():
            pg_ref[...] = jnp.zeros_like(pg_ref)

        dconf = dcc_ref[:, 0:DC]
        g = lng_ref[...]
        xh, rs, ln = _layernorm_parts(sv_ref[:, 0:DC], g, lnb_ref[...])
        sig = _sigmoid(ln)
        act_ref[...] = (ln * sig).astype(MM)
        dact = lax.dot_general(dconf.astype(MM), wpw_ref[...], NT, preferred_element_type=F32)
        dln = dact * (sig * (1.0 + ln * (1.0 - sig)))
        dxh = dln * g
        ddw = rs * (dxh - jnp.mean(dxh, axis=-1, keepdims=True) - xh * jnp.mean(dxh * xh, axis=-1, keepdims=True))
        gc_ref[:, 0:DC] = ddw
        gc_ref[:, DC:2 * DC] = dcc_ref[:, DC:2 * DC] * b_ref[...]
        cs = lambda t: jnp.sum(t, axis=0, keepdims=True)
        pg_ref[0:1, :] += cs(dconf)
        pg_ref[1:2, :] += cs(dln * xh)
        pg_ref[2:3, :] += cs(dln)
        pg_ref[3:4, :] += cs(ddw)

    rows = lambda n: pl.BlockSpec((TM, n), lambda i: (i, 0))
    return pl.pallas_call(
        body, name=name, grid=(tp // TM,),
        in_specs=[rows(2 * DC), rows(2 * DC), pl.BlockSpec((TM, DC), lambda i: (i, 2)),
                  _full((1, DC)), _full((1, DC)), _full((DC, DC))],
        out_specs=[rows(2 * DC), rows(DC), _full((8, DC))],
        out_shape=[S((tp, 2 * DC), F32), S((tp, DC), MM), S((8, DC), F32)],
        compiler_params=_params(1),
    )(dcc, sv, pc, lng, lnb, wpw)


def conv_bwd_taps(gc, pc, dcc, sv, wdw, wsc, name):
    tp = gc.shape[0]
    nb = tp // TM
    hb = TM // HALO

    def body(gc_ref, gn_ref, pc_ref, hl_ref, dcc_ref, sv_ref, wdw_ref, wsc_ref, dpc_ref, wg_ref, ge, xe, ce, gs, xs):
        i = pl.program_id(0)

        @pl.when(i == 0)
        def _():
            wg_ref[...] = jnp.zeros_like(wg_ref)

        a, gt = pc_ref[:, 0:DC], pc_ref[:, DC:2 * DC]
        c, u = pc_ref[:, 3 * DC:4 * DC], pc_ref[:, 4 * DC:5 * DC]
        sig = _sigmoid(gt)
        _fill_halo(xe, hl_ref[:, 0:DC] * _sigmoid(hl_ref[:, DC:2 * DC]), a * sig, i == 0)
        _fill_halo(ce, hl_ref[:, 3 * DC:4 * DC] * hl_ref[:, 4 * DC:5 * DC], c * u, i == 0)
        ge[0:TM, :] = gc_ref[...]
        ge[TM:TM + HALO, :] = jnp.where(i == nb - 1, 0.0, gn_ref[...])
        _shift_copies(gs, ge, slice(0, DC))
        _shift_copies(xs, xe)
        ddw, dcv = gc_ref[:, 0:DC], gc_ref[:, DC:2 * DC]
        dglu = jnp.zeros((TM, DC), F32)
        for k in range(CK):
            dglu = dglu + wdw_ref[k:k + 1, :] * _window(ge, gs, CK - 1 - k, slice(0, DC))
            wg_ref[k:k + 1, :] += jnp.sum(ddw * _window(xe, xs, HALO - (CK - 1) + k), axis=0, keepdims=True)
        dcu = jnp.zeros((TM, DC), F32)
        for k in range(SK):
            dcu = dcu + wsc_ref[k:k + 1, :] * ge[pl.ds(SK - 1 - k, TM), DC:2 * DC]
            wg_ref[32 + k:33 + k, :] += jnp.sum(dcv * ce[pl.ds(HALO - (SK - 1) + k, TM), :], axis=0, keepdims=True)
        dpc_ref[:, 0:DC] = (dglu * sig).astype(MM)
        dpc_ref[:, DC:2 * DC] = (dglu * a * sig * (1.0 - sig)).astype(MM)
        dpc_ref[:, 2 * DC:3 * DC] = (dcc_ref[:, DC:2 * DC] * sv_ref[:, DC:2 * DC]).astype(MM)
        dpc_ref[:, 3 * DC:4 * DC] = (dcu * u).astype(MM)
        dpc_ref[:, 4 * DC:5 * DC] = (dcu * c).astype(MM)

    rows = lambda n: pl.BlockSpec((TM, n), lambda i: (i, 0))
    return pl.pallas_call(
        body, name=name, grid=(nb,),
        in_specs=[rows(2 * DC), pl.BlockSpec((HALO, 2 * DC), lambda i: (jnp.minimum((i + 1) * hb, nb * hb - 1), 0)),
                  rows(5 * DC), pl.BlockSpec((HALO, 5 * DC), lambda i: (jnp.maximum(i * hb - 1, 0), 0)),
                  rows(2 * DC), rows(2 * DC), _full((32, DC)), _full((8, DC))],
        out_specs=[rows(5 * DC), _full((40, DC))],
        out_shape=[S((tp, 5 * DC), MM), S((40, DC), F32)],
        scratch_shapes=[pltpu.VMEM((TM + HALO, 2 * DC), F32), pltpu.VMEM((HALO + TM, DC), F32),
                        pltpu.VMEM((HALO + TM, DC), F32), pltpu.VMEM((7, TM + HALO - 8, DC), F32),
                        pltpu.VMEM((7, TM + HALO - 8, DC), F32)],
        compiler_params=_params(1),
    )(gc, gc, pc, pc, dcc, sv, wdw, wsc)


def in_proj_bwd(dproj, w, h, g, dh_in, name):
    tp = h.shape[0]

    def body(dp_ref, w_ref, h_ref, g_ref, di_ref, dh_ref, dg_ref):
        @pl.when(pl.program_id(0) == 0)
        def _():
            dg_ref[...] = jnp.zeros_like(dg_ref)

        dhn = lax.dot_general(dp_ref[...], w_ref[...], NT, preferred_element_type=F32)
        dx, dg = _rmsnorm_bwd(h_ref[...], g_ref[...], dhn)
        dh_ref[...] = di_ref[...] + dx
        dg_ref[...] += dg

    rows = lambda n: pl.BlockSpec((TM, n), lambda i: (i, 0))
    return pl.pallas_call(
        body, name=name, grid=(tp // TM,),
        in_specs=[rows(NP_IN), _full((D, NP_IN)), rows(D), _full((1, D)), rows(D)],
        out_specs=[rows(D), _full((1, D))],
        out_shape=[S((tp, D), F32), S((1, D), F32)],
        compiler_params=_params(1),
    )(dproj, w, h, g, dh_in)


def adamw(recvs, w, m, v, rb, name):
    l_n, r_n, c_n = w.shape

    def body(*refs):
        p_refs = refs[:l_n]
        w_ref, m_ref, v_ref, g_ref, d_ref, m2_ref, v2_ref = refs[l_n:]
        for l in range(l_n):
            g = p_refs[l][0].astype(F32)
            for s in range(1, NDEV):
                g = g + p_refs[l][s].astype(F32)
            m2 = ADAM_B1 * m_ref[l] + (1.0 - ADAM_B1) * g
            v2 = ADAM_B2 * v_ref[l] + (1.0 - ADAM_B2) * (g * g)
            m_hat = m2 / (1.0 - ADAM_B1 ** ADAM_STEP)
            v_hat = v2 / (1.0 - ADAM_B2 ** ADAM_STEP)
            g_ref[l] = g
            d_ref[l] = -ADAM_LR * (m_hat / (jnp.sqrt(v_hat) + ADAM_EPS) + ADAM_WD * w_ref[l])
            m2_ref[l] = m2
            v2_ref[l] = v2

    blk = pl.BlockSpec((l_n, rb, c_n), lambda r: (0, r, 0))
    return pl.pallas_call(
        body, name=name, grid=(r_n // rb,),
        in_specs=[pl.BlockSpec((NDEV, rb, c_n), lambda r: (0, r, 0))] * l_n + [blk, blk, blk],
        out_specs=[blk] * 4, out_shape=[S(w.shape, F32)] * 4,
        compiler_params=_params(1),
    )(*recvs, w, m, v)


TINY_ROWS = 168
REP_ROWS = 64


def _pack(parts, rows):
    flat = jnp.concatenate([p.reshape(-1) for p in parts])
    return jnp.pad(flat, (0, rows * 128 - flat.shape[0])).reshape(rows, 128)


def _unpack(buf, shapes):
    flat, out, o = buf.reshape(-1), [], 0
    for s in shapes:
        n = 1
        for d in s:
            n *= d
        out.append(flat[o:o + n].reshape(s))
        o += n
    return out


TINY_SHAPES = [(DEPTH, CK, DC // NDEV), (DEPTH, SK, DC // NDEV), (NM, D // NDEV), (DEPTH, DC // NDEV, DC)]
REP_SHAPES = [(DEPTH, D), (DEPTH, H), (DEPTH, DC), (DEPTH, DC), (DEPTH, DC), (DEPTH, DC), (DEPTH, D), (D,)]


def _to_padded_cols(w):
    pad = jnp.zeros(w.shape[:-1] + (NP_IN - N_IN,), w.dtype)
    return jnp.concatenate([w[..., :C0], w[..., C0 + H:], w[..., C0:C0 + H], pad], axis=-1)


def _from_padded_cols(w):
    return jnp.concatenate([w[..., :C0], w[..., F0:F0 + H], w[..., C0:F0]], axis=-1)


def kernel(x, meta_tokens, mix_norm_g, w_in, b_forget, w_conf_dw, b_conf_dw, conf_ln_g, conf_ln_b, w_conf_pw, b_conf_pw, w_sc_conv, w_out, mlp_norm_g, w_mlp1, w_mlp2, final_norm_g, loss_target, m_meta_tokens, m_mix_norm_g, m_w_in, m_b_forget, m_w_conf_dw, m_b_conf_dw, m_conf_ln_g, m_conf_ln_b, m_w_conf_pw, m_b_conf_pw, m_w_sc_conv, m_w_out, m_mlp_norm_g, m_w_mlp1, m_w_mlp2, m_final_norm_g, v_meta_tokens, v_mix_norm_g, v_w_in, v_b_forget, v_w_conf_dw, v_b_conf_dw, v_conf_ln_g, v_conf_ln_b, v_w_conf_pw, v_b_conf_pw, v_w_sc_conv, v_w_out, v_mlp_norm_g, v_w_mlp1, v_w_mlp2, v_final_norm_g):
    seq = x.shape[1]
    t_real = NM + seq
    tp = -(-t_real // TM) * TM

    tiny_w = _pack([w_conf_dw, w_sc_conv, meta_tokens, w_conf_pw], TINY_ROWS)
    big = lambda l: [w_in[l].astype(MM), w_out[l].astype(MM), w_mlp1[l].astype(MM), w_mlp2[l].astype(MM)]
    *first, g_tiny = gather(big(0) + [tiny_w], "gather_weights0")
    gathered = {0: first}
    n_sh = w_in.shape[-1]
    assert w_mlp1.shape[-1] == FC and w_mlp2.shape[-2] == FC
    tiny = [_unpack(g_tiny[s], TINY_SHAPES) for s in range(NDEV)]
    wdw = jnp.concatenate([t[0] for t in tiny], axis=-1)
    wsc = jnp.concatenate([t[1] for t in tiny], axis=-1)
    meta = jnp.concatenate([t[2] for t in tiny], axis=-1)
    wpw = jnp.concatenate([t[3] for t in tiny], axis=1).astype(MM)
    wdw = jnp.pad(wdw, ((0, 0), (0, 32 - CK), (0, 0)))
    wsc = jnp.pad(wsc, ((0, 0), (0, 8 - SK), (0, 0)))
    bfp = jnp.pad(b_forget, ((0, 0), (0, 128 - H)))

    row = lambda a: a.reshape(1, -1)

    h = jnp.concatenate([meta, x[0], jnp.zeros((tp - t_real, D), F32)], axis=0)
    tgt = jnp.pad(loss_target[0], ((NM, tp - t_real), (0, 0)))
    saved = []
    for l in range(DEPTH):
        g_in, g_out, g_w1, g_w2 = gathered[l]
        win = _to_padded_cols(g_in.transpose(1, 0, 2).reshape(D, NDEV * n_sh))
        hn, qa, ka, va, pc, sg = in_proj(h, row(mix_norm_g[l]), win, row(bfp[l]), f"in_proj{l}")
        rider = AllToAll(big(l + 1), [False] * 4) if l + 1 < DEPTH else None
        o, lse, *arrived = attn_fwd(qa, ka, va, f"attn_fwd{l}", rider)
        if rider:
            gathered[l + 1] = arrived
        cc, sv = conv_fwd(pc, wdw[l], row(b_conf_dw[l]), row(conf_ln_g[l]), row(conf_ln_b[l]), wpw[l],
                          row(b_conf_pw[l]), wsc[l], f"conv_fwd{l}")
        h2, hn2 = out_proj(h, o, cc, g_out, row(mlp_norm_g[l]), f"out_proj{l}")
        r, z, h3 = mlp_fwd(hn2, h2, g_w1, g_w2, f"mlp_fwd{l}")
        saved.append((h, hn, qa, ka, va, pc, sg, o, lse, cc, sv, h2, hn2, r, z, win))
        h = h3

    dh, loss_part, d_gf = loss_head(h, tgt, row(final_norm_g), t_real, "loss_head")
    loss = lax.psum(loss_part[0, 0], ("x", "y", "c"))

    gw, recv, d_win = {}, {}, {}
    for l in reversed(range(DEPTH)):
        h0, hn, qa, ka, va, pc, sg, o, lse, cc, sv, h2, hn2, r, z, win = saved[l]
        _, g_out, g_w1, g_w2 = gathered[l]
        da, dh2, gw["mlp_g", l] = mlp_bwd(dh, r, g_w1, g_w2, h2, row(mlp_norm_g[l]), f"mlp_bwd{l}")
        d_w1 = matmul_tn(hn2, da, D, FC, f"dw_mlp1_{l}", out_dtype=MM, shard="n")
        d_w2 = matmul_tn(z, dh, FC, D, f"dw_mlp2_{l}", out_dtype=MM, shard="m")
        do, dcc = out_proj_bwd(dh2, g_out, f"out_proj_bwd{l}")
        d_wout = jnp.concatenate([matmul_tn(o, dh2, DA, D, f"dw_out_a{l}"), matmul_tn(cc, dh2, 2 * DC, D, f"dw_out_c{l}")],
                                 axis=0).reshape(NDEV, D // NDEV, D).astype(MM)
        qb, dob, qbt, dobt = attn_bwd_prep(qa, do, o, lse, f"attn_bwd_prep{l}")
        riding = [("wout", l, d_wout), ("w1", l, d_w1), ("w2", l, d_w2)] + ([("win", l + 1, d_win[l + 1])] if l + 1 < DEPTH else [])
        dq, dk, dv, dcq, dck, *arrived = attn_bwd(qb, dob, qbt, dobt, ka, va, f"attn_bwd{l}",
                                                  AllToAll([a for _, _, a in riding], [True] * len(riding)))
        for (k, kl, _), a in zip(riding, arrived):
            recv[k, kl] = a
        dz, gw["bf", l] = fgate_bwd(dcq, jnp.pad(dck.T, ((0, 0), (0, 128 - H))), sg, f"fgate_bwd{l}")
        gc, act, pg = conv_bwd_pointwise(dcc, sv, pc, row(conf_ln_g[l]), row(conf_ln_b[l]), wpw[l], f"conv_bwd_pw{l}")
        gw["wpw", l] = matmul_tn(act, dcc, DC, DC, f"dw_conf_pw{l}", n=DC)
        dpc, wg = conv_bwd_taps(gc, pc, dcc, sv, wdw[l], wsc[l], f"conv_bwd_taps{l}")
        gw["pg", l], gw["wg", l] = pg, wg
        dproj = jnp.concatenate([dq, dk, dv, dpc, dz], axis=1)
        d_win[l] = _from_padded_cols(matmul_tn(hn, dproj, 512, NP_IN, f"dw_in{l}")).reshape(D, NDEV, n_sh).transpose(
            1, 0, 2).astype(MM)
        dh, gw["mix_g", l] = in_proj_bwd(dproj, win, h0, row(mix_norm_g[l]), dh2, f"in_proj_bwd{l}")

    grad_x = dh[NM:t_real][None]
    stack = lambda k: jnp.stack([gw[k, l] for l in range(DEPTH)])

    d_wdw = stack("wg")[:, 0:CK].reshape(DEPTH, CK, NDEV, DC // NDEV).transpose(2, 0, 1, 3)
    d_wsc = stack("wg")[:, 32:32 + SK].reshape(DEPTH, SK, NDEV, DC // NDEV).transpose(2, 0, 1, 3)
    d_meta = dh[0:NM].reshape(NM, NDEV, D // NDEV).transpose(1, 0, 2)
    d_wpw = stack("wpw").reshape(DEPTH, NDEV, DC // NDEV, DC).transpose(1, 0, 2, 3)
    d_tiny = jnp.stack([_pack([d_wdw[p], d_wsc[p], d_meta[p], d_wpw[p]], TINY_ROWS) for p in range(NDEV)])
    pgs = stack("pg")
    d_rep = _pack([stack("mix_g").reshape(DEPTH, D), stack("bf")[:, 0, :H], pgs[:, 3], pgs[:, 1], pgs[:, 2], pgs[:, 0],
                   stack("mlp_g").reshape(DEPTH, D), d_gf.reshape(D)], REP_ROWS)
    recv["win", 0], r_tiny, r_rep = exchange([d_win[0], d_tiny, d_rep], [True, True, False], "exchange_grads")

    per_layer = lambda k: [recv[k, l] for l in range(DEPTH)]
    res = {}
    res["w_in"] = adamw(per_layer("win"), w_in, m_w_in, v_w_in, 256, "adamw_w_in")
    res["w_out"] = adamw(per_layer("wout"), w_out, m_w_out, v_w_out, D // NDEV, "adamw_w_out")
    res["w_mlp1"] = adamw(per_layer("w1"), w_mlp1, m_w_mlp1, v_w_mlp1, 256, "adamw_w_mlp1")
    res["w_mlp2"] = adamw(per_layer("w2"), w_mlp2, m_w_mlp2, v_w_mlp2, 128, "adamw_w_mlp2")
    tiny_names = ["w_conf_dw", "w_sc_conv", "meta_tokens", "w_conf_pw"]
    tiny_wmv = [[w_conf_dw, w_sc_conv, meta_tokens, w_conf_pw], [m_w_conf_dw, m_w_sc_conv, m_meta_tokens, m_w_conf_pw],
                [v_w_conf_dw, v_w_sc_conv, v_meta_tokens, v_w_conf_pw]]
    rep_names = ["mix_norm_g", "b_forget", "b_conf_dw", "conf_ln_g", "conf_ln_b", "b_conf_pw", "mlp_norm_g", "final_norm_g"]
    rep_wmv = [[mix_norm_g, b_forget, b_conf_dw, conf_ln_g, conf_ln_b, b_conf_pw, mlp_norm_g, final_norm_g],
               [m_mix_norm_g, m_b_forget, m_b_conf_dw, m_conf_ln_g, m_conf_ln_b, m_b_conf_pw, m_mlp_norm_g, m_final_norm_g],
               [v_mix_norm_g, v_b_forget, v_b_conf_dw, v_conf_ln_g, v_conf_ln_b, v_b_conf_pw, v_mlp_norm_g, v_final_norm_g]]
    for names, wmv, shapes, recv_pack, rows_n, nm in ((tiny_names, tiny_wmv, TINY_SHAPES, r_tiny, TINY_ROWS, "adamw_tiny"),
                                                 (rep_names, rep_wmv, REP_SHAPES, r_rep, REP_ROWS, "adamw_rep")):
        packed = [_pack(group, rows_n)[None] for group in wmv]
        outs = adamw([recv_pack], *packed, rows_n, nm)
        parts = [_unpack(a[0], shapes) for a in outs]
        for k, n in enumerate(names):
            res[n] = [parts[q][k] for q in range(4)]

    order = ["meta_tokens", "mix_norm_g", "w_in", "b_forget", "w_conf_dw", "b_conf_dw", "conf_ln_g", "conf_ln_b",
             "w_conf_pw", "b_conf_pw", "w_sc_conv", "w_out", "mlp_norm_g", "w_mlp1", "w_mlp2", "final_norm_g"]
    return (loss, grad_x, *[res[n][0] for n in order], *[res[n][1] for n in order],
            *[res[n][2] for n in order], *[res[n][3] for n in order])
```

```python
import functools

import jax
import jax.numpy as jnp
from jax import lax
from jax.experimental import pallas as pl
from jax.experimental.pallas import tpu as pltpu

F32 = jnp.float32
MM = jnp.bfloat16

D = 1024
H = 8
DH = 64
DA = H * DH
HT = 128
DC = 256
NM = 16
CK = 31
SK = 3
DFF = 4096
DEPTH = 2
N_IN = 3 * DA + H + 2 * DC + 3 * DC
NP_IN = 3 * DA + 5 * DC + 128
C0 = 3 * DA
F0 = 3 * DA + 5 * DC
EPS = 1e-6
TM = 640
HALO = 32
FC = 512
NDEV = 8
SCALE = DH ** -0.5
NEG = -1e30

ADAM_LR, ADAM_B1, ADAM_B2, ADAM_EPS, ADAM_WD, ADAM_STEP = 0.001, 0.9, 0.999, 1e-08, 0.01, 10

VMEM_LIMIT = 56 * 1024 * 1024

S = jax.ShapeDtypeStruct
NT = (((1,), (1,)), ((), ()))
TN = (((0,), (0,)), ((), ()))


def _params(n_grid):
    return pltpu.CompilerParams(dimension_semantics=("arbitrary",) * n_grid, vmem_limit_bytes=VMEM_LIMIT)


def _sigmoid(x):
    return 1.0 / (1.0 + jnp.exp(-x))


def _full(shape):
    n = len(shape)
    return pl.BlockSpec(shape, lambda *_: (0,) * n)


def _lane_put(dst, col, h):
    lane = lax.broadcasted_iota(jnp.int32, dst.shape, 1)
    return jnp.where(lane == h, col, dst)


ANY = pl.BlockSpec(memory_space=pl.ANY)


class AllToAll:
    def __init__(self, srcs, per_peer):
        self.srcs, self.per_peer, self.k_n = list(srcs), list(per_peer), len(srcs)
        self.out_shape = [S((NDEV,) + (a.shape[1:] if pp else a.shape), a.dtype) for a, pp in zip(srcs, per_peer)]
        self.scratch = [pltpu.SemaphoreType.DMA((self.k_n, NDEV - 1)), pltpu.SemaphoreType.DMA((self.k_n, NDEV - 1)),
                        pltpu.SemaphoreType.DMA((self.k_n,))]

    def copies(self, src, out, send_sems, recv_sems, local_sems):
        x, y, c = lax.axis_index("x"), lax.axis_index("y"), lax.axis_index("c")
        me = 4 * x + 2 * y + c

        def piece(k, p):
            return src[k].at[p] if self.per_peer[k] else src[k]

        local = [pltpu.make_async_copy(piece(k, me), out[k].at[me], local_sems.at[k]) for k in range(self.k_n)]
        sends, recvs = [], []
        for r in range(1, NDEV):
            px = 1 - x if (r >> 2) & 1 else x
            py = 1 - y if (r >> 1) & 1 else y
            pc = 1 - c if r & 1 else c
            pidx = 4 * px + 2 * py + pc
            for k in range(self.k_n):
                sends.append(pltpu.make_async_remote_copy(
                    src_ref=piece(k, pidx), dst_ref=out[k].at[me],
                    send_sem=send_sems.at[k, r - 1], recv_sem=recv_sems.at[k, r - 1],
                    device_id=(px, py, pc), device_id_type=pl.DeviceIdType.MESH))
                recvs.append(pltpu.make_async_remote_copy(
                    src_ref=piece(k, pidx), dst_ref=out[k].at[pidx],
                    send_sem=send_sems.at[k, r - 1], recv_sem=recv_sems.at[k, r - 1],
                    device_id=(px, py, pc), device_id_type=pl.DeviceIdType.MESH))

        def start():
            for cp in local + sends:
                cp.start()

        def wait():
            for cp in recvs:
                cp.wait_recv()
            for cp in sends:
                cp.wait_send()
            for cp in local:
                cp.wait()

        return start, wait


def exchange(srcs, per_peer, name):
    plan = AllToAll(srcs, per_peer)

    def body(*refs):
        start, wait = plan.copies(refs[:plan.k_n], refs[plan.k_n:2 * plan.k_n], *refs[2 * plan.k_n:])
        start()
        wait()

    return pl.pallas_call(
        body, name=name, out_shape=plan.out_shape, in_specs=[ANY] * plan.k_n, out_specs=[ANY] * plan.k_n,
        scratch_shapes=plan.scratch,
    )(*srcs)


def gather(srcs, name):
    k_n = len(srcs)
    out_shape = [S((NDEV,) + a.shape, a.dtype) for a in srcs]

    def body(*refs):
        src, out = refs[:k_n], refs[k_n:2 * k_n]
        send_sems, recv_sems, local_sems = refs[2 * k_n:]
        x, y, c = lax.axis_index("x"), lax.axis_index("y"), lax.axis_index("c")
        me, sibling = (x, y, c), (x, y, 1 - c)
        chips = [(1 - x, y), (x, 1 - y), (1 - x, 1 - y)]

        def slot(k, dev):
            return out[k].at[4 * dev[0] + 2 * dev[1] + dev[2]]

        def copy(k, r, block, to, from_src=False):
            return pltpu.make_async_remote_copy(
                src_ref=src[k] if from_src else slot(k, block), dst_ref=slot(k, block),
                send_sem=send_sems.at[k, r], recv_sem=recv_sems.at[k, r],
                device_id=to, device_id_type=pl.DeviceIdType.MESH)

        local = [pltpu.make_async_copy(src[k], slot(k, me), local_sems.at[k]) for k in range(k_n)]
        first = [copy(k, 0, me, sibling, True) for k in range(k_n)]
        first += [copy(k, 1 + n, me, (*chip, c), True) for n, chip in enumerate(chips) for k in range(k_n)]
        for cp in local + first:
            cp.start()
        passed = []
        for n, chip in enumerate(chips):
            for k in range(k_n):
                copy(k, 1 + n, (*chip, c), me).wait_recv()
                passed.append(copy(k, 4 + n, (*chip, c), sibling))
                passed[-1].start()
        for k in range(k_n):
            copy(k, 0, sibling, me).wait_recv()
            for n, chip in enumerate(chips):
                copy(k, 4 + n, (*chip, 1 - c), me).wait_recv()
        for cp in first + passed:
            cp.wait_send()
        for cp in local:
            cp.wait()

    any_spec = pl.BlockSpec(memory_space=pl.ANY)
    return pl.pallas_call(
        body, name=name, out_shape=out_shape,
        in_specs=[any_spec] * k_n, out_specs=[any_spec] * k_n,
        scratch_shapes=[pltpu.SemaphoreType.DMA((k_n, NDEV - 1)), pltpu.SemaphoreType.DMA((k_n, NDEV - 1)),
                        pltpu.SemaphoreType.DMA((k_n,))],
    )(*srcs)


def in_proj(h, g, w, bf, name):
    tp = h.shape[0]

    def body(h_ref, g_ref, w_ref, bf_ref, hn_ref, qa_ref, ka_ref, va_ref, pc_ref, sg_ref, carry):
        i = pl.program_id(0)

        @pl.when(i == 0)
        def _():
            carry[...] = jnp.zeros_like(carry)

        x = h_ref[...]
        r = lax.rsqrt(jnp.mean(x * x, axis=-1, keepdims=True) + EPS)
        hn = (x * r * g_ref[...]).astype(MM)
        hn_ref[...] = hn
        pc_ref[...] = jnp.dot(hn, w_ref[:, C0:F0], preferred_element_type=F32)
        z = jnp.dot(hn, w_ref[:, F0:NP_IN], preferred_element_type=F32) + bf_ref[...]
        lane = lax.broadcasted_iota(jnp.int32, z.shape, 1)
        logf = jnp.where(lane < H, jnp.minimum(z, 0.0) - jnp.log(1.0 + jnp.exp(-jnp.abs(z))), 0.0)
        sg_ref[...] = 1.0 / (1.0 + jnp.exp(z))
        row = lax.broadcasted_iota(jnp.int32, (TM, TM), 0)
        col = lax.broadcasted_iota(jnp.int32, (TM, TM), 1)
        tri = (row >= col).astype(F32)
        c = jnp.dot(tri, logf, precision=lax.Precision.HIGHEST, preferred_element_type=F32) + carry[...]
        carry[...] = c[TM - 1:TM, :]
        qkv = jnp.dot(hn, w_ref[:, 0:C0], preferred_element_type=F32)
        one = [1.0, 1.0, 1.0]
        for hd in range(H):
            t = slice(hd * HT, (hd + 1) * HT)
            cs = list(_split3(c[:, hd:hd + 1]))
            qa_ref[:, t] = _aug_tile(hd, qkv[:, hd * DH:(hd + 1) * DH] * SCALE, cs + one).astype(MM)
            ka_ref[:, t] = _aug_tile(hd, qkv[:, DA + hd * DH:DA + (hd + 1) * DH], one + [-v for v in cs] + one).astype(MM)
            va_ref[:, t] = _aug_tile(hd, qkv[:, 2 * DA + hd * DH:2 * DA + (hd + 1) * DH], [1.0] + one).astype(MM)

    rows = lambda n: pl.BlockSpec((TM, n), lambda i: (i, 0))
    return pl.pallas_call(
        body, name=name, grid=(tp // TM,),
        in_specs=[rows(D), _full((1, D)), _full((D, NP_IN)), _full((1, 128))],
        out_specs=[rows(D), rows(H * HT), rows(H * HT), rows(H * HT), rows(5 * DC), rows(128)],
        out_shape=[S((tp, D), MM), S((tp, H * HT), MM), S((tp, H * HT), MM), S((tp, H * HT), MM),
                   S((tp, 5 * DC), F32), S((tp, 128), F32)],
        scratch_shapes=[pltpu.VMEM((1, 128), F32)],
        compiler_params=_params(1),
    )(h, g, w, bf)


def _split3(c):
    hi = c.astype(MM).astype(F32)
    mid = (c - hi).astype(MM).astype(F32)
    lo = (c - hi - mid).astype(MM).astype(F32)
    return hi, mid, lo


def _main(hd):
    return slice(0, DH) if hd % 2 == 0 else slice(DH, HT)


def _aug_lane(hd, pos):
    return pos + (DH if hd % 2 == 0 else 0)


def _aug_tile(hd, main, cols):
    lane = lax.broadcasted_iota(jnp.int32, main.shape, 1)
    aug = jnp.zeros(main.shape, F32)
    for pos, val in enumerate(cols):
        aug = jnp.where(lane == pos, val, aug)
    return jnp.concatenate([main, aug] if hd % 2 == 0 else [aug, main], axis=1)


def _merge_pairs(tiles):
    lane = lax.broadcasted_iota(jnp.int32, tiles[0].shape, 1)
    return jnp.concatenate([jnp.where(lane < DH, tiles[2 * m], tiles[2 * m + 1]) for m in range(H // 2)], axis=1)


def _causal_mask():
    return lax.broadcasted_iota(jnp.int32, (TM, TM), 0) >= lax.broadcasted_iota(jnp.int32, (TM, TM), 1)


def _split_rider(refs, n_in, n_out, rider):
    if rider is None:
        return refs, None
    k = rider.k_n
    own = refs[:n_in] + refs[n_in + k:n_in + k + n_out] + refs[n_in + 2 * k + n_out:-3]
    return own, rider.copies(refs[n_in:n_in + k], refs[n_in + k + n_out:n_in + 2 * k + n_out], *refs[-3:])


def attn_fwd(qa, ka, va, name, rider=None):
    tp = qa.shape[0]
    nb = tp // TM
    pairs = [(i, j) for i in range(nb) for j in range(i + 1)]
    i_tab, j_tab = (jnp.asarray([p[a] for p in pairs], jnp.int32) for a in (0, 1))

    def body(i_ref, j_ref, *refs):
        (q_ref, k_ref, v_ref, o_ref, lse_ref, *scr), copies = _split_rider(refs, 3, 2, rider)
        m_scr, acc_scr, bias = scr[:H], scr[H:2 * H], scr[2 * H]
        n = pl.program_id(0)
        i, j = i_ref[n], j_ref[n]

        @pl.when(n == 0)
        def _():
            if copies:
                copies[0]()
            bias[...] = jnp.where(_causal_mask(), 0.0, NEG)

        @pl.when(j == 0)
        def _():
            for hd in range(H):
                m_scr[hd][...] = jnp.full((TM, 1), NEG, F32)
                acc_scr[hd][...] = jnp.zeros((TM, HT), F32)

        def step(diag):
            def logits(hd):
                t = slice(hd * HT, (hd + 1) * HT)
                s = lax.dot_general(q_ref[:, t], k_ref[:, t], NT, preferred_element_type=F32)
                return s + bias[...] if diag else s

            s_next = logits(0)
            for hd in range(H):
                t = slice(hd * HT, (hd + 1) * HT)
                s = s_next
                if hd + 1 < H:
                    s_next = logits(hd + 1)
                m_prev = m_scr[hd][...]
                m_new = jnp.maximum(m_prev, jnp.max(s, axis=1, keepdims=True))
                p = jnp.exp(s - m_new).astype(MM)
                acc_scr[hd][...] = (jnp.exp(m_prev - m_new) * acc_scr[hd][...]
                                    + jnp.dot(p, v_ref[:, t], preferred_element_type=F32))
                m_scr[hd][...] = m_new

        @pl.when(j < i)
        def _():
            step(False)

        @pl.when(j == i)
        def _():
            step(True)

        @pl.when(j == i)
        def _():
            lse = jnp.zeros((TM, 128), F32)
            outs = []
            for hd in range(H):
                acc = acc_scr[hd][...]
                l = acc[:, _aug_lane(hd, 0):_aug_lane(hd, 0) + 1]
                outs.append(acc * (1.0 / l))
                lse = _lane_put(lse, m_scr[hd][...] + jnp.log(l), hd)
            o_ref[...] = _merge_pairs(outs).astype(MM)
            lse_ref[...] = lse

        if copies:
            @pl.when(n == len(pairs) - 1)
            def _():
                copies[1]()

    qi = lambda w: pl.BlockSpec((TM, w), lambda n, it, jt: (it[n], 0))
    kv = pl.BlockSpec((TM, H * HT), lambda n, it, jt: (jt[n], 0))
    r_n = rider.k_n if rider else 0
    return pl.pallas_call(
        body, name=name,
        grid_spec=pltpu.PrefetchScalarGridSpec(
            num_scalar_prefetch=2, grid=(len(pairs),),
            in_specs=[qi(H * HT), kv, kv] + [ANY] * r_n, out_specs=[qi(DA), qi(128)] + [ANY] * r_n,
            scratch_shapes=[pltpu.VMEM((TM, 1), F32)] * H + [pltpu.VMEM((TM, HT), F32)] * H + [pltpu.VMEM((TM, TM), F32)]
            + (rider.scratch if rider else [])),
        out_shape=[S((tp, DA), MM), S((tp, 128), F32)] + (rider.out_shape if rider else []),
        compiler_params=_params(1),
    )(i_tab, j_tab, qa, ka, va, *(rider.srcs if rider else []))


def _layernorm_parts(dw, g, b):
    mu = jnp.mean(dw, axis=-1, keepdims=True)
    xc = dw - mu
    rs = lax.rsqrt(jnp.mean(xc * xc, axis=-1, keepdims=True) + EPS)
    xh = xc * rs
    return xh, rs, xh * g + b


def _fill_halo(ext, halo, cur, first):
    ext[0:HALO, :] = jnp.where(first, 0.0, halo)
    ext[HALO:HALO + TM, :] = cur


def _shift_copies(dst, src, lanes=slice(None)):
    for r in range(1, 8):
        dst[r - 1] = src[pl.ds(r, dst.shape[1]), lanes]


def _window(src, shifted, off, lanes=slice(None)):
    if off % 8 == 0:
        return src[pl.ds(off, TM), lanes]
    return shifted[off % 8 - 1, pl.ds(off - off % 8, TM), :]


def conv_fwd(pc, wdw, bdw, lng, lnb, wpw, bpw, wsc, name):
    tp = pc.shape[0]

    def body(pc_ref, hl_ref, wdw_ref, bdw_ref, lng_ref, lnb_ref, wpw_ref, bpw_ref, wsc_ref, cc_ref, sv_ref, xe, ce, xs):
        first = pl.program_id(0) == 0
        glu = pc_ref[:, 0:DC] * _sigmoid(pc_ref[:, DC:2 * DC])
        cu = pc_ref[:, 3 * DC:4 * DC] * pc_ref[:, 4 * DC:5 * DC]
        _fill_halo(xe, hl_ref[:, 0:DC] * _sigmoid(hl_ref[:, DC:2 * DC]), glu, first)
        _fill_halo(ce, hl_ref[:, 3 * DC:4 * DC] * hl_ref[:, 4 * DC:5 * DC], cu, first)
        _shift_copies(xs, xe)
        dw = jnp.zeros((TM, DC), F32) + bdw_ref[...]
        for k in range(CK):
            dw = dw + wdw_ref[k:k + 1, :] * _window(xe, xs, HALO - (CK - 1) + k)
        cv = jnp.zeros((TM, DC), F32)
        for k in range(SK):
            cv = cv + wsc_ref[k:k + 1, :] * ce[pl.ds(HALO - (SK - 1) + k, TM), :]
        _, _, ln = _layernorm_parts(dw, lng_ref[...], lnb_ref[...])
        act = ln * _sigmoid(ln)
        conf = jnp.dot(act.astype(MM), wpw_ref[...], preferred_element_type=F32) + bpw_ref[...]
        cc_ref[:, 0:DC] = conf.astype(MM)
        cc_ref[:, DC:2 * DC] = (pc_ref[:, 2 * DC:3 * DC] * cv).astype(MM)
        sv_ref[:, 0:DC] = dw
        sv_ref[:, DC:2 * DC] = cv

    hb = TM // HALO
    rows = lambda n: pl.BlockSpec((TM, n), lambda i: (i, 0))
    return pl.pallas_call(
        body, name=name, grid=(tp // TM,),
        in_specs=[rows(5 * DC), pl.BlockSpec((HALO, 5 * DC), lambda i: (jnp.maximum(i * hb - 1, 0), 0)),
                  _full((32, DC)), _full((1, DC)), _full((1, DC)), _full((1, DC)), _full((DC, DC)), _full((1, DC)),
                  _full((8, DC))],
        out_specs=[rows(2 * DC), rows(2 * DC)],
        out_shape=[S((tp, 2 * DC), MM), S((tp, 2 * DC), F32)],
        scratch_shapes=[pltpu.VMEM((HALO + TM, DC), F32), pltpu.VMEM((HALO + TM, DC), F32),
                        pltpu.VMEM((7, TM + HALO - 8, DC), F32)],
        compiler_params=_params(1),
    )(pc, pc, wdw, bdw, lng, lnb, wpw, bpw, wsc)


def out_proj(h, o, cc, wo, g2, name):
    tp = h.shape[0]

    def body(h_ref, o_ref, cc_ref, wo_ref, g_ref, h2_ref, hn_ref):
        wo = wo_ref[...].reshape(D, D)
        h2 = (h_ref[...] + jnp.dot(o_ref[...], wo[0:DA, :], preferred_element_type=F32)
              + jnp.dot(cc_ref[...], wo[DA:D, :], preferred_element_type=F32))
        h2_ref[...] = h2
        r = lax.rsqrt(jnp.mean(h2 * h2, axis=-1, keepdims=True) + EPS)
        hn_ref[...] = (h2 * r * g_ref[...]).astype(MM)

    rows = lambda n: pl.BlockSpec((TM, n), lambda i: (i, 0))
    return pl.pallas_call(
        body, name=name, grid=(tp // TM,),
        in_specs=[rows(D), rows(DA), rows(2 * DC), _full((NDEV, D // NDEV, D)), _full((1, D))],
        out_specs=[rows(D), rows(D)],
        out_shape=[S((tp, D), F32), S((tp, D), MM)],
        compiler_params=_params(1),
    )(h, o, cc, wo, g2)


def mlp_fwd(hn, h2, w1, w2, name):
    tp = hn.shape[0]
    nf = DFF // FC

    def body(hn_ref, h2_ref, w1_ref, w2_ref, r_ref, z_ref, h3_ref, acc):
        j = pl.program_id(1)

        @pl.when(j == 0)
        def _():
            acc[...] = jnp.zeros_like(acc)

        r = jnp.maximum(jnp.dot(hn_ref[...], w1_ref[...], preferred_element_type=F32), 0.0)
        zb = (r * r).astype(MM)
        r_ref[...] = r.astype(MM)
        z_ref[...] = zb
        acc[...] += jnp.dot(zb, w2_ref[...], preferred_element_type=F32)

        @pl.when(j == nf - 1)
        def _():
            h3_ref[...] = h2_ref[...] + acc[...]

    return pl.pallas_call(
        body, name=name, grid=(tp // TM, nf),
        in_specs=[pl.BlockSpec((TM, D), lambda i, j: (i, 0)), pl.BlockSpec((TM, D), lambda i, j: (i, 0)),
                  pl.BlockSpec((None, D, FC), lambda i, j: (j, 0, 0)), pl.BlockSpec((None, FC, D), lambda i, j: (j, 0, 0))],
        out_specs=[pl.BlockSpec((TM, FC), lambda i, j: (i, j)), pl.BlockSpec((TM, FC), lambda i, j: (i, j)),
                   pl.BlockSpec((TM, D), lambda i, j: (i, 0))],
        out_shape=[S((tp, DFF), MM), S((tp, DFF), MM), S((tp, D), F32)],
        scratch_shapes=[pltpu.VMEM((TM, D), F32)],
        compiler_params=_params(2),
    )(hn, h2, w1, w2)


def loss_head(h, tgt, g, t_real, name):
    tp = h.shape[0]

    def body(h_ref, t_ref, g_ref, dh_ref, loss_ref, dg_ref):
        i = pl.program_id(0)

        @pl.when(i == 0)
        def _():
            loss_ref[...] = jnp.zeros_like(loss_ref)
            dg_ref[...] = jnp.zeros_like(dg_ref)

        x = h_ref[...]
        gg = g_ref[...]
        r = lax.rsqrt(jnp.mean(x * x, axis=-1, keepdims=True) + EPS)
        xn = x * r
        row = i * TM + lax.broadcasted_iota(jnp.int32, (TM, 1), 0)
        e = jnp.where((row >= NM) & (row < t_real), xn * gg - t_ref[...], 0.0)
        loss_ref[...] += jnp.sum(e * e) * (0.5 / D)
        dy = e * (1.0 / D)
        dg_ref[...] += jnp.sum(dy * xn, axis=0, keepdims=True)
        u = dy * gg
        dh_ref[...] = r * (u - xn * jnp.mean(u * xn, axis=-1, keepdims=True))

    rows = lambda n: pl.BlockSpec((TM, n), lambda i: (i, 0))
    return pl.pallas_call(
        body, name=name, grid=(tp // TM,),
        in_specs=[rows(D), rows(D), _full((1, D))],
        out_specs=[rows(D), _full((8, 128)), _full((1, D))],
        out_shape=[S((tp, D), F32), S((8, 128), F32), S((1, D), F32)],
        compiler_params=_params(1),
    )(h, tgt, g)


def _rmsnorm_bwd(x, g, dy):
    r = lax.rsqrt(jnp.mean(x * x, axis=-1, keepdims=True) + EPS)
    xn = x * r
    u = dy * g
    dx = r * (u - xn * jnp.mean(u * xn, axis=-1, keepdims=True))
    return dx, jnp.sum(dy * xn, axis=0, keepdims=True)


def matmul_tn(a, b, tm, tn, name, n=None, out_dtype=F32, shard=None, sw=None):
    tp, m = a.shape
    n = b.shape[1] if n is None else n
    nk = tp // TM

    def body(a_ref, b_ref, o_ref, acc):
        k = pl.program_id(2)

        @pl.when(k == 0)
        def _():
            acc[...] = jnp.zeros_like(acc)

        acc[...] += lax.dot_general(a_ref[...].astype(MM), b_ref[...].astype(MM), TN, preferred_element_type=F32)

        @pl.when(k == nk - 1)
        def _():
            if shard is None:
                o_ref[...] = acc[...].astype(out_dtype)
            elif shard == "m":
                for s in range(tm // sw):
                    o_ref[s] = acc[s * sw:(s + 1) * sw, :].astype(out_dtype)
            else:
                for s in range(tn // sw):
                    o_ref[s] = acc[:, s * sw:(s + 1) * sw].astype(out_dtype)

    if shard is None:
        out_spec, out_shape = pl.BlockSpec((tm, tn), lambda mi, ni, k: (mi, ni)), (m, n)
    elif shard == "m":
        out_spec, out_shape = pl.BlockSpec((tm // sw, sw, tn), lambda mi, ni, k: (mi, 0, ni)), (m // sw, sw, n)
    else:
        out_spec, out_shape = pl.BlockSpec((tn // sw, tm, sw), lambda mi, ni, k: (ni, mi, 0)), (n // sw, m, sw)
    return pl.pallas_call(
        body, name=name, grid=(m // tm, n // tn, nk),
        in_specs=[pl.BlockSpec((TM, tm), lambda mi, ni, k: (k, mi)), pl.BlockSpec((TM, tn), lambda mi, ni, k: (k, ni))],
        out_specs=out_spec, out_shape=S(out_shape, out_dtype),
        scratch_shapes=[pltpu.VMEM((tm, tn), F32)],
        compiler_params=_params(3),
    )(a, b)


def mlp_bwd(dh3, r, w1, w2, h2, g, name):
    tp = dh3.shape[0]
    nf = DFF // FC

    def body(dh3_ref, r_ref, w1_ref, w2_ref, h2_ref, g_ref, da_ref, dh2_ref, dg_ref, acc, dhb):
        i, j = pl.program_id(0), pl.program_id(1)

        @pl.when((i == 0) & (j == 0))
        def _():
            dg_ref[...] = jnp.zeros_like(dg_ref)

        @pl.when(j == 0)
        def _():
            acc[...] = jnp.zeros_like(acc)
            dhb[...] = dh3_ref[...].astype(MM)

        dz = lax.dot_general(dhb[...], w2_ref[...], NT, preferred_element_type=F32)
        da = (dz * (2.0 * r_ref[...].astype(F32))).astype(MM)
        da_ref[...] = da
        acc[...] += lax.dot_general(da, w1_ref[...], NT, preferred_element_type=F32)

        @pl.when(j == nf - 1)
        def _():
            dx, dg = _rmsnorm_bwd(h2_ref[...], g_ref[...], acc[...])
            dh2_ref[...] = dh3_ref[...] + dx
            dg_ref[...] += dg

    return pl.pallas_call(
        body, name=name, grid=(tp // TM, nf),
        in_specs=[pl.BlockSpec((TM, D), lambda i, j: (i, 0)), pl.BlockSpec((TM, FC), lambda i, j: (i, j)),
                  pl.BlockSpec((None, D, FC), lambda i, j: (j, 0, 0)), pl.BlockSpec((None, FC, D), lambda i, j: (j, 0, 0)),
                  pl.BlockSpec((TM, D), lambda i, j: (i, 0)), _full((1, D))],
        out_specs=[pl.BlockSpec((TM, FC), lambda i, j: (i, j)), pl.BlockSpec((TM, D), lambda i, j: (i, 0)),
                   _full((1, D))],
        out_shape=[S((tp, DFF), MM), S((tp, D), F32), S((1, D), F32)],
        scratch_shapes=[pltpu.VMEM((TM, D), F32), pltpu.VMEM((TM, D), MM)],
        compiler_params=_params(2),
    )(dh3, r, w1, w2, h2, g)


def out_proj_bwd(dh2, wo, name):
    tp = dh2.shape[0]

    def body(dh_ref, wo_ref, do_ref, dcc_ref):
        dcat = lax.dot_general(dh_ref[...].astype(MM), wo_ref[...].reshape(D, D), NT, preferred_element_type=F32)
        do_ref[...] = dcat[:, 0:DA].astype(MM)
        dcc_ref[...] = dcat[:, DA:D]

    rows = lambda n: pl.BlockSpec((TM, n), lambda i: (i, 0))
    return pl.pallas_call(
        body, name=name, grid=(tp // TM,),
        in_specs=[rows(D), _full((NDEV, D // NDEV, D))],
        out_specs=[rows(DA), rows(2 * DC)],
        out_shape=[S((tp, DA), MM), S((tp, 2 * DC), F32)],
        compiler_params=_params(1),
    )(dh2, wo)


def attn_bwd_prep(qa, do, o, lse, name):
    tp = qa.shape[0]

    def body(q_ref, do_ref, o_ref, lse_ref, qb_ref, dob_ref, qbt_ref, dobt_ref):
        lse = lse_ref[...]
        lane = lax.broadcasted_iota(jnp.int32, (TM, HT), 1)
        for hd in range(H):
            t, hs = slice(hd * HT, (hd + 1) * HT), slice(hd * DH, (hd + 1) * DH)
            dof = do_ref[:, hs].astype(F32)
            dd = jnp.sum(dof * o_ref[:, hs].astype(F32), axis=1, keepdims=True)
            dob = _aug_tile(hd, dof, [0.0] + [-v for v in _split3(dd)])
            qb = q_ref[:, t].astype(F32)
            for pos, v in enumerate(_split3(lse[:, hd:hd + 1])):
                qb = jnp.where(lane == _aug_lane(hd, 6 + pos), -v, qb)
            qb_ref[:, t] = qb.astype(MM)
            dob_ref[:, t] = dob.astype(MM)
            qbt_ref[t, :] = qb.T.astype(MM)
            dobt_ref[t, :] = dob.T.astype(MM)

    rows = lambda n: pl.BlockSpec((TM, n), lambda i: (i, 0))
    cols = pl.BlockSpec((H * HT, TM), lambda i: (0, i))
    return pl.pallas_call(
        body, name=name, grid=(tp // TM,),
        in_specs=[rows(H * HT), rows(DA), rows(DA), rows(128)],
        out_specs=[rows(H * HT), rows(H * HT), cols, cols],
        out_shape=[S((tp, H * HT), MM), S((tp, H * HT), MM), S((H * HT, tp), MM), S((H * HT, tp), MM)],
        compiler_params=_params(1),
    )(qa, do, o, lse)


def attn_bwd(qb, dob, qbt, dobt, ka, va, name, rider=None):
    tp = qb.shape[0]
    nb = tp // TM

    pairs = [(j, i) for j in range(nb) for i in range(j, nb)]
    j_tab, i_tab = (jnp.asarray([p[a] for p in pairs], jnp.int32) for a in (0, 1))

    def body(j_ref, i_ref, *refs):
        own, copies = _split_rider(refs, 6, 5, rider)
        (qb_ref, dob_ref, qbt_ref, dobt_ref, k_ref, v_ref, dq_ref, dk_ref, dv_ref, dcq_ref, dck_ref,
         dq_acc, dcq_acc, dkt_acc, dvt_acc, dck_acc, stage, sems) = own
        n = pl.program_id(0)
        j, i = j_ref[n], i_ref[n]

        @pl.when(n == 0)
        def _():
            if copies:
                copies[0]()
            dq_acc[...] = jnp.zeros_like(dq_acc)
            dcq_acc[...] = jnp.zeros_like(dcq_acc)

        @pl.when(i == j)
        def _():
            dkt_acc[...] = jnp.zeros_like(dkt_acc)
            dvt_acc[...] = jnp.zeros_like(dvt_acc)
            dck_acc[...] = jnp.zeros_like(dck_acc)

        def step(diag):
            rows = pl.ds(pl.multiple_of(i * TM, TM), TM)
            dcq = dcq_acc[rows, :]
            lane = lax.broadcasted_iota(jnp.int32, (TM, HT), 1)
            mask = _causal_mask() if diag else None
            for hd in range(H):
                t = slice(hd * HT, (hd + 1) * HT)
                s = lax.dot_general(qb_ref[:, t], k_ref[:, t], NT, preferred_element_type=F32)
                dp = lax.dot_general(dob_ref[:, t], v_ref[:, t], NT, preferred_element_type=F32)
                if diag:
                    s = jnp.where(mask, s, NEG)
                p = jnp.exp(s)
                ds = p * dp
                dsb = ds.astype(MM)
                dvt_acc[t, :] += jnp.dot(dobt_ref[t, :], p.astype(MM), preferred_element_type=F32)
                dkt_acc[t, :] += jnp.dot(qbt_ref[t, :], dsb, preferred_element_type=F32)
                dqh = jnp.dot(dsb, k_ref[:, t], preferred_element_type=F32)
                if hd % 2 == 0:
                    dq_even = dqh
                else:
                    pair = slice((hd // 2) * HT, (hd // 2 + 1) * HT)
                    dq_acc[rows, pair] += jnp.where(lane < DH, dq_even, dqh)
                dcq = dcq + jnp.where(lane == hd, jnp.sum(ds, axis=1, keepdims=True), 0.0)
                dck_acc[hd:hd + 1, :] += jnp.sum(ds, axis=0, keepdims=True)
            dcq_acc[rows, :] = dcq
            return rows

        @pl.when(i > j)
        def _():
            step(False)

        @pl.when(i == j)
        def _():
            rows = step(True)
            stage[...] = (dq_acc[rows, :] * SCALE).astype(MM)
            out = [pltpu.make_async_copy(stage, dq_ref.at[rows, :], sems.at[0]),
                   pltpu.make_async_copy(dcq_acc.at[rows, :], dcq_ref.at[rows, :], sems.at[1])]
            for cp in out:
                cp.start()
            for cp in out:
                cp.wait()

        @pl.when(i == nb - 1)
        def _():
            dk_ref[...] = _merge_pairs([dkt_acc[hd * HT:(hd + 1) * HT, :].T for hd in range(H)]).astype(MM)
            dv_ref[...] = _merge_pairs([dvt_acc[hd * HT:(hd + 1) * HT, :].T for hd in range(H)]).astype(MM)
            dck_ref[...] = dck_acc[...]

        if copies:
            @pl.when(n == len(pairs) - 1)
            def _():
                copies[1]()

    qi = lambda w: pl.BlockSpec((TM, w), lambda n, jt, it: (it[n], 0))
    qt = pl.BlockSpec((H * HT, TM), lambda n, jt, it: (0, it[n]))
    kj = lambda w: pl.BlockSpec((TM, w), lambda n, jt, it: (jt[n], 0))
    r_n = rider.k_n if rider else 0
    return pl.pallas_call(
        body, name=name,
        grid_spec=pltpu.PrefetchScalarGridSpec(
            num_scalar_prefetch=2, grid=(len(pairs),),
            in_specs=[qi(H * HT), qi(H * HT), qt, qt, kj(H * HT), kj(H * HT)] + [ANY] * r_n,
            out_specs=[ANY, kj(DA), kj(DA), ANY, pl.BlockSpec((H, TM), lambda n, jt, it: (0, jt[n]))] + [ANY] * r_n,
            scratch_shapes=[pltpu.VMEM((tp, DA), F32), pltpu.VMEM((tp, 128), F32), pltpu.VMEM((H * HT, TM), F32),
                            pltpu.VMEM((H * HT, TM), F32), pltpu.VMEM((H, TM), F32), pltpu.VMEM((TM, DA), MM),
                            pltpu.SemaphoreType.DMA((2,))] + (rider.scratch if rider else [])),
        out_shape=[S((tp, DA), MM), S((tp, DA), MM), S((tp, DA), MM), S((tp, 128), F32), S((H, tp), F32)]
        + (rider.out_shape if rider else []),
        compiler_params=_params(1),
    )(j_tab, i_tab, qb, dob, qbt, dobt, ka, va, *(rider.srcs if rider else []))


def fgate_bwd(dcq, dck, sg, name):
    tp = dcq.shape[0]
    nb = tp // TM

    def body(dcq_ref, dck_ref, sg_ref, dz_ref, db_ref, carry):
        @pl.when(pl.program_id(0) == 0)
        def _():
            carry[...] = jnp.zeros_like(carry)
            db_ref[...] = jnp.zeros_like(db_ref)

        row = lax.broadcasted_iota(jnp.int32, (TM, TM), 0)
        col = lax.broadcasted_iota(jnp.int32, (TM, TM), 1)
        tri = (row <= col).astype(F32)
        dl = jnp.dot(tri, dcq_ref[...] - dck_ref[...], precision=lax.Precision.HIGHEST,
                     preferred_element_type=F32) + carry[...]
        carry[...] = dl[0:1, :]
        dz = dl * sg_ref[...]
        dz_ref[...] = dz.astype(MM)
        db_ref[...] += jnp.sum(dz, axis=0, keepdims=True)

    rev = pl.BlockSpec((TM, 128), lambda i: (nb - 1 - i, 0))
    return pl.pallas_call(
        body, name=name, grid=(nb,),
        in_specs=[rev, rev, rev], out_specs=[rev, _full((1, 128))],
        out_shape=[S((tp, 128), MM), S((1, 128), F32)],
        scratch_shapes=[pltpu.VMEM((1, 128), F32)],
        compiler_params=_params(1),
    )(dcq, dck, sg)


def conv_bwd_pointwise(dcc, sv, pc, lng, lnb, wpw, name):
    tp = dcc.shape[0]

    def body(dcc_ref, sv_ref, b_ref, lng_ref, lnb_ref, wpw_ref, gc_ref, act_ref, pg_ref):
        @pl.when(pl.program_id(0) == 0)
        def _():
            pg_ref[...] = jnp.zeros_like(pg_ref)

        dconf = dcc_ref[:, 0:DC]
        g = lng_ref[...]
        xh, rs, ln = _layernorm_parts(sv_ref[:, 0:DC], g, lnb_ref[...])
        sig = _sigmoid(ln)
        act_ref[...] = (ln * sig).astype(MM)
        dact = lax.dot_general(dconf.astype(MM), wpw_ref[...], NT, preferred_element_type=F32)
        dln = dact * (sig * (1.0 + ln * (1.0 - sig)))
        dxh = dln * g
        ddw = rs * (dxh - jnp.mean(dxh, axis=-1, keepdims=True) - xh * jnp.mean(dxh * xh, axis=-1, keepdims=True))
        gc_ref[:, 0:DC] = ddw
        gc_ref[:, DC:2 * DC] = dcc_ref[:, DC:2 * DC] * b_ref[...]
        cs = lambda t: jnp.sum(t, axis=0, keepdims=True)
        pg_ref[0:1, :] += cs(dconf)
        pg_ref[1:2, :] += cs(dln * xh)
        pg_ref[2:3, :] += cs(dln)
        pg_ref[3:4, :] += cs(ddw)

    rows = lambda n: pl.BlockSpec((TM, n), lambda i: (i, 0))
    return pl.pallas_call(
        body, name=name, grid=(tp // TM,),
        in_specs=[rows(2 * DC), rows(2 * DC), pl.BlockSpec((TM, DC), lambda i: (i, 2)),
                  _full((1, DC)), _full((1, DC)), _full((DC, DC))],
        out_specs=[rows(2 * DC), rows(DC), _full((8, DC))],
        out_shape=[S((tp, 2 * DC), F32), S((tp, DC), MM), S((8, DC), F32)],
        compiler_params=_params(1),
    )(dcc, sv, pc, lng, lnb, wpw)


def conv_bwd_taps(gc, pc, dcc, sv, wdw, wsc, name):
    tp = gc.shape[0]
    nb = tp // TM
    hb = TM // HALO

    def body(gc_ref, gn_ref, pc_ref, hl_ref, dcc_ref, sv_ref, wdw_ref, wsc_ref, dpc_ref, wg_ref, ge, xe, ce, gs, xs):
        i = pl.program_id(0)

        @pl.when(i == 0)
        def _():
            wg_ref[...] = jnp.zeros_like(wg_ref)

        a, gt = pc_ref[:, 0:DC], pc_ref[:, DC:2 * DC]
        c, u = pc_ref[:, 3 * DC:4 * DC], pc_ref[:, 4 * DC:5 * DC]
        sig = _sigmoid(gt)
        _fill_halo(xe, hl_ref[:, 0:DC] * _sigmoid(hl_ref[:, DC:2 * DC]), a * sig, i == 0)
        _fill_halo(ce, hl_ref[:, 3 * DC:4 * DC] * hl_ref[:, 4 * DC:5 * DC], c * u, i == 0)
        ge[0:TM, :] = gc_ref[...]
        ge[TM:TM + HALO, :] = jnp.where(i == nb - 1, 0.0, gn_ref[...])
        _shift_copies(gs, ge, slice(0, DC))
        _shift_copies(xs, xe)
        ddw, dcv = gc_ref[:, 0:DC], gc_ref[:, DC:2 * DC]
        dglu = jnp.zeros((TM, DC), F32)
        for k in range(CK):
            dglu = dglu + wdw_ref[k:k + 1, :] * _window(ge, gs, CK - 1 - k, slice(0, DC))
            wg_ref[k:k + 1, :] += jnp.sum(ddw * _window(xe, xs, HALO - (CK - 1) + k), axis=0, keepdims=True)
        dcu = jnp.zeros((TM, DC), F32)
        for k in range(SK):
            dcu = dcu + wsc_ref[k:k + 1, :] * ge[pl.ds(SK - 1 - k, TM), DC:2 * DC]
            wg_ref[32 + k:33 + k, :] += jnp.sum(dcv * ce[pl.ds(HALO - (SK - 1) + k, TM), :], axis=0, keepdims=True)
        dpc_ref[:, 0:DC] = (dglu * sig).astype(MM)
        dpc_ref[:, DC:2 * DC] = (dglu * a * sig * (1.0 - sig)).astype(MM)
        dpc_ref[:, 2 * DC:3 * DC] = (dcc_ref[:, DC:2 * DC] * sv_ref[:, DC:2 * DC]).astype(MM)
        dpc_ref[:, 3 * DC:4 * DC] = (dcu * u).astype(MM)
        dpc_ref[:, 4 * DC:5 * DC] = (dcu * c).astype(MM)

    rows = lambda n: pl.BlockSpec((TM, n), lambda i: (i, 0))
    return pl.pallas_call(
        body, name=name, grid=(nb,),
        in_specs=[rows(2 * DC), pl.BlockSpec((HALO, 2 * DC), lambda i: (jnp.minimum((i + 1) * hb, nb * hb - 1), 0)),
                  rows(5 * DC), pl.BlockSpec((HALO, 5 * DC), lambda i: (jnp.maximum(i * hb - 1, 0), 0)),
                  rows(2 * DC), rows(2 * DC), _full((32, DC)), _full((8, DC))],
        out_specs=[rows(5 * DC), _full((40, DC))],
        out_shape=[S((tp, 5 * DC), MM), S((40, DC), F32)],
        scratch_shapes=[pltpu.VMEM((TM + HALO, 2 * DC), F32), pltpu.VMEM((HALO + TM, DC), F32),
                        pltpu.VMEM((HALO + TM, DC), F32), pltpu.VMEM((7, TM + HALO - 8, DC), F32),
                        pltpu.VMEM((7, TM + HALO - 8, DC), F32)],
        compiler_params=_params(1),
    )(gc, gc, pc, pc, dcc, sv, wdw, wsc)


def in_proj_bwd(dproj, w, h, g, dh_in, name):
    tp = h.shape[0]

    def body(dp_ref, w_ref, h_ref, g_ref, di_ref, dh_ref, dg_ref):
        @pl.when(pl.program_id(0) == 0)
        def _():
            dg_ref[...] = jnp.zeros_like(dg_ref)

        dhn = lax.dot_general(dp_ref[...], w_ref[...], NT, preferred_element_type=F32)
        dx, dg = _rmsnorm_bwd(h_ref[...], g_ref[...], dhn)
        dh_ref[...] = di_ref[...] + dx
        dg_ref[...] += dg

    rows = lambda n: pl.BlockSpec((TM, n), lambda i: (i, 0))
    return pl.pallas_call(
        body, name=name, grid=(tp // TM,),
        in_specs=[rows(NP_IN), _full((D, NP_IN)), rows(D), _full((1, D)), rows(D)],
        out_specs=[rows(D), _full((1, D))],
        out_shape=[S((tp, D), F32), S((1, D), F32)],
        compiler_params=_params(1),
    )(dproj, w, h, g, dh_in)


def adamw(recvs, w, m, v, rb, name):
    l_n, r_n, c_n = w.shape

    def body(*refs):
        p_refs = refs[:l_n]
        w_ref, m_ref, v_ref, g_ref, d_ref, m2_ref, v2_ref = refs[l_n:]
        for l in range(l_n):
            g = p_refs[l][0].astype(F32)
            for s in range(1, NDEV):
                g = g + p_refs[l][s].astype(F32)
            m2 = ADAM_B1 * m_ref[l] + (1.0 - ADAM_B1) * g
            v2 = ADAM_B2 * v_ref[l] + (1.0 - ADAM_B2) * (g * g)
            m_hat = m2 / (1.0 - ADAM_B1 ** ADAM_STEP)
            v_hat = v2 / (1.0 - ADAM_B2 ** ADAM_STEP)
            g_ref[l] = g
            d_ref[l] = -ADAM_LR * (m_hat / (jnp.sqrt(v_hat) + ADAM_EPS) + ADAM_WD * w_ref[l])
            m2_ref[l] = m2
            v2_ref[l] = v2

    blk = pl.BlockSpec((l_n, rb, c_n), lambda r: (0, r, 0))
    return pl.pallas_call(
        body, name=name, grid=(r_n // rb,),
        in_specs=[pl.BlockSpec((NDEV, rb, c_n), lambda r: (0, r, 0))] * l_n + [blk, blk, blk],
        out_specs=[blk] * 4, out_shape=[S(w.shape, F32)] * 4,
        compiler_params=_params(1),
    )(*recvs, w, m, v)


TINY_ROWS = 168
REP_ROWS = 64


def _pack(parts, rows):
    flat = jnp.concatenate([p.reshape(-1) for p in parts])
    return jnp.pad(flat, (0, rows * 128 - flat.shape[0])).reshape(rows, 128)


def _unpack(buf, shapes):
    flat, out, o = buf.reshape(-1), [], 0
    for s in shapes:
        n = 1
        for d in s:
            n *= d
        out.append(flat[o:o + n].reshape(s))
        o += n
    return out


TINY_SHAPES = [(DEPTH, CK, DC // NDEV), (DEPTH, SK, DC // NDEV), (NM, D // NDEV), (DEPTH, DC // NDEV, DC)]
REP_SHAPES = [(DEPTH, D), (DEPTH, H), (DEPTH, DC), (DEPTH, DC), (DEPTH, DC), (DEPTH, DC), (DEPTH, D), (D,)]


def _to_padded_cols(w):
    pad = jnp.zeros(w.shape[:-1] + (NP_IN - N_IN,), w.dtype)
    return jnp.concatenate([w[..., :C0], w[..., C0 + H:], w[..., C0:C0 + H], pad], axis=-1)


def _from_padded_cols(w):
    return jnp.concatenate([w[..., :C0], w[..., F0:F0 + H], w[..., C0:F0]], axis=-1)


def kernel(x, meta_tokens, mix_norm_g, w_in, b_forget, w_conf_dw, b_conf_dw, conf_ln_g, conf_ln_b, w_conf_pw, b_conf_pw, w_sc_conv, w_out, mlp_norm_g, w_mlp1, w_mlp2, final_norm_g, loss_target, m_meta_tokens, m_mix_norm_g, m_w_in, m_b_forget, m_w_conf_dw, m_b_conf_dw, m_conf_ln_g, m_conf_ln_b, m_w_conf_pw, m_b_conf_pw, m_w_sc_conv, m_w_out, m_mlp_norm_g, m_w_mlp1, m_w_mlp2, m_final_norm_g, v_meta_tokens, v_mix_norm_g, v_w_in, v_b_forget, v_w_conf_dw, v_b_conf_dw, v_conf_ln_g, v_conf_ln_b, v_w_conf_pw, v_b_conf_pw, v_w_sc_conv, v_w_out, v_mlp_norm_g, v_w_mlp1, v_w_mlp2, v_final_norm_g):
    seq = x.shape[1]
    t_real = NM + seq
    tp = -(-t_real // TM) * TM

    tiny_w = _pack([w_conf_dw, w_sc_conv, meta_tokens, w_conf_pw], TINY_ROWS)
    big = lambda l: [w_in[l].astype(MM), w_out[l].astype(MM), w_mlp1[l].astype(MM), w_mlp2[l].astype(MM)]
    *first, g_tiny = gather(big(0) + [tiny_w], "gather_weights0")
    gathered = {0: first}
    n_sh = w_in.shape[-1]
    assert w_mlp1.shape[-1] == FC and w_mlp2.shape[-2] == FC
    tiny = [_unpack(g_tiny[s], TINY_SHAPES) for s in range(NDEV)]
    wdw = jnp.concatenate([t[0] for t in tiny], axis=-1)
    wsc = jnp.concatenate([t[1] for t in tiny], axis=-1)
    meta = jnp.concatenate([t[2] for t in tiny], axis=-1)
    wpw = jnp.concatenate([t[3] for t in tiny], axis=1).astype(MM)
    wdw = jnp.pad(wdw, ((0, 0), (0, 32 - CK), (0, 0)))
    wsc = jnp.pad(wsc, ((0, 0), (0, 8 - SK), (0, 0)))
    bfp = jnp.pad(b_forget, ((0, 0), (0, 128 - H)))

    row = lambda a: a.reshape(1, -1)

    h = jnp.concatenate([meta, x[0], jnp.zeros((tp - t_real, D), F32)], axis=0)
    tgt = jnp.pad(loss_target[0], ((NM, tp - t_real), (0, 0)))
    saved = []
    for l in range(DEPTH):
        g_in, g_out, g_w1, g_w2 = gathered[l]
        win = _to_padded_cols(g_in.transpose(1, 0, 2).reshape(D, NDEV * n_sh))
        hn, qa, ka, va, pc, sg = in_proj(h, row(mix_norm_g[l]), win, row(bfp[l]), f"in_proj{l}")
        rider = AllToAll(big(l + 1), [False] * 4) if l + 1 < DEPTH else None
        o, lse, *arrived = attn_fwd(qa, ka, va, f"attn_fwd{l}", rider)
        if rider:
            gathered[l + 1] = arrived
        cc, sv = conv_fwd(pc, wdw[l], row(b_conf_dw[l]), row(conf_ln_g[l]), row(conf_ln_b[l]), wpw[l],
                          row(b_conf_pw[l]), wsc[l], f"conv_fwd{l}")
        h2, hn2 = out_proj(h, o, cc, g_out, row(mlp_norm_g[l]), f"out_proj{l}")
        r, z, h3 = mlp_fwd(hn2, h2, g_w1, g_w2, f"mlp_fwd{l}")
        saved.append((h, hn, qa, ka, va, pc, sg, o, lse, cc, sv, h2, hn2, r, z, win))
        h = h3

    dh, loss_part, d_gf = loss_head(h, tgt, row(final_norm_g), t_real, "loss_head")
    loss = lax.psum(loss_part[0, 0], ("x", "y", "c"))

    gw, recv, d_win = {}, {}, {}
    for l in reversed(range(DEPTH)):
        h0, hn, qa, ka, va, pc, sg, o, lse, cc, sv, h2, hn2, r, z, win = saved[l]
        _, g_out, g_w1, g_w2 = gathered[l]
        da, dh2, gw["mlp_g", l] = mlp_bwd(dh, r, g_w1, g_w2, h2, row(mlp_norm_g[l]), f"mlp_bwd{l}")
        d_w1 = matmul_tn(hn2, da, D, 2 * FC, f"dw_mlp1_{l}", out_dtype=MM, shard="n", sw=FC)
        d_w2 = matmul_tn(z, dh, 2 * FC, D, f"dw_mlp2_{l}", out_dtype=MM, shard="m", sw=FC)
        do, dcc = out_proj_bwd(dh2, g_out, f"out_proj_bwd{l}")
        d_wout = jnp.concatenate([matmul_tn(o, dh2, DA, D, f"dw_out_a{l}"), matmul_tn(cc, dh2, 2 * DC, D, f"dw_out_c{l}")],
                                 axis=0).reshape(NDEV, D // NDEV, D).astype(MM)
        qb, dob, qbt, dobt = attn_bwd_prep(qa, do, o, lse, f"attn_bwd_prep{l}")
        riding = [("wout", l, d_wout), ("w1", l, d_w1), ("w2", l, d_w2)] + ([("win", l + 1, d_win[l + 1])] if l + 1 < DEPTH else [])
        dq, dk, dv, dcq, dck, *arrived = attn_bwd(qb, dob, qbt, dobt, ka, va, f"attn_bwd{l}",
                                                  AllToAll([a for _, _, a in riding], [True] * len(riding)))
        for (k, kl, _), a in zip(riding, arrived):
            recv[k, kl] = a
        dz, gw["bf", l] = fgate_bwd(dcq, jnp.pad(dck.T, ((0, 0), (0, 128 - H))), sg, f"fgate_bwd{l}")
        gc, act, pg = conv_bwd_pointwise(dcc, sv, pc, row(conf_ln_g[l]), row(conf_ln_b[l]), wpw[l], f"conv_bwd_pw{l}")
        gw["wpw", l] = matmul_tn(act, dcc, DC, DC, f"dw_conf_pw{l}", n=DC)
        dpc, wg = conv_bwd_taps(gc, pc, dcc, sv, wdw[l], wsc[l], f"conv_bwd_taps{l}")
        gw["pg", l], gw["wg", l] = pg, wg
        dproj = jnp.concatenate([dq, dk, dv, dpc, dz], axis=1)
        d_win[l] = _from_padded_cols(matmul_tn(hn, dproj, 512, NP_IN, f"dw_in{l}")).reshape(D, NDEV, n_sh).transpose(
            1, 0, 2).astype(MM)
        dh, gw["mix_g", l] = in_proj_bwd(dproj, win, h0, row(mix_norm_g[l]), dh2, f"in_proj_bwd{l}")

    grad_x = dh[NM:t_real][None]
    stack = lambda k: jnp.stack([gw[k, l] for l in range(DEPTH)])

    d_wdw = stack("wg")[:, 0:CK].reshape(DEPTH, CK, NDEV, DC // NDEV).transpose(2, 0, 1, 3)
    d_wsc = stack("wg")[:, 32:32 + SK].reshape(DEPTH, SK, NDEV, DC // NDEV).transpose(2, 0, 1, 3)
    d_meta = dh[0:NM].reshape(NM, NDEV, D // NDEV).transpose(1, 0, 2)
    d_wpw = stack("wpw").reshape(DEPTH, NDEV, DC // NDEV, DC).transpose(1, 0, 2, 3)
    d_tiny = jnp.stack([_pack([d_wdw[p], d_wsc[p], d_meta[p], d_wpw[p]], TINY_ROWS) for p in range(NDEV)])
    pgs = stack("pg")
    d_rep = _pack([stack("mix_g").reshape(DEPTH, D), stack("bf")[:, 0, :H], pgs[:, 3], pgs[:, 1], pgs[:, 2], pgs[:, 0],
                   stack("mlp_g").reshape(DEPTH, D), d_gf.reshape(D)], REP_ROWS)
    recv["win", 0], r_tiny, r_rep = exchange([d_win[0], d_tiny, d_rep], [True, True, False], "exchange_grads")

    per_layer = lambda k: [recv[k, l] for l in range(DEPTH)]
    res = {}
    res["w_in"] = adamw(per_layer("win"), w_in, m_w_in, v_w_in, 256, "adamw_w_in")
    res["w_out"] = adamw(per_layer("wout"), w_out, m_w_out, v_w_out, D // NDEV, "adamw_w_out")
    res["w_mlp1"] = adamw(per_layer("w1"), w_mlp1, m_w_mlp1, v_w_mlp1, 256, "adamw_w_mlp1")
    res["w_mlp2"] = adamw(per_layer("w2"), w_mlp2, m_w_mlp2, v_w_mlp2, 128, "adamw_w_mlp2")
    tiny_names = ["w_conf_dw", "w_sc_conv", "meta_tokens", "w_conf_pw"]
    tiny_wmv = [[w_conf_dw, w_sc_conv, meta_tokens, w_conf_pw], [m_w_conf_dw, m_w_sc_conv, m_meta_tokens, m_w_conf_pw],
                [v_w_conf_dw, v_w_sc_conv, v_meta_tokens, v_w_conf_pw]]
    rep_names = ["mix_norm_g", "b_forget", "b_conf_dw", "conf_ln_g", "conf_ln_b", "b_conf_pw", "mlp_norm_g", "final_norm_g"]
    rep_wmv = [[mix_norm_g, b_forget, b_conf_dw, conf_ln_g, conf_ln_b, b_conf_pw, mlp_norm_g, final_norm_g],
               [m_mix_norm_g, m_b_forget, m_b_conf_dw, m_conf_ln_g, m_conf_ln_b, m_b_conf_pw, m_mlp_norm_g, m_final_norm_g],
               [v_mix_norm_g, v_b_forget, v_b_conf_dw, v_conf_ln_g, v_conf_ln_b, v_b_conf_pw, v_mlp_norm_g, v_final_norm_g]]
    for names, wmv, shapes, recv_pack, rows_n, nm in ((tiny_names, tiny_wmv, TINY_SHAPES, r_tiny, TINY_ROWS, "adamw_tiny"),
                                                 (rep_names, rep_wmv, REP_SHAPES, r_rep, REP_ROWS, "adamw_rep")):
        packed = [_pack(group, rows_n)[None] for group in wmv]
        outs = adamw([recv_pack], *packed, rows_n, nm)
        parts = [_unpack(a[0], shapes) for a in outs]
        for k, n in enumerate(names):
            res[n] = [parts[q][k] for q in range(4)]

    order = ["meta_tokens", "mix_norm_g", "w_in", "b_forget", "w_conf_dw", "b_conf_dw", "conf_ln_g", "conf_ln_b",
             "w_conf_pw", "b_conf_pw", "w_sc_conv", "w_out", "mlp_norm_g", "w_mlp1", "w_mlp2", "final_norm_g"]
    return (loss, grad_x, *[res[n][0] for n in order], *[res[n][1] for n in order],
            *[res[n][2] for n in order], *[res[n][3] for n in order])
```

```python
import functools

import jax
import jax.numpy as jnp
from jax import lax
from jax.experimental import pallas as pl
from jax.experimental.pallas import tpu as pltpu

F32 = jnp.float32
MM = jnp.bfloat16

D = 1024
H = 8
DH = 64
DA = H * DH
HT = 128
DC = 256
NM = 16
CK = 31
SK = 3
DFF = 4096
DEPTH = 2
N_IN = 3 * DA + H + 2 * DC + 3 * DC
NP_IN = 3 * DA + 5 * DC + 128
C0 = 3 * DA
F0 = 3 * DA + 5 * DC
EPS = 1e-6
TM = 640
HALO = 32
FC = 512
FS = 2
NDEV = 8
SCALE = DH ** -0.5
NEG = -1e30

ADAM_LR, ADAM_B1, ADAM_B2, ADAM_EPS, ADAM_WD, ADAM_STEP = 0.001, 0.9, 0.999, 1e-08, 0.01, 10

VMEM_LIMIT = 56 * 1024 * 1024

S = jax.ShapeDtypeStruct
NT = (((1,), (1,)), ((), ()))
TN = (((0,), (0,)), ((), ()))


def _params(n_grid):
    return pltpu.CompilerParams(dimension_semantics=("arbitrary",) * n_grid, vmem_limit_bytes=VMEM_LIMIT)


def _sigmoid(x):
    return 1.0 / (1.0 + jnp.exp(-x))


def _full(shape):
    n = len(shape)
    return pl.BlockSpec(shape, lambda *_: (0,) * n)


def _lane_put(dst, col, h):
    lane = lax.broadcasted_iota(jnp.int32, dst.shape, 1)
    return jnp.where(lane == h, col, dst)


ANY = pl.BlockSpec(memory_space=pl.ANY)


class AllToAll:
    def __init__(self, srcs, per_peer):
        self.srcs, self.per_peer, self.k_n = list(srcs), list(per_peer), len(srcs)
        self.out_shape = [S((NDEV,) + (a.shape[1:] if pp else a.shape), a.dtype) for a, pp in zip(srcs, per_peer)]
        self.scratch = [pltpu.SemaphoreType.DMA((self.k_n, NDEV - 1)), pltpu.SemaphoreType.DMA((self.k_n, NDEV - 1)),
                        pltpu.SemaphoreType.DMA((self.k_n,))]

    def copies(self, src, out, send_sems, recv_sems, local_sems):
        x, y, c = lax.axis_index("x"), lax.axis_index("y"), lax.axis_index("c")
        me = 4 * x + 2 * y + c

        def piece(k, p):
            return src[k].at[p] if self.per_peer[k] else src[k]

        local = [pltpu.make_async_copy(piece(k, me), out[k].at[me], local_sems.at[k]) for k in range(self.k_n)]
        sends, recvs = [], []
        for r in range(1, NDEV):
            px = 1 - x if (r >> 2) & 1 else x
            py = 1 - y if (r >> 1) & 1 else y
            pc = 1 - c if r & 1 else c
            pidx = 4 * px + 2 * py + pc
            for k in range(self.k_n):
                sends.append(pltpu.make_async_remote_copy(
                    src_ref=piece(k, pidx), dst_ref=out[k].at[me],
                    send_sem=send_sems.at[k, r - 1], recv_sem=recv_sems.at[k, r - 1],
                    device_id=(px, py, pc), device_id_type=pl.DeviceIdType.MESH))
                recvs.append(pltpu.make_async_remote_copy(
                    src_ref=piece(k, pidx), dst_ref=out[k].at[pidx],
                    send_sem=send_sems.at[k, r - 1], recv_sem=recv_sems.at[k, r - 1],
                    device_id=(px, py, pc), device_id_type=pl.DeviceIdType.MESH))

        def start():
            for cp in local + sends:
                cp.start()

        def wait():
            for cp in recvs:
                cp.wait_recv()
            for cp in sends:
                cp.wait_send()
            for cp in local:
                cp.wait()

        return start, wait


def exchange(srcs, per_peer, name):
    plan = AllToAll(srcs, per_peer)

    def body(*refs):
        start, wait = plan.copies(refs[:plan.k_n], refs[plan.k_n:2 * plan.k_n], *refs[2 * plan.k_n:])
        start()
        wait()

    return pl.pallas_call(
        body, name=name, out_shape=plan.out_shape, in_specs=[ANY] * plan.k_n, out_specs=[ANY] * plan.k_n,
        scratch_shapes=plan.scratch,
    )(*srcs)


def gather(srcs, name):
    k_n = len(srcs)
    out_shape = [S((NDEV,) + a.shape, a.dtype) for a in srcs]

    def body(*refs):
        src, out = refs[:k_n], refs[k_n:2 * k_n]
        send_sems, recv_sems, local_sems = refs[2 * k_n:]
        x, y, c = lax.axis_index("x"), lax.axis_index("y"), lax.axis_index("c")
        me, sibling = (x, y, c), (x, y, 1 - c)
        chips = [(1 - x, y), (x, 1 - y), (1 - x, 1 - y)]

        def slot(k, dev):
            return out[k].at[4 * dev[0] + 2 * dev[1] + dev[2]]

        def copy(k, r, block, to, from_src=False):
            return pltpu.make_async_remote_copy(
                src_ref=src[k] if from_src else slot(k, block), dst_ref=slot(k, block),
                send_sem=send_sems.at[k, r], recv_sem=recv_sems.at[k, r],
                device_id=to, device_id_type=pl.DeviceIdType.MESH)

        local = [pltpu.make_async_copy(src[k], slot(k, me), local_sems.at[k]) for k in range(k_n)]
        first = [copy(k, 0, me, sibling, True) for k in range(k_n)]
        first += [copy(k, 1 + n, me, (*chip, c), True) for n, chip in enumerate(chips) for k in range(k_n)]
        for cp in local + first:
            cp.start()
        passed = []
        for n, chip in enumerate(chips):
            for k in range(k_n):
                copy(k, 1 + n, (*chip, c), me).wait_recv()
                passed.append(copy(k, 4 + n, (*chip, c), sibling))
                passed[-1].start()
        for k in range(k_n):
            copy(k, 0, sibling, me).wait_recv()
            for n, chip in enumerate(chips):
                copy(k, 4 + n, (*chip, 1 - c), me).wait_recv()
        for cp in first + passed:
            cp.wait_send()
        for cp in local:
            cp.wait()

    any_spec = pl.BlockSpec(memory_space=pl.ANY)
    return pl.pallas_call(
        body, name=name, out_shape=out_shape,
        in_specs=[any_spec] * k_n, out_specs=[any_spec] * k_n,
        scratch_shapes=[pltpu.SemaphoreType.DMA((k_n, NDEV - 1)), pltpu.SemaphoreType.DMA((k_n, NDEV - 1)),
                        pltpu.SemaphoreType.DMA((k_n,))],
    )(*srcs)


def in_proj(h, g, w, bf, name):
    tp = h.shape[0]

    def body(h_ref, g_ref, w_ref, bf_ref, hn_ref, qa_ref, ka_ref, va_ref, pc_ref, sg_ref, carry):
        i = pl.program_id(0)

        @pl.when(i == 0)
        def _():
            carry[...] = jnp.zeros_like(carry)

        x = h_ref[...]
        r = lax.rsqrt(jnp.mean(x * x, axis=-1, keepdims=True) + EPS)
        hn = (x * r * g_ref[...]).astype(MM)
        hn_ref[...] = hn
        pc_ref[...] = jnp.dot(hn, w_ref[:, C0:F0], preferred_element_type=F32)
        z = jnp.dot(hn, w_ref[:, F0:NP_IN], preferred_element_type=F32) + bf_ref[...]
        lane = lax.broadcasted_iota(jnp.int32, z.shape, 1)
        logf = jnp.where(lane < H, jnp.minimum(z, 0.0) - jnp.log(1.0 + jnp.exp(-jnp.abs(z))), 0.0)
        sg_ref[...] = 1.0 / (1.0 + jnp.exp(z))
        row = lax.broadcasted_iota(jnp.int32, (TM, TM), 0)
        col = lax.broadcasted_iota(jnp.int32, (TM, TM), 1)
        tri = (row >= col).astype(F32)
        c = jnp.dot(tri, logf, precision=lax.Precision.HIGHEST, preferred_element_type=F32) + carry[...]
        carry[...] = c[TM - 1:TM, :]
        qkv = jnp.dot(hn, w_ref[:, 0:C0], preferred_element_type=F32)
        one = [1.0, 1.0, 1.0]
        for hd in range(H):
            t = slice(hd * HT, (hd + 1) * HT)
            cs = list(_split3(c[:, hd:hd + 1]))
            qa_ref[:, t] = _aug_tile(hd, qkv[:, hd * DH:(hd + 1) * DH] * SCALE, cs + one).astype(MM)
            ka_ref[:, t] = _aug_tile(hd, qkv[:, DA + hd * DH:DA + (hd + 1) * DH], one + [-v for v in cs] + one).astype(MM)
            va_ref[:, t] = _aug_tile(hd, qkv[:, 2 * DA + hd * DH:2 * DA + (hd + 1) * DH], [1.0] + one).astype(MM)

    rows = lambda n: pl.BlockSpec((TM, n), lambda i: (i, 0))
    return pl.pallas_call(
        body, name=name, grid=(tp // TM,),
        in_specs=[rows(D), _full((1, D)), _full((D, NP_IN)), _full((1, 128))],
        out_specs=[rows(D), rows(H * HT), rows(H * HT), rows(H * HT), rows(5 * DC), rows(128)],
        out_shape=[S((tp, D), MM), S((tp, H * HT), MM), S((tp, H * HT), MM), S((tp, H * HT), MM),
                   S((tp, 5 * DC), F32), S((tp, 128), F32)],
        scratch_shapes=[pltpu.VMEM((1, 128), F32)],
        compiler_params=_params(1),
    )(h, g, w, bf)


def _split3(c):
    hi = c.astype(MM).astype(F32)
    mid = (c - hi).astype(MM).astype(F32)
    lo = (c - hi - mid).astype(MM).astype(F32)
    return hi, mid, lo


def _main(hd):
    return slice(0, DH) if hd % 2 == 0 else slice(DH, HT)


def _aug_lane(hd, pos):
    return pos + (DH if hd % 2 == 0 else 0)


def _aug_tile(hd, main, cols):
    lane = lax.broadcasted_iota(jnp.int32, main.shape, 1)
    aug = jnp.zeros(main.shape, F32)
    for pos, val in enumerate(cols):
        aug = jnp.where(lane == pos, val, aug)
    return jnp.concatenate([main, aug] if hd % 2 == 0 else [aug, main], axis=1)


def _merge_pairs(tiles):
    lane = lax.broadcasted_iota(jnp.int32, tiles[0].shape, 1)
    return jnp.concatenate([jnp.where(lane < DH, tiles[2 * m], tiles[2 * m + 1]) for m in range(H // 2)], axis=1)


def _causal_mask():
    return lax.broadcasted_iota(jnp.int32, (TM, TM), 0) >= lax.broadcasted_iota(jnp.int32, (TM, TM), 1)


def _split_rider(refs, n_in, n_out, rider):
    if rider is None:
        return refs, None
    k = rider.k_n
    own = refs[:n_in] + refs[n_in + k:n_in + k + n_out] + refs[n_in + 2 * k + n_out:-3]
    return own, rider.copies(refs[n_in:n_in + k], refs[n_in + k + n_out:n_in + 2 * k + n_out], *refs[-3:])


def attn_fwd(qa, ka, va, name, rider=None):
    tp = qa.shape[0]
    nb = tp // TM
    pairs = [(i, j) for i in range(nb) for j in range(i + 1)]
    i_tab, j_tab = (jnp.asarray([p[a] for p in pairs], jnp.int32) for a in (0, 1))

    def body(i_ref, j_ref, *refs):
        (q_ref, k_ref, v_ref, o_ref, lse_ref, *scr), copies = _split_rider(refs, 3, 2, rider)
        m_scr, acc_scr, bias = scr[:H], scr[H:2 * H], scr[2 * H]
        n = pl.program_id(0)
        i, j = i_ref[n], j_ref[n]

        @pl.when(n == 0)
        def _():
            if copies:
                copies[0]()
            bias[...] = jnp.where(_causal_mask(), 0.0, NEG)

        @pl.when(j == 0)
        def _():
            for hd in range(H):
                m_scr[hd][...] = jnp.full((TM, 1), NEG, F32)
                acc_scr[hd][...] = jnp.zeros((TM, HT), F32)

        def step(diag):
            def logits(hd):
                t = slice(hd * HT, (hd + 1) * HT)
                s = lax.dot_general(q_ref[:, t], k_ref[:, t], NT, preferred_element_type=F32)
                return s + bias[...] if diag else s

            s_next = logits(0)
            for hd in range(H):
                t = slice(hd * HT, (hd + 1) * HT)
                s = s_next
                if hd + 1 < H:
                    s_next = logits(hd + 1)
                m_prev = m_scr[hd][...]
                m_new = jnp.maximum(m_prev, jnp.max(s, axis=1, keepdims=True))
                p = jnp.exp(s - m_new).astype(MM)
                acc_scr[hd][...] = (jnp.exp(m_prev - m_new) * acc_scr[hd][...]
                                    + jnp.dot(p, v_ref[:, t], preferred_element_type=F32))
                m_scr[hd][...] = m_new

        @pl.when(j < i)
        def _():
            step(False)

        @pl.when(j == i)
        def _():
            step(True)

        @pl.when(j == i)
        def _():
            lse = jnp.zeros((TM, 128), F32)
            outs = []
            for hd in range(H):
                acc = acc_scr[hd][...]
                l = acc[:, _aug_lane(hd, 0):_aug_lane(hd, 0) + 1]
                outs.append(acc * (1.0 / l))
                lse = _lane_put(lse, m_scr[hd][...] + jnp.log(l), hd)
            o_ref[...] = _merge_pairs(outs).astype(MM)
            lse_ref[...] = lse

        if copies:
            @pl.when(n == len(pairs) - 1)
            def _():
                copies[1]()

    qi = lambda w: pl.BlockSpec((TM, w), lambda n, it, jt: (it[n], 0))
    kv = pl.BlockSpec((TM, H * HT), lambda n, it, jt: (jt[n], 0))
    r_n = rider.k_n if rider else 0
    return pl.pallas_call(
        body, name=name,
        grid_spec=pltpu.PrefetchScalarGridSpec(
            num_scalar_prefetch=2, grid=(len(pairs),),
            in_specs=[qi(H * HT), kv, kv] + [ANY] * r_n, out_specs=[qi(DA), qi(128)] + [ANY] * r_n,
            scratch_shapes=[pltpu.VMEM((TM, 1), F32)] * H + [pltpu.VMEM((TM, HT), F32)] * H + [pltpu.VMEM((TM, TM), F32)]
            + (rider.scratch if rider else [])),
        out_shape=[S((tp, DA), MM), S((tp, 128), F32)] + (rider.out_shape if rider else []),
        compiler_params=_params(1),
    )(i_tab, j_tab, qa, ka, va, *(rider.srcs if rider else []))


def _layernorm_parts(dw, g, b):
    mu = jnp.mean(dw, axis=-1, keepdims=True)
    xc = dw - mu
    rs = lax.rsqrt(jnp.mean(xc * xc, axis=-1, keepdims=True) + EPS)
    xh = xc * rs
    return xh, rs, xh * g + b


def _fill_halo(ext, halo, cur, first):
    ext[0:HALO, :] = jnp.where(first, 0.0, halo)
    ext[HALO:HALO + TM, :] = cur


def _shift_copies(dst, src, lanes=slice(None)):
    for r in range(1, 8):
        dst[r - 1] = src[pl.ds(r, dst.shape[1]), lanes]


def _window(src, shifted, off, lanes=slice(None)):
    if off % 8 == 0:
        return src[pl.ds(off, TM), lanes]
    return shifted[off % 8 - 1, pl.ds(off - off % 8, TM), :]


def conv_fwd(pc, wdw, bdw, lng, lnb, wpw, bpw, wsc, name):
    tp = pc.shape[0]

    def body(pc_ref, hl_ref, wdw_ref, bdw_ref, lng_ref, lnb_ref, wpw_ref, bpw_ref, wsc_ref, cc_ref, sv_ref, xe, ce, xs):
        first = pl.program_id(0) == 0
        glu = pc_ref[:, 0:DC] * _sigmoid(pc_ref[:, DC:2 * DC])
        cu = pc_ref[:, 3 * DC:4 * DC] * pc_ref[:, 4 * DC:5 * DC]
        _fill_halo(xe, hl_ref[:, 0:DC] * _sigmoid(hl_ref[:, DC:2 * DC]), glu, first)
        _fill_halo(ce, hl_ref[:, 3 * DC:4 * DC] * hl_ref[:, 4 * DC:5 * DC], cu, first)
        _shift_copies(xs, xe)
        dw = jnp.zeros((TM, DC), F32) + bdw_ref[...]
        for k in range(CK):
            dw = dw + wdw_ref[k:k + 1, :] * _window(xe, xs, HALO - (CK - 1) + k)
        cv = jnp.zeros((TM, DC), F32)
        for k in range(SK):
            cv = cv + wsc_ref[k:k + 1, :] * ce[pl.ds(HALO - (SK - 1) + k, TM), :]
        _, _, ln = _layernorm_parts(dw, lng_ref[...], lnb_ref[...])
        act = ln * _sigmoid(ln)
        conf = jnp.dot(act.astype(MM), wpw_ref[...], preferred_element_type=F32) + bpw_ref[...]
        cc_ref[:, 0:DC] = conf.astype(MM)
        cc_ref[:, DC:2 * DC] = (pc_ref[:, 2 * DC:3 * DC] * cv).astype(MM)
        sv_ref[:, 0:DC] = dw
        sv_ref[:, DC:2 * DC] = cv

    hb = TM // HALO
    rows = lambda n: pl.BlockSpec((TM, n), lambda i: (i, 0))
    return pl.pallas_call(
        body, name=name, grid=(tp // TM,),
        in_specs=[rows(5 * DC), pl.BlockSpec((HALO, 5 * DC), lambda i: (jnp.maximum(i * hb - 1, 0), 0)),
                  _full((32, DC)), _full((1, DC)), _full((1, DC)), _full((1, DC)), _full((DC, DC)), _full((1, DC)),
                  _full((8, DC))],
        out_specs=[rows(2 * DC), rows(2 * DC)],
        out_shape=[S((tp, 2 * DC), MM), S((tp, 2 * DC), F32)],
        scratch_shapes=[pltpu.VMEM((HALO + TM, DC), F32), pltpu.VMEM((HALO + TM, DC), F32),
                        pltpu.VMEM((7, TM + HALO - 8, DC), F32)],
        compiler_params=_params(1),
    )(pc, pc, wdw, bdw, lng, lnb, wpw, bpw, wsc)


def out_proj(h, o, cc, wo, g2, name):
    tp = h.shape[0]

    def body(h_ref, o_ref, cc_ref, wo_ref, g_ref, h2_ref, hn_ref):
        wo = wo_ref[...].reshape(D, D)
        h2 = (h_ref[...] + jnp.dot(o_ref[...], wo[0:DA, :], preferred_element_type=F32)
              + jnp.dot(cc_ref[...], wo[DA:D, :], preferred_element_type=F32))
        h2_ref[...] = h2
        r = lax.rsqrt(jnp.mean(h2 * h2, axis=-1, keepdims=True) + EPS)
        hn_ref[...] = (h2 * r * g_ref[...]).astype(MM)

    rows = lambda n: pl.BlockSpec((TM, n), lambda i: (i, 0))
    return pl.pallas_call(
        body, name=name, grid=(tp // TM,),
        in_specs=[rows(D), rows(DA), rows(2 * DC), _full((NDEV, D // NDEV, D)), _full((1, D))],
        out_specs=[rows(D), rows(D)],
        out_shape=[S((tp, D), F32), S((tp, D), MM)],
        compiler_params=_params(1),
    )(h, o, cc, wo, g2)


def mlp_fwd(hn, h2, w1, w2, name):
    tp = hn.shape[0]
    nf = DFF // (FS * FC)

    def body(hn_ref, h2_ref, w1_ref, w2_ref, r_ref, z_ref, h3_ref, acc):
        j = pl.program_id(1)

        @pl.when(j == 0)
        def _():
            acc[...] = jnp.zeros_like(acc)

        for s in range(FS):
            cols = slice(s * FC, (s + 1) * FC)
            r = jnp.maximum(jnp.dot(hn_ref[...], w1_ref[s], preferred_element_type=F32), 0.0)
            zb = (r * r).astype(MM)
            r_ref[:, cols] = r.astype(MM)
            z_ref[:, cols] = zb
            acc[...] += jnp.dot(zb, w2_ref[s], preferred_element_type=F32)

        @pl.when(j == nf - 1)
        def _():
            h3_ref[...] = h2_ref[...] + acc[...]

    return pl.pallas_call(
        body, name=name, grid=(tp // TM, nf),
        in_specs=[pl.BlockSpec((TM, D), lambda i, j: (i, 0)), pl.BlockSpec((TM, D), lambda i, j: (i, 0)),
                  pl.BlockSpec((FS, D, FC), lambda i, j: (j, 0, 0)), pl.BlockSpec((FS, FC, D), lambda i, j: (j, 0, 0))],
        out_specs=[pl.BlockSpec((TM, FS * FC), lambda i, j: (i, j)), pl.BlockSpec((TM, FS * FC), lambda i, j: (i, j)),
                   pl.BlockSpec((TM, D), lambda i, j: (i, 0))],
        out_shape=[S((tp, DFF), MM), S((tp, DFF), MM), S((tp, D), F32)],
        scratch_shapes=[pltpu.VMEM((TM, D), F32)],
        compiler_params=_params(2),
    )(hn, h2, w1, w2)


def loss_head(h, tgt, g, t_real, name):
    tp = h.shape[0]

    def body(h_ref, t_ref, g_ref, dh_ref, loss_ref, dg_ref):
        i = pl.program_id(0)

        @pl.when(i == 0)
        def _():
            loss_ref[...] = jnp.zeros_like(loss_ref)
            dg_ref[...] = jnp.zeros_like(dg_ref)

        x = h_ref[...]
        gg = g_ref[...]
        r = lax.rsqrt(jnp.mean(x * x, axis=-1, keepdims=True) + EPS)
        xn = x * r
        row = i * TM + lax.broadcasted_iota(jnp.int32, (TM, 1), 0)
        e = jnp.where((row >= NM) & (row < t_real), xn * gg - t_ref[...], 0.0)
        loss_ref[...] += jnp.sum(e * e) * (0.5 / D)
        dy = e * (1.0 / D)
        dg_ref[...] += jnp.sum(dy * xn, axis=0, keepdims=True)
        u = dy * gg
        dh_ref[...] = r * (u - xn * jnp.mean(u * xn, axis=-1, keepdims=True))

    rows = lambda n: pl.BlockSpec((TM, n), lambda i: (i, 0))
    return pl.pallas_call(
        body, name=name, grid=(tp // TM,),
        in_specs=[rows(D), rows(D), _full((1, D))],
        out_specs=[rows(D), _full((8, 128)), _full((1, D))],
        out_shape=[S((tp, D), F32), S((8, 128), F32), S((1, D), F32)],
        compiler_params=_params(1),
    )(h, tgt, g)


def _rmsnorm_bwd(x, g, dy):
    r = lax.rsqrt(jnp.mean(x * x, axis=-1, keepdims=True) + EPS)
    xn = x * r
    u = dy * g
    dx = r * (u - xn * jnp.mean(u * xn, axis=-1, keepdims=True))
    return dx, jnp.sum(dy * xn, axis=0, keepdims=True)


def matmul_tn(a, b, tm, tn, name, n=None, out_dtype=F32, shard=None, sw=None):
    tp, m = a.shape
    n = b.shape[1] if n is None else n
    nk = tp // TM

    def body(a_ref, b_ref, o_ref, acc):
        k = pl.program_id(2)

        @pl.when(k == 0)
        def _():
            acc[...] = jnp.zeros_like(acc)

        acc[...] += lax.dot_general(a_ref[...].astype(MM), b_ref[...].astype(MM), TN, preferred_element_type=F32)

        @pl.when(k == nk - 1)
        def _():
            if shard is None:
                o_ref[...] = acc[...].astype(out_dtype)
            elif shard == "m":
                for s in range(tm // sw):
                    o_ref[s] = acc[s * sw:(s + 1) * sw, :].astype(out_dtype)
            else:
                for s in range(tn // sw):
                    o_ref[s] = acc[:, s * sw:(s + 1) * sw].astype(out_dtype)

    if shard is None:
        out_spec, out_shape = pl.BlockSpec((tm, tn), lambda mi, ni, k: (mi, ni)), (m, n)
    elif shard == "m":
        out_spec, out_shape = pl.BlockSpec((tm // sw, sw, tn), lambda mi, ni, k: (mi, 0, ni)), (m // sw, sw, n)
    else:
        out_spec, out_shape = pl.BlockSpec((tn // sw, tm, sw), lambda mi, ni, k: (ni, mi, 0)), (n // sw, m, sw)
    return pl.pallas_call(
        body, name=name, grid=(m // tm, n // tn, nk),
        in_specs=[pl.BlockSpec((TM, tm), lambda mi, ni, k: (k, mi)), pl.BlockSpec((TM, tn), lambda mi, ni, k: (k, ni))],
        out_specs=out_spec, out_shape=S(out_shape, out_dtype),
        scratch_shapes=[pltpu.VMEM((tm, tn), F32)],
        compiler_params=_params(3),
    )(a, b)


def mlp_bwd(dh3, r, w1, w2, h2, g, name):
    tp = dh3.shape[0]
    nf = DFF // (FS * FC)

    def body(dh3_ref, r_ref, w1_ref, w2_ref, h2_ref, g_ref, da_ref, dh2_ref, dg_ref, acc, dhb):
        i, j = pl.program_id(0), pl.program_id(1)

        @pl.when((i == 0) & (j == 0))
        def _():
            dg_ref[...] = jnp.zeros_like(dg_ref)

        @pl.when(j == 0)
        def _():
            acc[...] = jnp.zeros_like(acc)
            dhb[...] = dh3_ref[...].astype(MM)

        for s in range(FS):
            cols = slice(s * FC, (s + 1) * FC)
            dz = lax.dot_general(dhb[...], w2_ref[s], NT, preferred_element_type=F32)
            da = (dz * (2.0 * r_ref[:, cols].astype(F32))).astype(MM)
            da_ref[:, cols] = da
            acc[...] += lax.dot_general(da, w1_ref[s], NT, preferred_element_type=F32)

        @pl.when(j == nf - 1)
        def _():
            dx, dg = _rmsnorm_bwd(h2_ref[...], g_ref[...], acc[...])
            dh2_ref[...] = dh3_ref[...] + dx
            dg_ref[...] += dg

    return pl.pallas_call(
        body, name=name, grid=(tp // TM, nf),
        in_specs=[pl.BlockSpec((TM, D), lambda i, j: (i, 0)), pl.BlockSpec((TM, FS * FC), lambda i, j: (i, j)),
                  pl.BlockSpec((FS, D, FC), lambda i, j: (j, 0, 0)), pl.BlockSpec((FS, FC, D), lambda i, j: (j, 0, 0)),
                  pl.BlockSpec((TM, D), lambda i, j: (i, 0)), _full((1, D))],
        out_specs=[pl.BlockSpec((TM, FS * FC), lambda i, j: (i, j)), pl.BlockSpec((TM, D), lambda i, j: (i, 0)),
                   _full((1, D))],
        out_shape=[S((tp, DFF), MM), S((tp, D), F32), S((1, D), F32)],
        scratch_shapes=[pltpu.VMEM((TM, D), F32), pltpu.VMEM((TM, D), MM)],
        compiler_params=_params(2),
    )(dh3, r, w1, w2, h2, g)


def out_proj_bwd(dh2, wo, name):
    tp = dh2.shape[0]

    def body(dh_ref, wo_ref, do_ref, dcc_ref):
        dcat = lax.dot_general(dh_ref[...].astype(MM), wo_ref[...].reshape(D, D), NT, preferred_element_type=F32)
        do_ref[...] = dcat[:, 0:DA].astype(MM)
        dcc_ref[...] = dcat[:, DA:D]

    rows = lambda n: pl.BlockSpec((TM, n), lambda i: (i, 0))
    return pl.pallas_call(
        body, name=name, grid=(tp // TM,),
        in_specs=[rows(D), _full((NDEV, D // NDEV, D))],
        out_specs=[rows(DA), rows(2 * DC)],
        out_shape=[S((tp, DA), MM), S((tp, 2 * DC), F32)],
        compiler_params=_params(1),
    )(dh2, wo)


def attn_bwd_prep(qa, do, o, lse, name):
    tp = qa.shape[0]

    def body(q_ref, do_ref, o_ref, lse_ref, qb_ref, dob_ref, qbt_ref, dobt_ref):
        lse = lse_ref[...]
        lane = lax.broadcasted_iota(jnp.int32, (TM, HT), 1)
        for hd in range(H):
            t, hs = slice(hd * HT, (hd + 1) * HT), slice(hd * DH, (hd + 1) * DH)
            dof = do_ref[:, hs].astype(F32)
            dd = jnp.sum(dof * o_ref[:, hs].astype(F32), axis=1, keepdims=True)
            dob = _aug_tile(hd, dof, [0.0] + [-v for v in _split3(dd)])
            qb = q_ref[:, t].astype(F32)
            for pos, v in enumerate(_split3(lse[:, hd:hd + 1])):
                qb = jnp.where(lane == _aug_lane(hd, 6 + pos), -v, qb)
            qb_ref[:, t] = qb.astype(MM)
            dob_ref[:, t] = dob.astype(MM)
            qbt_ref[t, :] = qb.T.astype(MM)
            dobt_ref[t, :] = dob.T.astype(MM)

    rows = lambda n: pl.BlockSpec((TM, n), lambda i: (i, 0))
    cols = pl.BlockSpec((H * HT, TM), lambda i: (0, i))
    return pl.pallas_call(
        body, name=name, grid=(tp // TM,),
        in_specs=[rows(H * HT), rows(DA), rows(DA), rows(128)],
        out_specs=[rows(H * HT), rows(H * HT), cols, cols],
        out_shape=[S((tp, H * HT), MM), S((tp, H * HT), MM), S((H * HT, tp), MM), S((H * HT, tp), MM)],
        compiler_params=_params(1),
    )(qa, do, o, lse)


def attn_bwd(qb, dob, qbt, dobt, ka, va, name, rider=None):
    tp = qb.shape[0]
    nb = tp // TM

    pairs = [(j, i) for j in range(nb) for i in range(j, nb)]
    j_tab, i_tab = (jnp.asarray([p[a] for p in pairs], jnp.int32) for a in (0, 1))

    def body(j_ref, i_ref, *refs):
        own, copies = _split_rider(refs, 6, 5, rider)
        (qb_ref, dob_ref, qbt_ref, dobt_ref, k_ref, v_ref, dq_ref, dk_ref, dv_ref, dcq_ref, dck_ref,
         dq_acc, dcq_acc, dkt_acc, dvt_acc, dck_acc, stage, sems) = own
        n = pl.program_id(0)
        j, i = j_ref[n], i_ref[n]

        @pl.when(n == 0)
        def _():
            if copies:
                copies[0]()
            dq_acc[...] = jnp.zeros_like(dq_acc)
            dcq_acc[...] = jnp.zeros_like(dcq_acc)

        @pl.when(i == j)
        def _():
            dkt_acc[...] = jnp.zeros_like(dkt_acc)
            dvt_acc[...] = jnp.zeros_like(dvt_acc)
            dck_acc[...] = jnp.zeros_like(dck_acc)

        def step(diag):
            rows = pl.ds(pl.multiple_of(i * TM, TM), TM)
            dcq = dcq_acc[rows, :]
            lane = lax.broadcasted_iota(jnp.int32, (TM, HT), 1)
            mask = _causal_mask() if diag else None
            for hd in range(H):
                t = slice(hd * HT, (hd + 1) * HT)
                s = lax.dot_general(qb_ref[:, t], k_ref[:, t], NT, preferred_element_type=F32)
                dp = lax.dot_general(dob_ref[:, t], v_ref[:, t], NT, preferred_element_type=F32)
                if diag:
                    s = jnp.where(mask, s, NEG)
                p = jnp.exp(s)
                ds = p * dp
                dsb = ds.astype(MM)
                dvt_acc[t, :] += jnp.dot(dobt_ref[t, :], p.astype(MM), preferred_element_type=F32)
                dkt_acc[t, :] += jnp.dot(qbt_ref[t, :], dsb, preferred_element_type=F32)
                dqh = jnp.dot(dsb, k_ref[:, t], preferred_element_type=F32)
                if hd % 2 == 0:
                    dq_even = dqh
                else:
                    pair = slice((hd // 2) * HT, (hd // 2 + 1) * HT)
                    dq_acc[rows, pair] += jnp.where(lane < DH, dq_even, dqh)
                dcq = dcq + jnp.where(lane == hd, jnp.sum(ds, axis=1, keepdims=True), 0.0)
                dck_acc[hd:hd + 1, :] += jnp.sum(ds, axis=0, keepdims=True)
            dcq_acc[rows, :] = dcq
            return rows

        @pl.when(i > j)
        def _():
            step(False)

        @pl.when(i == j)
        def _():
            rows = step(True)
            stage[...] = (dq_acc[rows, :] * SCALE).astype(MM)
            out = [pltpu.make_async_copy(stage, dq_ref.at[rows, :], sems.at[0]),
                   pltpu.make_async_copy(dcq_acc.at[rows, :], dcq_ref.at[rows, :], sems.at[1])]
            for cp in out:
                cp.start()
            for cp in out:
                cp.wait()

        @pl.when(i == nb - 1)
        def _():
            dk_ref[...] = _merge_pairs([dkt_acc[hd * HT:(hd + 1) * HT, :].T for hd in range(H)]).astype(MM)
            dv_ref[...] = _merge_pairs([dvt_acc[hd * HT:(hd + 1) * HT, :].T for hd in range(H)]).astype(MM)
            dck_ref[...] = dck_acc[...]

        if copies:
            @pl.when(n == len(pairs) - 1)
            def _():
                copies[1]()

    qi = lambda w: pl.BlockSpec((TM, w), lambda n, jt, it: (it[n], 0))
    qt = pl.BlockSpec((H * HT, TM), lambda n, jt, it: (0, it[n]))
    kj = lambda w: pl.BlockSpec((TM, w), lambda n, jt, it: (jt[n], 0))
    r_n = rider.k_n if rider else 0
    return pl.pallas_call(
        body, name=name,
        grid_spec=pltpu.PrefetchScalarGridSpec(
            num_scalar_prefetch=2, grid=(len(pairs),),
            in_specs=[qi(H * HT), qi(H * HT), qt, qt, kj(H * HT), kj(H * HT)] + [ANY] * r_n,
            out_specs=[ANY, kj(DA), kj(DA), ANY, pl.BlockSpec((H, TM), lambda n, jt, it: (0, jt[n]))] + [ANY] * r_n,
            scratch_shapes=[pltpu.VMEM((tp, DA), F32), pltpu.VMEM((tp, 128), F32), pltpu.VMEM((H * HT, TM), F32),
                            pltpu.VMEM((H * HT, TM), F32), pltpu.VMEM((H, TM), F32), pltpu.VMEM((TM, DA), MM),
                            pltpu.SemaphoreType.DMA((2,))] + (rider.scratch if rider else [])),
        out_shape=[S((tp, DA), MM), S((tp, DA), MM), S((tp, DA), MM), S((tp, 128), F32), S((H, tp), F32)]
        + (rider.out_shape if rider else []),
        compiler_params=_params(1),
    )(j_tab, i_tab, qb, dob, qbt, dobt, ka, va, *(rider.srcs if rider else []))


def fgate_bwd(dcq, dck, sg, name):
    tp = dcq.shape[0]
    nb = tp // TM

    def body(dcq_ref, dck_ref, sg_ref, dz_ref, db_ref, carry):
        @pl.when(pl.program_id(0) == 0)
        def _():
            carry[...] = jnp.zeros_like(carry)
            db_ref[...] = jnp.zeros_like(db_ref)

        row = lax.broadcasted_iota(jnp.int32, (TM, TM), 0)
        col = lax.broadcasted_iota(jnp.int32, (TM, TM), 1)
        tri = (row <= col).astype(F32)
        dl = jnp.dot(tri, dcq_ref[...] - dck_ref[...], precision=lax.Precision.HIGHEST,
                     preferred_element_type=F32) + carry[...]
        carry[...] = dl[0:1, :]
        dz = dl * sg_ref[...]
        dz_ref[...] = dz.astype(MM)
        db_ref[...] += jnp.sum(dz, axis=0, keepdims=True)

    rev = pl.BlockSpec((TM, 128), lambda i: (nb - 1 - i, 0))
    return pl.pallas_call(
        body, name=name, grid=(nb,),
        in_specs=[rev, rev, rev], out_specs=[rev, _full((1, 128))],
        out_shape=[S((tp, 128), MM), S((1, 128), F32)],
        scratch_shapes=[pltpu.VMEM((1, 128), F32)],
        compiler_params=_params(1),
    )(dcq, dck, sg)


def conv_bwd_pointwise(dcc, sv, pc, lng, lnb, wpw, name):
    tp = dcc.shape[0]

    def body(dcc_ref, sv_ref, b_ref, lng_ref, lnb_ref, wpw_ref, gc_ref, act_ref, pg_ref):
        @pl.when(pl.program_id(0) == 0)
        def _():
            pg_ref[...] = jnp.zeros_like(pg_ref)

        dconf = dcc_ref[:, 0:DC]
        g = lng_ref[...]
        xh, rs, ln = _layernorm_parts(sv_ref[:, 0:DC], g, lnb_ref[...])
        sig = _sigmoid(ln)
        act_ref[...] = (ln * sig).astype(MM)
        dact = lax.dot_general(dconf.astype(MM), wpw_ref[...], NT, preferred_element_type=F32)
        dln = dact * (sig * (1.0 + ln * (1.0 - sig)))
        dxh = dln * g
        ddw = rs * (dxh - jnp.mean(dxh, axis=-1, keepdims=True) - xh * jnp.mean(dxh * xh, axis=-1, keepdims=True))
        gc_ref[:, 0:DC] = ddw
        gc_ref[:, DC:2 * DC] = dcc_ref[:, DC:2 * DC] * b_ref[...]
        cs = lambda t: jnp.sum(t, axis=0, keepdims=True)
        pg_ref[0:1, :] += cs(dconf)
        pg_ref[1:2, :] += cs(dln * xh)
        pg_ref[2:3, :] += cs(dln)
        pg_ref[3:4, :] += cs(ddw)

    rows = lambda n: pl.BlockSpec((TM, n), lambda i: (i, 0))
    return pl.pallas_call(
        body, name=name, grid=(tp // TM,),
        in_specs=[rows(2 * DC), rows(2 * DC), pl.BlockSpec((TM, DC), lambda i: (i, 2)),
                  _full((1, DC)), _full((1, DC)), _full((DC, DC))],
        out_specs=[rows(2 * DC), rows(DC), _full((8, DC))],
        out_shape=[S((tp, 2 * DC), F32), S((tp, DC), MM), S((8, DC), F32)],
        compiler_params=_params(1),
    )(dcc, sv, pc, lng, lnb, wpw)


def conv_bwd_taps(gc, pc, dcc, sv, wdw, wsc, name):
    tp = gc.shape[0]
    nb = tp // TM
    hb = TM // HALO

    def body(gc_ref, gn_ref, pc_ref, hl_ref, dcc_ref, sv_ref, wdw_ref, wsc_ref, dpc_ref, wg_ref, ge, xe, ce, gs, xs):
        i = pl.program_id(0)

        @pl.when(i == 0)
        def _():
            wg_ref[...] = jnp.zeros_like(wg_ref)

        a, gt = pc_ref[:, 0:DC], pc_ref[:, DC:2 * DC]
        c, u = pc_ref[:, 3 * DC:4 * DC], pc_ref[:, 4 * DC:5 * DC]
        sig = _sigmoid(gt)
        _fill_halo(xe, hl_ref[:, 0:DC] * _sigmoid(hl_ref[:, DC:2 * DC]), a * sig, i == 0)
        _fill_halo(ce, hl_ref[:, 3 * DC:4 * DC] * hl_ref[:, 4 * DC:5 * DC], c * u, i == 0)
        ge[0:TM, :] = gc_ref[...]
        ge[TM:TM + HALO, :] = jnp.where(i == nb - 1, 0.0, gn_ref[...])
        _shift_copies(gs, ge, slice(0, DC))
        _shift_copies(xs, xe)
        ddw, dcv = gc_ref[:, 0:DC], gc_ref[:, DC:2 * DC]
        dglu = jnp.zeros((TM, DC), F32)
        for k in range(CK):
            dglu = dglu + wdw_ref[k:k + 1, :] * _window(ge, gs, CK - 1 - k, slice(0, DC))
            wg_ref[k:k + 1, :] += jnp.sum(ddw * _window(xe, xs, HALO - (CK - 1) + k), axis=0, keepdims=True)
        dcu = jnp.zeros((TM, DC), F32)
        for k in range(SK):
            dcu = dcu + wsc_ref[k:k + 1, :] * ge[pl.ds(SK - 1 - k, TM), DC:2 * DC]
            wg_ref[32 + k:33 + k, :] += jnp.sum(dcv * ce[pl.ds(HALO - (SK - 1) + k, TM), :], axis=0, keepdims=True)
        dpc_ref[:, 0:DC] = (dglu * sig).astype(MM)
        dpc_ref[:, DC:2 * DC] = (dglu * a * sig * (1.0 - sig)).astype(MM)
        dpc_ref[:, 2 * DC:3 * DC] = (dcc_ref[:, DC:2 * DC] * sv_ref[:, DC:2 * DC]).astype(MM)
        dpc_ref[:, 3 * DC:4 * DC] = (dcu * u).astype(MM)
        dpc_ref[:, 4 * DC:5 * DC] = (dcu * c).astype(MM)

    rows = lambda n: pl.BlockSpec((TM, n), lambda i: (i, 0))
    return pl.pallas_call(
        body, name=name, grid=(nb,),
        in_specs=[rows(2 * DC), pl.BlockSpec((HALO, 2 * DC), lambda i: (jnp.minimum((i + 1) * hb, nb * hb - 1), 0)),
                  rows(5 * DC), pl.BlockSpec((HALO, 5 * DC), lambda i: (jnp.maximum(i * hb - 1, 0), 0)),
                  rows(2 * DC), rows(2 * DC), _full((32, DC)), _full((8, DC))],
        out_specs=[rows(5 * DC), _full((40, DC))],
        out_shape=[S((tp, 5 * DC), MM), S((40, DC), F32)],
        scratch_shapes=[pltpu.VMEM((TM + HALO, 2 * DC), F32), pltpu.VMEM((HALO + TM, DC), F32),
                        pltpu.VMEM((HALO + TM, DC), F32), pltpu.VMEM((7, TM + HALO - 8, DC), F32),
                        pltpu.VMEM((7, TM + HALO - 8, DC), F32)],
        compiler_params=_params(1),
    )(gc, gc, pc, pc, dcc, sv, wdw, wsc)


def in_proj_bwd(dproj, w, h, g, dh_in, name):
    tp = h.shape[0]

    def body(dp_ref, w_ref, h_ref, g_ref, di_ref, dh_ref, dg_ref):
        @pl.when(pl.program_id(0) == 0)
        def _():
            dg_ref[...] = jnp.zeros_like(dg_ref)

        dhn = lax.dot_general(dp_ref[...], w_ref[...], NT, preferred_element_type=F32)
        dx, dg = _rmsnorm_bwd(h_ref[...], g_ref[...], dhn)
        dh_ref[...] = di_ref[...] + dx
        dg_ref[...] += dg

    rows = lambda n: pl.BlockSpec((TM, n), lambda i: (i, 0))
    return pl.pallas_call(
        body, name=name, grid=(tp // TM,),
        in_specs=[rows(NP_IN), _full((D, NP_IN)), rows(D), _full((1, D)), rows(D)],
        out_specs=[rows(D), _full((1, D))],
        out_shape=[S((tp, D), F32), S((1, D), F32)],
        compiler_params=_params(1),
    )(dproj, w, h, g, dh_in)


def adamw(recvs, w, m, v, rb, name):
    l_n, r_n, c_n = w.shape

    def body(*refs):
        p_refs = refs[:l_n]
        w_ref, m_ref, v_ref, g_ref, d_ref, m2_ref, v2_ref = refs[l_n:]
        for l in range(l_n):
            g = p_refs[l][0].astype(F32)
            for s in range(1, NDEV):
                g = g + p_refs[l][s].astype(F32)
            m2 = ADAM_B1 * m_ref[l] + (1.0 - ADAM_B1) * g
            v2 = ADAM_B2 * v_ref[l] + (1.0 - ADAM_B2) * (g * g)
            m_hat = m2 / (1.0 - ADAM_B1 ** ADAM_STEP)
            v_hat = v2 / (1.0 - ADAM_B2 ** ADAM_STEP)
            g_ref[l] = g
            d_ref[l] = -ADAM_LR * (m_hat / (jnp.sqrt(v_hat) + ADAM_EPS) + ADAM_WD * w_ref[l])
            m2_ref[l] = m2
            v2_ref[l] = v2

    blk = pl.BlockSpec((l_n, rb, c_n), lambda r: (0, r, 0))
    return pl.pallas_call(
        body, name=name, grid=(r_n // rb,),
        in_specs=[pl.BlockSpec((NDEV, rb, c_n), lambda r: (0, r, 0))] * l_n + [blk, blk, blk],
        out_specs=[blk] * 4, out_shape=[S(w.shape, F32)] * 4,
        compiler_params=_params(1),
    )(*recvs, w, m, v)


TINY_ROWS = 168
REP_ROWS = 64


def _pack(parts, rows):
    flat = jnp.concatenate([p.reshape(-1) for p in parts])
    return jnp.pad(flat, (0, rows * 128 - flat.shape[0])).reshape(rows, 128)


def _unpack(buf, shapes):
    flat, out, o = buf.reshape(-1), [], 0
    for s in shapes:
        n = 1
        for d in s:
            n *= d
        out.append(flat[o:o + n].reshape(s))
        o += n
    return out


TINY_SHAPES = [(DEPTH, CK, DC // NDEV), (DEPTH, SK, DC // NDEV), (NM, D // NDEV), (DEPTH, DC // NDEV, DC)]
REP_SHAPES = [(DEPTH, D), (DEPTH, H), (DEPTH, DC), (DEPTH, DC), (DEPTH, DC), (DEPTH, DC), (DEPTH, D), (D,)]


def _to_padded_cols(w):
    pad = jnp.zeros(w.shape[:-1] + (NP_IN - N_IN,), w.dtype)
    return jnp.concatenate([w[..., :C0], w[..., C0 + H:], w[..., C0:C0 + H], pad], axis=-1)


def _from_padded_cols(w):
    return jnp.concatenate([w[..., :C0], w[..., F0:F0 + H], w[..., C0:F0]], axis=-1)


def kernel(x, meta_tokens, mix_norm_g, w_in, b_forget, w_conf_dw, b_conf_dw, conf_ln_g, conf_ln_b, w_conf_pw, b_conf_pw, w_sc_conv, w_out, mlp_norm_g, w_mlp1, w_mlp2, final_norm_g, loss_target, m_meta_tokens, m_mix_norm_g, m_w_in, m_b_forget, m_w_conf_dw, m_b_conf_dw, m_conf_ln_g, m_conf_ln_b, m_w_conf_pw, m_b_conf_pw, m_w_sc_conv, m_w_out, m_mlp_norm_g, m_w_mlp1, m_w_mlp2, m_final_norm_g, v_meta_tokens, v_mix_norm_g, v_w_in, v_b_forget, v_w_conf_dw, v_b_conf_dw, v_conf_ln_g, v_conf_ln_b, v_w_conf_pw, v_b_conf_pw, v_w_sc_conv, v_w_out, v_mlp_norm_g, v_w_mlp1, v_w_mlp2, v_final_norm_g):
    seq = x.shape[1]
    t_real = NM + seq
    tp = -(-t_real // TM) * TM

    tiny_w = _pack([w_conf_dw, w_sc_conv, meta_tokens, w_conf_pw], TINY_ROWS)
    big = lambda l: [w_in[l].astype(MM), w_out[l].astype(MM), w_mlp1[l].astype(MM), w_mlp2[l].astype(MM)]
    *first, g_tiny = gather(big(0) + [tiny_w], "gather_weights0")
    gathered = {0: first}
    n_sh = w_in.shape[-1]
    assert w_mlp1.shape[-1] == FC and w_mlp2.shape[-2] == FC
    tiny = [_unpack(g_tiny[s], TINY_SHAPES) for s in range(NDEV)]
    wdw = jnp.concatenate([t[0] for t in tiny], axis=-1)
    wsc = jnp.concatenate([t[1] for t in tiny], axis=-1)
    meta = jnp.concatenate([t[2] for t in tiny], axis=-1)
    wpw = jnp.concatenate([t[3] for t in tiny], axis=1).astype(MM)
    wdw = jnp.pad(wdw, ((0, 0), (0, 32 - CK), (0, 0)))
    wsc = jnp.pad(wsc, ((0, 0), (0, 8 - SK), (0, 0)))
    bfp = jnp.pad(b_forget, ((0, 0), (0, 128 - H)))

    row = lambda a: a.reshape(1, -1)

    h = jnp.concatenate([meta, x[0], jnp.zeros((tp - t_real, D), F32)], axis=0)
    tgt = jnp.pad(loss_target[0], ((NM, tp - t_real), (0, 0)))
    saved = []
    for l in range(DEPTH):
        g_in, g_out, g_w1, g_w2 = gathered[l]
        win = _to_padded_cols(g_in.transpose(1, 0, 2).reshape(D, NDEV * n_sh))
        hn, qa, ka, va, pc, sg = in_proj(h, row(mix_norm_g[l]), win, row(bfp[l]), f"in_proj{l}")
        rider = AllToAll(big(l + 1), [False] * 4) if l + 1 < DEPTH else None
        o, lse, *arrived = attn_fwd(qa, ka, va, f"attn_fwd{l}", rider)
        if rider:
            gathered[l + 1] = arrived
        cc, sv = conv_fwd(pc, wdw[l], row(b_conf_dw[l]), row(conf_ln_g[l]), row(conf_ln_b[l]), wpw[l],
                          row(b_conf_pw[l]), wsc[l], f"conv_fwd{l}")
        h2, hn2 = out_proj(h, o, cc, g_out, row(mlp_norm_g[l]), f"out_proj{l}")
        r, z, h3 = mlp_fwd(hn2, h2, g_w1, g_w2, f"mlp_fwd{l}")
        saved.append((h, hn, qa, ka, va, pc, sg, o, lse, cc, sv, h2, hn2, r, z, win))
        h = h3

    dh, loss_part, d_gf = loss_head(h, tgt, row(final_norm_g), t_real, "loss_head")
    loss = lax.psum(loss_part[0, 0], ("x", "y", "c"))

    gw, recv, d_win = {}, {}, {}
    for l in reversed(range(DEPTH)):
        h0, hn, qa, ka, va, pc, sg, o, lse, cc, sv, h2, hn2, r, z, win = saved[l]
        _, g_out, g_w1, g_w2 = gathered[l]
        da, dh2, gw["mlp_g", l] = mlp_bwd(dh, r, g_w1, g_w2, h2, row(mlp_norm_g[l]), f"mlp_bwd{l}")
        d_w1 = matmul_tn(hn2, da, D, 4 * FC, f"dw_mlp1_{l}", out_dtype=MM, shard="n", sw=FC)
        d_w2 = matmul_tn(z, dh, 4 * FC, D, f"dw_mlp2_{l}", out_dtype=MM, shard="m", sw=FC)
        do, dcc = out_proj_bwd(dh2, g_out, f"out_proj_bwd{l}")
        d_wout = jnp.concatenate([matmul_tn(o, dh2, DA, D, f"dw_out_a{l}"), matmul_tn(cc, dh2, 2 * DC, D, f"dw_out_c{l}")],
                                 axis=0).reshape(NDEV, D // NDEV, D).astype(MM)
        qb, dob, qbt, dobt = attn_bwd_prep(qa, do, o, lse, f"attn_bwd_prep{l}")
        riding = [("wout", l, d_wout), ("w1", l, d_w1), ("w2", l, d_w2)] + ([("win", l + 1, d_win[l + 1])] if l + 1 < DEPTH else [])
        dq, dk, dv, dcq, dck, *arrived = attn_bwd(qb, dob, qbt, dobt, ka, va, f"attn_bwd{l}",
                                                  AllToAll([a for _, _, a in riding], [True] * len(riding)))
        for (k, kl, _), a in zip(riding, arrived):
            recv[k, kl] = a
        dz, gw["bf", l] = fgate_bwd(dcq, jnp.pad(dck.T, ((0, 0), (0, 128 - H))), sg, f"fgate_bwd{l}")
        gc, act, pg = conv_bwd_pointwise(dcc, sv, pc, row(conf_ln_g[l]), row(conf_ln_b[l]), wpw[l], f"conv_bwd_pw{l}")
        gw["wpw", l] = matmul_tn(act, dcc, DC, DC, f"dw_conf_pw{l}", n=DC)
        dpc, wg = conv_bwd_taps(gc, pc, dcc, sv, wdw[l], wsc[l], f"conv_bwd_taps{l}")
        gw["pg", l], gw["wg", l] = pg, wg
        dproj = jnp.concatenate([dq, dk, dv, dpc, dz], axis=1)
        d_win[l] = _from_padded_cols(matmul_tn(hn, dproj, 512, NP_IN, f"dw_in{l}")).reshape(D, NDEV, n_sh).transpose(
            1, 0, 2).astype(MM)
        dh, gw["mix_g", l] = in_proj_bwd(dproj, win, h0, row(mix_norm_g[l]), dh2, f"in_proj_bwd{l}")

    grad_x = dh[NM:t_real][None]
    stack = lambda k: jnp.stack([gw[k, l] for l in range(DEPTH)])

    d_wdw = stack("wg")[:, 0:CK].reshape(DEPTH, CK, NDEV, DC // NDEV).transpose(2, 0, 1, 3)
    d_wsc = stack("wg")[:, 32:32 + SK].reshape(DEPTH, SK, NDEV, DC // NDEV).transpose(2, 0, 1, 3)
    d_meta = dh[0:NM].reshape(NM, NDEV, D // NDEV).transpose(1, 0, 2)
    d_wpw = stack("wpw").reshape(DEPTH, NDEV, DC // NDEV, DC).transpose(1, 0, 2, 3)
    d_tiny = jnp.stack([_pack([d_wdw[p], d_wsc[p], d_meta[p], d_wpw[p]], TINY_ROWS) for p in range(NDEV)])
    pgs = stack("pg")
    d_rep = _pack([stack("mix_g").reshape(DEPTH, D), stack("bf")[:, 0, :H], pgs[:, 3], pgs[:, 1], pgs[:, 2], pgs[:, 0],
                   stack("mlp_g").reshape(DEPTH, D), d_gf.reshape(D)], REP_ROWS)
    recv["win", 0], r_tiny, r_rep = exchange([d_win[0], d_tiny, d_rep], [True, True, False], "exchange_grads")

    per_layer = lambda k: [recv[k, l] for l in range(DEPTH)]
    res = {}
    res["w_in"] = adamw(per_layer("win"), w_in, m_w_in, v_w_in, 256, "adamw_w_in")
    res["w_out"] = adamw(per_layer("wout"), w_out, m_w_out, v_w_out, D // NDEV, "adamw_w_out")
    res["w_mlp1"] = adamw(per_layer("w1"), w_mlp1, m_w_mlp1, v_w_mlp1, 256, "adamw_w_mlp1")
    res["w_mlp2"] = adamw(per_layer("w2"), w_mlp2, m_w_mlp2, v_w_mlp2, 128, "adamw_w_mlp2")
    tiny_names = ["w_conf_dw", "w_sc_conv", "meta_tokens", "w_conf_pw"]
    tiny_wmv = [[w_conf_dw, w_sc_conv, meta_tokens, w_conf_pw], [m_w_conf_dw, m_w_sc_conv, m_meta_tokens, m_w_conf_pw],
                [v_w_conf_dw, v_w_sc_conv, v_meta_tokens, v_w_conf_pw]]
    rep_names = ["mix_norm_g", "b_forget", "b_conf_dw", "conf_ln_g", "conf_ln_b", "b_conf_pw", "mlp_norm_g", "final_norm_g"]
    rep_wmv = [[mix_norm_g, b_forget, b_conf_dw, conf_ln_g, conf_ln_b, b_conf_pw, mlp_norm_g, final_norm_g],
               [m_mix_norm_g, m_b_forget, m_b_conf_dw, m_conf_ln_g, m_conf_ln_b, m_b_conf_pw, m_mlp_norm_g, m_final_norm_g],
               [v_mix_norm_g, v_b_forget, v_b_conf_dw, v_conf_ln_g, v_conf_ln_b, v_b_conf_pw, v_mlp_norm_g, v_final_norm_g]]
    for names, wmv, shapes, recv_pack, rows_n, nm in ((tiny_names, tiny_wmv, TINY_SHAPES, r_tiny, TINY_ROWS, "adamw_tiny"),
                                                 (rep_names, rep_wmv, REP_SHAPES, r_rep, REP_ROWS, "adamw_rep")):
        packed = [_pack(group, rows_n)[None] for group in wmv]
        outs = adamw([recv_pack], *packed, rows_n, nm)
        parts = [_unpack(a[0], shapes) for a in outs]
        for k, n in enumerate(names):
            res[n] = [parts[q][k] for q in range(4)]

    order = ["meta_tokens", "mix_norm_g", "w_in", "b_forget", "w_conf_dw", "b_conf_dw", "conf_ln_g", "conf_ln_b",
             "w_conf_pw", "b_conf_pw", "w_sc_conv", "w_out", "mlp_norm_g", "w_mlp1", "w_mlp2", "final_norm_g"]
    return (loss, grad_x, *[res[n][0] for n in order], *[res[n][1] for n in order],
            *[res[n][2] for n in order], *[res[n][3] for n in order])
```

```python
import functools

import jax
import jax.numpy as jnp
from jax import lax
from jax.experimental import pallas as pl
from jax.experimental.pallas import tpu as pltpu

F32 = jnp.float32
MM = jnp.bfloat16

D = 1024
H = 8
DH = 64
DA = H * DH
HT = 128
DC = 256
NM = 16
CK = 31
SK = 3
DFF = 4096
DEPTH = 2
N_IN = 3 * DA + H + 2 * DC + 3 * DC
NP_IN = 3 * DA + 5 * DC + 128
C0 = 3 * DA
F0 = 3 * DA + 5 * DC
EPS = 1e-6
TM = 640
HALO = 32
FC = 512
FS = 4
NDEV = 8
SCALE = DH ** -0.5
NEG = -1e30

ADAM_LR, ADAM_B1, ADAM_B2, ADAM_EPS, ADAM_WD, ADAM_STEP = 0.001, 0.9, 0.999, 1e-08, 0.01, 10

VMEM_LIMIT = 56 * 1024 * 1024

S = jax.ShapeDtypeStruct
NT = (((1,), (1,)), ((), ()))
TN = (((0,), (0,)), ((), ()))


def _params(n_grid):
    return pltpu.CompilerParams(dimension_semantics=("arbitrary",) * n_grid, vmem_limit_bytes=VMEM_LIMIT)


def _sigmoid(x):
    return 1.0 / (1.0 + jnp.exp(-x))


def _full(shape):
    n = len(shape)
    return pl.BlockSpec(shape, lambda *_: (0,) * n)


def _lane_put(dst, col, h):
    lane = lax.broadcasted_iota(jnp.int32, dst.shape, 1)
    return jnp.where(lane == h, col, dst)


ANY = pl.BlockSpec(memory_space=pl.ANY)


class AllToAll:
    def __init__(self, srcs, per_peer):
        self.srcs, self.per_peer, self.k_n = list(srcs), list(per_peer), len(srcs)
        self.out_shape = [S((NDEV,) + (a.shape[1:] if pp else a.shape), a.dtype) for a, pp in zip(srcs, per_peer)]
        self.scratch = [pltpu.SemaphoreType.DMA((self.k_n, NDEV - 1)), pltpu.SemaphoreType.DMA((self.k_n, NDEV - 1)),
                        pltpu.SemaphoreType.DMA((self.k_n,))]

    def copies(self, src, out, send_sems, recv_sems, local_sems):
        x, y, c = lax.axis_index("x"), lax.axis_index("y"), lax.axis_index("c")
        me = 4 * x + 2 * y + c

        def piece(k, p):
            return src[k].at[p] if self.per_peer[k] else src[k]

        local = [pltpu.make_async_copy(piece(k, me), out[k].at[me], local_sems.at[k]) for k in range(self.k_n)]
        sends, recvs = [], []
        for r in range(1, NDEV):
            px = 1 - x if (r >> 2) & 1 else x
            py = 1 - y if (r >> 1) & 1 else y
            pc = 1 - c if r & 1 else c
            pidx = 4 * px + 2 * py + pc
            for k in range(self.k_n):
                sends.append(pltpu.make_async_remote_copy(
                    src_ref=piece(k, pidx), dst_ref=out[k].at[me],
                    send_sem=send_sems.at[k, r - 1], recv_sem=recv_sems.at[k, r - 1],
                    device_id=(px, py, pc), device_id_type=pl.DeviceIdType.MESH))
                recvs.append(pltpu.make_async_remote_copy(
                    src_ref=piece(k, pidx), dst_ref=out[k].at[pidx],
                    send_sem=send_sems.at[k, r - 1], recv_sem=recv_sems.at[k, r - 1],
                    device_id=(px, py, pc), device_id_type=pl.DeviceIdType.MESH))

        def start():
            for cp in local + sends:
                cp.start()

        def wait():
            for cp in recvs:
                cp.wait_recv()
            for cp in sends:
                cp.wait_send()
            for cp in local:
                cp.wait()

        return start, wait


def exchange(srcs, per_peer, name):
    plan = AllToAll(srcs, per_peer)

    def body(*refs):
        start, wait = plan.copies(refs[:plan.k_n], refs[plan.k_n:2 * plan.k_n], *refs[2 * plan.k_n:])
        start()
        wait()

    return pl.pallas_call(
        body, name=name, out_shape=plan.out_shape, in_specs=[ANY] * plan.k_n, out_specs=[ANY] * plan.k_n,
        scratch_shapes=plan.scratch,
    )(*srcs)


def gather(srcs, name):
    k_n = len(srcs)
    out_shape = [S((NDEV,) + a.shape, a.dtype) for a in srcs]

    def body(*refs):
        src, out = refs[:k_n], refs[k_n:2 * k_n]
        send_sems, recv_sems, local_sems = refs[2 * k_n:]
        x, y, c = lax.axis_index("x"), lax.axis_index("y"), lax.axis_index("c")
        me, sibling = (x, y, c), (x, y, 1 - c)
        chips = [(1 - x, y), (x, 1 - y), (1 - x, 1 - y)]

        def slot(k, dev):
            return out[k].at[4 * dev[0] + 2 * dev[1] + dev[2]]

        def copy(k, r, block, to, from_src=False):
            return pltpu.make_async_remote_copy(
                src_ref=src[k] if from_src else slot(k, block), dst_ref=slot(k, block),
                send_sem=send_sems.at[k, r], recv_sem=recv_sems.at[k, r],
                device_id=to, device_id_type=pl.DeviceIdType.MESH)

        local = [pltpu.make_async_copy(src[k], slot(k, me), local_sems.at[k]) for k in range(k_n)]
        first = [copy(k, 0, me, sibling, True) for k in range(k_n)]
        first += [copy(k, 1 + n, me, (*chip, c), True) for n, chip in enumerate(chips) for k in range(k_n)]
        for cp in local + first:
            cp.start()
        passed = []
        for n, chip in enumerate(chips):
            for k in range(k_n):
                copy(k, 1 + n, (*chip, c), me).wait_recv()
                passed.append(copy(k, 4 + n, (*chip, c), sibling))
                passed[-1].start()
        for k in range(k_n):
            copy(k, 0, sibling, me).wait_recv()
            for n, chip in enumerate(chips):
                copy(k, 4 + n, (*chip, 1 - c), me).wait_recv()
        for cp in first + passed:
            cp.wait_send()
        for cp in local:
            cp.wait()

    any_spec = pl.BlockSpec(memory_space=pl.ANY)
    return pl.pallas_call(
        body, name=name, out_shape=out_shape,
        in_specs=[any_spec] * k_n, out_specs=[any_spec] * k_n,
        scratch_shapes=[pltpu.SemaphoreType.DMA((k_n, NDEV - 1)), pltpu.SemaphoreType.DMA((k_n, NDEV - 1)),
                        pltpu.SemaphoreType.DMA((k_n,))],
    )(*srcs)


def in_proj(h, g, w, bf, name):
    tp = h.shape[0]

    def body(h_ref, g_ref, w_ref, bf_ref, hn_ref, qa_ref, ka_ref, va_ref, pc_ref, sg_ref, carry):
        i = pl.program_id(0)

        @pl.when(i == 0)
        def _():
            carry[...] = jnp.zeros_like(carry)

        x = h_ref[...]
        r = lax.rsqrt(jnp.mean(x * x, axis=-1, keepdims=True) + EPS)
        hn = (x * r * g_ref[...]).astype(MM)
        hn_ref[...] = hn
        pc_ref[...] = jnp.dot(hn, w_ref[:, C0:F0], preferred_element_type=F32)
        z = jnp.dot(hn, w_ref[:, F0:NP_IN], preferred_element_type=F32) + bf_ref[...]
        lane = lax.broadcasted_iota(jnp.int32, z.shape, 1)
        logf = jnp.where(lane < H, jnp.minimum(z, 0.0) - jnp.log(1.0 + jnp.exp(-jnp.abs(z))), 0.0)
        sg_ref[...] = 1.0 / (1.0 + jnp.exp(z))
        row = lax.broadcasted_iota(jnp.int32, (TM, TM), 0)
        col = lax.broadcasted_iota(jnp.int32, (TM, TM), 1)
        tri = (row >= col).astype(F32)
        c = jnp.dot(tri, logf, precision=lax.Precision.HIGHEST, preferred_element_type=F32) + carry[...]
        carry[...] = c[TM - 1:TM, :]
        qkv = jnp.dot(hn, w_ref[:, 0:C0], preferred_element_type=F32)
        one = [1.0, 1.0, 1.0]
        for hd in range(H):
            t = slice(hd * HT, (hd + 1) * HT)
            cs = list(_split3(c[:, hd:hd + 1]))
            qa_ref[:, t] = _aug_tile(hd, qkv[:, hd * DH:(hd + 1) * DH] * SCALE, cs + one).astype(MM)
            ka_ref[:, t] = _aug_tile(hd, qkv[:, DA + hd * DH:DA + (hd + 1) * DH], one + [-v for v in cs] + one).astype(MM)
            va_ref[:, t] = _aug_tile(hd, qkv[:, 2 * DA + hd * DH:2 * DA + (hd + 1) * DH], [1.0] + one).astype(MM)

    rows = lambda n: pl.BlockSpec((TM, n), lambda i: (i, 0))
    return pl.pallas_call(
        body, name=name, grid=(tp // TM,),
        in_specs=[rows(D), _full((1, D)), _full((D, NP_IN)), _full((1, 128))],
        out_specs=[rows(D), rows(H * HT), rows(H * HT), rows(H * HT), rows(5 * DC), rows(128)],
        out_shape=[S((tp, D), MM), S((tp, H * HT), MM), S((tp, H * HT), MM), S((tp, H * HT), MM),
                   S((tp, 5 * DC), F32), S((tp, 128), F32)],
        scratch_shapes=[pltpu.VMEM((1, 128), F32)],
        compiler_params=_params(1),
    )(h, g, w, bf)


def _split3(c):
    hi = c.astype(MM).astype(F32)
    mid = (c - hi).astype(MM).astype(F32)
    lo = (c - hi - mid).astype(MM).astype(F32)
    return hi, mid, lo


def _main(hd):
    return slice(0, DH) if hd % 2 == 0 else slice(DH, HT)


def _aug_lane(hd, pos):
    return pos + (DH if hd % 2 == 0 else 0)


def _aug_tile(hd, main, cols):
    lane = lax.broadcasted_iota(jnp.int32, main.shape, 1)
    aug = jnp.zeros(main.shape, F32)
    for pos, val in enumerate(cols):
        aug = jnp.where(lane == pos, val, aug)
    return jnp.concatenate([main, aug] if hd % 2 == 0 else [aug, main], axis=1)


def _merge_pairs(tiles):
    lane = lax.broadcasted_iota(jnp.int32, tiles[0].shape, 1)
    return jnp.concatenate([jnp.where(lane < DH, tiles[2 * m], tiles[2 * m + 1]) for m in range(H // 2)], axis=1)


def _causal_mask():
    return lax.broadcasted_iota(jnp.int32, (TM, TM), 0) >= lax.broadcasted_iota(jnp.int32, (TM, TM), 1)


def _split_rider(refs, n_in, n_out, rider):
    if rider is None:
        return refs, None
    k = rider.k_n
    own = refs[:n_in] + refs[n_in + k:n_in + k + n_out] + refs[n_in + 2 * k + n_out:-3]
    return own, rider.copies(refs[n_in:n_in + k], refs[n_in + k + n_out:n_in + 2 * k + n_out], *refs[-3:])


def attn_fwd(qa, ka, va, name, rider=None):
    tp = qa.shape[0]
    nb = tp // TM
    pairs = [(i, j) for i in range(nb) for j in range(i + 1)]
    i_tab, j_tab = (jnp.asarray([p[a] for p in pairs], jnp.int32) for a in (0, 1))

    def body(i_ref, j_ref, *refs):
        (q_ref, k_ref, v_ref, o_ref, lse_ref, *scr), copies = _split_rider(refs, 3, 2, rider)
        m_scr, acc_scr, bias = scr[:H], scr[H:2 * H], scr[2 * H]
        n = pl.program_id(0)
        i, j = i_ref[n], j_ref[n]

        @pl.when(n == 0)
        def _():
            if copies:
                copies[0]()
            bias[...] = jnp.where(_causal_mask(), 0.0, NEG)

        @pl.when(j == 0)
        def _():
            for hd in range(H):
                m_scr[hd][...] = jnp.full((TM, 1), NEG, F32)
                acc_scr[hd][...] = jnp.zeros((TM, HT), F32)

        def step(diag):
            def logits(hd):
                t = slice(hd * HT, (hd + 1) * HT)
                s = lax.dot_general(q_ref[:, t], k_ref[:, t], NT, preferred_element_type=F32)
                return s + bias[...] if diag else s

            s_next = logits(0)
            for hd in range(H):
                t = slice(hd * HT, (hd + 1) * HT)
                s = s_next
                if hd + 1 < H:
                    s_next = logits(hd + 1)
                m_prev = m_scr[hd][...]
                m_new = jnp.maximum(m_prev, jnp.max(s, axis=1, keepdims=True))
                p = jnp.exp(s - m_new).astype(MM)
                acc_scr[hd][...] = (jnp.exp(m_prev - m_new) * acc_scr[hd][...]
                                    + jnp.dot(p, v_ref[:, t], preferred_element_type=F32))
                m_scr[hd][...] = m_new

        @pl.when(j < i)
        def _():
            step(False)

        @pl.when(j == i)
        def _():
            step(True)

        @pl.when(j == i)
        def _():
            lse = jnp.zeros((TM, 128), F32)
            outs = []
            for hd in range(H):
                acc = acc_scr[hd][...]
                l = acc[:, _aug_lane(hd, 0):_aug_lane(hd, 0) + 1]
                outs.append(acc * (1.0 / l))
                lse = _lane_put(lse, m_scr[hd][...] + jnp.log(l), hd)
            o_ref[...] = _merge_pairs(outs).astype(MM)
            lse_ref[...] = lse

        if copies:
            @pl.when(n == len(pairs) - 1)
            def _():
                copies[1]()

    qi = lambda w: pl.BlockSpec((TM, w), lambda n, it, jt: (it[n], 0))
    kv = pl.BlockSpec((TM, H * HT), lambda n, it, jt: (jt[n], 0))
    r_n = rider.k_n if rider else 0
    return pl.pallas_call(
        body, name=name,
        grid_spec=pltpu.PrefetchScalarGridSpec(
            num_scalar_prefetch=2, grid=(len(pairs),),
            in_specs=[qi(H * HT), kv, kv] + [ANY] * r_n, out_specs=[qi(DA), qi(128)] + [ANY] * r_n,
            scratch_shapes=[pltpu.VMEM((TM, 1), F32)] * H + [pltpu.VMEM((TM, HT), F32)] * H + [pltpu.VMEM((TM, TM), F32)]
            + (rider.scratch if rider else [])),
        out_shape=[S((tp, DA), MM), S((tp, 128), F32)] + (rider.out_shape if rider else []),
        compiler_params=_params(1),
    )(i_tab, j_tab, qa, ka, va, *(rider.srcs if rider else []))


def _layernorm_parts(dw, g, b):
    mu = jnp.mean(dw, axis=-1, keepdims=True)
    xc = dw - mu
    rs = lax.rsqrt(jnp.mean(xc * xc, axis=-1, keepdims=True) + EPS)
    xh = xc * rs
    return xh, rs, xh * g + b


def _fill_halo(ext, halo, cur, first):
    ext[0:HALO, :] = jnp.where(first, 0.0, halo)
    ext[HALO:HALO + TM, :] = cur


def _shift_copies(dst, src, lanes=slice(None)):
    for r in range(1, 8):
        dst[r - 1] = src[pl.ds(r, dst.shape[1]), lanes]


def _window(src, shifted, off, lanes=slice(None)):
    if off % 8 == 0:
        return src[pl.ds(off, TM), lanes]
    return shifted[off % 8 - 1, pl.ds(off - off % 8, TM), :]


def conv_fwd(pc, wdw, bdw, lng, lnb, wpw, bpw, wsc, name):
    tp = pc.shape[0]

    def body(pc_ref, hl_ref, wdw_ref, bdw_ref, lng_ref, lnb_ref, wpw_ref, bpw_ref, wsc_ref, cc_ref, sv_ref, xe, ce, xs):
        first = pl.program_id(0) == 0
        glu = pc_ref[:, 0:DC] * _sigmoid(pc_ref[:, DC:2 * DC])
        cu = pc_ref[:, 3 * DC:4 * DC] * pc_ref[:, 4 * DC:5 * DC]
        _fill_halo(xe, hl_ref[:, 0:DC] * _sigmoid(hl_ref[:, DC:2 * DC]), glu, first)
        _fill_halo(ce, hl_ref[:, 3 * DC:4 * DC] * hl_ref[:, 4 * DC:5 * DC], cu, first)
        _shift_copies(xs, xe)
        dw = jnp.zeros((TM, DC), F32) + bdw_ref[...]
        for k in range(CK):
            dw = dw + wdw_ref[k:k + 1, :] * _window(xe, xs, HALO - (CK - 1) + k)
        cv = jnp.zeros((TM, DC), F32)
        for k in range(SK):
            cv = cv + wsc_ref[k:k + 1, :] * ce[pl.ds(HALO - (SK - 1) + k, TM), :]
        _, _, ln = _layernorm_parts(dw, lng_ref[...], lnb_ref[...])
        act = ln * _sigmoid(ln)
        conf = jnp.dot(act.astype(MM), wpw_ref[...], preferred_element_type=F32) + bpw_ref[...]
        cc_ref[:, 0:DC] = conf.astype(MM)
        cc_ref[:, DC:2 * DC] = (pc_ref[:, 2 * DC:3 * DC] * cv).astype(MM)
        sv_ref[:, 0:DC] = dw
        sv_ref[:, DC:2 * DC] = cv

    hb = TM // HALO
    rows = lambda n: pl.BlockSpec((TM, n), lambda i: (i, 0))
    return pl.pallas_call(
        body, name=name, grid=(tp // TM,),
        in_specs=[rows(5 * DC), pl.BlockSpec((HALO, 5 * DC), lambda i: (jnp.maximum(i * hb - 1, 0), 0)),
                  _full((32, DC)), _full((1, DC)), _full((1, DC)), _full((1, DC)), _full((DC, DC)), _full((1, DC)),
                  _full((8, DC))],
        out_specs=[rows(2 * DC), rows(2 * DC)],
        out_shape=[S((tp, 2 * DC), MM), S((tp, 2 * DC), F32)],
        scratch_shapes=[pltpu.VMEM((HALO + TM, DC), F32), pltpu.VMEM((HALO + TM, DC), F32),
                        pltpu.VMEM((7, TM + HALO - 8, DC), F32)],
        compiler_params=_params(1),
    )(pc, pc, wdw, bdw, lng, lnb, wpw, bpw, wsc)


def out_proj(h, o, cc, wo, g2, name):
    tp = h.shape[0]

    def body(h_ref, o_ref, cc_ref, wo_ref, g_ref, h2_ref, hn_ref):
        wo = wo_ref[...].reshape(D, D)
        h2 = (h_ref[...] + jnp.dot(o_ref[...], wo[0:DA, :], preferred_element_type=F32)
              + jnp.dot(cc_ref[...], wo[DA:D, :], preferred_element_type=F32))
        h2_ref[...] = h2
        r = lax.rsqrt(jnp.mean(h2 * h2, axis=-1, keepdims=True) + EPS)
        hn_ref[...] = (h2 * r * g_ref[...]).astype(MM)

    rows = lambda n: pl.BlockSpec((TM, n), lambda i: (i, 0))
    return pl.pallas_call(
        body, name=name, grid=(tp // TM,),
        in_specs=[rows(D), rows(DA), rows(2 * DC), _full((NDEV, D // NDEV, D)), _full((1, D))],
        out_specs=[rows(D), rows(D)],
        out_shape=[S((tp, D), F32), S((tp, D), MM)],
        compiler_params=_params(1),
    )(h, o, cc, wo, g2)


def mlp_fwd(hn, h2, w1, w2, name):
    tp = hn.shape[0]
    nf = DFF // (FS * FC)

    def body(hn_ref, h2_ref, w1_ref, w2_ref, r_ref, z_ref, h3_ref, acc):
        j = pl.program_id(1)

        @pl.when(j == 0)
        def _():
            acc[...] = jnp.zeros_like(acc)

        for s in range(FS):
            cols = slice(s * FC, (s + 1) * FC)
            r = jnp.maximum(jnp.dot(hn_ref[...], w1_ref[s], preferred_element_type=F32), 0.0)
            zb = (r * r).astype(MM)
            r_ref[:, cols] = r.astype(MM)
            z_ref[:, cols] = zb
            acc[...] += jnp.dot(zb, w2_ref[s], preferred_element_type=F32)

        @pl.when(j == nf - 1)
        def _():
            h3_ref[...] = h2_ref[...] + acc[...]

    return pl.pallas_call(
        body, name=name, grid=(tp // TM, nf),
        in_specs=[pl.BlockSpec((TM, D), lambda i, j: (i, 0)), pl.BlockSpec((TM, D), lambda i, j: (i, 0)),
                  pl.BlockSpec((FS, D, FC), lambda i, j: (j, 0, 0)), pl.BlockSpec((FS, FC, D), lambda i, j: (j, 0, 0))],
        out_specs=[pl.BlockSpec((TM, FS * FC), lambda i, j: (i, j)), pl.BlockSpec((TM, FS * FC), lambda i, j: (i, j)),
                   pl.BlockSpec((TM, D), lambda i, j: (i, 0))],
        out_shape=[S((tp, DFF), MM), S((tp, DFF), MM), S((tp, D), F32)],
        scratch_shapes=[pltpu.VMEM((TM, D), F32)],
        compiler_params=_params(2),
    )(hn, h2, w1, w2)


def loss_head(h, tgt, g, t_real, name):
    tp = h.shape[0]

    def body(h_ref, t_ref, g_ref, dh_ref, loss_ref, dg_ref):
        i = pl.program_id(0)

        @pl.when(i == 0)
        def _():
            loss_ref[...] = jnp.zeros_like(loss_ref)
            dg_ref[...] = jnp.zeros_like(dg_ref)

        x = h_ref[...]
        gg = g_ref[...]
        r = lax.rsqrt(jnp.mean(x * x, axis=-1, keepdims=True) + EPS)
        xn = x * r
        row = i * TM + lax.broadcasted_iota(jnp.int32, (TM, 1), 0)
        e = jnp.where((row >= NM) & (row < t_real), xn * gg - t_ref[...], 0.0)
        loss_ref[...] += jnp.sum(e * e) * (0.5 / D)
        dy = e * (1.0 / D)
        dg_ref[...] += jnp.sum(dy * xn, axis=0, keepdims=True)
        u = dy * gg
        dh_ref[...] = r * (u - xn * jnp.mean(u * xn, axis=-1, keepdims=True))

    rows = lambda n: pl.BlockSpec((TM, n), lambda i: (i, 0))
    return pl.pallas_call(
        body, name=name, grid=(tp // TM,),
        in_specs=[rows(D), rows(D), _full((1, D))],
        out_specs=[rows(D), _full((8, 128)), _full((1, D))],
        out_shape=[S((tp, D), F32), S((8, 128), F32), S((1, D), F32)],
        compiler_params=_params(1),
    )(h, tgt, g)


def _rmsnorm_bwd(x, g, dy):
    r = lax.rsqrt(jnp.mean(x * x, axis=-1, keepdims=True) + EPS)
    xn = x * r
    u = dy * g
    dx = r * (u - xn * jnp.mean(u * xn, axis=-1, keepdims=True))
    return dx, jnp.sum(dy * xn, axis=0, keepdims=True)


def matmul_tn(a, b, tm, tn, name, n=None, out_dtype=F32, shard=None, sw=None):
    tp, m = a.shape
    n = b.shape[1] if n is None else n
    nk = tp // TM

    def body(a_ref, b_ref, o_ref, acc):
        k = pl.program_id(2)

        @pl.when(k == 0)
        def _():
            acc[...] = jnp.zeros_like(acc)

        acc[...] += lax.dot_general(a_ref[...].astype(MM), b_ref[...].astype(MM), TN, preferred_element_type=F32)

        @pl.when(k == nk - 1)
        def _():
            if shard is None:
                o_ref[...] = acc[...].astype(out_dtype)
            elif shard == "m":
                for s in range(tm // sw):
                    o_ref[s] = acc[s * sw:(s + 1) * sw, :].astype(out_dtype)
            else:
                for s in range(tn // sw):
                    o_ref[s] = acc[:, s * sw:(s + 1) * sw].astype(out_dtype)

    if shard is None:
        out_spec, out_shape = pl.BlockSpec((tm, tn), lambda mi, ni, k: (mi, ni)), (m, n)
    elif shard == "m":
        out_spec, out_shape = pl.BlockSpec((tm // sw, sw, tn), lambda mi, ni, k: (mi, 0, ni)), (m // sw, sw, n)
    else:
        out_spec, out_shape = pl.BlockSpec((tn // sw, tm, sw), lambda mi, ni, k: (ni, mi, 0)), (n // sw, m, sw)
    return pl.pallas_call(
        body, name=name, grid=(m // tm, n // tn, nk),
        in_specs=[pl.BlockSpec((TM, tm), lambda mi, ni, k: (k, mi)), pl.BlockSpec((TM, tn), lambda mi, ni, k: (k, ni))],
        out_specs=out_spec, out_shape=S(out_shape, out_dtype),
        scratch_shapes=[pltpu.VMEM((tm, tn), F32)],
        compiler_params=_params(3),
    )(a, b)


def mlp_bwd(dh3, r, w1, w2, h2, g, name):
    tp = dh3.shape[0]
    nf = DFF // (FS * FC)

    def body(dh3_ref, r_ref, w1_ref, w2_ref, h2_ref, g_ref, da_ref, dh2_ref, dg_ref, acc, dhb):
        i, j = pl.program_id(0), pl.program_id(1)

        @pl.when((i == 0) & (j == 0))
        def _():
            dg_ref[...] = jnp.zeros_like(dg_ref)

        @pl.when(j == 0)
        def _():
            acc[...] = jnp.zeros_like(acc)
            dhb[...] = dh3_ref[...].astype(MM)

        for s in range(FS):
            cols = slice(s * FC, (s + 1) * FC)
            dz = lax.dot_general(dhb[...], w2_ref[s], NT, preferred_element_type=F32)
            da = (dz * (2.0 * r_ref[:, cols].astype(F32))).astype(MM)
            da_ref[:, cols] = da
            acc[...] += lax.dot_general(da, w1_ref[s], NT, preferred_element_type=F32)

        @pl.when(j == nf - 1)
        def _():
            dx, dg = _rmsnorm_bwd(h2_ref[...], g_ref[...], acc[...])
            dh2_ref[...] = dh3_ref[...] + dx
            dg_ref[...] += dg

    return pl.pallas_call(
        body, name=name, grid=(tp // TM, nf),
        in_specs=[pl.BlockSpec((TM, D), lambda i, j: (i, 0)), pl.BlockSpec((TM, FS * FC), lambda i, j: (i, j)),
                  pl.BlockSpec((FS, D, FC), lambda i, j: (j, 0, 0)), pl.BlockSpec((FS, FC, D), lambda i, j: (j, 0, 0)),
                  pl.BlockSpec((TM, D), lambda i, j: (i, 0)), _full((1, D))],
        out_specs=[pl.BlockSpec((TM, FS * FC), lambda i, j: (i, j)), pl.BlockSpec((TM, D), lambda i, j: (i, 0)),
                   _full((1, D))],
        out_shape=[S((tp, DFF), MM), S((tp, D), F32), S((1, D), F32)],
        scratch_shapes=[pltpu.VMEM((TM, D), F32), pltpu.VMEM((TM, D), MM)],
        compiler_params=_params(2),
    )(dh3, r, w1, w2, h2, g)


def out_proj_bwd(dh2, wo, name):
    tp = dh2.shape[0]

    def body(dh_ref, wo_ref, do_ref, dcc_ref):
        dcat = lax.dot_general(dh_ref[...].astype(MM), wo_ref[...].reshape(D, D), NT, preferred_element_type=F32)
        do_ref[...] = dcat[:, 0:DA].astype(MM)
        dcc_ref[...] = dcat[:, DA:D]

    rows = lambda n: pl.BlockSpec((TM, n), lambda i: (i, 0))
    return pl.pallas_call(
        body, name=name, grid=(tp // TM,),
        in_specs=[rows(D), _full((NDEV, D // NDEV, D))],
        out_specs=[rows(DA), rows(2 * DC)],
        out_shape=[S((tp, DA), MM), S((tp, 2 * DC), F32)],
        compiler_params=_params(1),
    )(dh2, wo)


def attn_bwd_prep(qa, do, o, lse, name):
    tp = qa.shape[0]

    def body(q_ref, do_ref, o_ref, lse_ref, qb_ref, dob_ref, qbt_ref, dobt_ref):
        lse = lse_ref[...]
        lane = lax.broadcasted_iota(jnp.int32, (TM, HT), 1)
        for hd in range(H):
            t, hs = slice(hd * HT, (hd + 1) * HT), slice(hd * DH, (hd + 1) * DH)
            dof = do_ref[:, hs].astype(F32)
            dd = jnp.sum(dof * o_ref[:, hs].astype(F32), axis=1, keepdims=True)
            dob = _aug_tile(hd, dof, [0.0] + [-v for v in _split3(dd)])
            qb = q_ref[:, t].astype(F32)
            for pos, v in enumerate(_split3(lse[:, hd:hd + 1])):
                qb = jnp.where(lane == _aug_lane(hd, 6 + pos), -v, qb)
            qb_ref[:, t] = qb.astype(MM)
            dob_ref[:, t] = dob.astype(MM)
            qbt_ref[t, :] = qb.T.astype(MM)
            dobt_ref[t, :] = dob.T.astype(MM)

    rows = lambda n: pl.BlockSpec((TM, n), lambda i: (i, 0))
    cols = pl.BlockSpec((H * HT, TM), lambda i: (0, i))
    return pl.pallas_call(
        body, name=name, grid=(tp // TM,),
        in_specs=[rows(H * HT), rows(DA), rows(DA), rows(128)],
        out_specs=[rows(H * HT), rows(H * HT), cols, cols],
        out_shape=[S((tp, H * HT), MM), S((tp, H * HT), MM), S((H * HT, tp), MM), S((H * HT, tp), MM)],
        compiler_params=_params(1),
    )(qa, do, o, lse)


def attn_bwd(qb, dob, qbt, dobt, ka, va, name, rider=None):
    tp = qb.shape[0]
    nb = tp // TM

    pairs = [(j, i) for j in range(nb) for i in range(j, nb)]
    j_tab, i_tab = (jnp.asarray([p[a] for p in pairs], jnp.int32) for a in (0, 1))

    def body(j_ref, i_ref, *refs):
        own, copies = _split_rider(refs, 6, 5, rider)
        (qb_ref, dob_ref, qbt_ref, dobt_ref, k_ref, v_ref, dq_ref, dk_ref, dv_ref, dcq_ref, dck_ref,
         dq_acc, dcq_acc, dkt_acc, dvt_acc, dck_acc, stage, sems) = own
        n = pl.program_id(0)
        j, i = j_ref[n], i_ref[n]

        @pl.when(n == 0)
        def _():
            if copies:
                copies[0]()
            dq_acc[...] = jnp.zeros_like(dq_acc)
            dcq_acc[...] = jnp.zeros_like(dcq_acc)

        @pl.when(i == j)
        def _():
            dkt_acc[...] = jnp.zeros_like(dkt_acc)
            dvt_acc[...] = jnp.zeros_like(dvt_acc)
            dck_acc[...] = jnp.zeros_like(dck_acc)

        def step(diag):
            rows = pl.ds(pl.multiple_of(i * TM, TM), TM)
            dcq = dcq_acc[rows, :]
            lane = lax.broadcasted_iota(jnp.int32, (TM, HT), 1)
            mask = _causal_mask() if diag else None
            for hd in range(H):
                t = slice(hd * HT, (hd + 1) * HT)
                s = lax.dot_general(qb_ref[:, t], k_ref[:, t], NT, preferred_element_type=F32)
                dp = lax.dot_general(dob_ref[:, t], v_ref[:, t], NT, preferred_element_type=F32)
                if diag:
                    s = jnp.where(mask, s, NEG)
                p = jnp.exp(s)
                ds = p * dp
                dsb = ds.astype(MM)
                dvt_acc[t, :] += jnp.dot(dobt_ref[t, :], p.astype(MM), preferred_element_type=F32)
                dkt_acc[t, :] += jnp.dot(qbt_ref[t, :], dsb, preferred_element_type=F32)
                dqh = jnp.dot(dsb, k_ref[:, t], preferred_element_type=F32)
                if hd % 2 == 0:
                    dq_even = dqh
                else:
                    pair = slice((hd // 2) * HT, (hd // 2 + 1) * HT)
                    dq_acc[rows, pair] += jnp.where(lane < DH, dq_even, dqh)
                dcq = dcq + jnp.where(lane == hd, jnp.sum(ds, axis=1, keepdims=True), 0.0)
                dck_acc[hd:hd + 1, :] += jnp.sum(ds, axis=0, keepdims=True)
            dcq_acc[rows, :] = dcq
            return rows

        @pl.when(i > j)
        def _():
            step(False)

        @pl.when(i == j)
        def _():
            rows = step(True)
            stage[...] = (dq_acc[rows, :] * SCALE).astype(MM)
            out = [pltpu.make_async_copy(stage, dq_ref.at[rows, :], sems.at[0]),
                   pltpu.make_async_copy(dcq_acc.at[rows, :], dcq_ref.at[rows, :], sems.at[1])]
            for cp in out:
                cp.start()
            for cp in out:
                cp.wait()

        @pl.when(i == nb - 1)
        def _():
            dk_ref[...] = _merge_pairs([dkt_acc[hd * HT:(hd + 1) * HT, :].T for hd in range(H)]).astype(MM)
            dv_ref[...] = _merge_pairs([dvt_acc[hd * HT:(hd + 1) * HT, :].T for hd in range(H)]).astype(MM)
            dck_ref[...] = dck_acc[...]

        if copies:
            @pl.when(n == len(pairs) - 1)
            def _():
                copies[1]()

    qi = lambda w: pl.BlockSpec((TM, w), lambda n, jt, it: (it[n], 0))
    qt = pl.BlockSpec((H * HT, TM), lambda n, jt, it: (0, it[n]))
    kj = lambda w: pl.BlockSpec((TM, w), lambda n, jt, it: (jt[n], 0))
    r_n = rider.k_n if rider else 0
    return pl.pallas_call(
        body, name=name,
        grid_spec=pltpu.PrefetchScalarGridSpec(
            num_scalar_prefetch=2, grid=(len(pairs),),
            in_specs=[qi(H * HT), qi(H * HT), qt, qt, kj(H * HT), kj(H * HT)] + [ANY] * r_n,
            out_specs=[ANY, kj(DA), kj(DA), ANY, pl.BlockSpec((H, TM), lambda n, jt, it: (0, jt[n]))] + [ANY] * r_n,
            scratch_shapes=[pltpu.VMEM((tp, DA), F32), pltpu.VMEM((tp, 128), F32), pltpu.VMEM((H * HT, TM), F32),
                            pltpu.VMEM((H * HT, TM), F32), pltpu.VMEM((H, TM), F32), pltpu.VMEM((TM, DA), MM),
                            pltpu.SemaphoreType.DMA((2,))] + (rider.scratch if rider else [])),
        out_shape=[S((tp, DA), MM), S((tp, DA), MM), S((tp, DA), MM), S((tp, 128), F32), S((H, tp), F32)]
        + (rider.out_shape if rider else []),
        compiler_params=_params(1),
    )(j_tab, i_tab, qb, dob, qbt, dobt, ka, va, *(rider.srcs if rider else []))


def fgate_bwd(dcq, dck, sg, name):
    tp = dcq.shape[0]
    nb = tp // TM

    def body(dcq_ref, dck_ref, sg_ref, dz_ref, db_ref, carry):
        @pl.when(pl.program_id(0) == 0)
        def _():
            carry[...] = jnp.zeros_like(carry)
            db_ref[...] = jnp.zeros_like(db_ref)

        row = lax.broadcasted_iota(jnp.int32, (TM, TM), 0)
        col = lax.broadcasted_iota(jnp.int32, (TM, TM), 1)
        tri = (row <= col).astype(F32)
        dl = jnp.dot(tri, dcq_ref[...] - dck_ref[...], precision=lax.Precision.HIGHEST,
                     preferred_element_type=F32) + carry[...]
        carry[...] = dl[0:1, :]
        dz = dl * sg_ref[...]
        dz_ref[...] = dz.astype(MM)
        db_ref[...] += jnp.sum(dz, axis=0, keepdims=True)

    rev = pl.BlockSpec((TM, 128), lambda i: (nb - 1 - i, 0))
    return pl.pallas_call(
        body, name=name, grid=(nb,),
        in_specs=[rev, rev, rev], out_specs=[rev, _full((1, 128))],
        out_shape=[S((tp, 128), MM), S((1, 128), F32)],
        scratch_shapes=[pltpu.VMEM((1, 128), F32)],
        compiler_params=_params(1),
    )(dcq, dck, sg)


def conv_bwd_pointwise(dcc, sv, pc, lng, lnb, wpw, name):
    tp = dcc.shape[0]

    def body(dcc_ref, sv_ref, b_ref, lng_ref, lnb_ref, wpw_ref, gc_ref, act_ref, pg_ref):
        @pl.when(pl.program_id(0) == 0)
        def _():
            pg_ref[...] = jnp.zeros_like(pg_ref)

        dconf = dcc_ref[:, 0:DC]
        g = lng_ref[...]
        xh, rs, ln = _layernorm_parts(sv_ref[:, 0:DC], g, lnb_ref[...])
        sig = _sigmoid(ln)
        act_ref[...] = (ln * sig).astype(MM)
        dact = lax.dot_general(dconf.astype(MM), wpw_ref[...], NT, preferred_element_type=F32)
        dln = dact * (sig * (1.0 + ln * (1.0 - sig)))
        dxh = dln * g
        ddw = rs * (dxh - jnp.mean(dxh, axis=-1, keepdims=True) - xh * jnp.mean(dxh * xh, axis=-1, keepdims=True))
        gc_ref[:, 0:DC] = ddw
        gc_ref[:, DC:2 * DC] = dcc_ref[:, DC:2 * DC] * b_ref[...]
        cs = lambda t: jnp.sum(t, axis=0, keepdims=True)
        pg_ref[0:1, :] += cs(dconf)
        pg_ref[1:2, :] += cs(dln * xh)
        pg_ref[2:3, :] += cs(dln)
        pg_ref[3:4, :] += cs(ddw)

    rows = lambda n: pl.BlockSpec((TM, n), lambda i: (i, 0))
    return pl.pallas_call(
        body, name=name, grid=(tp // TM,),
        in_specs=[rows(2 * DC), rows(2 * DC), pl.BlockSpec((TM, DC), lambda i: (i, 2)),
                  _full((1, DC)), _full((1, DC)), _full((DC, DC))],
        out_specs=[rows(2 * DC), rows(DC), _full((8, DC))],
        out_shape=[S((tp, 2 * DC), F32), S((tp, DC), MM), S((8, DC), F32)],
        compiler_params=_params(1),
    )(dcc, sv, pc, lng, lnb, wpw)


def conv_bwd_taps(gc, pc, dcc, sv, wdw, wsc, name):
    tp = gc.shape[0]
    nb = tp // TM
    hb = TM // HALO

    def body(gc_ref, gn_ref, pc_ref, hl_ref, dcc_ref, sv_ref, wdw_ref, wsc_ref, dpc_ref, wg_ref, ge, xe, ce, gs, xs):
        i = pl.program_id(0)

        @pl.when(i == 0)
        def _():
            wg_ref[...] = jnp.zeros_like(wg_ref)

        a, gt = pc_ref[:, 0:DC], pc_ref[:, DC:2 * DC]
        c, u = pc_ref[:, 3 * DC:4 * DC], pc_ref[:, 4 * DC:5 * DC]
        sig = _sigmoid(gt)
        _fill_halo(xe, hl_ref[:, 0:DC] * _sigmoid(hl_ref[:, DC:2 * DC]), a * sig, i == 0)
        _fill_halo(ce, hl_ref[:, 3 * DC:4 * DC] * hl_ref[:, 4 * DC:5 * DC], c * u, i == 0)
        ge[0:TM, :] = gc_ref[...]
        ge[TM:TM + HALO, :] = jnp.where(i == nb - 1, 0.0, gn_ref[...])
        _shift_copies(gs, ge, slice(0, DC))
        _shift_copies(xs, xe)
        ddw, dcv = gc_ref[:, 0:DC], gc_ref[:, DC:2 * DC]
        dglu = jnp.zeros((TM, DC), F32)
        for k in range(CK):
            dglu = dglu + wdw_ref[k:k + 1, :] * _window(ge, gs, CK - 1 - k, slice(0, DC))
            wg_ref[k:k + 1, :] += jnp.sum(ddw * _window(xe, xs, HALO - (CK - 1) + k), axis=0, keepdims=True)
        dcu = jnp.zeros((TM, DC), F32)
        for k in range(SK):
            dcu = dcu + wsc_ref[k:k + 1, :] * ge[pl.ds(SK - 1 - k, TM), DC:2 * DC]
            wg_ref[32 + k:33 + k, :] += jnp.sum(dcv * ce[pl.ds(HALO - (SK - 1) + k, TM), :], axis=0, keepdims=True)
        dpc_ref[:, 0:DC] = (dglu * sig).astype(MM)
        dpc_ref[:, DC:2 * DC] = (dglu * a * sig * (1.0 - sig)).astype(MM)
        dpc_ref[:, 2 * DC:3 * DC] = (dcc_ref[:, DC:2 * DC] * sv_ref[:, DC:2 * DC]).astype(MM)
        dpc_ref[:, 3 * DC:4 * DC] = (dcu * u).astype(MM)
        dpc_ref[:, 4 * DC:5 * DC] = (dcu * c).astype(MM)

    rows = lambda n: pl.BlockSpec((TM, n), lambda i: (i, 0))
    return pl.pallas_call(
        body, name=name, grid=(nb,),
        in_specs=[rows(2 * DC), pl.BlockSpec((HALO, 2 * DC), lambda i: (jnp.minimum((i + 1) * hb, nb * hb - 1), 0)),
                  rows(5 * DC), pl.BlockSpec((HALO, 5 * DC), lambda i: (jnp.maximum(i * hb - 1, 0), 0)),
                  rows(2 * DC), rows(2 * DC), _full((32, DC)), _full((8, DC))],
        out_specs=[rows(5 * DC), _full((40, DC))],
        out_shape=[S((tp, 5 * DC), MM), S((40, DC), F32)],
        scratch_shapes=[pltpu.VMEM((TM + HALO, 2 * DC), F32), pltpu.VMEM((HALO + TM, DC), F32),
                        pltpu.VMEM((HALO + TM, DC), F32), pltpu.VMEM((7, TM + HALO - 8, DC), F32),
                        pltpu.VMEM((7, TM + HALO - 8, DC), F32)],
        compiler_params=_params(1),
    )(gc, gc, pc, pc, dcc, sv, wdw, wsc)


def in_proj_bwd(dproj, w, h, g, dh_in, name, rider=None):
    tp = h.shape[0]
    nb = tp // TM

    def body(*refs):
        (dp_ref, w_ref, h_ref, g_ref, di_ref, dh_ref, dg_ref), copies = _split_rider(refs, 5, 2, rider)
        i = pl.program_id(0)

        @pl.when(i == 0)
        def _():
            if copies:
                copies[0]()
            dg_ref[...] = jnp.zeros_like(dg_ref)

        dhn = lax.dot_general(dp_ref[...], w_ref[...], NT, preferred_element_type=F32)
        dx, dg = _rmsnorm_bwd(h_ref[...], g_ref[...], dhn)
        dh_ref[...] = di_ref[...] + dx
        dg_ref[...] += dg

        if copies:
            @pl.when(i == nb - 1)
            def _():
                copies[1]()

    rows = lambda n: pl.BlockSpec((TM, n), lambda i: (i, 0))
    r_n = rider.k_n if rider else 0
    return pl.pallas_call(
        body, name=name, grid=(nb,),
        in_specs=[rows(NP_IN), _full((D, NP_IN)), rows(D), _full((1, D)), rows(D)] + [ANY] * r_n,
        out_specs=[rows(D), _full((1, D))] + [ANY] * r_n,
        out_shape=[S((tp, D), F32), S((1, D), F32)] + (rider.out_shape if rider else []),
        scratch_shapes=rider.scratch if rider else [],
        compiler_params=_params(1),
    )(dproj, w, h, g, dh_in, *(rider.srcs if rider else []))


def adamw(recvs, w, m, v, rb, name):
    l_n, r_n, c_n = w.shape

    def body(*refs):
        p_refs = refs[:l_n]
        w_ref, m_ref, v_ref, g_ref, d_ref, m2_ref, v2_ref = refs[l_n:]
        for l in range(l_n):
            g = p_refs[l][0].astype(F32)
            for s in range(1, NDEV):
                g = g + p_refs[l][s].astype(F32)
            m2 = ADAM_B1 * m_ref[l] + (1.0 - ADAM_B1) * g
            v2 = ADAM_B2 * v_ref[l] + (1.0 - ADAM_B2) * (g * g)
            m_hat = m2 / (1.0 - ADAM_B1 ** ADAM_STEP)
            v_hat = v2 / (1.0 - ADAM_B2 ** ADAM_STEP)
            g_ref[l] = g
            d_ref[l] = -ADAM_LR * (m_hat / (jnp.sqrt(v_hat) + ADAM_EPS) + ADAM_WD * w_ref[l])
            m2_ref[l] = m2
            v2_ref[l] = v2

    blk = pl.BlockSpec((l_n, rb, c_n), lambda r: (0, r, 0))
    return pl.pallas_call(
        body, name=name, grid=(r_n // rb,),
        in_specs=[pl.BlockSpec((NDEV, rb, c_n), lambda r: (0, r, 0))] * l_n + [blk, blk, blk],
        out_specs=[blk] * 4, out_shape=[S(w.shape, F32)] * 4,
        compiler_params=_params(1),
    )(*recvs, w, m, v)


TINY_ROWS = 168
REP_ROWS = 64


def _pack(parts, rows):
    flat = jnp.concatenate([p.reshape(-1) for p in parts])
    return jnp.pad(flat, (0, rows * 128 - flat.shape[0])).reshape(rows, 128)


def _unpack(buf, shapes):
    flat, out, o = buf.reshape(-1), [], 0
    for s in shapes:
        n = 1
        for d in s:
            n *= d
        out.append(flat[o:o + n].reshape(s))
        o += n
    return out


TINY_SHAPES = [(DEPTH, CK, DC // NDEV), (DEPTH, SK, DC // NDEV), (NM, D // NDEV), (DEPTH, DC // NDEV, DC)]
REP_SHAPES = [(DEPTH, D), (DEPTH, H), (DEPTH, DC), (DEPTH, DC), (DEPTH, DC), (DEPTH, DC), (DEPTH, D), (D,)]


def _to_padded_cols(w):
    pad = jnp.zeros(w.shape[:-1] + (NP_IN - N_IN,), w.dtype)
    return jnp.concatenate([w[..., :C0], w[..., C0 + H:], w[..., C0:C0 + H], pad], axis=-1)


def _from_padded_cols(w):
    return jnp.concatenate([w[..., :C0], w[..., F0:F0 + H], w[..., C0:F0]], axis=-1)


def kernel(x, meta_tokens, mix_norm_g, w_in, b_forget, w_conf_dw, b_conf_dw, conf_ln_g, conf_ln_b, w_conf_pw, b_conf_pw, w_sc_conv, w_out, mlp_norm_g, w_mlp1, w_mlp2, final_norm_g, loss_target, m_meta_tokens, m_mix_norm_g, m_w_in, m_b_forget, m_w_conf_dw, m_b_conf_dw, m_conf_ln_g, m_conf_ln_b, m_w_conf_pw, m_b_conf_pw, m_w_sc_conv, m_w_out, m_mlp_norm_g, m_w_mlp1, m_w_mlp2, m_final_norm_g, v_meta_tokens, v_mix_norm_g, v_w_in, v_b_forget, v_w_conf_dw, v_b_conf_dw, v_conf_ln_g, v_conf_ln_b, v_w_conf_pw, v_b_conf_pw, v_w_sc_conv, v_w_out, v_mlp_norm_g, v_w_mlp1, v_w_mlp2, v_final_norm_g):
    seq = x.shape[1]
    t_real = NM + seq
    tp = -(-t_real // TM) * TM

    tiny_w = _pack([w_conf_dw, w_sc_conv, meta_tokens, w_conf_pw], TINY_ROWS)
    big = lambda l: [w_in[l].astype(MM), w_out[l].astype(MM), w_mlp1[l].astype(MM), w_mlp2[l].astype(MM)]
    *first, g_tiny = gather(big(0) + [tiny_w], "gather_weights0")
    gathered = {0: first}
    n_sh = w_in.shape[-1]
    assert w_mlp1.shape[-1] == FC and w_mlp2.shape[-2] == FC
    tiny = [_unpack(g_tiny[s], TINY_SHAPES) for s in range(NDEV)]
    wdw = jnp.concatenate([t[0] for t in tiny], axis=-1)
    wsc = jnp.concatenate([t[1] for t in tiny], axis=-1)
    meta = jnp.concatenate([t[2] for t in tiny], axis=-1)
    wpw = jnp.concatenate([t[3] for t in tiny], axis=1).astype(MM)
    wdw = jnp.pad(wdw, ((0, 0), (0, 32 - CK), (0, 0)))
    wsc = jnp.pad(wsc, ((0, 0), (0, 8 - SK), (0, 0)))
    bfp = jnp.pad(b_forget, ((0, 0), (0, 128 - H)))

    row = lambda a: a.reshape(1, -1)

    h = jnp.concatenate([meta, x[0], jnp.zeros((tp - t_real, D), F32)], axis=0)
    tgt = jnp.pad(loss_target[0], ((NM, tp - t_real), (0, 0)))
    saved = []
    for l in range(DEPTH):
        g_in, g_out, g_w1, g_w2 = gathered[l]
        win = _to_padded_cols(g_in.transpose(1, 0, 2).reshape(D, NDEV * n_sh))
        hn, qa, ka, va, pc, sg = in_proj(h, row(mix_norm_g[l]), win, row(bfp[l]), f"in_proj{l}")
        rider = AllToAll(big(l + 1), [False] * 4) if l + 1 < DEPTH else None
        o, lse, *arrived = attn_fwd(qa, ka, va, f"attn_fwd{l}", rider)
        if rider:
            gathered[l + 1] = arrived
        cc, sv = conv_fwd(pc, wdw[l], row(b_conf_dw[l]), row(conf_ln_g[l]), row(conf_ln_b[l]), wpw[l],
                          row(b_conf_pw[l]), wsc[l], f"conv_fwd{l}")
        h2, hn2 = out_proj(h, o, cc, g_out, row(mlp_norm_g[l]), f"out_proj{l}")
        r, z, h3 = mlp_fwd(hn2, h2, g_w1, g_w2, f"mlp_fwd{l}")
        saved.append((h, hn, qa, ka, va, pc, sg, o, lse, cc, sv, h2, hn2, r, z, win))
        h = h3

    dh, loss_part, d_gf = loss_head(h, tgt, row(final_norm_g), t_real, "loss_head")
    loss = lax.psum(loss_part[0, 0], ("x", "y", "c"))

    gw, recv, d_win = {}, {}, {}
    for l in reversed(range(DEPTH)):
        h0, hn, qa, ka, va, pc, sg, o, lse, cc, sv, h2, hn2, r, z, win = saved[l]
        _, g_out, g_w1, g_w2 = gathered[l]
        da, dh2, gw["mlp_g", l] = mlp_bwd(dh, r, g_w1, g_w2, h2, row(mlp_norm_g[l]), f"mlp_bwd{l}")
        d_w1 = matmul_tn(hn2, da, D, 4 * FC, f"dw_mlp1_{l}", out_dtype=MM, shard="n", sw=FC)
        d_w2 = matmul_tn(z, dh, 4 * FC, D, f"dw_mlp2_{l}", out_dtype=MM, shard="m", sw=FC)
        do, dcc = out_proj_bwd(dh2, g_out, f"out_proj_bwd{l}")
        d_wout = jnp.concatenate([matmul_tn(o, dh2, DA, D, f"dw_out_a{l}"), matmul_tn(cc, dh2, 2 * DC, D, f"dw_out_c{l}")],
                                 axis=0).reshape(NDEV, D // NDEV, D).astype(MM)
        qb, dob, qbt, dobt = attn_bwd_prep(qa, do, o, lse, f"attn_bwd_prep{l}")
        riding = [("wout", l, d_wout), ("w1", l, d_w1), ("w2", l, d_w2)] + ([("win", l + 1, d_win[l + 1])] if l + 1 < DEPTH else [])
        dq, dk, dv, dcq, dck, *arrived = attn_bwd(qb, dob, qbt, dobt, ka, va, f"attn_bwd{l}",
                                                  AllToAll([a for _, _, a in riding], [True] * len(riding)))
        for (k, kl, _), a in zip(riding, arrived):
            recv[k, kl] = a
        dz, gw["bf", l] = fgate_bwd(dcq, jnp.pad(dck.T, ((0, 0), (0, 128 - H))), sg, f"fgate_bwd{l}")
        gc, act, pg = conv_bwd_pointwise(dcc, sv, pc, row(conf_ln_g[l]), row(conf_ln_b[l]), wpw[l], f"conv_bwd_pw{l}")
        gw["wpw", l] = matmul_tn(act, dcc, DC, DC, f"dw_conf_pw{l}", n=DC)
        dpc, wg = conv_bwd_taps(gc, pc, dcc, sv, wdw[l], wsc[l], f"conv_bwd_taps{l}")
        gw["pg", l], gw["wg", l] = pg, wg
        dproj = jnp.concatenate([dq, dk, dv, dpc, dz], axis=1)
        d_win[l] = _from_padded_cols(matmul_tn(hn, dproj, 512, NP_IN, f"dw_in{l}")).reshape(D, NDEV, n_sh).transpose(
            1, 0, 2).astype(MM)
        rider = AllToAll([d_win[0]], [True]) if l == 0 else None
        dh, gw["mix_g", l], *arrived = in_proj_bwd(dproj, win, h0, row(mix_norm_g[l]), dh2, f"in_proj_bwd{l}", rider)
        if rider:
            recv["win", 0] = arrived[0]

    grad_x = dh[NM:t_real][None]
    stack = lambda k: jnp.stack([gw[k, l] for l in range(DEPTH)])

    d_wdw = stack("wg")[:, 0:CK].reshape(DEPTH, CK, NDEV, DC // NDEV).transpose(2, 0, 1, 3)
    d_wsc = stack("wg")[:, 32:32 + SK].reshape(DEPTH, SK, NDEV, DC // NDEV).transpose(2, 0, 1, 3)
    d_meta = dh[0:NM].reshape(NM, NDEV, D // NDEV).transpose(1, 0, 2)
    d_wpw = stack("wpw").reshape(DEPTH, NDEV, DC // NDEV, DC).transpose(1, 0, 2, 3)
    d_tiny = jnp.stack([_pack([d_wdw[p], d_wsc[p], d_meta[p], d_wpw[p]], TINY_ROWS) for p in range(NDEV)])
    pgs = stack("pg")
    d_rep = _pack([stack("mix_g").reshape(DEPTH, D), stack("bf")[:, 0, :H], pgs[:, 3], pgs[:, 1], pgs[:, 2], pgs[:, 0],
                   stack("mlp_g").reshape(DEPTH, D), d_gf.reshape(D)], REP_ROWS)
    r_tiny, r_rep = exchange([d_tiny, d_rep], [True, False], "exchange_grads")

    per_layer = lambda k: [recv[k, l] for l in range(DEPTH)]
    res = {}
    res["w_in"] = adamw(per_layer("win"), w_in, m_w_in, v_w_in, 256, "adamw_w_in")
    res["w_out"] = adamw(per_layer("wout"), w_out, m_w_out, v_w_out, D // NDEV, "adamw_w_out")
    res["w_mlp1"] = adamw(per_layer("w1"), w_mlp1, m_w_mlp1, v_w_mlp1, 256, "adamw_w_mlp1")
    res["w_mlp2"] = adamw(per_layer("w2"), w_mlp2, m_w_mlp2, v_w_mlp2, 128, "adamw_w_mlp2")
    tiny_names = ["w_conf_dw", "w_sc_conv", "meta_tokens", "w_conf_pw"]
    tiny_wmv = [[w_conf_dw, w_sc_conv, meta_tokens, w_conf_pw], [m_w_conf_dw, m_w_sc_conv, m_meta_tokens, m_w_conf_pw],
                [v_w_conf_dw, v_w_sc_conv, v_meta_tokens, v_w_conf_pw]]
    rep_names = ["mix_norm_g", "b_forget", "b_conf_dw", "conf_ln_g", "conf_ln_b", "b_conf_pw", "mlp_norm_g", "final_norm_g"]
    rep_wmv = [[mix_norm_g, b_forget, b_conf_dw, conf_ln_g, conf_ln_b, b_conf_pw, mlp_norm_g, final_norm_g],
               [m_mix_norm_g, m_b_forget, m_b_conf_dw, m_conf_ln_g, m_conf_ln_b, m_b_conf_pw, m_mlp_norm_g, m_final_norm_g],
               [v_mix_norm_g, v_b_forget, v_b_conf_dw, v_conf_ln_g, v_conf_ln_b, v_b_conf_pw, v_mlp_norm_g, v_final_norm_g]]
    for names, wmv, shapes, recv_pack, rows_n, nm in ((tiny_names, tiny_wmv, TINY_SHAPES, r_tiny, TINY_ROWS, "adamw_tiny"),
                                                 (rep_names, rep_wmv, REP_SHAPES, r_rep, REP_ROWS, "adamw_rep")):
        packed = [_pack(group, rows_n)[None] for group in wmv]
        outs = adamw([recv_pack], *packed, rows_n, nm)
        parts = [_unpack(a[0], shapes) for a in outs]
        for k, n in enumerate(names):
            res[n] = [parts[q][k] for q in range(4)]

    order = ["meta_tokens", "mix_norm_g", "w_in", "b_forget", "w_conf_dw", "b_conf_dw", "conf_ln_g", "conf_ln_b",
             "w_conf_pw", "b_conf_pw", "w_sc_conv", "w_out", "mlp_norm_g", "w_mlp1", "w_mlp2", "final_norm_g"]
    return (loss, grad_x, *[res[n][0] for n in order], *[res[n][1] for n in order],
            *[res[n][2] for n in order], *[res[n][3] for n in order])
```

```python
import functools

import jax
import jax.numpy as jnp
from jax import lax
from jax.experimental import pallas as pl
from jax.experimental.pallas import tpu as pltpu

F32 = jnp.float32
MM = jnp.bfloat16

D = 1024
H = 8
DH = 64
DA = H * DH
HT = 128
DC = 256
NM = 16
CK = 31
SK = 3
DFF = 4096
DEPTH = 2
N_IN = 3 * DA + H + 2 * DC + 3 * DC
NP_IN = 3 * DA + 5 * DC + 128
C0 = 3 * DA
F0 = 3 * DA + 5 * DC
EPS = 1e-6
TM = 640
HALO = 32
FC = 512
FS = 4
NDEV = 8
SCALE = DH ** -0.5
NEG = -1e30

ADAM_LR, ADAM_B1, ADAM_B2, ADAM_EPS, ADAM_WD, ADAM_STEP = 0.001, 0.9, 0.999, 1e-08, 0.01, 10

VMEM_LIMIT = 56 * 1024 * 1024

S = jax.ShapeDtypeStruct
NT = (((1,), (1,)), ((), ()))
TN = (((0,), (0,)), ((), ()))


def _params(n_grid):
    return pltpu.CompilerParams(dimension_semantics=("arbitrary",) * n_grid, vmem_limit_bytes=VMEM_LIMIT)


def _sigmoid(x):
    return 1.0 / (1.0 + jnp.exp(-x))


def _full(shape):
    n = len(shape)
    return pl.BlockSpec(shape, lambda *_: (0,) * n)


def _lane_put(dst, col, h):
    lane = lax.broadcasted_iota(jnp.int32, dst.shape, 1)
    return jnp.where(lane == h, col, dst)


ANY = pl.BlockSpec(memory_space=pl.ANY)


class AllToAll:
    def __init__(self, srcs, per_peer):
        self.srcs, self.per_peer, self.k_n = list(srcs), list(per_peer), len(srcs)
        self.out_shape = [S((NDEV,) + (a.shape[1:] if pp else a.shape), a.dtype) for a, pp in zip(srcs, per_peer)]
        self.scratch = [pltpu.SemaphoreType.DMA((self.k_n, NDEV - 1)), pltpu.SemaphoreType.DMA((self.k_n, NDEV - 1)),
                        pltpu.SemaphoreType.DMA((self.k_n,))]

    def copies(self, src, out, send_sems, recv_sems, local_sems):
        x, y, c = lax.axis_index("x"), lax.axis_index("y"), lax.axis_index("c")
        me = 4 * x + 2 * y + c

        def piece(k, p):
            return src[k].at[p] if self.per_peer[k] else src[k]

        local = [pltpu.make_async_copy(piece(k, me), out[k].at[me], local_sems.at[k]) for k in range(self.k_n)]
        sends, recvs = [], []
        for r in range(1, NDEV):
            px = 1 - x if (r >> 2) & 1 else x
            py = 1 - y if (r >> 1) & 1 else y
            pc = 1 - c if r & 1 else c
            pidx = 4 * px + 2 * py + pc
            for k in range(self.k_n):
                sends.append(pltpu.make_async_remote_copy(
                    src_ref=piece(k, pidx), dst_ref=out[k].at[me],
                    send_sem=send_sems.at[k, r - 1], recv_sem=recv_sems.at[k, r - 1],
                    device_id=(px, py, pc), device_id_type=pl.DeviceIdType.MESH))
                recvs.append(pltpu.make_async_remote_copy(
                    src_ref=piece(k, pidx), dst_ref=out[k].at[pidx],
                    send_sem=send_sems.at[k, r - 1], recv_sem=recv_sems.at[k, r - 1],
                    device_id=(px, py, pc), device_id_type=pl.DeviceIdType.MESH))

        def start():
            for cp in local + sends:
                cp.start()

        def wait():
            for cp in recvs:
                cp.wait_recv()
            for cp in sends:
                cp.wait_send()
            for cp in local:
                cp.wait()

        return start, wait


def exchange(srcs, per_peer, name):
    plan = AllToAll(srcs, per_peer)

    def body(*refs):
        start, wait = plan.copies(refs[:plan.k_n], refs[plan.k_n:2 * plan.k_n], *refs[2 * plan.k_n:])
        start()
        wait()

    return pl.pallas_call(
        body, name=name, out_shape=plan.out_shape, in_specs=[ANY] * plan.k_n, out_specs=[ANY] * plan.k_n,
        scratch_shapes=plan.scratch,
    )(*srcs)


def gather(srcs, name):
    k_n = len(srcs)
    out_shape = [S((NDEV,) + a.shape, a.dtype) for a in srcs]

    def body(*refs):
        src, out = refs[:k_n], refs[k_n:2 * k_n]
        send_sems, recv_sems, local_sems = refs[2 * k_n:]
        x, y, c = lax.axis_index("x"), lax.axis_index("y"), lax.axis_index("c")
        me, sibling = (x, y, c), (x, y, 1 - c)
        chips = [(1 - x, y), (x, 1 - y), (1 - x, 1 - y)]

        def slot(k, dev):
            return out[k].at[4 * dev[0] + 2 * dev[1] + dev[2]]

        def copy(k, r, block, to, from_src=False):
            return pltpu.make_async_remote_copy(
                src_ref=src[k] if from_src else slot(k, block), dst_ref=slot(k, block),
                send_sem=send_sems.at[k, r], recv_sem=recv_sems.at[k, r],
                device_id=to, device_id_type=pl.DeviceIdType.MESH)

        local = [pltpu.make_async_copy(src[k], slot(k, me), local_sems.at[k]) for k in range(k_n)]
        first = [copy(k, 0, me, sibling, True) for k in range(k_n)]
        first += [copy(k, 1 + n, me, (*chip, c), True) for n, chip in enumerate(chips) for k in range(k_n)]
        for cp in local + first:
            cp.start()
        passed = []
        for n, chip in enumerate(chips):
            for k in range(k_n):
                copy(k, 1 + n, (*chip, c), me).wait_recv()
                passed.append(copy(k, 4 + n, (*chip, c), sibling))
                passed[-1].start()
        for k in range(k_n):
            copy(k, 0, sibling, me).wait_recv()
            for n, chip in enumerate(chips):
                copy(k, 4 + n, (*chip, 1 - c), me).wait_recv()
        for cp in first + passed:
            cp.wait_send()
        for cp in local:
            cp.wait()

    any_spec = pl.BlockSpec(memory_space=pl.ANY)
    return pl.pallas_call(
        body, name=name, out_shape=out_shape,
        in_specs=[any_spec] * k_n, out_specs=[any_spec] * k_n,
        scratch_shapes=[pltpu.SemaphoreType.DMA((k_n, NDEV - 1)), pltpu.SemaphoreType.DMA((k_n, NDEV - 1)),
                        pltpu.SemaphoreType.DMA((k_n,))],
    )(*srcs)


def in_proj(h, g, w, bf, name):
    tp = h.shape[0]

    def body(h_ref, g_ref, w_ref, bf_ref, hn_ref, qa_ref, ka_ref, va_ref, pc_ref, sg_ref, carry):
        i = pl.program_id(0)

        @pl.when(i == 0)
        def _():
            carry[...] = jnp.zeros_like(carry)

        x = h_ref[...]
        r = lax.rsqrt(jnp.mean(x * x, axis=-1, keepdims=True) + EPS)
        hn = (x * r * g_ref[...]).astype(MM)
        hn_ref[...] = hn
        pc_ref[...] = jnp.dot(hn, w_ref[:, C0:F0], preferred_element_type=F32)
        z = jnp.dot(hn, w_ref[:, F0:NP_IN], preferred_element_type=F32) + bf_ref[...]
        lane = lax.broadcasted_iota(jnp.int32, z.shape, 1)
        logf = jnp.where(lane < H, jnp.minimum(z, 0.0) - jnp.log(1.0 + jnp.exp(-jnp.abs(z))), 0.0)
        sg_ref[...] = 1.0 / (1.0 + jnp.exp(z))
        row = lax.broadcasted_iota(jnp.int32, (TM, TM), 0)
        col = lax.broadcasted_iota(jnp.int32, (TM, TM), 1)
        c = _tri_matmul(row >= col, logf) + carry[...]
        carry[...] = c[TM - 1:TM, :]
        qkv = jnp.dot(hn, w_ref[:, 0:C0], preferred_element_type=F32)
        one = [1.0, 1.0, 1.0]
        for hd in range(H):
            t = slice(hd * HT, (hd + 1) * HT)
            cs = list(_split3(c[:, hd:hd + 1]))
            qa_ref[:, t] = _aug_tile(hd, qkv[:, hd * DH:(hd + 1) * DH] * SCALE, cs + one).astype(MM)
            ka_ref[:, t] = _aug_tile(hd, qkv[:, DA + hd * DH:DA + (hd + 1) * DH], one + [-v for v in cs] + one).astype(MM)
            va_ref[:, t] = _aug_tile(hd, qkv[:, 2 * DA + hd * DH:2 * DA + (hd + 1) * DH], [1.0] + one).astype(MM)

    rows = lambda n: pl.BlockSpec((TM, n), lambda i: (i, 0))
    return pl.pallas_call(
        body, name=name, grid=(tp // TM,),
        in_specs=[rows(D), _full((1, D)), _full((D, NP_IN)), _full((1, 128))],
        out_specs=[rows(D), rows(H * HT), rows(H * HT), rows(H * HT), rows(5 * DC), rows(128)],
        out_shape=[S((tp, D), MM), S((tp, H * HT), MM), S((tp, H * HT), MM), S((tp, H * HT), MM),
                   S((tp, 5 * DC), F32), S((tp, 128), F32)],
        scratch_shapes=[pltpu.VMEM((1, 128), F32)],
        compiler_params=_params(1),
    )(h, g, w, bf)


def _split3(c):
    hi = c.astype(MM).astype(F32)
    mid = (c - hi).astype(MM).astype(F32)
    lo = (c - hi - mid).astype(MM).astype(F32)
    return hi, mid, lo


def _tri_matmul(keep, x):
    pieces = jnp.concatenate(_split3(x), axis=1).astype(MM)
    r = jnp.dot(jnp.where(keep, 1.0, 0.0).astype(MM), pieces, preferred_element_type=F32)
    return r[:, 0:128] + r[:, 128:256] + r[:, 256:384]


def _main(hd):
    return slice(0, DH) if hd % 2 == 0 else slice(DH, HT)


def _aug_lane(hd, pos):
    return pos + (DH if hd % 2 == 0 else 0)


def _aug_tile(hd, main, cols):
    lane = lax.broadcasted_iota(jnp.int32, main.shape, 1)
    aug = jnp.zeros(main.shape, F32)
    for pos, val in enumerate(cols):
        aug = jnp.where(lane == pos, val, aug)
    return jnp.concatenate([main, aug] if hd % 2 == 0 else [aug, main], axis=1)


def _merge_pairs(tiles):
    lane = lax.broadcasted_iota(jnp.int32, tiles[0].shape, 1)
    return jnp.concatenate([jnp.where(lane < DH, tiles[2 * m], tiles[2 * m + 1]) for m in range(H // 2)], axis=1)


def _causal_mask():
    return lax.broadcasted_iota(jnp.int32, (TM, TM), 0) >= lax.broadcasted_iota(jnp.int32, (TM, TM), 1)


def _split_rider(refs, n_in, n_out, rider):
    if rider is None:
        return refs, None
    k = rider.k_n
    own = refs[:n_in] + refs[n_in + k:n_in + k + n_out] + refs[n_in + 2 * k + n_out:-3]
    return own, rider.copies(refs[n_in:n_in + k], refs[n_in + k + n_out:n_in + 2 * k + n_out], *refs[-3:])


def attn_fwd(qa, ka, va, name, rider=None):
    tp = qa.shape[0]
    nb = tp // TM
    pairs = [(i, j) for i in range(nb) for j in range(i + 1)]
    i_tab, j_tab = (jnp.asarray([p[a] for p in pairs], jnp.int32) for a in (0, 1))

    def body(i_ref, j_ref, *refs):
        (q_ref, k_ref, v_ref, o_ref, lse_ref, *scr), copies = _split_rider(refs, 3, 2, rider)
        m_scr, acc_scr, bias = scr[:H], scr[H:2 * H], scr[2 * H]
        n = pl.program_id(0)
        i, j = i_ref[n], j_ref[n]

        @pl.when(n == 0)
        def _():
            if copies:
                copies[0]()
            bias[...] = jnp.where(_causal_mask(), 0.0, NEG)

        @pl.when(j == 0)
        def _():
            for hd in range(H):
                m_scr[hd][...] = jnp.full((TM, 1), NEG, F32)
                acc_scr[hd][...] = jnp.zeros((TM, HT), F32)

        def step(diag):
            def logits(hd):
                t = slice(hd * HT, (hd + 1) * HT)
                s = lax.dot_general(q_ref[:, t], k_ref[:, t], NT, preferred_element_type=F32)
                return s + bias[...] if diag else s

            s_next = logits(0)
            for hd in range(H):
                t = slice(hd * HT, (hd + 1) * HT)
                s = s_next
                if hd + 1 < H:
                    s_next = logits(hd + 1)
                m_prev = m_scr[hd][...]
                m_new = jnp.maximum(m_prev, jnp.max(s, axis=1, keepdims=True))
                p = jnp.exp(s - m_new).astype(MM)
                acc_scr[hd][...] = (jnp.exp(m_prev - m_new) * acc_scr[hd][...]
                                    + jnp.dot(p, v_ref[:, t], preferred_element_type=F32))
                m_scr[hd][...] = m_new

        @pl.when(j < i)
        def _():
            step(False)

        @pl.when(j == i)
        def _():
            step(True)

        @pl.when(j == i)
        def _():
            lse = jnp.zeros((TM, 128), F32)
            outs = []
            for hd in range(H):
                acc = acc_scr[hd][...]
                l = acc[:, _aug_lane(hd, 0):_aug_lane(hd, 0) + 1]
                outs.append(acc * (1.0 / l))
                lse = _lane_put(lse, m_scr[hd][...] + jnp.log(l), hd)
            o_ref[...] = _merge_pairs(outs).astype(MM)
            lse_ref[...] = lse

        if copies:
            @pl.when(n == len(pairs) - 1)
            def _():
                copies[1]()

    qi = lambda w: pl.BlockSpec((TM, w), lambda n, it, jt: (it[n], 0))
    kv = pl.BlockSpec((TM, H * HT), lambda n, it, jt: (jt[n], 0))
    r_n = rider.k_n if rider else 0
    return pl.pallas_call(
        body, name=name,
        grid_spec=pltpu.PrefetchScalarGridSpec(
            num_scalar_prefetch=2, grid=(len(pairs),),
            in_specs=[qi(H * HT), kv, kv] + [ANY] * r_n, out_specs=[qi(DA), qi(128)] + [ANY] * r_n,
            scratch_shapes=[pltpu.VMEM((TM, 1), F32)] * H + [pltpu.VMEM((TM, HT), F32)] * H + [pltpu.VMEM((TM, TM), F32)]
            + (rider.scratch if rider else [])),
        out_shape=[S((tp, DA), MM), S((tp, 128), F32)] + (rider.out_shape if rider else []),
        compiler_params=_params(1),
    )(i_tab, j_tab, qa, ka, va, *(rider.srcs if rider else []))


def _layernorm_parts(dw, g, b):
    mu = jnp.mean(dw, axis=-1, keepdims=True)
    xc = dw - mu
    rs = lax.rsqrt(jnp.mean(xc * xc, axis=-1, keepdims=True) + EPS)
    xh = xc * rs
    return xh, rs, xh * g + b


def _fill_halo(ext, halo, cur, first):
    ext[0:HALO, :] = jnp.where(first, 0.0, halo)
    ext[HALO:HALO + TM, :] = cur


def _shift_copies(dst, src, lanes=slice(None)):
    for r in range(1, 8):
        dst[r - 1] = src[pl.ds(r, dst.shape[1]), lanes]


def _window(src, shifted, off, lanes=slice(None)):
    if off % 8 == 0:
        return src[pl.ds(off, TM), lanes]
    return shifted[off % 8 - 1, pl.ds(off - off % 8, TM), :]


def conv_fwd(pc, wdw, bdw, lng, lnb, wpw, bpw, wsc, name):
    tp = pc.shape[0]

    def body(pc_ref, hl_ref, wdw_ref, bdw_ref, lng_ref, lnb_ref, wpw_ref, bpw_ref, wsc_ref, cc_ref, sv_ref, xe, ce, xs):
        first = pl.program_id(0) == 0
        glu = pc_ref[:, 0:DC] * _sigmoid(pc_ref[:, DC:2 * DC])
        cu = pc_ref[:, 3 * DC:4 * DC] * pc_ref[:, 4 * DC:5 * DC]
        _fill_halo(xe, hl_ref[:, 0:DC] * _sigmoid(hl_ref[:, DC:2 * DC]), glu, first)
        _fill_halo(ce, hl_ref[:, 3 * DC:4 * DC] * hl_ref[:, 4 * DC:5 * DC], cu, first)
        _shift_copies(xs, xe)
        dw = jnp.zeros((TM, DC), F32) + bdw_ref[...]
        for k in range(CK):
            dw = dw + wdw_ref[k:k + 1, :] * _window(xe, xs, HALO - (CK - 1) + k)
        cv = jnp.zeros((TM, DC), F32)
        for k in range(SK):
            cv = cv + wsc_ref[k:k + 1, :] * ce[pl.ds(HALO - (SK - 1) + k, TM), :]
        _, _, ln = _layernorm_parts(dw, lng_ref[...], lnb_ref[...])
        act = ln * _sigmoid(ln)
        conf = jnp.dot(act.astype(MM), wpw_ref[...], preferred_element_type=F32) + bpw_ref[...]
        cc_ref[:, 0:DC] = conf.astype(MM)
        cc_ref[:, DC:2 * DC] = (pc_ref[:, 2 * DC:3 * DC] * cv).astype(MM)
        sv_ref[:, 0:DC] = dw
        sv_ref[:, DC:2 * DC] = cv

    hb = TM // HALO
    rows = lambda n: pl.BlockSpec((TM, n), lambda i: (i, 0))
    return pl.pallas_call(
        body, name=name, grid=(tp // TM,),
        in_specs=[rows(5 * DC), pl.BlockSpec((HALO, 5 * DC), lambda i: (jnp.maximum(i * hb - 1, 0), 0)),
                  _full((32, DC)), _full((1, DC)), _full((1, DC)), _full((1, DC)), _full((DC, DC)), _full((1, DC)),
                  _full((8, DC))],
        out_specs=[rows(2 * DC), rows(2 * DC)],
        out_shape=[S((tp, 2 * DC), MM), S((tp, 2 * DC), F32)],
        scratch_shapes=[pltpu.VMEM((HALO + TM, DC), F32), pltpu.VMEM((HALO + TM, DC), F32),
                        pltpu.VMEM((7, TM + HALO - 8, DC), F32)],
        compiler_params=_params(1),
    )(pc, pc, wdw, bdw, lng, lnb, wpw, bpw, wsc)


def out_proj(h, o, cc, wo, g2, name):
    tp = h.shape[0]

    def body(h_ref, o_ref, cc_ref, wo_ref, g_ref, h2_ref, hn_ref):
        wo = wo_ref[...].reshape(D, D)
        h2 = (h_ref[...] + jnp.dot(o_ref[...], wo[0:DA, :], preferred_element_type=F32)
              + jnp.dot(cc_ref[...], wo[DA:D, :], preferred_element_type=F32))
        h2_ref[...] = h2
        r = lax.rsqrt(jnp.mean(h2 * h2, axis=-1, keepdims=True) + EPS)
        hn_ref[...] = (h2 * r * g_ref[...]).astype(MM)

    rows = lambda n: pl.BlockSpec((TM, n), lambda i: (i, 0))
    return pl.pallas_call(
        body, name=name, grid=(tp // TM,),
        in_specs=[rows(D), rows(DA), rows(2 * DC), _full((NDEV, D // NDEV, D)), _full((1, D))],
        out_specs=[rows(D), rows(D)],
        out_shape=[S((tp, D), F32), S((tp, D), MM)],
        compiler_params=_params(1),
    )(h, o, cc, wo, g2)


def mlp_fwd(hn, h2, w1, w2, name):
    tp = hn.shape[0]
    nf = DFF // (FS * FC)

    def body(hn_ref, h2_ref, w1_ref, w2_ref, r_ref, z_ref, h3_ref, acc):
        j = pl.program_id(1)

        @pl.when(j == 0)
        def _():
            acc[...] = jnp.zeros_like(acc)

        for s in range(FS):
            cols = slice(s * FC, (s + 1) * FC)
            r = jnp.maximum(jnp.dot(hn_ref[...], w1_ref[s], preferred_element_type=F32), 0.0)
            zb = (r * r).astype(MM)
            r_ref[:, cols] = r.astype(MM)
            z_ref[:, cols] = zb
            acc[...] += jnp.dot(zb, w2_ref[s], preferred_element_type=F32)

        @pl.when(j == nf - 1)
        def _():
            h3_ref[...] = h2_ref[...] + acc[...]

    return pl.pallas_call(
        body, name=name, grid=(tp // TM, nf),
        in_specs=[pl.BlockSpec((TM, D), lambda i, j: (i, 0)), pl.BlockSpec((TM, D), lambda i, j: (i, 0)),
                  pl.BlockSpec((FS, D, FC), lambda i, j: (j, 0, 0)), pl.BlockSpec((FS, FC, D), lambda i, j: (j, 0, 0))],
        out_specs=[pl.BlockSpec((TM, FS * FC), lambda i, j: (i, j)), pl.BlockSpec((TM, FS * FC), lambda i, j: (i, j)),
                   pl.BlockSpec((TM, D), lambda i, j: (i, 0))],
        out_shape=[S((tp, DFF), MM), S((tp, DFF), MM), S((tp, D), F32)],
        scratch_shapes=[pltpu.VMEM((TM, D), F32)],
        compiler_params=_params(2),
    )(hn, h2, w1, w2)


def loss_head(h, tgt, g, t_real, name):
    tp = h.shape[0]

    def body(h_ref, t_ref, g_ref, dh_ref, loss_ref, dg_ref):
        i = pl.program_id(0)

        @pl.when(i == 0)
        def _():
            loss_ref[...] = jnp.zeros_like(loss_ref)
            dg_ref[...] = jnp.zeros_like(dg_ref)

        x = h_ref[...]
        gg = g_ref[...]
        r = lax.rsqrt(jnp.mean(x * x, axis=-1, keepdims=True) + EPS)
        xn = x * r
        row = i * TM + lax.broadcasted_iota(jnp.int32, (TM, 1), 0)
        e = jnp.where((row >= NM) & (row < t_real), xn * gg - t_ref[...], 0.0)
        loss_ref[...] += jnp.sum(e * e) * (0.5 / D)
        dy = e * (1.0 / D)
        dg_ref[...] += jnp.sum(dy * xn, axis=0, keepdims=True)
        u = dy * gg
        dh_ref[...] = r * (u - xn * jnp.mean(u * xn, axis=-1, keepdims=True))

    rows = lambda n: pl.BlockSpec((TM, n), lambda i: (i, 0))
    return pl.pallas_call(
        body, name=name, grid=(tp // TM,),
        in_specs=[rows(D), rows(D), _full((1, D))],
        out_specs=[rows(D), _full((8, 128)), _full((1, D))],
        out_shape=[S((tp, D), F32), S((8, 128), F32), S((1, D), F32)],
        compiler_params=_params(1),
    )(h, tgt, g)


def _rmsnorm_bwd(x, g, dy):
    r = lax.rsqrt(jnp.mean(x * x, axis=-1, keepdims=True) + EPS)
    xn = x * r
    u = dy * g
    dx = r * (u - xn * jnp.mean(u * xn, axis=-1, keepdims=True))
    return dx, jnp.sum(dy * xn, axis=0, keepdims=True)


def matmul_tn(a, b, tm, tn, name, n=None, out_dtype=F32, shard=None, sw=None):
    tp, m = a.shape
    n = b.shape[1] if n is None else n
    nk = tp // TM

    def body(a_ref, b_ref, o_ref, acc):
        k = pl.program_id(2)

        @pl.when(k == 0)
        def _():
            acc[...] = jnp.zeros_like(acc)

        acc[...] += lax.dot_general(a_ref[...].astype(MM), b_ref[...].astype(MM), TN, preferred_element_type=F32)

        @pl.when(k == nk - 1)
        def _():
            if shard is None:
                o_ref[...] = acc[...].astype(out_dtype)
            elif shard == "m":
                for s in range(tm // sw):
                    o_ref[s] = acc[s * sw:(s + 1) * sw, :].astype(out_dtype)
            else:
                for s in range(tn // sw):
                    o_ref[s] = acc[:, s * sw:(s + 1) * sw].astype(out_dtype)

    if shard is None:
        out_spec, out_shape = pl.BlockSpec((tm, tn), lambda mi, ni, k: (mi, ni)), (m, n)
    elif shard == "m":
        out_spec, out_shape = pl.BlockSpec((tm // sw, sw, tn), lambda mi, ni, k: (mi, 0, ni)), (m // sw, sw, n)
    else:
        out_spec, out_shape = pl.BlockSpec((tn // sw, tm, sw), lambda mi, ni, k: (ni, mi, 0)), (n // sw, m, sw)
    return pl.pallas_call(
        body, name=name, grid=(m // tm, n // tn, nk),
        in_specs=[pl.BlockSpec((TM, tm), lambda mi, ni, k: (k, mi)), pl.BlockSpec((TM, tn), lambda mi, ni, k: (k, ni))],
        out_specs=out_spec, out_shape=S(out_shape, out_dtype),
        scratch_shapes=[pltpu.VMEM((tm, tn), F32)],
        compiler_params=_params(3),
    )(a, b)


def mlp_bwd(dh3, r, w1, w2, h2, g, name):
    tp = dh3.shape[0]
    nf = DFF // (FS * FC)

    def body(dh3_ref, r_ref, w1_ref, w2_ref, h2_ref, g_ref, da_ref, dh2_ref, dg_ref, acc, dhb):
        i, j = pl.program_id(0), pl.program_id(1)

        @pl.when((i == 0) & (j == 0))
        def _():
            dg_ref[...] = jnp.zeros_like(dg_ref)

        @pl.when(j == 0)
        def _():
            acc[...] = jnp.zeros_like(acc)
            dhb[...] = dh3_ref[...].astype(MM)

        for s in range(FS):
            cols = slice(s * FC, (s + 1) * FC)
            dz = lax.dot_general(dhb[...], w2_ref[s], NT, preferred_element_type=F32)
            da = (dz * (2.0 * r_ref[:, cols].astype(F32))).astype(MM)
            da_ref[:, cols] = da
            acc[...] += lax.dot_general(da, w1_ref[s], NT, preferred_element_type=F32)

        @pl.when(j == nf - 1)
        def _():
            dx, dg = _rmsnorm_bwd(h2_ref[...], g_ref[...], acc[...])
            dh2_ref[...] = dh3_ref[...] + dx
            dg_ref[...] += dg

    return pl.pallas_call(
        body, name=name, grid=(tp // TM, nf),
        in_specs=[pl.BlockSpec((TM, D), lambda i, j: (i, 0)), pl.BlockSpec((TM, FS * FC), lambda i, j: (i, j)),
                  pl.BlockSpec((FS, D, FC), lambda i, j: (j, 0, 0)), pl.BlockSpec((FS, FC, D), lambda i, j: (j, 0, 0)),
                  pl.BlockSpec((TM, D), lambda i, j: (i, 0)), _full((1, D))],
        out_specs=[pl.BlockSpec((TM, FS * FC), lambda i, j: (i, j)), pl.BlockSpec((TM, D), lambda i, j: (i, 0)),
                   _full((1, D))],
        out_shape=[S((tp, DFF), MM), S((tp, D), F32), S((1, D), F32)],
        scratch_shapes=[pltpu.VMEM((TM, D), F32), pltpu.VMEM((TM, D), MM)],
        compiler_params=_params(2),
    )(dh3, r, w1, w2, h2, g)


def out_proj_bwd(dh2, wo, name):
    tp = dh2.shape[0]

    def body(dh_ref, wo_ref, do_ref, dcc_ref):
        dcat = lax.dot_general(dh_ref[...].astype(MM), wo_ref[...].reshape(D, D), NT, preferred_element_type=F32)
        do_ref[...] = dcat[:, 0:DA].astype(MM)
        dcc_ref[...] = dcat[:, DA:D]

    rows = lambda n: pl.BlockSpec((TM, n), lambda i: (i, 0))
    return pl.pallas_call(
        body, name=name, grid=(tp // TM,),
        in_specs=[rows(D), _full((NDEV, D // NDEV, D))],
        out_specs=[rows(DA), rows(2 * DC)],
        out_shape=[S((tp, DA), MM), S((tp, 2 * DC), F32)],
        compiler_params=_params(1),
    )(dh2, wo)


def attn_bwd_prep(qa, do, o, lse, name):
    tp = qa.shape[0]

    def body(q_ref, do_ref, o_ref, lse_ref, qb_ref, dob_ref, qbt_ref, dobt_ref):
        lse = lse_ref[...]
        lane = lax.broadcasted_iota(jnp.int32, (TM, HT), 1)
        for hd in range(H):
            t, hs = slice(hd * HT, (hd + 1) * HT), slice(hd * DH, (hd + 1) * DH)
            dof = do_ref[:, hs].astype(F32)
            dd = jnp.sum(dof * o_ref[:, hs].astype(F32), axis=1, keepdims=True)
            dob = _aug_tile(hd, dof, [0.0] + [-v for v in _split3(dd)])
            qb = q_ref[:, t].astype(F32)
            for pos, v in enumerate(_split3(lse[:, hd:hd + 1])):
                qb = jnp.where(lane == _aug_lane(hd, 6 + pos), -v, qb)
            qb_ref[:, t] = qb.astype(MM)
            dob_ref[:, t] = dob.astype(MM)
            qbt_ref[t, :] = qb.T.astype(MM)
            dobt_ref[t, :] = dob.T.astype(MM)

    rows = lambda n: pl.BlockSpec((TM, n), lambda i: (i, 0))
    cols = pl.BlockSpec((H * HT, TM), lambda i: (0, i))
    return pl.pallas_call(
        body, name=name, grid=(tp // TM,),
        in_specs=[rows(H * HT), rows(DA), rows(DA), rows(128)],
        out_specs=[rows(H * HT), rows(H * HT), cols, cols],
        out_shape=[S((tp, H * HT), MM), S((tp, H * HT), MM), S((H * HT, tp), MM), S((H * HT, tp), MM)],
        compiler_params=_params(1),
    )(qa, do, o, lse)


def attn_bwd(qb, dob, qbt, dobt, ka, va, name, rider=None):
    tp = qb.shape[0]
    nb = tp // TM

    pairs = [(j, i) for j in range(nb) for i in range(j, nb)]
    j_tab, i_tab = (jnp.asarray([p[a] for p in pairs], jnp.int32) for a in (0, 1))

    def body(j_ref, i_ref, *refs):
        own, copies = _split_rider(refs, 6, 5, rider)
        (qb_ref, dob_ref, qbt_ref, dobt_ref, k_ref, v_ref, dq_ref, dk_ref, dv_ref, dcq_ref, dck_ref,
         dq_acc, dcq_acc, dkt_acc, dvt_acc, dck_acc, stage, sems) = own
        n = pl.program_id(0)
        j, i = j_ref[n], i_ref[n]

        @pl.when(n == 0)
        def _():
            if copies:
                copies[0]()
            dq_acc[...] = jnp.zeros_like(dq_acc)
            dcq_acc[...] = jnp.zeros_like(dcq_acc)

        @pl.when(i == j)
        def _():
            dkt_acc[...] = jnp.zeros_like(dkt_acc)
            dvt_acc[...] = jnp.zeros_like(dvt_acc)
            dck_acc[...] = jnp.zeros_like(dck_acc)

        def step(diag):
            rows = pl.ds(pl.multiple_of(i * TM, TM), TM)
            dcq = dcq_acc[rows, :]
            lane = lax.broadcasted_iota(jnp.int32, (TM, HT), 1)
            mask = _causal_mask() if diag else None
            for hd in range(H):
                t = slice(hd * HT, (hd + 1) * HT)
                s = lax.dot_general(qb_ref[:, t], k_ref[:, t], NT, preferred_element_type=F32)
                dp = lax.dot_general(dob_ref[:, t], v_ref[:, t], NT, preferred_element_type=F32)
                if diag:
                    s = jnp.where(mask, s, NEG)
                p = jnp.exp(s)
                ds = p * dp
                dsb = ds.astype(MM)
                dvt_acc[t, :] += jnp.dot(dobt_ref[t, :], p.astype(MM), preferred_element_type=F32)
                dkt_acc[t, :] += jnp.dot(qbt_ref[t, :], dsb, preferred_element_type=F32)
                dqh = jnp.dot(dsb, k_ref[:, t], preferred_element_type=F32)
                if hd % 2 == 0:
                    dq_even = dqh
                else:
                    pair = slice((hd // 2) * HT, (hd // 2 + 1) * HT)
                    dq_acc[rows, pair] += jnp.where(lane < DH, dq_even, dqh)
                dcq = dcq + jnp.where(lane == hd, jnp.sum(ds, axis=1, keepdims=True), 0.0)
                dck_acc[hd:hd + 1, :] += jnp.sum(ds, axis=0, keepdims=True)
            dcq_acc[rows, :] = dcq
            return rows

        @pl.when(i > j)
        def _():
            step(False)

        @pl.when(i == j)
        def _():
            rows = step(True)
            stage[...] = (dq_acc[rows, :] * SCALE).astype(MM)
            out = [pltpu.make_async_copy(stage, dq_ref.at[rows, :], sems.at[0]),
                   pltpu.make_async_copy(dcq_acc.at[rows, :], dcq_ref.at[rows, :], sems.at[1])]
            for cp in out:
                cp.start()
            for cp in out:
                cp.wait()

        @pl.when(i == nb - 1)
        def _():
            dk_ref[...] = _merge_pairs([dkt_acc[hd * HT:(hd + 1) * HT, :].T for hd in range(H)]).astype(MM)
            dv_ref[...] = _merge_pairs([dvt_acc[hd * HT:(hd + 1) * HT, :].T for hd in range(H)]).astype(MM)
            dck_ref[...] = dck_acc[...]

        if copies:
            @pl.when(n == len(pairs) - 1)
            def _():
                copies[1]()

    qi = lambda w: pl.BlockSpec((TM, w), lambda n, jt, it: (it[n], 0))
    qt = pl.BlockSpec((H * HT, TM), lambda n, jt, it: (0, it[n]))
    kj = lambda w: pl.BlockSpec((TM, w), lambda n, jt, it: (jt[n], 0))
    r_n = rider.k_n if rider else 0
    return pl.pallas_call(
        body, name=name,
        grid_spec=pltpu.PrefetchScalarGridSpec(
            num_scalar_prefetch=2, grid=(len(pairs),),
            in_specs=[qi(H * HT), qi(H * HT), qt, qt, kj(H * HT), kj(H * HT)] + [ANY] * r_n,
            out_specs=[ANY, kj(DA), kj(DA), ANY, pl.BlockSpec((H, TM), lambda n, jt, it: (0, jt[n]))] + [ANY] * r_n,
            scratch_shapes=[pltpu.VMEM((tp, DA), F32), pltpu.VMEM((tp, 128), F32), pltpu.VMEM((H * HT, TM), F32),
                            pltpu.VMEM((H * HT, TM), F32), pltpu.VMEM((H, TM), F32), pltpu.VMEM((TM, DA), MM),
                            pltpu.SemaphoreType.DMA((2,))] + (rider.scratch if rider else [])),
        out_shape=[S((tp, DA), MM), S((tp, DA), MM), S((tp, DA), MM), S((tp, 128), F32), S((H, tp), F32)]
        + (rider.out_shape if rider else []),
        compiler_params=_params(1),
    )(j_tab, i_tab, qb, dob, qbt, dobt, ka, va, *(rider.srcs if rider else []))


def fgate_bwd(dcq, dck, sg, name):
    tp = dcq.shape[0]
    nb = tp // TM

    def body(dcq_ref, dck_ref, sg_ref, dz_ref, db_ref, carry):
        @pl.when(pl.program_id(0) == 0)
        def _():
            carry[...] = jnp.zeros_like(carry)
            db_ref[...] = jnp.zeros_like(db_ref)

        row = lax.broadcasted_iota(jnp.int32, (TM, TM), 0)
        col = lax.broadcasted_iota(jnp.int32, (TM, TM), 1)
        dl = _tri_matmul(row <= col, dcq_ref[...] - dck_ref[...]) + carry[...]
        carry[...] = dl[0:1, :]
        dz = dl * sg_ref[...]
        dz_ref[...] = dz.astype(MM)
        db_ref[...] += jnp.sum(dz, axis=0, keepdims=True)

    rev = pl.BlockSpec((TM, 128), lambda i: (nb - 1 - i, 0))
    return pl.pallas_call(
        body, name=name, grid=(nb,),
        in_specs=[rev, rev, rev], out_specs=[rev, _full((1, 128))],
        out_shape=[S((tp, 128), MM), S((1, 128), F32)],
        scratch_shapes=[pltpu.VMEM((1, 128), F32)],
        compiler_params=_params(1),
    )(dcq, dck, sg)


def conv_bwd_pointwise(dcc, sv, pc, lng, lnb, wpw, name):
    tp = dcc.shape[0]

    def body(dcc_ref, sv_ref, b_ref, lng_ref, lnb_ref, wpw_ref, gc_ref, act_ref, pg_ref):
        @pl.when(pl.program_id(0) == 0)
        def _():
            pg_ref[...] = jnp.zeros_like(pg_ref)

        dconf = dcc_ref[:, 0:DC]
        g = lng_ref[...]
        xh, rs, ln = _layernorm_parts(sv_ref[:, 0:DC], g, lnb_ref[...])
        sig = _sigmoid(ln)
        act_ref[...] = (ln * sig).astype(MM)
        dact = lax.dot_general(dconf.astype(MM), wpw_ref[...], NT, preferred_element_type=F32)
        dln = dact * (sig * (1.0 + ln * (1.0 - sig)))
        dxh = dln * g
        ddw = rs * (dxh - jnp.mean(dxh, axis=-1, keepdims=True) - xh * jnp.mean(dxh * xh, axis=-1, keepdims=True))
        gc_ref[:, 0:DC] = ddw
        gc_ref[:, DC:2 * DC] = dcc_ref[:, DC:2 * DC] * b_ref[...]
        cs = lambda t: jnp.sum(t, axis=0, keepdims=True)
        pg_ref[0:1, :] += cs(dconf)
        pg_ref[1:2, :] += cs(dln * xh)
        pg_ref[2:3, :] += cs(dln)
        pg_ref[3:4, :] += cs(ddw)

    rows = lambda n: pl.BlockSpec((TM, n), lambda i: (i, 0))
    return pl.pallas_call(
        body, name=name, grid=(tp // TM,),
        in_specs=[rows(2 * DC), rows(2 * DC), pl.BlockSpec((TM, DC), lambda i: (i, 2)),
                  _full((1, DC)), _full((1, DC)), _full((DC, DC))],
        out_specs=[rows(2 * DC), rows(DC), _full((8, DC))],
        out_shape=[S((tp, 2 * DC), F32), S((tp, DC), MM), S((8, DC), F32)],
        compiler_params=_params(1),
    )(dcc, sv, pc, lng, lnb, wpw)


def conv_bwd_taps(gc, pc, dcc, sv, wdw, wsc, name):
    tp = gc.shape[0]
    nb = tp // TM
    hb = TM // HALO

    def body(gc_ref, gn_ref, pc_ref, hl_ref, dcc_ref, sv_ref, wdw_ref, wsc_ref, dpc_ref, wg_ref, ge, xe, ce, gs, xs):
        i = pl.program_id(0)

        @pl.when(i == 0)
        def _():
            wg_ref[...] = jnp.zeros_like(wg_ref)

        a, gt = pc_ref[:, 0:DC], pc_ref[:, DC:2 * DC]
        c, u = pc_ref[:, 3 * DC:4 * DC], pc_ref[:, 4 * DC:5 * DC]
        sig = _sigmoid(gt)
        _fill_halo(xe, hl_ref[:, 0:DC] * _sigmoid(hl_ref[:, DC:2 * DC]), a * sig, i == 0)
        _fill_halo(ce, hl_ref[:, 3 * DC:4 * DC] * hl_ref[:, 4 * DC:5 * DC], c * u, i == 0)
        ge[0:TM, :] = gc_ref[...]
        ge[TM:TM + HALO, :] = jnp.where(i == nb - 1, 0.0, gn_ref[...])
        _shift_copies(gs, ge, slice(0, DC))
        _shift_copies(xs, xe)
        ddw, dcv = gc_ref[:, 0:DC], gc_ref[:, DC:2 * DC]
        dglu = jnp.zeros((TM, DC), F32)
        for k in range(CK):
            dglu = dglu + wdw_ref[k:k + 1, :] * _window(ge, gs, CK - 1 - k, slice(0, DC))
            wg_ref[k:k + 1, :] += jnp.sum(ddw * _window(xe, xs, HALO - (CK - 1) + k), axis=0, keepdims=True)
        dcu = jnp.zeros((TM, DC), F32)
        for k in range(SK):
            dcu = dcu + wsc_ref[k:k + 1, :] * ge[pl.ds(SK - 1 - k, TM), DC:2 * DC]
            wg_ref[32 + k:33 + k, :] += jnp.sum(dcv * ce[pl.ds(HALO - (SK - 1) + k, TM), :], axis=0, keepdims=True)
        dpc_ref[:, 0:DC] = (dglu * sig).astype(MM)
        dpc_ref[:, DC:2 * DC] = (dglu * a * sig * (1.0 - sig)).astype(MM)
        dpc_ref[:, 2 * DC:3 * DC] = (dcc_ref[:, DC:2 * DC] * sv_ref[:, DC:2 * DC]).astype(MM)
        dpc_ref[:, 3 * DC:4 * DC] = (dcu * u).astype(MM)
        dpc_ref[:, 4 * DC:5 * DC] = (dcu * c).astype(MM)

    rows = lambda n: pl.BlockSpec((TM, n), lambda i: (i, 0))
    return pl.pallas_call(
        body, name=name, grid=(nb,),
        in_specs=[rows(2 * DC), pl.BlockSpec((HALO, 2 * DC), lambda i: (jnp.minimum((i + 1) * hb, nb * hb - 1), 0)),
                  rows(5 * DC), pl.BlockSpec((HALO, 5 * DC), lambda i: (jnp.maximum(i * hb - 1, 0), 0)),
                  rows(2 * DC), rows(2 * DC), _full((32, DC)), _full((8, DC))],
        out_specs=[rows(5 * DC), _full((40, DC))],
        out_shape=[S((tp, 5 * DC), MM), S((40, DC), F32)],
        scratch_shapes=[pltpu.VMEM((TM + HALO, 2 * DC), F32), pltpu.VMEM((HALO + TM, DC), F32),
                        pltpu.VMEM((HALO + TM, DC), F32), pltpu.VMEM((7, TM + HALO - 8, DC), F32),
                        pltpu.VMEM((7, TM + HALO - 8, DC), F32)],
        compiler_params=_params(1),
    )(gc, gc, pc, pc, dcc, sv, wdw, wsc)


def in_proj_bwd(dproj, w, h, g, dh_in, name, rider=None):
    tp = h.shape[0]
    nb = tp // TM

    def body(*refs):
        (dp_ref, w_ref, h_ref, g_ref, di_ref, dh_ref, dg_ref), copies = _split_rider(refs, 5, 2, rider)
        i = pl.program_id(0)

        @pl.when(i == 0)
        def _():
            if copies:
                copies[0]()
            dg_ref[...] = jnp.zeros_like(dg_ref)

        dhn = lax.dot_general(dp_ref[...], w_ref[...], NT, preferred_element_type=F32)
        dx, dg = _rmsnorm_bwd(h_ref[...], g_ref[...], dhn)
        dh_ref[...] = di_ref[...] + dx
        dg_ref[...] += dg

        if copies:
            @pl.when(i == nb - 1)
            def _():
                copies[1]()

    rows = lambda n: pl.BlockSpec((TM, n), lambda i: (i, 0))
    r_n = rider.k_n if rider else 0
    return pl.pallas_call(
        body, name=name, grid=(nb,),
        in_specs=[rows(NP_IN), _full((D, NP_IN)), rows(D), _full((1, D)), rows(D)] + [ANY] * r_n,
        out_specs=[rows(D), _full((1, D))] + [ANY] * r_n,
        out_shape=[S((tp, D), F32), S((1, D), F32)] + (rider.out_shape if rider else []),
        scratch_shapes=rider.scratch if rider else [],
        compiler_params=_params(1),
    )(dproj, w, h, g, dh_in, *(rider.srcs if rider else []))


def adamw(recvs, w, m, v, rb, name):
    l_n, r_n, c_n = w.shape

    def body(*refs):
        p_refs = refs[:l_n]
        w_ref, m_ref, v_ref, g_ref, d_ref, m2_ref, v2_ref = refs[l_n:]
        for l in range(l_n):
            g = p_refs[l][0].astype(F32)
            for s in range(1, NDEV):
                g = g + p_refs[l][s].astype(F32)
            m2 = ADAM_B1 * m_ref[l] + (1.0 - ADAM_B1) * g
            v2 = ADAM_B2 * v_ref[l] + (1.0 - ADAM_B2) * (g * g)
            m_hat = m2 / (1.0 - ADAM_B1 ** ADAM_STEP)
            v_hat = v2 / (1.0 - ADAM_B2 ** ADAM_STEP)
            g_ref[l] = g
            d_ref[l] = -ADAM_LR * (m_hat / (jnp.sqrt(v_hat) + ADAM_EPS) + ADAM_WD * w_ref[l])
            m2_ref[l] = m2
            v2_ref[l] = v2

    blk = pl.BlockSpec((l_n, rb, c_n), lambda r: (0, r, 0))
    return pl.pallas_call(
        body, name=name, grid=(r_n // rb,),
        in_specs=[pl.BlockSpec((NDEV, rb, c_n), lambda r: (0, r, 0))] * l_n + [blk, blk, blk],
        out_specs=[blk] * 4, out_shape=[S(w.shape, F32)] * 4,
        compiler_params=_params(1),
    )(*recvs, w, m, v)


TINY_ROWS = 168
REP_ROWS = 64


def _pack(parts, rows):
    flat = jnp.concatenate([p.reshape(-1) for p in parts])
    return jnp.pad(flat, (0, rows * 128 - flat.shape[0])).reshape(rows, 128)


def _unpack(buf, shapes):
    flat, out, o = buf.reshape(-1), [], 0
    for s in shapes:
        n = 1
        for d in s:
            n *= d
        out.append(flat[o:o + n].reshape(s))
        o += n
    return out


TINY_SHAPES = [(DEPTH, CK, DC // NDEV), (DEPTH, SK, DC // NDEV), (NM, D // NDEV), (DEPTH, DC // NDEV, DC)]
REP_SHAPES = [(DEPTH, D), (DEPTH, H), (DEPTH, DC), (DEPTH, DC), (DEPTH, DC), (DEPTH, DC), (DEPTH, D), (D,)]


def _to_padded_cols(w):
    pad = jnp.zeros(w.shape[:-1] + (NP_IN - N_IN,), w.dtype)
    return jnp.concatenate([w[..., :C0], w[..., C0 + H:], w[..., C0:C0 + H], pad], axis=-1)


def _from_padded_cols(w):
    return jnp.concatenate([w[..., :C0], w[..., F0:F0 + H], w[..., C0:F0]], axis=-1)


def kernel(x, meta_tokens, mix_norm_g, w_in, b_forget, w_conf_dw, b_conf_dw, conf_ln_g, conf_ln_b, w_conf_pw, b_conf_pw, w_sc_conv, w_out, mlp_norm_g, w_mlp1, w_mlp2, final_norm_g, loss_target, m_meta_tokens, m_mix_norm_g, m_w_in, m_b_forget, m_w_conf_dw, m_b_conf_dw, m_conf_ln_g, m_conf_ln_b, m_w_conf_pw, m_b_conf_pw, m_w_sc_conv, m_w_out, m_mlp_norm_g, m_w_mlp1, m_w_mlp2, m_final_norm_g, v_meta_tokens, v_mix_norm_g, v_w_in, v_b_forget, v_w_conf_dw, v_b_conf_dw, v_conf_ln_g, v_conf_ln_b, v_w_conf_pw, v_b_conf_pw, v_w_sc_conv, v_w_out, v_mlp_norm_g, v_w_mlp1, v_w_mlp2, v_final_norm_g):
    seq = x.shape[1]
    t_real = NM + seq
    tp = -(-t_real // TM) * TM

    tiny_w = _pack([w_conf_dw, w_sc_conv, meta_tokens, w_conf_pw], TINY_ROWS)
    big = lambda l: [w_in[l].astype(MM), w_out[l].astype(MM), w_mlp1[l].astype(MM), w_mlp2[l].astype(MM)]
    *first, g_tiny = gather(big(0) + [tiny_w], "gather_weights0")
    gathered = {0: first}
    n_sh = w_in.shape[-1]
    assert w_mlp1.shape[-1] == FC and w_mlp2.shape[-2] == FC
    tiny = [_unpack(g_tiny[s], TINY_SHAPES) for s in range(NDEV)]
    wdw = jnp.concatenate([t[0] for t in tiny], axis=-1)
    wsc = jnp.concatenate([t[1] for t in tiny], axis=-1)
    meta = jnp.concatenate([t[2] for t in tiny], axis=-1)
    wpw = jnp.concatenate([t[3] for t in tiny], axis=1).astype(MM)
    wdw = jnp.pad(wdw, ((0, 0), (0, 32 - CK), (0, 0)))
    wsc = jnp.pad(wsc, ((0, 0), (0, 8 - SK), (0, 0)))
    bfp = jnp.pad(b_forget, ((0, 0), (0, 128 - H)))

    row = lambda a: a.reshape(1, -1)

    h = jnp.concatenate([meta, x[0], jnp.zeros((tp - t_real, D), F32)], axis=0)
    tgt = jnp.pad(loss_target[0], ((NM, tp - t_real), (0, 0)))
    saved = []
    for l in range(DEPTH):
        g_in, g_out, g_w1, g_w2 = gathered[l]
        win = _to_padded_cols(g_in.transpose(1, 0, 2).reshape(D, NDEV * n_sh))
        hn, qa, ka, va, pc, sg = in_proj(h, row(mix_norm_g[l]), win, row(bfp[l]), f"in_proj{l}")
        rider = AllToAll(big(l + 1), [False] * 4) if l + 1 < DEPTH else None
        o, lse, *arrived = attn_fwd(qa, ka, va, f"attn_fwd{l}", rider)
        if rider:
            gathered[l + 1] = arrived
        cc, sv = conv_fwd(pc, wdw[l], row(b_conf_dw[l]), row(conf_ln_g[l]), row(conf_ln_b[l]), wpw[l],
                          row(b_conf_pw[l]), wsc[l], f"conv_fwd{l}")
        h2, hn2 = out_proj(h, o, cc, g_out, row(mlp_norm_g[l]), f"out_proj{l}")
        r, z, h3 = mlp_fwd(hn2, h2, g_w1, g_w2, f"mlp_fwd{l}")
        saved.append((h, hn, qa, ka, va, pc, sg, o, lse, cc, sv, h2, hn2, r, z, win))
        h = h3

    dh, loss_part, d_gf = loss_head(h, tgt, row(final_norm_g), t_real, "loss_head")
    loss = lax.psum(loss_part[0, 0], ("x", "y", "c"))

    gw, recv, d_win = {}, {}, {}
    for l in reversed(range(DEPTH)):
        h0, hn, qa, ka, va, pc, sg, o, lse, cc, sv, h2, hn2, r, z, win = saved[l]
        _, g_out, g_w1, g_w2 = gathered[l]
        da, dh2, gw["mlp_g", l] = mlp_bwd(dh, r, g_w1, g_w2, h2, row(mlp_norm_g[l]), f"mlp_bwd{l}")
        d_w1 = matmul_tn(hn2, da, D, 4 * FC, f"dw_mlp1_{l}", out_dtype=MM, shard="n", sw=FC)
        d_w2 = matmul_tn(z, dh, 4 * FC, D, f"dw_mlp2_{l}", out_dtype=MM, shard="m", sw=FC)
        do, dcc = out_proj_bwd(dh2, g_out, f"out_proj_bwd{l}")
        d_wout = jnp.concatenate([matmul_tn(o, dh2, DA, D, f"dw_out_a{l}"), matmul_tn(cc, dh2, 2 * DC, D, f"dw_out_c{l}")],
                                 axis=0).reshape(NDEV, D // NDEV, D).astype(MM)
        qb, dob, qbt, dobt = attn_bwd_prep(qa, do, o, lse, f"attn_bwd_prep{l}")
        riding = [("wout", l, d_wout), ("w1", l, d_w1), ("w2", l, d_w2)] + ([("win", l + 1, d_win[l + 1])] if l + 1 < DEPTH else [])
        dq, dk, dv, dcq, dck, *arrived = attn_bwd(qb, dob, qbt, dobt, ka, va, f"attn_bwd{l}",
                                                  AllToAll([a for _, _, a in riding], [True] * len(riding)))
        for (k, kl, _), a in zip(riding, arrived):
            recv[k, kl] = a
        dz, gw["bf", l] = fgate_bwd(dcq, jnp.pad(dck.T, ((0, 0), (0, 128 - H))), sg, f"fgate_bwd{l}")
        gc, act, pg = conv_bwd_pointwise(dcc, sv, pc, row(conf_ln_g[l]), row(conf_ln_b[l]), wpw[l], f"conv_bwd_pw{l}")
        gw["wpw", l] = matmul_tn(act, dcc, DC, DC, f"dw_conf_pw{l}", n=DC)
        dpc, wg = conv_bwd_taps(gc, pc, dcc, sv, wdw[l], wsc[l], f"conv_bwd_taps{l}")
        gw["pg", l], gw["wg", l] = pg, wg
        dproj = jnp.concatenate([dq, dk, dv, dpc, dz], axis=1)
        d_win[l] = _from_padded_cols(matmul_tn(hn, dproj, 512, NP_IN, f"dw_in{l}")).reshape(D, NDEV, n_sh).transpose(
            1, 0, 2).astype(MM)
        rider = AllToAll([d_win[0]], [True]) if l == 0 else None
        dh, gw["mix_g", l], *arrived = in_proj_bwd(dproj, win, h0, row(mix_norm_g[l]), dh2, f"in_proj_bwd{l}", rider)
        if rider:
            recv["win", 0] = arrived[0]

    grad_x = dh[NM:t_real][None]
    stack = lambda k: jnp.stack([gw[k, l] for l in range(DEPTH)])

    d_wdw = stack("wg")[:, 0:CK].reshape(DEPTH, CK, NDEV, DC // NDEV).transpose(2, 0, 1, 3)
    d_wsc = stack("wg")[:, 32:32 + SK].reshape(DEPTH, SK, NDEV, DC // NDEV).transpose(2, 0, 1, 3)
    d_meta = dh[0:NM].reshape(NM, NDEV, D // NDEV).transpose(1, 0, 2)
    d_wpw = stack("wpw").reshape(DEPTH, NDEV, DC // NDEV, DC).transpose(1, 0, 2, 3)
    d_tiny = jnp.stack([_pack([d_wdw[p], d_wsc[p], d_meta[p], d_wpw[p]], TINY_ROWS) for p in range(NDEV)])
    pgs = stack("pg")
    d_rep = _pack([stack("mix_g").reshape(DEPTH, D), stack("bf")[:, 0, :H], pgs[:, 3], pgs[:, 1], pgs[:, 2], pgs[:, 0],
                   stack("mlp_g").reshape(DEPTH, D), d_gf.reshape(D)], REP_ROWS)
    r_tiny, r_rep = exchange([d_tiny, d_rep], [True, False], "exchange_grads")

    per_layer = lambda k: [recv[k, l] for l in range(DEPTH)]
    res = {}
    res["w_in"] = adamw(per_layer("win"), w_in, m_w_in, v_w_in, 256, "adamw_w_in")
    res["w_out"] = adamw(per_layer("wout"), w_out, m_w_out, v_w_out, D // NDEV, "adamw_w_out")
    res["w_mlp1"] = adamw(per_layer("w1"), w_mlp1, m_w_mlp1, v_w_mlp1, 256, "adamw_w_mlp1")
    res["w_mlp2"] = adamw(per_layer("w2"), w_mlp2, m_w_mlp2, v_w_mlp2, 128, "adamw_w_mlp2")
    tiny_names = ["w_conf_dw", "w_sc_conv", "meta_tokens", "w_conf_pw"]
    tiny_wmv = [[w_conf_dw, w_sc_conv, meta_tokens, w_conf_pw], [m_w_conf_dw, m_w_sc_conv, m_meta_tokens, m_w_conf_pw],
                [v_w_conf_dw, v_w_sc_conv, v_meta_tokens, v_w_conf_pw]]
    rep_names = ["mix_norm_g", "b_forget", "b_conf_dw", "conf_ln_g", "conf_ln_b", "b_conf_pw", "mlp_norm_g", "final_norm_g"]
    rep_wmv = [[mix_norm_g, b_forget, b_conf_dw, conf_ln_g, conf_ln_b, b_conf_pw, mlp_norm_g, final_norm_g],
               [m_mix_norm_g, m_b_forget, m_b_conf_dw, m_conf_ln_g, m_conf_ln_b, m_b_conf_pw, m_mlp_norm_g, m_final_norm_g],
               [v_mix_norm_g, v_b_forget, v_b_conf_dw, v_conf_ln_g, v_conf_ln_b, v_b_conf_pw, v_mlp_norm_g, v_final_norm_g]]
    for names, wmv, shapes, recv_pack, rows_n, nm in ((tiny_names, tiny_wmv, TINY_SHAPES, r_tiny, TINY_ROWS, "adamw_tiny"),
                                                 (rep_names, rep_wmv, REP_SHAPES, r_rep, REP_ROWS, "adamw_rep")):
        packed = [_pack(group, rows_n)[None] for group in wmv]
        outs = adamw([recv_pack], *packed, rows_n, nm)
        parts = [_unpack(a[0], shapes) for a in outs]
        for k, n in enumerate(names):
            res[n] = [parts[q][k] for q in range(4)]

    order = ["meta_tokens", "mix_norm_g", "w_in", "b_forget", "w_conf_dw", "b_conf_dw", "conf_ln_g", "conf_ln_b",
             "w_conf_pw", "b_conf_pw", "w_sc_conv", "w_out", "mlp_norm_g", "w_mlp1", "w_mlp2", "final_norm_g"]
    return (loss, grad_x, *[res[n][0] for n in order], *[res[n][1] for n in order],
            *[res[n][2] for n in order], *[res[n][3] for n in order])
```

```python
import functools

import jax
import jax.numpy as jnp
from jax import lax
from jax.experimental import pallas as pl
from jax.experimental.pallas import tpu as pltpu

F32 = jnp.float32
MM = jnp.bfloat16

D = 1024
H = 8
DH = 64
DA = H * DH
HT = 128
DC = 256
NM = 16
CK = 31
SK = 3
DFF = 4096
DEPTH = 2
N_IN = 3 * DA + H + 2 * DC + 3 * DC
NP_IN = 3 * DA + 5 * DC + 128
C0 = 3 * DA
F0 = 3 * DA + 5 * DC
EPS = 1e-6
TM = 640
HALO = 32
FC = 512
FS = 4
NDEV = 8
SCALE = DH ** -0.5
NEG = -1e30

ADAM_LR, ADAM_B1, ADAM_B2, ADAM_EPS, ADAM_WD, ADAM_STEP = 0.001, 0.9, 0.999, 1e-08, 0.01, 10

VMEM_LIMIT = 56 * 1024 * 1024

S = jax.ShapeDtypeStruct
NT = (((1,), (1,)), ((), ()))
TN = (((0,), (0,)), ((), ()))


def _params(n_grid):
    return pltpu.CompilerParams(dimension_semantics=("arbitrary",) * n_grid, vmem_limit_bytes=VMEM_LIMIT)


def _sigmoid(x):
    return 1.0 / (1.0 + jnp.exp(-x))


def _full(shape):
    n = len(shape)
    return pl.BlockSpec(shape, lambda *_: (0,) * n)


def _lane_put(dst, col, h):
    lane = lax.broadcasted_iota(jnp.int32, dst.shape, 1)
    return jnp.where(lane == h, col, dst)


ANY = pl.BlockSpec(memory_space=pl.ANY)


class AllToAll:
    def __init__(self, srcs, per_peer):
        self.srcs, self.per_peer, self.k_n = list(srcs), list(per_peer), len(srcs)
        self.out_shape = [S((NDEV,) + (a.shape[1:] if pp else a.shape), a.dtype) for a, pp in zip(srcs, per_peer)]
        self.scratch = [pltpu.SemaphoreType.DMA((self.k_n, NDEV - 1)), pltpu.SemaphoreType.DMA((self.k_n, NDEV - 1)),
                        pltpu.SemaphoreType.DMA((self.k_n,))]

    def copies(self, src, out, send_sems, recv_sems, local_sems):
        x, y, c = lax.axis_index("x"), lax.axis_index("y"), lax.axis_index("c")
        me = 4 * x + 2 * y + c

        def piece(k, p):
            return src[k].at[p] if self.per_peer[k] else src[k]

        local = [pltpu.make_async_copy(piece(k, me), out[k].at[me], local_sems.at[k]) for k in range(self.k_n)]
        sends, recvs = [], []
        for r in range(1, NDEV):
            px = 1 - x if (r >> 2) & 1 else x
            py = 1 - y if (r >> 1) & 1 else y
            pc = 1 - c if r & 1 else c
            pidx = 4 * px + 2 * py + pc
            for k in range(self.k_n):
                sends.append(pltpu.make_async_remote_copy(
                    src_ref=piece(k, pidx), dst_ref=out[k].at[me],
                    send_sem=send_sems.at[k, r - 1], recv_sem=recv_sems.at[k, r - 1],
                    device_id=(px, py, pc), device_id_type=pl.DeviceIdType.MESH))
                recvs.append(pltpu.make_async_remote_copy(
                    src_ref=piece(k, pidx), dst_ref=out[k].at[pidx],
                    send_sem=send_sems.at[k, r - 1], recv_sem=recv_sems.at[k, r - 1],
                    device_id=(px, py, pc), device_id_type=pl.DeviceIdType.MESH))

        def start():
            for cp in local + sends:
                cp.start()

        def wait():
            for cp in recvs:
                cp.wait_recv()
            for cp in sends:
                cp.wait_send()
            for cp in local:
                cp.wait()

        return start, wait


def exchange(srcs, per_peer, name):
    plan = AllToAll(srcs, per_peer)

    def body(*refs):
        start, wait = plan.copies(refs[:plan.k_n], refs[plan.k_n:2 * plan.k_n], *refs[2 * plan.k_n:])
        start()
        wait()

    return pl.pallas_call(
        body, name=name, out_shape=plan.out_shape, in_specs=[ANY] * plan.k_n, out_specs=[ANY] * plan.k_n,
        scratch_shapes=plan.scratch,
    )(*srcs)


def gather(srcs, name):
    k_n = len(srcs)
    out_shape = [S((NDEV,) + a.shape, a.dtype) for a in srcs]

    def body(*refs):
        src, out = refs[:k_n], refs[k_n:2 * k_n]
        send_sems, recv_sems, local_sems = refs[2 * k_n:]
        x, y, c = lax.axis_index("x"), lax.axis_index("y"), lax.axis_index("c")
        me, sibling = (x, y, c), (x, y, 1 - c)
        chips = [(1 - x, y), (x, 1 - y), (1 - x, 1 - y)]

        def slot(k, dev):
            return out[k].at[4 * dev[0] + 2 * dev[1] + dev[2]]

        def copy(k, r, block, to, from_src=False):
            return pltpu.make_async_remote_copy(
                src_ref=src[k] if from_src else slot(k, block), dst_ref=slot(k, block),
                send_sem=send_sems.at[k, r], recv_sem=recv_sems.at[k, r],
                device_id=to, device_id_type=pl.DeviceIdType.MESH)

        local = [pltpu.make_async_copy(src[k], slot(k, me), local_sems.at[k]) for k in range(k_n)]
        first = [copy(k, 0, me, sibling, True) for k in range(k_n)]
        first += [copy(k, 1 + n, me, (*chip, c), True) for n, chip in enumerate(chips) for k in range(k_n)]
        for cp in local + first:
            cp.start()
        passed = []
        for n, chip in enumerate(chips):
            for k in range(k_n):
                copy(k, 1 + n, (*chip, c), me).wait_recv()
                passed.append(copy(k, 4 + n, (*chip, c), sibling))
                passed[-1].start()
        for k in range(k_n):
            copy(k, 0, sibling, me).wait_recv()
            for n, chip in enumerate(chips):
                copy(k, 4 + n, (*chip, 1 - c), me).wait_recv()
        for cp in first + passed:
            cp.wait_send()
        for cp in local:
            cp.wait()

    any_spec = pl.BlockSpec(memory_space=pl.ANY)
    return pl.pallas_call(
        body, name=name, out_shape=out_shape,
        in_specs=[any_spec] * k_n, out_specs=[any_spec] * k_n,
        scratch_shapes=[pltpu.SemaphoreType.DMA((k_n, NDEV - 1)), pltpu.SemaphoreType.DMA((k_n, NDEV - 1)),
                        pltpu.SemaphoreType.DMA((k_n,))],
    )(*srcs)


def in_proj(h, g, w, bf, name):
    tp = h.shape[0]

    def body(h_ref, g_ref, w_ref, bf_ref, hn_ref, qa_ref, ka_ref, va_ref, pc_ref, sg_ref, carry):
        i = pl.program_id(0)

        @pl.when(i == 0)
        def _():
            carry[...] = jnp.zeros_like(carry)

        x = h_ref[...]
        r = lax.rsqrt(jnp.mean(x * x, axis=-1, keepdims=True) + EPS)
        hn = (x * r * g_ref[...]).astype(MM)
        hn_ref[...] = hn
        pc_ref[...] = jnp.dot(hn, w_ref[:, C0:F0], preferred_element_type=F32)
        z = jnp.dot(hn, w_ref[:, F0:NP_IN], preferred_element_type=F32) + bf_ref[...]
        lane = lax.broadcasted_iota(jnp.int32, z.shape, 1)
        logf = jnp.where(lane < H, jnp.minimum(z, 0.0) - jnp.log(1.0 + jnp.exp(-jnp.abs(z))), 0.0)
        sg_ref[...] = 1.0 / (1.0 + jnp.exp(z))
        row = lax.broadcasted_iota(jnp.int32, (TM, TM), 0)
        col = lax.broadcasted_iota(jnp.int32, (TM, TM), 1)
        c = _tri_matmul(row >= col, logf) + carry[...]
        carry[...] = c[TM - 1:TM, :]
        qkv = jnp.dot(hn, w_ref[:, 0:C0], preferred_element_type=F32)
        one = [1.0, 1.0, 1.0]
        for hd in range(H):
            t = slice(hd * HT, (hd + 1) * HT)
            cs = list(_split3(c[:, hd:hd + 1]))
            qa_ref[:, t] = _aug_tile(hd, qkv[:, hd * DH:(hd + 1) * DH] * SCALE, cs + one).astype(MM)
            ka_ref[:, t] = _aug_tile(hd, qkv[:, DA + hd * DH:DA + (hd + 1) * DH], one + [-v for v in cs] + one).astype(MM)
            va_ref[:, t] = _aug_tile(hd, qkv[:, 2 * DA + hd * DH:2 * DA + (hd + 1) * DH], [1.0] + one).astype(MM)

    rows = lambda n: pl.BlockSpec((TM, n), lambda i: (i, 0))
    return pl.pallas_call(
        body, name=name, grid=(tp // TM,),
        in_specs=[rows(D), _full((1, D)), _full((D, NP_IN)), _full((1, 128))],
        out_specs=[rows(D), rows(H * HT), rows(H * HT), rows(H * HT), rows(5 * DC), rows(128)],
        out_shape=[S((tp, D), MM), S((tp, H * HT), MM), S((tp, H * HT), MM), S((tp, H * HT), MM),
                   S((tp, 5 * DC), F32), S((tp, 128), F32)],
        scratch_shapes=[pltpu.VMEM((1, 128), F32)],
        compiler_params=_params(1),
    )(h, g, w, bf)


def _split3(c):
    hi = c.astype(MM).astype(F32)
    mid = (c - hi).astype(MM).astype(F32)
    lo = (c - hi - mid).astype(MM).astype(F32)
    return hi, mid, lo


def _tri_matmul(keep, x):
    pieces = jnp.concatenate(_split3(x), axis=1).astype(MM)
    r = jnp.dot(jnp.where(keep, 1.0, 0.0).astype(MM), pieces, preferred_element_type=F32)
    return r[:, 0:128] + r[:, 128:256] + r[:, 256:384]


def _main(hd):
    return slice(0, DH) if hd % 2 == 0 else slice(DH, HT)


def _aug_lane(hd, pos):
    return pos + (DH if hd % 2 == 0 else 0)


def _aug_tile(hd, main, cols):
    lane = lax.broadcasted_iota(jnp.int32, main.shape, 1)
    aug = jnp.zeros(main.shape, F32)
    for pos, val in enumerate(cols):
        aug = jnp.where(lane == pos, val, aug)
    return jnp.concatenate([main, aug] if hd % 2 == 0 else [aug, main], axis=1)


def _merge_pairs(tiles):
    lane = lax.broadcasted_iota(jnp.int32, tiles[0].shape, 1)
    return jnp.concatenate([jnp.where(lane < DH, tiles[2 * m], tiles[2 * m + 1]) for m in range(H // 2)], axis=1)


def _causal_mask():
    return lax.broadcasted_iota(jnp.int32, (TM, TM), 0) >= lax.broadcasted_iota(jnp.int32, (TM, TM), 1)


def _split_rider(refs, n_in, n_out, rider):
    if rider is None:
        return refs, None
    k = rider.k_n
    own = refs[:n_in] + refs[n_in + k:n_in + k + n_out] + refs[n_in + 2 * k + n_out:-3]
    return own, rider.copies(refs[n_in:n_in + k], refs[n_in + k + n_out:n_in + 2 * k + n_out], *refs[-3:])


def attn_fwd(qa, ka, va, name, rider=None):
    tp = qa.shape[0]
    nb = tp // TM
    pairs = [(i, j) for i in range(nb) for j in range(i + 1)]
    i_tab, j_tab = (jnp.asarray([p[a] for p in pairs], jnp.int32) for a in (0, 1))

    def body(i_ref, j_ref, *refs):
        (q_ref, k_ref, v_ref, o_ref, lse_ref, *scr), copies = _split_rider(refs, 3, 2, rider)
        m_scr, acc_scr, bias = scr[:H], scr[H:2 * H], scr[2 * H]
        n = pl.program_id(0)
        i, j = i_ref[n], j_ref[n]

        @pl.when(n == 0)
        def _():
            if copies:
                copies[0]()
            bias[...] = jnp.where(_causal_mask(), 0.0, NEG)

        @pl.when(j == 0)
        def _():
            for hd in range(H):
                m_scr[hd][...] = jnp.full((TM, 1), NEG, F32)
                acc_scr[hd][...] = jnp.zeros((TM, HT), F32)

        def step(diag):
            def logits(hd):
                t = slice(hd * HT, (hd + 1) * HT)
                s = lax.dot_general(q_ref[:, t], k_ref[:, t], NT, preferred_element_type=F32)
                return s + bias[...] if diag else s

            s_next = logits(0)
            for hd in range(H):
                t = slice(hd * HT, (hd + 1) * HT)
                s = s_next
                if hd + 1 < H:
                    s_next = logits(hd + 1)
                m_prev = m_scr[hd][...]
                m_new = jnp.maximum(m_prev, jnp.max(s, axis=1, keepdims=True))
                p = jnp.exp(s - m_new).astype(MM)
                acc_scr[hd][...] = (jnp.exp(m_prev - m_new) * acc_scr[hd][...]
                                    + jnp.dot(p, v_ref[:, t], preferred_element_type=F32))
                m_scr[hd][...] = m_new

        @pl.when(j < i)
        def _():
            step(False)

        @pl.when(j == i)
        def _():
            step(True)

        @pl.when(j == i)
        def _():
            lse = jnp.zeros((TM, 128), F32)
            outs = []
            for hd in range(H):
                acc = acc_scr[hd][...]
                l = acc[:, _aug_lane(hd, 0):_aug_lane(hd, 0) + 1]
                outs.append(acc * (1.0 / l))
                lse = _lane_put(lse, m_scr[hd][...] + jnp.log(l), hd)
            o_ref[...] = _merge_pairs(outs).astype(MM)
            lse_ref[...] = lse

        if copies:
            @pl.when(n == len(pairs) - 1)
            def _():
                copies[1]()

    qi = lambda w: pl.BlockSpec((TM, w), lambda n, it, jt: (it[n], 0))
    kv = pl.BlockSpec((TM, H * HT), lambda n, it, jt: (jt[n], 0))
    r_n = rider.k_n if rider else 0
    return pl.pallas_call(
        body, name=name,
        grid_spec=pltpu.PrefetchScalarGridSpec(
            num_scalar_prefetch=2, grid=(len(pairs),),
            in_specs=[qi(H * HT), kv, kv] + [ANY] * r_n, out_specs=[qi(DA), qi(128)] + [ANY] * r_n,
            scratch_shapes=[pltpu.VMEM((TM, 1), F32)] * H + [pltpu.VMEM((TM, HT), F32)] * H + [pltpu.VMEM((TM, TM), F32)]
            + (rider.scratch if rider else [])),
        out_shape=[S((tp, DA), MM), S((tp, 128), F32)] + (rider.out_shape if rider else []),
        compiler_params=_params(1),
    )(i_tab, j_tab, qa, ka, va, *(rider.srcs if rider else []))


def _layernorm_parts(dw, g, b):
    mu = jnp.mean(dw, axis=-1, keepdims=True)
    xc = dw - mu
    rs = lax.rsqrt(jnp.mean(xc * xc, axis=-1, keepdims=True) + EPS)
    xh = xc * rs
    return xh, rs, xh * g + b


def _fill_halo(ext, halo, cur, first):
    ext[0:HALO, :] = jnp.where(first, 0.0, halo)
    ext[HALO:HALO + TM, :] = cur


def _shift_copies(dst, src, lanes=slice(None)):
    for r in range(1, 8):
        dst[r - 1] = src[pl.ds(r, dst.shape[1]), lanes]


RC = 64


def _window(src, shifted, off, lanes=slice(None), r0=0, rows=None):
    rows = TM if rows is None else rows
    if off % 8 == 0:
        return src[pl.ds(off + r0, rows), lanes]
    return shifted[off % 8 - 1, pl.ds(off - off % 8 + r0, rows), :]


def conv_fwd(pc, wdw, bdw, lng, lnb, wpw, bpw, wsc, name):
    tp = pc.shape[0]

    def body(pc_ref, hl_ref, wdw_ref, bdw_ref, lng_ref, lnb_ref, wpw_ref, bpw_ref, wsc_ref, cc_ref, sv_ref, xe, ce, xs):
        first = pl.program_id(0) == 0
        glu = pc_ref[:, 0:DC] * _sigmoid(pc_ref[:, DC:2 * DC])
        cu = pc_ref[:, 3 * DC:4 * DC] * pc_ref[:, 4 * DC:5 * DC]
        _fill_halo(xe, hl_ref[:, 0:DC] * _sigmoid(hl_ref[:, DC:2 * DC]), glu, first)
        _fill_halo(ce, hl_ref[:, 3 * DC:4 * DC] * hl_ref[:, 4 * DC:5 * DC], cu, first)
        _shift_copies(xs, xe)
        for r0 in range(0, TM, RC):
            acc = jnp.zeros((RC, DC), F32) + bdw_ref[...]
            for k in range(CK):
                acc = acc + wdw_ref[k:k + 1, :] * _window(xe, xs, HALO - (CK - 1) + k, r0=r0, rows=RC)
            sv_ref[r0:r0 + RC, 0:DC] = acc
        dw = sv_ref[:, 0:DC]
        cv = jnp.zeros((TM, DC), F32)
        for k in range(SK):
            cv = cv + wsc_ref[k:k + 1, :] * ce[pl.ds(HALO - (SK - 1) + k, TM), :]
        _, _, ln = _layernorm_parts(dw, lng_ref[...], lnb_ref[...])
        act = ln * _sigmoid(ln)
        conf = jnp.dot(act.astype(MM), wpw_ref[...], preferred_element_type=F32) + bpw_ref[...]
        cc_ref[:, 0:DC] = conf.astype(MM)
        cc_ref[:, DC:2 * DC] = (pc_ref[:, 2 * DC:3 * DC] * cv).astype(MM)
        sv_ref[:, DC:2 * DC] = cv

    hb = TM // HALO
    rows = lambda n: pl.BlockSpec((TM, n), lambda i: (i, 0))
    return pl.pallas_call(
        body, name=name, grid=(tp // TM,),
        in_specs=[rows(5 * DC), pl.BlockSpec((HALO, 5 * DC), lambda i: (jnp.maximum(i * hb - 1, 0), 0)),
                  _full((32, DC)), _full((1, DC)), _full((1, DC)), _full((1, DC)), _full((DC, DC)), _full((1, DC)),
                  _full((8, DC))],
        out_specs=[rows(2 * DC), rows(2 * DC)],
        out_shape=[S((tp, 2 * DC), MM), S((tp, 2 * DC), F32)],
        scratch_shapes=[pltpu.VMEM((HALO + TM, DC), F32), pltpu.VMEM((HALO + TM, DC), F32),
                        pltpu.VMEM((7, TM + HALO - 8, DC), F32)],
        compiler_params=_params(1),
    )(pc, pc, wdw, bdw, lng, lnb, wpw, bpw, wsc)


def out_proj(h, o, cc, wo, g2, name):
    tp = h.shape[0]

    def body(h_ref, o_ref, cc_ref, wo_ref, g_ref, h2_ref, hn_ref):
        wo = wo_ref[...].reshape(D, D)
        h2 = (h_ref[...] + jnp.dot(o_ref[...], wo[0:DA, :], preferred_element_type=F32)
              + jnp.dot(cc_ref[...], wo[DA:D, :], preferred_element_type=F32))
        h2_ref[...] = h2
        r = lax.rsqrt(jnp.mean(h2 * h2, axis=-1, keepdims=True) + EPS)
        hn_ref[...] = (h2 * r * g_ref[...]).astype(MM)

    rows = lambda n: pl.BlockSpec((TM, n), lambda i: (i, 0))
    return pl.pallas_call(
        body, name=name, grid=(tp // TM,),
        in_specs=[rows(D), rows(DA), rows(2 * DC), _full((NDEV, D // NDEV, D)), _full((1, D))],
        out_specs=[rows(D), rows(D)],
        out_shape=[S((tp, D), F32), S((tp, D), MM)],
        compiler_params=_params(1),
    )(h, o, cc, wo, g2)


def mlp_fwd(hn, h2, w1, w2, name):
    tp = hn.shape[0]
    nf = DFF // (FS * FC)

    def body(hn_ref, h2_ref, w1_ref, w2_ref, r_ref, z_ref, h3_ref, acc):
        j = pl.program_id(1)

        @pl.when(j == 0)
        def _():
            acc[...] = jnp.zeros_like(acc)

        for s in range(FS):
            cols = slice(s * FC, (s + 1) * FC)
            r = jnp.maximum(jnp.dot(hn_ref[...], w1_ref[s], preferred_element_type=F32), 0.0)
            zb = (r * r).astype(MM)
            r_ref[:, cols] = r.astype(MM)
            z_ref[:, cols] = zb
            acc[...] += jnp.dot(zb, w2_ref[s], preferred_element_type=F32)

        @pl.when(j == nf - 1)
        def _():
            h3_ref[...] = h2_ref[...] + acc[...]

    return pl.pallas_call(
        body, name=name, grid=(tp // TM, nf),
        in_specs=[pl.BlockSpec((TM, D), lambda i, j: (i, 0)), pl.BlockSpec((TM, D), lambda i, j: (i, 0)),
                  pl.BlockSpec((FS, D, FC), lambda i, j: (j, 0, 0)), pl.BlockSpec((FS, FC, D), lambda i, j: (j, 0, 0))],
        out_specs=[pl.BlockSpec((TM, FS * FC), lambda i, j: (i, j)), pl.BlockSpec((TM, FS * FC), lambda i, j: (i, j)),
                   pl.BlockSpec((TM, D), lambda i, j: (i, 0))],
        out_shape=[S((tp, DFF), MM), S((tp, DFF), MM), S((tp, D), F32)],
        scratch_shapes=[pltpu.VMEM((TM, D), F32)],
        compiler_params=_params(2),
    )(hn, h2, w1, w2)


def loss_head(h, tgt, g, t_real, name):
    tp = h.shape[0]

    def body(h_ref, t_ref, g_ref, dh_ref, loss_ref, dg_ref):
        i = pl.program_id(0)

        @pl.when(i == 0)
        def _():
            loss_ref[...] = jnp.zeros_like(loss_ref)
            dg_ref[...] = jnp.zeros_like(dg_ref)

        x = h_ref[...]
        gg = g_ref[...]
        r = lax.rsqrt(jnp.mean(x * x, axis=-1, keepdims=True) + EPS)
        xn = x * r
        row = i * TM + lax.broadcasted_iota(jnp.int32, (TM, 1), 0)
        e = jnp.where((row >= NM) & (row < t_real), xn * gg - t_ref[...], 0.0)
        loss_ref[...] += jnp.sum(e * e) * (0.5 / D)
        dy = e * (1.0 / D)
        dg_ref[...] += jnp.sum(dy * xn, axis=0, keepdims=True)
        u = dy * gg
        dh_ref[...] = r * (u - xn * jnp.mean(u * xn, axis=-1, keepdims=True))

    rows = lambda n: pl.BlockSpec((TM, n), lambda i: (i, 0))
    return pl.pallas_call(
        body, name=name, grid=(tp // TM,),
        in_specs=[rows(D), rows(D), _full((1, D))],
        out_specs=[rows(D), _full((8, 128)), _full((1, D))],
        out_shape=[S((tp, D), F32), S((8, 128), F32), S((1, D), F32)],
        compiler_params=_params(1),
    )(h, tgt, g)


def _rmsnorm_bwd(x, g, dy):
    r = lax.rsqrt(jnp.mean(x * x, axis=-1, keepdims=True) + EPS)
    xn = x * r
    u = dy * g
    dx = r * (u - xn * jnp.mean(u * xn, axis=-1, keepdims=True))
    return dx, jnp.sum(dy * xn, axis=0, keepdims=True)


def matmul_tn(a, b, tm, tn, name, n=None, out_dtype=F32, shard=None, sw=None):
    tp, m = a.shape
    n = b.shape[1] if n is None else n
    nk = tp // TM

    def body(a_ref, b_ref, o_ref, acc):
        k = pl.program_id(2)

        @pl.when(k == 0)
        def _():
            acc[...] = jnp.zeros_like(acc)

        acc[...] += lax.dot_general(a_ref[...].astype(MM), b_ref[...].astype(MM), TN, preferred_element_type=F32)

        @pl.when(k == nk - 1)
        def _():
            if shard is None:
                o_ref[...] = acc[...].astype(out_dtype)
            elif shard == "m":
                for s in range(tm // sw):
                    o_ref[s] = acc[s * sw:(s + 1) * sw, :].astype(out_dtype)
            else:
                for s in range(tn // sw):
                    o_ref[s] = acc[:, s * sw:(s + 1) * sw].astype(out_dtype)

    if shard is None:
        out_spec, out_shape = pl.BlockSpec((tm, tn), lambda mi, ni, k: (mi, ni)), (m, n)
    elif shard == "m":
        out_spec, out_shape = pl.BlockSpec((tm // sw, sw, tn), lambda mi, ni, k: (mi, 0, ni)), (m // sw, sw, n)
    else:
        out_spec, out_shape = pl.BlockSpec((tn // sw, tm, sw), lambda mi, ni, k: (ni, mi, 0)), (n // sw, m, sw)
    return pl.pallas_call(
        body, name=name, grid=(m // tm, n // tn, nk),
        in_specs=[pl.BlockSpec((TM, tm), lambda mi, ni, k: (k, mi)), pl.BlockSpec((TM, tn), lambda mi, ni, k: (k, ni))],
        out_specs=out_spec, out_shape=S(out_shape, out_dtype),
        scratch_shapes=[pltpu.VMEM((tm, tn), F32)],
        compiler_params=_params(3),
    )(a, b)


def mlp_bwd(dh3, r, w1, w2, h2, g, name):
    tp = dh3.shape[0]
    nf = DFF // (FS * FC)

    def body(dh3_ref, r_ref, w1_ref, w2_ref, h2_ref, g_ref, da_ref, dh2_ref, dg_ref, acc, dhb):
        i, j = pl.program_id(0), pl.program_id(1)

        @pl.when((i == 0) & (j == 0))
        def _():
            dg_ref[...] = jnp.zeros_like(dg_ref)

        @pl.when(j == 0)
        def _():
            acc[...] = jnp.zeros_like(acc)
            dhb[...] = dh3_ref[...].astype(MM)

        for s in range(FS):
            cols = slice(s * FC, (s + 1) * FC)
            dz = lax.dot_general(dhb[...], w2_ref[s], NT, preferred_element_type=F32)
            da = (dz * (2.0 * r_ref[:, cols].astype(F32))).astype(MM)
            da_ref[:, cols] = da
            acc[...] += lax.dot_general(da, w1_ref[s], NT, preferred_element_type=F32)

        @pl.when(j == nf - 1)
        def _():
            dx, dg = _rmsnorm_bwd(h2_ref[...], g_ref[...], acc[...])
            dh2_ref[...] = dh3_ref[...] + dx
            dg_ref[...] += dg

    return pl.pallas_call(
        body, name=name, grid=(tp // TM, nf),
        in_specs=[pl.BlockSpec((TM, D), lambda i, j: (i, 0)), pl.BlockSpec((TM, FS * FC), lambda i, j: (i, j)),
                  pl.BlockSpec((FS, D, FC), lambda i, j: (j, 0, 0)), pl.BlockSpec((FS, FC, D), lambda i, j: (j, 0, 0)),
                  pl.BlockSpec((TM, D), lambda i, j: (i, 0)), _full((1, D))],
        out_specs=[pl.BlockSpec((TM, FS * FC), lambda i, j: (i, j)), pl.BlockSpec((TM, D), lambda i, j: (i, 0)),
                   _full((1, D))],
        out_shape=[S((tp, DFF), MM), S((tp, D), F32), S((1, D), F32)],
        scratch_shapes=[pltpu.VMEM((TM, D), F32), pltpu.VMEM((TM, D), MM)],
        compiler_params=_params(2),
    )(dh3, r, w1, w2, h2, g)


def out_proj_bwd(dh2, wo, name):
    tp = dh2.shape[0]

    def body(dh_ref, wo_ref, do_ref, dcc_ref):
        dcat = lax.dot_general(dh_ref[...].astype(MM), wo_ref[...].reshape(D, D), NT, preferred_element_type=F32)
        do_ref[...] = dcat[:, 0:DA].astype(MM)
        dcc_ref[...] = dcat[:, DA:D]

    rows = lambda n: pl.BlockSpec((TM, n), lambda i: (i, 0))
    return pl.pallas_call(
        body, name=name, grid=(tp // TM,),
        in_specs=[rows(D), _full((NDEV, D // NDEV, D))],
        out_specs=[rows(DA), rows(2 * DC)],
        out_shape=[S((tp, DA), MM), S((tp, 2 * DC), F32)],
        compiler_params=_params(1),
    )(dh2, wo)


def attn_bwd_prep(qa, do, o, lse, name):
    tp = qa.shape[0]

    def body(q_ref, do_ref, o_ref, lse_ref, qb_ref, dob_ref, qbt_ref, dobt_ref):
        lse = lse_ref[...]
        lane = lax.broadcasted_iota(jnp.int32, (TM, HT), 1)
        for hd in range(H):
            t, hs = slice(hd * HT, (hd + 1) * HT), slice(hd * DH, (hd + 1) * DH)
            dof = do_ref[:, hs].astype(F32)
            dd = jnp.sum(dof * o_ref[:, hs].astype(F32), axis=1, keepdims=True)
            dob = _aug_tile(hd, dof, [0.0] + [-v for v in _split3(dd)])
            qb = q_ref[:, t].astype(F32)
            for pos, v in enumerate(_split3(lse[:, hd:hd + 1])):
                qb = jnp.where(lane == _aug_lane(hd, 6 + pos), -v, qb)
            qb_ref[:, t] = qb.astype(MM)
            dob_ref[:, t] = dob.astype(MM)
            qbt_ref[t, :] = qb.T.astype(MM)
            dobt_ref[t, :] = dob.T.astype(MM)

    rows = lambda n: pl.BlockSpec((TM, n), lambda i: (i, 0))
    cols = pl.BlockSpec((H * HT, TM), lambda i: (0, i))
    return pl.pallas_call(
        body, name=name, grid=(tp // TM,),
        in_specs=[rows(H * HT), rows(DA), rows(DA), rows(128)],
        out_specs=[rows(H * HT), rows(H * HT), cols, cols],
        out_shape=[S((tp, H * HT), MM), S((tp, H * HT), MM), S((H * HT, tp), MM), S((H * HT, tp), MM)],
        compiler_params=_params(1),
    )(qa, do, o, lse)


def attn_bwd(qb, dob, qbt, dobt, ka, va, name, rider=None):
    tp = qb.shape[0]
    nb = tp // TM

    pairs = [(j, i) for j in range(nb) for i in range(j, nb)]
    j_tab, i_tab = (jnp.asarray([p[a] for p in pairs], jnp.int32) for a in (0, 1))

    def body(j_ref, i_ref, *refs):
        own, copies = _split_rider(refs, 6, 5, rider)
        (qb_ref, dob_ref, qbt_ref, dobt_ref, k_ref, v_ref, dq_ref, dk_ref, dv_ref, dcq_ref, dck_ref,
         dq_acc, dcq_acc, dkt_acc, dvt_acc, dck_acc, stage, sems) = own
        n = pl.program_id(0)
        j, i = j_ref[n], i_ref[n]

        @pl.when(n == 0)
        def _():
            if copies:
                copies[0]()
            dq_acc[...] = jnp.zeros_like(dq_acc)
            dcq_acc[...] = jnp.zeros_like(dcq_acc)

        @pl.when(i == j)
        def _():
            dkt_acc[...] = jnp.zeros_like(dkt_acc)
            dvt_acc[...] = jnp.zeros_like(dvt_acc)
            dck_acc[...] = jnp.zeros_like(dck_acc)

        def step(diag):
            rows = pl.ds(pl.multiple_of(i * TM, TM), TM)
            dcq = dcq_acc[rows, :]
            lane = lax.broadcasted_iota(jnp.int32, (TM, HT), 1)
            mask = _causal_mask() if diag else None
            for hd in range(H):
                t = slice(hd * HT, (hd + 1) * HT)
                s = lax.dot_general(qb_ref[:, t], k_ref[:, t], NT, preferred_element_type=F32)
                dp = lax.dot_general(dob_ref[:, t], v_ref[:, t], NT, preferred_element_type=F32)
                if diag:
                    s = jnp.where(mask, s, NEG)
                p = jnp.exp(s)
                ds = p * dp
                dsb = ds.astype(MM)
                dvt_acc[t, :] += jnp.dot(dobt_ref[t, :], p.astype(MM), preferred_element_type=F32)
                dkt_acc[t, :] += jnp.dot(qbt_ref[t, :], dsb, preferred_element_type=F32)
                dqh = jnp.dot(dsb, k_ref[:, t], preferred_element_type=F32)
                if hd % 2 == 0:
                    dq_even = dqh
                else:
                    pair = slice((hd // 2) * HT, (hd // 2 + 1) * HT)
                    dq_acc[rows, pair] += jnp.where(lane < DH, dq_even, dqh)
                dcq = dcq + jnp.where(lane == hd, jnp.sum(ds, axis=1, keepdims=True), 0.0)
                dck_acc[hd:hd + 1, :] += jnp.sum(ds, axis=0, keepdims=True)
            dcq_acc[rows, :] = dcq
            return rows

        @pl.when(i > j)
        def _():
            step(False)

        @pl.when(i == j)
        def _():
            rows = step(True)
            stage[...] = (dq_acc[rows, :] * SCALE).astype(MM)
            out = [pltpu.make_async_copy(stage, dq_ref.at[rows, :], sems.at[0]),
                   pltpu.make_async_copy(dcq_acc.at[rows, :], dcq_ref.at[rows, :], sems.at[1])]
            for cp in out:
                cp.start()
            for cp in out:
                cp.wait()

        @pl.when(i == nb - 1)
        def _():
            dk_ref[...] = _merge_pairs([dkt_acc[hd * HT:(hd + 1) * HT, :].T for hd in range(H)]).astype(MM)
            dv_ref[...] = _merge_pairs([dvt_acc[hd * HT:(hd + 1) * HT, :].T for hd in range(H)]).astype(MM)
            dck_ref[...] = dck_acc[...]

        if copies:
            @pl.when(n == len(pairs) - 1)
            def _():
                copies[1]()

    qi = lambda w: pl.BlockSpec((TM, w), lambda n, jt, it: (it[n], 0))
    qt = pl.BlockSpec((H * HT, TM), lambda n, jt, it: (0, it[n]))
    kj = lambda w: pl.BlockSpec((TM, w), lambda n, jt, it: (jt[n], 0))
    r_n = rider.k_n if rider else 0
    return pl.pallas_call(
        body, name=name,
        grid_spec=pltpu.PrefetchScalarGridSpec(
            num_scalar_prefetch=2, grid=(len(pairs),),
            in_specs=[qi(H * HT), qi(H * HT), qt, qt, kj(H * HT), kj(H * HT)] + [ANY] * r_n,
            out_specs=[ANY, kj(DA), kj(DA), ANY, pl.BlockSpec((H, TM), lambda n, jt, it: (0, jt[n]))] + [ANY] * r_n,
            scratch_shapes=[pltpu.VMEM((tp, DA), F32), pltpu.VMEM((tp, 128), F32), pltpu.VMEM((H * HT, TM), F32),
                            pltpu.VMEM((H * HT, TM), F32), pltpu.VMEM((H, TM), F32), pltpu.VMEM((TM, DA), MM),
                            pltpu.SemaphoreType.DMA((2,))] + (rider.scratch if rider else [])),
        out_shape=[S((tp, DA), MM), S((tp, DA), MM), S((tp, DA), MM), S((tp, 128), F32), S((H, tp), F32)]
        + (rider.out_shape if rider else []),
        compiler_params=_params(1),
    )(j_tab, i_tab, qb, dob, qbt, dobt, ka, va, *(rider.srcs if rider else []))


def fgate_bwd(dcq, dck, sg, name):
    tp = dcq.shape[0]
    nb = tp // TM

    def body(dcq_ref, dck_ref, sg_ref, dz_ref, db_ref, carry):
        @pl.when(pl.program_id(0) == 0)
        def _():
            carry[...] = jnp.zeros_like(carry)
            db_ref[...] = jnp.zeros_like(db_ref)

        row = lax.broadcasted_iota(jnp.int32, (TM, TM), 0)
        col = lax.broadcasted_iota(jnp.int32, (TM, TM), 1)
        dl = _tri_matmul(row <= col, dcq_ref[...] - dck_ref[...]) + carry[...]
        carry[...] = dl[0:1, :]
        dz = dl * sg_ref[...]
        dz_ref[...] = dz.astype(MM)
        db_ref[...] += jnp.sum(dz, axis=0, keepdims=True)

    rev = pl.BlockSpec((TM, 128), lambda i: (nb - 1 - i, 0))
    return pl.pallas_call(
        body, name=name, grid=(nb,),
        in_specs=[rev, rev, rev], out_specs=[rev, _full((1, 128))],
        out_shape=[S((tp, 128), MM), S((1, 128), F32)],
        scratch_shapes=[pltpu.VMEM((1, 128), F32)],
        compiler_params=_params(1),
    )(dcq, dck, sg)


def conv_bwd_pointwise(dcc, sv, pc, lng, lnb, wpw, name):
    tp = dcc.shape[0]

    def body(dcc_ref, sv_ref, b_ref, lng_ref, lnb_ref, wpw_ref, gc_ref, act_ref, pg_ref):
        @pl.when(pl.program_id(0) == 0)
        def _():
            pg_ref[...] = jnp.zeros_like(pg_ref)

        dconf = dcc_ref[:, 0:DC]
        g = lng_ref[...]
        xh, rs, ln = _layernorm_parts(sv_ref[:, 0:DC], g, lnb_ref[...])
        sig = _sigmoid(ln)
        act_ref[...] = (ln * sig).astype(MM)
        dact = lax.dot_general(dconf.astype(MM), wpw_ref[...], NT, preferred_element_type=F32)
        dln = dact * (sig * (1.0 + ln * (1.0 - sig)))
        dxh = dln * g
        ddw = rs * (dxh - jnp.mean(dxh, axis=-1, keepdims=True) - xh * jnp.mean(dxh * xh, axis=-1, keepdims=True))
        gc_ref[:, 0:DC] = ddw
        gc_ref[:, DC:2 * DC] = dcc_ref[:, DC:2 * DC] * b_ref[...]
        cs = lambda t: jnp.sum(t, axis=0, keepdims=True)
        pg_ref[0:1, :] += cs(dconf)
        pg_ref[1:2, :] += cs(dln * xh)
        pg_ref[2:3, :] += cs(dln)
        pg_ref[3:4, :] += cs(ddw)

    rows = lambda n: pl.BlockSpec((TM, n), lambda i: (i, 0))
    return pl.pallas_call(
        body, name=name, grid=(tp // TM,),
        in_specs=[rows(2 * DC), rows(2 * DC), pl.BlockSpec((TM, DC), lambda i: (i, 2)),
                  _full((1, DC)), _full((1, DC)), _full((DC, DC))],
        out_specs=[rows(2 * DC), rows(DC), _full((8, DC))],
        out_shape=[S((tp, 2 * DC), F32), S((tp, DC), MM), S((8, DC), F32)],
        compiler_params=_params(1),
    )(dcc, sv, pc, lng, lnb, wpw)


def conv_bwd_taps(gc, pc, dcc, sv, wdw, wsc, name):
    tp = gc.shape[0]
    nb = tp // TM
    hb = TM // HALO

    def body(gc_ref, gn_ref, pc_ref, hl_ref, dcc_ref, sv_ref, wdw_ref, wsc_ref, dpc_ref, wg_ref, ge, xe, ce, gs, xs):
        i = pl.program_id(0)

        @pl.when(i == 0)
        def _():
            wg_ref[...] = jnp.zeros_like(wg_ref)

        a, gt = pc_ref[:, 0:DC], pc_ref[:, DC:2 * DC]
        c, u = pc_ref[:, 3 * DC:4 * DC], pc_ref[:, 4 * DC:5 * DC]
        sig = _sigmoid(gt)
        _fill_halo(xe, hl_ref[:, 0:DC] * _sigmoid(hl_ref[:, DC:2 * DC]), a * sig, i == 0)
        _fill_halo(ce, hl_ref[:, 3 * DC:4 * DC] * hl_ref[:, 4 * DC:5 * DC], c * u, i == 0)
        ge[0:TM, :] = gc_ref[...]
        ge[TM:TM + HALO, :] = jnp.where(i == nb - 1, 0.0, gn_ref[...])
        _shift_copies(gs, ge, slice(0, DC))
        _shift_copies(xs, xe)
        dcv = gc_ref[:, DC:2 * DC]
        for r0 in range(0, TM, RC):
            rs = slice(r0, r0 + RC)
            ddw = gc_ref[rs, 0:DC]
            dglu = jnp.zeros((RC, DC), F32)
            for k in range(CK):
                dglu = dglu + wdw_ref[k:k + 1, :] * _window(ge, gs, CK - 1 - k, slice(0, DC), r0, RC)
                wg_ref[k:k + 1, :] += jnp.sum(ddw * _window(xe, xs, HALO - (CK - 1) + k, r0=r0, rows=RC), axis=0,
                                              keepdims=True)
            sig_c = _sigmoid(pc_ref[rs, DC:2 * DC])
            dpc_ref[rs, 0:DC] = (dglu * sig_c).astype(MM)
            dpc_ref[rs, DC:2 * DC] = (dglu * pc_ref[rs, 0:DC] * sig_c * (1.0 - sig_c)).astype(MM)
        dcu = jnp.zeros((TM, DC), F32)
        for k in range(SK):
            dcu = dcu + wsc_ref[k:k + 1, :] * ge[pl.ds(SK - 1 - k, TM), DC:2 * DC]
            wg_ref[32 + k:33 + k, :] += jnp.sum(dcv * ce[pl.ds(HALO - (SK - 1) + k, TM), :], axis=0, keepdims=True)
        dpc_ref[:, 2 * DC:3 * DC] = (dcc_ref[:, DC:2 * DC] * sv_ref[:, DC:2 * DC]).astype(MM)
        dpc_ref[:, 3 * DC:4 * DC] = (dcu * u).astype(MM)
        dpc_ref[:, 4 * DC:5 * DC] = (dcu * c).astype(MM)

    rows = lambda n: pl.BlockSpec((TM, n), lambda i: (i, 0))
    return pl.pallas_call(
        body, name=name, grid=(nb,),
        in_specs=[rows(2 * DC), pl.BlockSpec((HALO, 2 * DC), lambda i: (jnp.minimum((i + 1) * hb, nb * hb - 1), 0)),
                  rows(5 * DC), pl.BlockSpec((HALO, 5 * DC), lambda i: (jnp.maximum(i * hb - 1, 0), 0)),
                  rows(2 * DC), rows(2 * DC), _full((32, DC)), _full((8, DC))],
        out_specs=[rows(5 * DC), _full((40, DC))],
        out_shape=[S((tp, 5 * DC), MM), S((40, DC), F32)],
        scratch_shapes=[pltpu.VMEM((TM + HALO, 2 * DC), F32), pltpu.VMEM((HALO + TM, DC), F32),
                        pltpu.VMEM((HALO + TM, DC), F32), pltpu.VMEM((7, TM + HALO - 8, DC), F32),
                        pltpu.VMEM((7, TM + HALO - 8, DC), F32)],
        compiler_params=_params(1),
    )(gc, gc, pc, pc, dcc, sv, wdw, wsc)


def in_proj_bwd(dproj, w, h, g, dh_in, name, rider=None):
    tp = h.shape[0]
    nb = tp // TM

    def body(*refs):
        (dp_ref, w_ref, h_ref, g_ref, di_ref, dh_ref, dg_ref), copies = _split_rider(refs, 5, 2, rider)
        i = pl.program_id(0)

        @pl.when(i == 0)
        def _():
            if copies:
                copies[0]()
            dg_ref[...] = jnp.zeros_like(dg_ref)

        dhn = lax.dot_general(dp_ref[...], w_ref[...], NT, preferred_element_type=F32)
        dx, dg = _rmsnorm_bwd(h_ref[...], g_ref[...], dhn)
        dh_ref[...] = di_ref[...] + dx
        dg_ref[...] += dg

        if copies:
            @pl.when(i == nb - 1)
            def _():
                copies[1]()

    rows = lambda n: pl.BlockSpec((TM, n), lambda i: (i, 0))
    r_n = rider.k_n if rider else 0
    return pl.pallas_call(
        body, name=name, grid=(nb,),
        in_specs=[rows(NP_IN), _full((D, NP_IN)), rows(D), _full((1, D)), rows(D)] + [ANY] * r_n,
        out_specs=[rows(D), _full((1, D))] + [ANY] * r_n,
        out_shape=[S((tp, D), F32), S((1, D), F32)] + (rider.out_shape if rider else []),
        scratch_shapes=rider.scratch if rider else [],
        compiler_params=_params(1),
    )(dproj, w, h, g, dh_in, *(rider.srcs if rider else []))


def adamw(recvs, w, m, v, rb, name):
    l_n, r_n, c_n = w.shape

    def body(*refs):
        p_refs = refs[:l_n]
        w_ref, m_ref, v_ref, g_ref, d_ref, m2_ref, v2_ref = refs[l_n:]
        for l in range(l_n):
            g = p_refs[l][0].astype(F32)
            for s in range(1, NDEV):
                g = g + p_refs[l][s].astype(F32)
            m2 = ADAM_B1 * m_ref[l] + (1.0 - ADAM_B1) * g
            v2 = ADAM_B2 * v_ref[l] + (1.0 - ADAM_B2) * (g * g)
            m_hat = m2 / (1.0 - ADAM_B1 ** ADAM_STEP)
            v_hat = v2 / (1.0 - ADAM_B2 ** ADAM_STEP)
            g_ref[l] = g
            d_ref[l] = -ADAM_LR * (m_hat / (jnp.sqrt(v_hat) + ADAM_EPS) + ADAM_WD * w_ref[l])
            m2_ref[l] = m2
            v2_ref[l] = v2

    blk = pl.BlockSpec((l_n, rb, c_n), lambda r: (0, r, 0))
    return pl.pallas_call(
        body, name=name, grid=(r_n // rb,),
        in_specs=[pl.BlockSpec((NDEV, rb, c_n), lambda r: (0, r, 0))] * l_n + [blk, blk, blk],
        out_specs=[blk] * 4, out_shape=[S(w.shape, F32)] * 4,
        compiler_params=_params(1),
    )(*recvs, w, m, v)


TINY_ROWS = 168
REP_ROWS = 64


def _pack(parts, rows):
    flat = jnp.concatenate([p.reshape(-1) for p in parts])
    return jnp.pad(flat, (0, rows * 128 - flat.shape[0])).reshape(rows, 128)


def _unpack(buf, shapes):
    flat, out, o = buf.reshape(-1), [], 0
    for s in shapes:
        n = 1
        for d in s:
            n *= d
        out.append(flat[o:o + n].reshape(s))
        o += n
    return out


TINY_SHAPES = [(DEPTH, CK, DC // NDEV), (DEPTH, SK, DC // NDEV), (NM, D // NDEV), (DEPTH, DC // NDEV, DC)]
REP_SHAPES = [(DEPTH, D), (DEPTH, H), (DEPTH, DC), (DEPTH, DC), (DEPTH, DC), (DEPTH, DC), (DEPTH, D), (D,)]


def _to_padded_cols(w):
    pad = jnp.zeros(w.shape[:-1] + (NP_IN - N_IN,), w.dtype)
    return jnp.concatenate([w[..., :C0], w[..., C0 + H:], w[..., C0:C0 + H], pad], axis=-1)


def _from_padded_cols(w):
    return jnp.concatenate([w[..., :C0], w[..., F0:F0 + H], w[..., C0:F0]], axis=-1)


def kernel(x, meta_tokens, mix_norm_g, w_in, b_forget, w_conf_dw, b_conf_dw, conf_ln_g, conf_ln_b, w_conf_pw, b_conf_pw, w_sc_conv, w_out, mlp_norm_g, w_mlp1, w_mlp2, final_norm_g, loss_target, m_meta_tokens, m_mix_norm_g, m_w_in, m_b_forget, m_w_conf_dw, m_b_conf_dw, m_conf_ln_g, m_conf_ln_b, m_w_conf_pw, m_b_conf_pw, m_w_sc_conv, m_w_out, m_mlp_norm_g, m_w_mlp1, m_w_mlp2, m_final_norm_g, v_meta_tokens, v_mix_norm_g, v_w_in, v_b_forget, v_w_conf_dw, v_b_conf_dw, v_conf_ln_g, v_conf_ln_b, v_w_conf_pw, v_b_conf_pw, v_w_sc_conv, v_w_out, v_mlp_norm_g, v_w_mlp1, v_w_mlp2, v_final_norm_g):
    seq = x.shape[1]
    t_real = NM + seq
    tp = -(-t_real // TM) * TM

    tiny_w = _pack([w_conf_dw, w_sc_conv, meta_tokens, w_conf_pw], TINY_ROWS)
    big = lambda l: [w_in[l].astype(MM), w_out[l].astype(MM), w_mlp1[l].astype(MM), w_mlp2[l].astype(MM)]
    *first, g_tiny = gather(big(0) + [tiny_w], "gather_weights0")
    gathered = {0: first}
    n_sh = w_in.shape[-1]
    assert w_mlp1.shape[-1] == FC and w_mlp2.shape[-2] == FC
    tiny = [_unpack(g_tiny[s], TINY_SHAPES) for s in range(NDEV)]
    wdw = jnp.concatenate([t[0] for t in tiny], axis=-1)
    wsc = jnp.concatenate([t[1] for t in tiny], axis=-1)
    meta = jnp.concatenate([t[2] for t in tiny], axis=-1)
    wpw = jnp.concatenate([t[3] for t in tiny], axis=1).astype(MM)
    wdw = jnp.pad(wdw, ((0, 0), (0, 32 - CK), (0, 0)))
    wsc = jnp.pad(wsc, ((0, 0), (0, 8 - SK), (0, 0)))
    bfp = jnp.pad(b_forget, ((0, 0), (0, 128 - H)))

    row = lambda a: a.reshape(1, -1)

    h = jnp.concatenate([meta, x[0], jnp.zeros((tp - t_real, D), F32)], axis=0)
    tgt = jnp.pad(loss_target[0], ((NM, tp - t_real), (0, 0)))
    saved = []
    for l in range(DEPTH):
        g_in, g_out, g_w1, g_w2 = gathered[l]
        win = _to_padded_cols(g_in.transpose(1, 0, 2).reshape(D, NDEV * n_sh))
        hn, qa, ka, va, pc, sg = in_proj(h, row(mix_norm_g[l]), win, row(bfp[l]), f"in_proj{l}")
        rider = AllToAll(big(l + 1), [False] * 4) if l + 1 < DEPTH else None
        o, lse, *arrived = attn_fwd(qa, ka, va, f"attn_fwd{l}", rider)
        if rider:
            gathered[l + 1] = arrived
        cc, sv = conv_fwd(pc, wdw[l], row(b_conf_dw[l]), row(conf_ln_g[l]), row(conf_ln_b[l]), wpw[l],
                          row(b_conf_pw[l]), wsc[l], f"conv_fwd{l}")
        h2, hn2 = out_proj(h, o, cc, g_out, row(mlp_norm_g[l]), f"out_proj{l}")
        r, z, h3 = mlp_fwd(hn2, h2, g_w1, g_w2, f"mlp_fwd{l}")
        saved.append((h, hn, qa, ka, va, pc, sg, o, lse, cc, sv, h2, hn2, r, z, win))
        h = h3

    dh, loss_part, d_gf = loss_head(h, tgt, row(final_norm_g), t_real, "loss_head")
    loss = lax.psum(loss_part[0, 0], ("x", "y", "c"))

    gw, recv, d_win = {}, {}, {}
    for l in reversed(range(DEPTH)):
        h0, hn, qa, ka, va, pc, sg, o, lse, cc, sv, h2, hn2, r, z, win = saved[l]
        _, g_out, g_w1, g_w2 = gathered[l]
        da, dh2, gw["mlp_g", l] = mlp_bwd(dh, r, g_w1, g_w2, h2, row(mlp_norm_g[l]), f"mlp_bwd{l}")
        d_w1 = matmul_tn(hn2, da, D, 4 * FC, f"dw_mlp1_{l}", out_dtype=MM, shard="n", sw=FC)
        d_w2 = matmul_tn(z, dh, 4 * FC, D, f"dw_mlp2_{l}", out_dtype=MM, shard="m", sw=FC)
        do, dcc = out_proj_bwd(dh2, g_out, f"out_proj_bwd{l}")
        d_wout = jnp.concatenate([matmul_tn(o, dh2, DA, D, f"dw_out_a{l}"), matmul_tn(cc, dh2, 2 * DC, D, f"dw_out_c{l}")],
                                 axis=0).reshape(NDEV, D // NDEV, D).astype(MM)
        qb, dob, qbt, dobt = attn_bwd_prep(qa, do, o, lse, f"attn_bwd_prep{l}")
        riding = [("wout", l, d_wout), ("w1", l, d_w1), ("w2", l, d_w2)] + ([("win", l + 1, d_win[l + 1])] if l + 1 < DEPTH else [])
        dq, dk, dv, dcq, dck, *arrived = attn_bwd(qb, dob, qbt, dobt, ka, va, f"attn_bwd{l}",
                                                  AllToAll([a for _, _, a in riding], [True] * len(riding)))
        for (k, kl, _), a in zip(riding, arrived):
            recv[k, kl] = a
        dz, gw["bf", l] = fgate_bwd(dcq, jnp.pad(dck.T, ((0, 0), (0, 128 - H))), sg, f"fgate_bwd{l}")
        gc, act, pg = conv_bwd_pointwise(dcc, sv, pc, row(conf_ln_g[l]), row(conf_ln_b[l]), wpw[l], f"conv_bwd_pw{l}")
        gw["wpw", l] = matmul_tn(act, dcc, DC, DC, f"dw_conf_pw{l}", n=DC)
        dpc, wg = conv_bwd_taps(gc, pc, dcc, sv, wdw[l], wsc[l], f"conv_bwd_taps{l}")
        gw["pg", l], gw["wg", l] = pg, wg
        dproj = jnp.concatenate([dq, dk, dv, dpc, dz], axis=1)
        d_win[l] = _from_padded_cols(matmul_tn(hn, dproj, 512, NP_IN, f"dw_in{l}")).reshape(D, NDEV, n_sh).transpose(
            1, 0, 2).astype(MM)
        rider = AllToAll([d_win[0]], [True]) if l == 0 else None
        dh, gw["mix_g", l], *arrived = in_proj_bwd(dproj, win, h0, row(mix_norm_g[l]), dh2, f"in_proj_bwd{l}", rider)
        if rider:
            recv["win", 0] = arrived[0]

    grad_x = dh[NM:t_real][None]
    stack = lambda k: jnp.stack([gw[k, l] for l in range(DEPTH)])

    d_wdw = stack("wg")[:, 0:CK].reshape(DEPTH, CK, NDEV, DC // NDEV).transpose(2, 0, 1, 3)
    d_wsc = stack("wg")[:, 32:32 + SK].reshape(DEPTH, SK, NDEV, DC // NDEV).transpose(2, 0, 1, 3)
    d_meta = dh[0:NM].reshape(NM, NDEV, D // NDEV).transpose(1, 0, 2)
    d_wpw = stack("wpw").reshape(DEPTH, NDEV, DC // NDEV, DC).transpose(1, 0, 2, 3)
    d_tiny = jnp.stack([_pack([d_wdw[p], d_wsc[p], d_meta[p], d_wpw[p]], TINY_ROWS) for p in range(NDEV)])
    pgs = stack("pg")
    d_rep = _pack([stack("mix_g").reshape(DEPTH, D), stack("bf")[:, 0, :H], pgs[:, 3], pgs[:, 1], pgs[:, 2], pgs[:, 0],
                   stack("mlp_g").reshape(DEPTH, D), d_gf.reshape(D)], REP_ROWS)
    r_tiny, r_rep = exchange([d_tiny, d_rep], [True, False], "exchange_grads")

    per_layer = lambda k: [recv[k, l] for l in range(DEPTH)]
    res = {}
    res["w_in"] = adamw(per_layer("win"), w_in, m_w_in, v_w_in, 256, "adamw_w_in")
    res["w_out"] = adamw(per_layer("wout"), w_out, m_w_out, v_w_out, D // NDEV, "adamw_w_out")
    res["w_mlp1"] = adamw(per_layer("w1"), w_mlp1, m_w_mlp1, v_w_mlp1, 256, "adamw_w_mlp1")
    res["w_mlp2"] = adamw(per_layer("w2"), w_mlp2, m_w_mlp2, v_w_mlp2, 128, "adamw_w_mlp2")
    tiny_names = ["w_conf_dw", "w_sc_conv", "meta_tokens", "w_conf_pw"]
    tiny_wmv = [[w_conf_dw, w_sc_conv, meta_tokens, w_conf_pw], [m_w_conf_dw, m_w_sc_conv, m_meta_tokens, m_w_conf_pw],
                [v_w_conf_dw, v_w_sc_conv, v_meta_tokens, v_w_conf_pw]]
    rep_names = ["mix_norm_g", "b_forget", "b_conf_dw", "conf_ln_g", "conf_ln_b", "b_conf_pw", "mlp_norm_g", "final_norm_g"]
    rep_wmv = [[mix_norm_g, b_forget, b_conf_dw, conf_ln_g, conf_ln_b, b_conf_pw, mlp_norm_g, final_norm_g],
               [m_mix_norm_g, m_b_forget, m_b_conf_dw, m_conf_ln_g, m_conf_ln_b, m_b_conf_pw, m_mlp_norm_g, m_final_norm_g],
               [v_mix_norm_g, v_b_forget, v_b_conf_dw, v_conf_ln_g, v_conf_ln_b, v_b_conf_pw, v_mlp_norm_g, v_final_norm_g]]
    for names, wmv, shapes, recv_pack, rows_n, nm in ((tiny_names, tiny_wmv, TINY_SHAPES, r_tiny, TINY_ROWS, "adamw_tiny"),
                                                 (rep_names, rep_wmv, REP_SHAPES, r_rep, REP_ROWS, "adamw_rep")):
        packed = [_pack(group, rows_n)[None] for group in wmv]
        outs = adamw([recv_pack], *packed, rows_n, nm)
        parts = [_unpack(a[0], shapes) for a in outs]
        for k, n in enumerate(names):
            res[n] = [parts[q][k] for q in range(4)]

    order = ["meta_tokens", "mix_norm_g", "w_in", "b_forget", "w_conf_dw", "b_conf_dw", "conf_ln_g", "conf_ln_b",
             "w_conf_pw", "b_conf_pw", "w_sc_conv", "w_out", "mlp_norm_g", "w_mlp1", "w_mlp2", "final_norm_g"]
    return (loss, grad_x, *[res[n][0] for n in order], *[res[n][1] for n in order],
            *[res[n][2] for n in order], *[res[n][3] for n in order])
```

```python
import functools

import jax
import jax.numpy as jnp
from jax import lax
from jax.experimental import pallas as pl
from jax.experimental.pallas import tpu as pltpu

F32 = jnp.float32
MM = jnp.bfloat16

D = 1024
H = 8
DH = 64
DA = H * DH
HT = 128
DC = 256
NM = 16
CK = 31
SK = 3
DFF = 4096
DEPTH = 2
N_IN = 3 * DA + H + 2 * DC + 3 * DC
NP_IN = 3 * DA + 5 * DC + 128
C0 = 3 * DA
F0 = 3 * DA + 5 * DC
EPS = 1e-6
TM = 640
HALO = 32
FC = 512
FS = 4
NDEV = 8
SCALE = DH ** -0.5
NEG = -1e30

ADAM_LR, ADAM_B1, ADAM_B2, ADAM_EPS, ADAM_WD, ADAM_STEP = 0.001, 0.9, 0.999, 1e-08, 0.01, 10

VMEM_LIMIT = 56 * 1024 * 1024

S = jax.ShapeDtypeStruct
NT = (((1,), (1,)), ((), ()))
TN = (((0,), (0,)), ((), ()))


def _params(n_grid):
    return pltpu.CompilerParams(dimension_semantics=("arbitrary",) * n_grid, vmem_limit_bytes=VMEM_LIMIT)


def _sigmoid(x):
    return 1.0 / (1.0 + jnp.exp(-x))


def _full(shape):
    n = len(shape)
    return pl.BlockSpec(shape, lambda *_: (0,) * n)


def _lane_put(dst, col, h):
    lane = lax.broadcasted_iota(jnp.int32, dst.shape, 1)
    return jnp.where(lane == h, col, dst)


ANY = pl.BlockSpec(memory_space=pl.ANY)


class AllToAll:
    def __init__(self, srcs, per_peer):
        self.srcs, self.per_peer, self.k_n = list(srcs), list(per_peer), len(srcs)
        self.out_shape = [S((NDEV,) + (a.shape[1:] if pp else a.shape), a.dtype) for a, pp in zip(srcs, per_peer)]
        self.scratch = [pltpu.SemaphoreType.DMA((self.k_n, NDEV - 1)), pltpu.SemaphoreType.DMA((self.k_n, NDEV - 1)),
                        pltpu.SemaphoreType.DMA((self.k_n,))]

    def copies(self, src, out, send_sems, recv_sems, local_sems):
        x, y, c = lax.axis_index("x"), lax.axis_index("y"), lax.axis_index("c")
        me = 4 * x + 2 * y + c

        def piece(k, p):
            return src[k].at[p] if self.per_peer[k] else src[k]

        local = [pltpu.make_async_copy(piece(k, me), out[k].at[me], local_sems.at[k]) for k in range(self.k_n)]
        sends, recvs = [], []
        for r in range(1, NDEV):
            px = 1 - x if (r >> 2) & 1 else x
            py = 1 - y if (r >> 1) & 1 else y
            pc = 1 - c if r & 1 else c
            pidx = 4 * px + 2 * py + pc
            for k in range(self.k_n):
                sends.append(pltpu.make_async_remote_copy(
                    src_ref=piece(k, pidx), dst_ref=out[k].at[me],
                    send_sem=send_sems.at[k, r - 1], recv_sem=recv_sems.at[k, r - 1],
                    device_id=(px, py, pc), device_id_type=pl.DeviceIdType.MESH))
                recvs.append(pltpu.make_async_remote_copy(
                    src_ref=piece(k, pidx), dst_ref=out[k].at[pidx],
                    send_sem=send_sems.at[k, r - 1], recv_sem=recv_sems.at[k, r - 1],
                    device_id=(px, py, pc), device_id_type=pl.DeviceIdType.MESH))

        def start():
            for cp in local + sends:
                cp.start()

        def wait():
            for cp in recvs:
                cp.wait_recv()
            for cp in sends:
                cp.wait_send()
            for cp in local:
                cp.wait()

        return start, wait


def exchange(srcs, per_peer, name):
    plan = AllToAll(srcs, per_peer)

    def body(*refs):
        start, wait = plan.copies(refs[:plan.k_n], refs[plan.k_n:2 * plan.k_n], *refs[2 * plan.k_n:])
        start()
        wait()

    return pl.pallas_call(
        body, name=name, out_shape=plan.out_shape, in_specs=[ANY] * plan.k_n, out_specs=[ANY] * plan.k_n,
        scratch_shapes=plan.scratch,
    )(*srcs)


def gather(srcs, name):
    k_n = len(srcs)
    out_shape = [S((NDEV,) + a.shape, a.dtype) for a in srcs]

    def body(*refs):
        src, out = refs[:k_n], refs[k_n:2 * k_n]
        send_sems, recv_sems, local_sems = refs[2 * k_n:]
        x, y, c = lax.axis_index("x"), lax.axis_index("y"), lax.axis_index("c")
        me, sibling = (x, y, c), (x, y, 1 - c)
        chips = [(1 - x, y), (x, 1 - y), (1 - x, 1 - y)]

        def slot(k, dev):
            return out[k].at[4 * dev[0] + 2 * dev[1] + dev[2]]

        def copy(k, r, block, to, from_src=False):
            return pltpu.make_async_remote_copy(
                src_ref=src[k] if from_src else slot(k, block), dst_ref=slot(k, block),
                send_sem=send_sems.at[k, r], recv_sem=recv_sems.at[k, r],
                device_id=to, device_id_type=pl.DeviceIdType.MESH)

        local = [pltpu.make_async_copy(src[k], slot(k, me), local_sems.at[k]) for k in range(k_n)]
        first = [copy(k, 0, me, sibling, True) for k in range(k_n)]
        first += [copy(k, 1 + n, me, (*chip, c), True) for n, chip in enumerate(chips) for k in range(k_n)]
        for cp in local + first:
            cp.start()
        passed = []
        for n, chip in enumerate(chips):
            for k in range(k_n):
                copy(k, 1 + n, (*chip, c), me).wait_recv()
                passed.append(copy(k, 4 + n, (*chip, c), sibling))
                passed[-1].start()
        for k in range(k_n):
            copy(k, 0, sibling, me).wait_recv()
            for n, chip in enumerate(chips):
                copy(k, 4 + n, (*chip, 1 - c), me).wait_recv()
        for cp in first + passed:
            cp.wait_send()
        for cp in local:
            cp.wait()

    any_spec = pl.BlockSpec(memory_space=pl.ANY)
    return pl.pallas_call(
        body, name=name, out_shape=out_shape,
        in_specs=[any_spec] * k_n, out_specs=[any_spec] * k_n,
        scratch_shapes=[pltpu.SemaphoreType.DMA((k_n, NDEV - 1)), pltpu.SemaphoreType.DMA((k_n, NDEV - 1)),
                        pltpu.SemaphoreType.DMA((k_n,))],
    )(*srcs)


def in_proj(h, g, w, bf, name):
    tp = h.shape[0]

    def body(h_ref, g_ref, w_ref, bf_ref, hn_ref, qa_ref, ka_ref, va_ref, pc_ref, sg_ref, carry):
        i = pl.program_id(0)

        @pl.when(i == 0)
        def _():
            carry[...] = jnp.zeros_like(carry)

        x = h_ref[...]
        r = lax.rsqrt(jnp.mean(x * x, axis=-1, keepdims=True) + EPS)
        hn = (x * r * g_ref[...]).astype(MM)
        hn_ref[...] = hn
        pc_ref[...] = jnp.dot(hn, w_ref[:, C0:F0], preferred_element_type=F32)
        z = jnp.dot(hn, w_ref[:, F0:NP_IN], preferred_element_type=F32) + bf_ref[...]
        lane = lax.broadcasted_iota(jnp.int32, z.shape, 1)
        logf = jnp.where(lane < H, jnp.minimum(z, 0.0) - jnp.log(1.0 + jnp.exp(-jnp.abs(z))), 0.0)
        sg_ref[...] = 1.0 / (1.0 + jnp.exp(z))
        row = lax.broadcasted_iota(jnp.int32, (TM, TM), 0)
        col = lax.broadcasted_iota(jnp.int32, (TM, TM), 1)
        c = _tri_matmul(row >= col, logf) + carry[...]
        carry[...] = c[TM - 1:TM, :]
        qkv = jnp.dot(hn, w_ref[:, 0:C0], preferred_element_type=F32)
        one = [1.0, 1.0, 1.0]
        for hd in range(H):
            t = slice(hd * HT, (hd + 1) * HT)
            cs = list(_split3(c[:, hd:hd + 1]))
            qa_ref[:, t] = _aug_tile(hd, qkv[:, hd * DH:(hd + 1) * DH] * SCALE, cs + one).astype(MM)
            ka_ref[:, t] = _aug_tile(hd, qkv[:, DA + hd * DH:DA + (hd + 1) * DH], one + [-v for v in cs] + one).astype(MM)
            va_ref[:, t] = _aug_tile(hd, qkv[:, 2 * DA + hd * DH:2 * DA + (hd + 1) * DH], [1.0] + one).astype(MM)

    rows = lambda n: pl.BlockSpec((TM, n), lambda i: (i, 0))
    return pl.pallas_call(
        body, name=name, grid=(tp // TM,),
        in_specs=[rows(D), _full((1, D)), _full((D, NP_IN)), _full((1, 128))],
        out_specs=[rows(D), rows(H * HT), rows(H * HT), rows(H * HT), rows(5 * DC), rows(128)],
        out_shape=[S((tp, D), MM), S((tp, H * HT), MM), S((tp, H * HT), MM), S((tp, H * HT), MM),
                   S((tp, 5 * DC), F32), S((tp, 128), F32)],
        scratch_shapes=[pltpu.VMEM((1, 128), F32)],
        compiler_params=_params(1),
    )(h, g, w, bf)


def _split3(c):
    hi = c.astype(MM).astype(F32)
    mid = (c - hi).astype(MM).astype(F32)
    lo = (c - hi - mid).astype(MM).astype(F32)
    return hi, mid, lo


def _tri_matmul(keep, x):
    pieces = jnp.concatenate(_split3(x), axis=1).astype(MM)
    r = jnp.dot(jnp.where(keep, 1.0, 0.0).astype(MM), pieces, preferred_element_type=F32)
    return r[:, 0:128] + r[:, 128:256] + r[:, 256:384]


def _main(hd):
    return slice(0, DH) if hd % 2 == 0 else slice(DH, HT)


def _aug_lane(hd, pos):
    return pos + (DH if hd % 2 == 0 else 0)


def _aug_tile(hd, main, cols):
    lane = lax.broadcasted_iota(jnp.int32, main.shape, 1)
    aug = jnp.zeros(main.shape, F32)
    for pos, val in enumerate(cols):
        aug = jnp.where(lane == pos, val, aug)
    return jnp.concatenate([main, aug] if hd % 2 == 0 else [aug, main], axis=1)


def _merge_pairs(tiles):
    lane = lax.broadcasted_iota(jnp.int32, tiles[0].shape, 1)
    return jnp.concatenate([jnp.where(lane < DH, tiles[2 * m], tiles[2 * m + 1]) for m in range(H // 2)], axis=1)


def _causal_mask():
    return lax.broadcasted_iota(jnp.int32, (TM, TM), 0) >= lax.broadcasted_iota(jnp.int32, (TM, TM), 1)


def _split_rider(refs, n_in, n_out, rider):
    if rider is None:
        return refs, None
    k = rider.k_n
    own = refs[:n_in] + refs[n_in + k:n_in + k + n_out] + refs[n_in + 2 * k + n_out:-3]
    return own, rider.copies(refs[n_in:n_in + k], refs[n_in + k + n_out:n_in + 2 * k + n_out], *refs[-3:])


def attn_fwd(qa, ka, va, name, rider=None):
    tp = qa.shape[0]
    nb = tp // TM
    pairs = [(i, j) for i in range(nb) for j in range(i + 1)]
    i_tab, j_tab = (jnp.asarray([p[a] for p in pairs], jnp.int32) for a in (0, 1))

    def body(i_ref, j_ref, *refs):
        (q_ref, k_ref, v_ref, o_ref, lse_ref, *scr), copies = _split_rider(refs, 3, 2, rider)
        m_scr, acc_scr, bias = scr[:H], scr[H:2 * H], scr[2 * H]
        n = pl.program_id(0)
        i, j = i_ref[n], j_ref[n]

        @pl.when(n == 0)
        def _():
            if copies:
                copies[0]()
            bias[...] = jnp.where(_causal_mask(), 0.0, NEG)

        @pl.when(j == 0)
        def _():
            for hd in range(H):
                m_scr[hd][...] = jnp.full((TM, 1), NEG, F32)
                acc_scr[hd][...] = jnp.zeros((TM, HT), F32)

        def step(diag):
            def logits(hd):
                t = slice(hd * HT, (hd + 1) * HT)
                s = lax.dot_general(q_ref[:, t], k_ref[:, t], NT, preferred_element_type=F32)
                return s + bias[...] if diag else s

            s_next = logits(0)
            for hd in range(H):
                t = slice(hd * HT, (hd + 1) * HT)
                s = s_next
                if hd + 1 < H:
                    s_next = logits(hd + 1)
                m_prev = m_scr[hd][...]
                m_new = jnp.maximum(m_prev, jnp.max(s, axis=1, keepdims=True))
                p = jnp.exp(s - m_new).astype(MM)
                acc_scr[hd][...] = (jnp.exp(m_prev - m_new) * acc_scr[hd][...]
                                    + jnp.dot(p, v_ref[:, t], preferred_element_type=F32))
                m_scr[hd][...] = m_new

        @pl.when(j < i)
        def _():
            step(False)

        @pl.when(j == i)
        def _():
            step(True)

        @pl.when(j == i)
        def _():
            lse = jnp.zeros((TM, 128), F32)
            outs = []
            for hd in range(H):
                acc = acc_scr[hd][...]
                l = acc[:, _aug_lane(hd, 0):_aug_lane(hd, 0) + 1]
                outs.append(acc * (1.0 / l))
                lse = _lane_put(lse, m_scr[hd][...] + jnp.log(l), hd)
            o_ref[...] = _merge_pairs(outs).astype(MM)
            lse_ref[...] = lse

        if copies:
            @pl.when(n == len(pairs) - 1)
            def _():
                copies[1]()

    qi = lambda w: pl.BlockSpec((TM, w), lambda n, it, jt: (it[n], 0))
    kv = pl.BlockSpec((TM, H * HT), lambda n, it, jt: (jt[n], 0))
    r_n = rider.k_n if rider else 0
    return pl.pallas_call(
        body, name=name,
        grid_spec=pltpu.PrefetchScalarGridSpec(
            num_scalar_prefetch=2, grid=(len(pairs),),
            in_specs=[qi(H * HT), kv, kv] + [ANY] * r_n, out_specs=[qi(DA), qi(128)] + [ANY] * r_n,
            scratch_shapes=[pltpu.VMEM((TM, 1), F32)] * H + [pltpu.VMEM((TM, HT), F32)] * H + [pltpu.VMEM((TM, TM), F32)]
            + (rider.scratch if rider else [])),
        out_shape=[S((tp, DA), MM), S((tp, 128), F32)] + (rider.out_shape if rider else []),
        compiler_params=_params(1),
    )(i_tab, j_tab, qa, ka, va, *(rider.srcs if rider else []))


def _layernorm_parts(dw, g, b):
    mu = jnp.mean(dw, axis=-1, keepdims=True)
    xc = dw - mu
    rs = lax.rsqrt(jnp.mean(xc * xc, axis=-1, keepdims=True) + EPS)
    xh = xc * rs
    return xh, rs, xh * g + b


def _fill_halo(ext, halo, cur, first):
    ext[0:HALO, :] = jnp.where(first, 0.0, halo)
    ext[HALO:HALO + TM, :] = cur


def _shift_copies(dst, src, lanes=slice(None)):
    for r in range(1, 8):
        dst[r - 1] = src[pl.ds(r, dst.shape[1]), lanes]


RC = 64


def _window(src, shifted, off, lanes=slice(None), r0=0, rows=None):
    rows = TM if rows is None else rows
    if off % 8 == 0:
        return src[pl.ds(off + r0, rows), lanes]
    return shifted[off % 8 - 1, pl.ds(off - off % 8 + r0, rows), :]


def conv_fwd(pc, wdw, bdw, lng, lnb, wpw, bpw, wsc, name):
    tp = pc.shape[0]

    def body(pc_ref, hl_ref, wdw_ref, bdw_ref, lng_ref, lnb_ref, wpw_ref, bpw_ref, wsc_ref, cc_ref, sv_ref, xe, ce, xs):
        first = pl.program_id(0) == 0
        glu = pc_ref[:, 0:DC] * _sigmoid(pc_ref[:, DC:2 * DC])
        cu = pc_ref[:, 3 * DC:4 * DC] * pc_ref[:, 4 * DC:5 * DC]
        _fill_halo(xe, hl_ref[:, 0:DC] * _sigmoid(hl_ref[:, DC:2 * DC]), glu, first)
        _fill_halo(ce, hl_ref[:, 3 * DC:4 * DC] * hl_ref[:, 4 * DC:5 * DC], cu, first)
        _shift_copies(xs, xe)
        for r0 in range(0, TM, RC):
            acc = jnp.zeros((RC, DC), F32) + bdw_ref[...]
            for k in range(CK):
                acc = acc + wdw_ref[k:k + 1, :] * _window(xe, xs, HALO - (CK - 1) + k, r0=r0, rows=RC)
            sv_ref[r0:r0 + RC, 0:DC] = acc
        dw = sv_ref[:, 0:DC]
        cv = jnp.zeros((TM, DC), F32)
        for k in range(SK):
            cv = cv + wsc_ref[k:k + 1, :] * ce[pl.ds(HALO - (SK - 1) + k, TM), :]
        _, _, ln = _layernorm_parts(dw, lng_ref[...], lnb_ref[...])
        act = ln * _sigmoid(ln)
        conf = jnp.dot(act.astype(MM), wpw_ref[...], preferred_element_type=F32) + bpw_ref[...]
        cc_ref[:, 0:DC] = conf.astype(MM)
        cc_ref[:, DC:2 * DC] = (pc_ref[:, 2 * DC:3 * DC] * cv).astype(MM)
        sv_ref[:, DC:2 * DC] = cv

    hb = TM // HALO
    rows = lambda n: pl.BlockSpec((TM, n), lambda i: (i, 0))
    return pl.pallas_call(
        body, name=name, grid=(tp // TM,),
        in_specs=[rows(5 * DC), pl.BlockSpec((HALO, 5 * DC), lambda i: (jnp.maximum(i * hb - 1, 0), 0)),
                  _full((32, DC)), _full((1, DC)), _full((1, DC)), _full((1, DC)), _full((DC, DC)), _full((1, DC)),
                  _full((8, DC))],
        out_specs=[rows(2 * DC), rows(2 * DC)],
        out_shape=[S((tp, 2 * DC), MM), S((tp, 2 * DC), F32)],
        scratch_shapes=[pltpu.VMEM((HALO + TM, DC), F32), pltpu.VMEM((HALO + TM, DC), F32),
                        pltpu.VMEM((7, TM + HALO - 8, DC), F32)],
        compiler_params=_params(1),
    )(pc, pc, wdw, bdw, lng, lnb, wpw, bpw, wsc)


def out_proj(h, o, cc, wo, g2, name):
    tp = h.shape[0]

    def body(h_ref, o_ref, cc_ref, wo_ref, g_ref, h2_ref, hn_ref):
        wo = wo_ref[...].reshape(D, D)
        h2 = (h_ref[...] + jnp.dot(o_ref[...], wo[0:DA, :], preferred_element_type=F32)
              + jnp.dot(cc_ref[...], wo[DA:D, :], preferred_element_type=F32))
        h2_ref[...] = h2
        r = lax.rsqrt(jnp.mean(h2 * h2, axis=-1, keepdims=True) + EPS)
        hn_ref[...] = (h2 * r * g_ref[...]).astype(MM)

    rows = lambda n: pl.BlockSpec((TM, n), lambda i: (i, 0))
    return pl.pallas_call(
        body, name=name, grid=(tp // TM,),
        in_specs=[rows(D), rows(DA), rows(2 * DC), _full((NDEV, D // NDEV, D)), _full((1, D))],
        out_specs=[rows(D), rows(D)],
        out_shape=[S((tp, D), F32), S((tp, D), MM)],
        compiler_params=_params(1),
    )(h, o, cc, wo, g2)


def mlp_fwd(hn, h2, w1, w2, name):
    tp = hn.shape[0]
    nf = DFF // (FS * FC)

    def body(hn_ref, h2_ref, w1_ref, w2_ref, r_ref, z_ref, h3_ref, acc):
        j = pl.program_id(1)

        @pl.when(j == 0)
        def _():
            acc[...] = jnp.zeros_like(acc)

        for s in range(FS):
            cols = slice(s * FC, (s + 1) * FC)
            r = jnp.maximum(jnp.dot(hn_ref[...], w1_ref[s], preferred_element_type=F32), 0.0)
            zb = (r * r).astype(MM)
            r_ref[:, cols] = r.astype(MM)
            z_ref[:, cols] = zb
            acc[...] += jnp.dot(zb, w2_ref[s], preferred_element_type=F32)

        @pl.when(j == nf - 1)
        def _():
            h3_ref[...] = h2_ref[...] + acc[...]

    return pl.pallas_call(
        body, name=name, grid=(tp // TM, nf),
        in_specs=[pl.BlockSpec((TM, D), lambda i, j: (i, 0)), pl.BlockSpec((TM, D), lambda i, j: (i, 0)),
                  pl.BlockSpec((FS, D, FC), lambda i, j: (j, 0, 0)), pl.BlockSpec((FS, FC, D), lambda i, j: (j, 0, 0))],
        out_specs=[pl.BlockSpec((TM, FS * FC), lambda i, j: (i, j)), pl.BlockSpec((TM, FS * FC), lambda i, j: (i, j)),
                   pl.BlockSpec((TM, D), lambda i, j: (i, 0))],
        out_shape=[S((tp, DFF), MM), S((tp, DFF), MM), S((tp, D), F32)],
        scratch_shapes=[pltpu.VMEM((TM, D), F32)],
        compiler_params=_params(2),
    )(hn, h2, w1, w2)


def loss_head(h, tgt, g, t_real, name):
    tp = h.shape[0]

    def body(h_ref, t_ref, g_ref, dh_ref, loss_ref, dg_ref):
        i = pl.program_id(0)

        @pl.when(i == 0)
        def _():
            loss_ref[...] = jnp.zeros_like(loss_ref)
            dg_ref[...] = jnp.zeros_like(dg_ref)

        x = h_ref[...]
        gg = g_ref[...]
        r = lax.rsqrt(jnp.mean(x * x, axis=-1, keepdims=True) + EPS)
        xn = x * r
        row = i * TM + lax.broadcasted_iota(jnp.int32, (TM, 1), 0)
        e = jnp.where((row >= NM) & (row < t_real), xn * gg - t_ref[...], 0.0)
        loss_ref[...] += jnp.sum(e * e) * (0.5 / D)
        dy = e * (1.0 / D)
        dg_ref[...] += jnp.sum(dy * xn, axis=0, keepdims=True)
        u = dy * gg
        dh_ref[...] = r * (u - xn * jnp.mean(u * xn, axis=-1, keepdims=True))

    rows = lambda n: pl.BlockSpec((TM, n), lambda i: (i, 0))
    return pl.pallas_call(
        body, name=name, grid=(tp // TM,),
        in_specs=[rows(D), rows(D), _full((1, D))],
        out_specs=[rows(D), _full((8, 128)), _full((1, D))],
        out_shape=[S((tp, D), F32), S((8, 128), F32), S((1, D), F32)],
        compiler_params=_params(1),
    )(h, tgt, g)


def _rmsnorm_bwd(x, g, dy):
    r = lax.rsqrt(jnp.mean(x * x, axis=-1, keepdims=True) + EPS)
    xn = x * r
    u = dy * g
    dx = r * (u - xn * jnp.mean(u * xn, axis=-1, keepdims=True))
    return dx, jnp.sum(dy * xn, axis=0, keepdims=True)


def matmul_tn(a, b, tm, tn, name, n=None, out_dtype=F32, shard=None, sw=None):
    tp, m = a.shape
    n = b.shape[1] if n is None else n
    nk = tp // TM

    def body(a_ref, b_ref, o_ref, acc):
        k = pl.program_id(2)

        @pl.when(k == 0)
        def _():
            acc[...] = jnp.zeros_like(acc)

        acc[...] += lax.dot_general(a_ref[...].astype(MM), b_ref[...].astype(MM), TN, preferred_element_type=F32)

        @pl.when(k == nk - 1)
        def _():
            if shard is None:
                o_ref[...] = acc[...].astype(out_dtype)
            elif shard == "m":
                for s in range(tm // sw):
                    o_ref[s] = acc[s * sw:(s + 1) * sw, :].astype(out_dtype)
            else:
                for s in range(tn // sw):
                    o_ref[s] = acc[:, s * sw:(s + 1) * sw].astype(out_dtype)

    if shard is None:
        out_spec, out_shape = pl.BlockSpec((tm, tn), lambda mi, ni, k: (mi, ni)), (m, n)
    elif shard == "m":
        out_spec, out_shape = pl.BlockSpec((tm // sw, sw, tn), lambda mi, ni, k: (mi, 0, ni)), (m // sw, sw, n)
    else:
        out_spec, out_shape = pl.BlockSpec((tn // sw, tm, sw), lambda mi, ni, k: (ni, mi, 0)), (n // sw, m, sw)
    return pl.pallas_call(
        body, name=name, grid=(m // tm, n // tn, nk),
        in_specs=[pl.BlockSpec((TM, tm), lambda mi, ni, k: (k, mi)), pl.BlockSpec((TM, tn), lambda mi, ni, k: (k, ni))],
        out_specs=out_spec, out_shape=S(out_shape, out_dtype),
        scratch_shapes=[pltpu.VMEM((tm, tn), F32)],
        compiler_params=_params(3),
    )(a, b)


def mlp_bwd(dh3, r, w1, w2, h2, g, name):
    tp = dh3.shape[0]
    nf = DFF // (FS * FC)

    def body(dh3_ref, r_ref, w1_ref, w2_ref, h2_ref, g_ref, da_ref, dh2_ref, dg_ref, acc, dhb):
        i, j = pl.program_id(0), pl.program_id(1)

        @pl.when((i == 0) & (j == 0))
        def _():
            dg_ref[...] = jnp.zeros_like(dg_ref)

        @pl.when(j == 0)
        def _():
            acc[...] = jnp.zeros_like(acc)
            dhb[...] = dh3_ref[...].astype(MM)

        for s in range(FS):
            cols = slice(s * FC, (s + 1) * FC)
            dz = lax.dot_general(dhb[...], w2_ref[s], NT, preferred_element_type=F32)
            da = (dz * (2.0 * r_ref[:, cols].astype(F32))).astype(MM)
            da_ref[:, cols] = da
            acc[...] += lax.dot_general(da, w1_ref[s], NT, preferred_element_type=F32)

        @pl.when(j == nf - 1)
        def _():
            dx, dg = _rmsnorm_bwd(h2_ref[...], g_ref[...], acc[...])
            dh2_ref[...] = dh3_ref[...] + dx
            dg_ref[...] += dg

    return pl.pallas_call(
        body, name=name, grid=(tp // TM, nf),
        in_specs=[pl.BlockSpec((TM, D), lambda i, j: (i, 0)), pl.BlockSpec((TM, FS * FC), lambda i, j: (i, j)),
                  pl.BlockSpec((FS, D, FC), lambda i, j: (j, 0, 0)), pl.BlockSpec((FS, FC, D), lambda i, j: (j, 0, 0)),
                  pl.BlockSpec((TM, D), lambda i, j: (i, 0)), _full((1, D))],
        out_specs=[pl.BlockSpec((TM, FS * FC), lambda i, j: (i, j)), pl.BlockSpec((TM, D), lambda i, j: (i, 0)),
                   _full((1, D))],
        out_shape=[S((tp, DFF), MM), S((tp, D), F32), S((1, D), F32)],
        scratch_shapes=[pltpu.VMEM((TM, D), F32), pltpu.VMEM((TM, D), MM)],
        compiler_params=_params(2),
    )(dh3, r, w1, w2, h2, g)


def out_proj_bwd(dh2, wo, name):
    tp = dh2.shape[0]

    def body(dh_ref, wo_ref, do_ref, dcc_ref):
        dcat = lax.dot_general(dh_ref[...].astype(MM), wo_ref[...].reshape(D, D), NT, preferred_element_type=F32)
        do_ref[...] = dcat[:, 0:DA].astype(MM)
        dcc_ref[...] = dcat[:, DA:D]

    rows = lambda n: pl.BlockSpec((TM, n), lambda i: (i, 0))
    return pl.pallas_call(
        body, name=name, grid=(tp // TM,),
        in_specs=[rows(D), _full((NDEV, D // NDEV, D))],
        out_specs=[rows(DA), rows(2 * DC)],
        out_shape=[S((tp, DA), MM), S((tp, 2 * DC), F32)],
        compiler_params=_params(1),
    )(dh2, wo)


def attn_bwd_prep(qa, do, o, lse, name):
    tp = qa.shape[0]

    def body(q_ref, do_ref, o_ref, lse_ref, qb_ref, dob_ref, qbt_ref, dobt_ref):
        lse = lse_ref[...]
        lane = lax.broadcasted_iota(jnp.int32, (TM, HT), 1)
        for hd in range(H):
            t, hs = slice(hd * HT, (hd + 1) * HT), slice(hd * DH, (hd + 1) * DH)
            dof = do_ref[:, hs].astype(F32)
            dd = jnp.sum(dof * o_ref[:, hs].astype(F32), axis=1, keepdims=True)
            dob = _aug_tile(hd, dof, [0.0] + [-v for v in _split3(dd)])
            qb = q_ref[:, t].astype(F32)
            for pos, v in enumerate(_split3(lse[:, hd:hd + 1])):
                qb = jnp.where(lane == _aug_lane(hd, 6 + pos), -v, qb)
            qb_ref[:, t] = qb.astype(MM)
            dob_ref[:, t] = dob.astype(MM)
            qbt_ref[t, :] = qb.T.astype(MM)
            dobt_ref[t, :] = dob.T.astype(MM)

    rows = lambda n: pl.BlockSpec((TM, n), lambda i: (i, 0))
    cols = pl.BlockSpec((H * HT, TM), lambda i: (0, i))
    return pl.pallas_call(
        body, name=name, grid=(tp // TM,),
        in_specs=[rows(H * HT), rows(DA), rows(DA), rows(128)],
        out_specs=[rows(H * HT), rows(H * HT), cols, cols],
        out_shape=[S((tp, H * HT), MM), S((tp, H * HT), MM), S((H * HT, tp), MM), S((H * HT, tp), MM)],
        compiler_params=_params(1),
    )(qa, do, o, lse)


def attn_bwd(qb, dob, qbt, dobt, ka, va, name, rider=None):
    tp = qb.shape[0]
    nb = tp // TM

    pairs = [(j, i) for j in range(nb) for i in range(j, nb)]
    j_tab, i_tab = (jnp.asarray([p[a] for p in pairs], jnp.int32) for a in (0, 1))

    def body(j_ref, i_ref, *refs):
        own, copies = _split_rider(refs, 6, 5, rider)
        (qb_ref, dob_ref, qbt_ref, dobt_ref, k_ref, v_ref, dq_ref, dk_ref, dv_ref, dcq_ref, dck_ref,
         dq_acc, dcq_acc, dkt_acc, dvt_acc, dck_acc, stage, sems) = own
        n = pl.program_id(0)
        j, i = j_ref[n], i_ref[n]

        @pl.when(n == 0)
        def _():
            if copies:
                copies[0]()
            dq_acc[...] = jnp.zeros_like(dq_acc)
            dcq_acc[...] = jnp.zeros_like(dcq_acc)

        @pl.when(i == j)
        def _():
            dkt_acc[...] = jnp.zeros_like(dkt_acc)
            dvt_acc[...] = jnp.zeros_like(dvt_acc)
            dck_acc[...] = jnp.zeros_like(dck_acc)

        def step(diag):
            rows = pl.ds(pl.multiple_of(i * TM, TM), TM)
            dcq = dcq_acc[rows, :]
            lane = lax.broadcasted_iota(jnp.int32, (TM, HT), 1)
            mask = _causal_mask() if diag else None
            for hd in range(H):
                t = slice(hd * HT, (hd + 1) * HT)
                s = lax.dot_general(qb_ref[:, t], k_ref[:, t], NT, preferred_element_type=F32)
                dp = lax.dot_general(dob_ref[:, t], v_ref[:, t], NT, preferred_element_type=F32)
                if diag:
                    s = jnp.where(mask, s, NEG)
                p = jnp.exp(s)
                ds = p * dp
                dsb = ds.astype(MM)
                dvt_acc[t, :] += jnp.dot(dobt_ref[t, :], p.astype(MM), preferred_element_type=F32)
                dkt_acc[t, :] += jnp.dot(qbt_ref[t, :], dsb, preferred_element_type=F32)
                dqh = jnp.dot(dsb, k_ref[:, t], preferred_element_type=F32)
                if hd % 2 == 0:
                    dq_even = dqh
                else:
                    pair = slice((hd // 2) * HT, (hd // 2 + 1) * HT)
                    dq_acc[rows, pair] += jnp.where(lane < DH, dq_even, dqh)
                dcq = dcq + jnp.where(lane == hd, jnp.sum(ds, axis=1, keepdims=True), 0.0)
                dck_acc[hd:hd + 1, :] += jnp.sum(ds, axis=0, keepdims=True)
            dcq_acc[rows, :] = dcq
            return rows

        @pl.when(i > j)
        def _():
            step(False)

        @pl.when(i == j)
        def _():
            rows = step(True)
            stage[...] = (dq_acc[rows, :] * SCALE).astype(MM)
            out = [pltpu.make_async_copy(stage, dq_ref.at[rows, :], sems.at[0]),
                   pltpu.make_async_copy(dcq_acc.at[rows, :], dcq_ref.at[rows, :], sems.at[1])]
            for cp in out:
                cp.start()
            for cp in out:
                cp.wait()

        @pl.when(i == nb - 1)
        def _():
            dk_ref[...] = _merge_pairs([dkt_acc[hd * HT:(hd + 1) * HT, :].T for hd in range(H)]).astype(MM)
            dv_ref[...] = _merge_pairs([dvt_acc[hd * HT:(hd + 1) * HT, :].T for hd in range(H)]).astype(MM)
            dck_ref[...] = dck_acc[...]

        if copies:
            @pl.when(n == len(pairs) - 1)
            def _():
                copies[1]()

    qi = lambda w: pl.BlockSpec((TM, w), lambda n, jt, it: (it[n], 0))
    qt = pl.BlockSpec((H * HT, TM), lambda n, jt, it: (0, it[n]))
    kj = lambda w: pl.BlockSpec((TM, w), lambda n, jt, it: (jt[n], 0))
    r_n = rider.k_n if rider else 0
    return pl.pallas_call(
        body, name=name,
        grid_spec=pltpu.PrefetchScalarGridSpec(
            num_scalar_prefetch=2, grid=(len(pairs),),
            in_specs=[qi(H * HT), qi(H * HT), qt, qt, kj(H * HT), kj(H * HT)] + [ANY] * r_n,
            out_specs=[ANY, kj(DA), kj(DA), ANY, pl.BlockSpec((H, TM), lambda n, jt, it: (0, jt[n]))] + [ANY] * r_n,
            scratch_shapes=[pltpu.VMEM((tp, DA), F32), pltpu.VMEM((tp, 128), F32), pltpu.VMEM((H * HT, TM), F32),
                            pltpu.VMEM((H * HT, TM), F32), pltpu.VMEM((H, TM), F32), pltpu.VMEM((TM, DA), MM),
                            pltpu.SemaphoreType.DMA((2,))] + (rider.scratch if rider else [])),
        out_shape=[S((tp, DA), MM), S((tp, DA), MM), S((tp, DA), MM), S((tp, 128), F32), S((H, tp), F32)]
        + (rider.out_shape if rider else []),
        compiler_params=_params(1),
    )(j_tab, i_tab, qb, dob, qbt, dobt, ka, va, *(rider.srcs if rider else []))


def fgate_bwd(dcq, dck, sg, name):
    tp = dcq.shape[0]
    nb = tp // TM

    def body(dcq_ref, dck_ref, sg_ref, dz_ref, db_ref, carry):
        @pl.when(pl.program_id(0) == 0)
        def _():
            carry[...] = jnp.zeros_like(carry)
            db_ref[...] = jnp.zeros_like(db_ref)

        row = lax.broadcasted_iota(jnp.int32, (TM, TM), 0)
        col = lax.broadcasted_iota(jnp.int32, (TM, TM), 1)
        dl = _tri_matmul(row <= col, dcq_ref[...] - dck_ref[...]) + carry[...]
        carry[...] = dl[0:1, :]
        dz = dl * sg_ref[...]
        dz_ref[...] = dz.astype(MM)
        db_ref[...] += jnp.sum(dz, axis=0, keepdims=True)

    rev = pl.BlockSpec((TM, 128), lambda i: (nb - 1 - i, 0))
    return pl.pallas_call(
        body, name=name, grid=(nb,),
        in_specs=[rev, rev, rev], out_specs=[rev, _full((1, 128))],
        out_shape=[S((tp, 128), MM), S((1, 128), F32)],
        scratch_shapes=[pltpu.VMEM((1, 128), F32)],
        compiler_params=_params(1),
    )(dcq, dck, sg)


def conv_bwd_pointwise(dcc, sv, pc, lng, lnb, wpw, name):
    tp = dcc.shape[0]

    def body(dcc_ref, sv_ref, b_ref, lng_ref, lnb_ref, wpw_ref, gc_ref, act_ref, pg_ref):
        @pl.when(pl.program_id(0) == 0)
        def _():
            pg_ref[...] = jnp.zeros_like(pg_ref)

        dconf = dcc_ref[:, 0:DC]
        g = lng_ref[...]
        xh, rs, ln = _layernorm_parts(sv_ref[:, 0:DC], g, lnb_ref[...])
        sig = _sigmoid(ln)
        act_ref[...] = (ln * sig).astype(MM)
        dact = lax.dot_general(dconf.astype(MM), wpw_ref[...], NT, preferred_element_type=F32)
        dln = dact * (sig * (1.0 + ln * (1.0 - sig)))
        dxh = dln * g
        ddw = rs * (dxh - jnp.mean(dxh, axis=-1, keepdims=True) - xh * jnp.mean(dxh * xh, axis=-1, keepdims=True))
        gc_ref[:, 0:DC] = ddw
        gc_ref[:, DC:2 * DC] = dcc_ref[:, DC:2 * DC] * b_ref[...]
        cs = lambda t: jnp.sum(t, axis=0, keepdims=True)
        pg_ref[0:1, :] += cs(dconf)
        pg_ref[1:2, :] += cs(dln * xh)
        pg_ref[2:3, :] += cs(dln)
        pg_ref[3:4, :] += cs(ddw)

    rows = lambda n: pl.BlockSpec((TM, n), lambda i: (i, 0))
    return pl.pallas_call(
        body, name=name, grid=(tp // TM,),
        in_specs=[rows(2 * DC), rows(2 * DC), pl.BlockSpec((TM, DC), lambda i: (i, 2)),
                  _full((1, DC)), _full((1, DC)), _full((DC, DC))],
        out_specs=[rows(2 * DC), rows(DC), _full((8, DC))],
        out_shape=[S((tp, 2 * DC), F32), S((tp, DC), MM), S((8, DC), F32)],
        compiler_params=_params(1),
    )(dcc, sv, pc, lng, lnb, wpw)


def conv_bwd_taps(gc, pc, dcc, sv, wdw, wsc, name):
    tp = gc.shape[0]
    nb = tp // TM
    hb = TM // HALO

    def body(gc_ref, gn_ref, pc_ref, hl_ref, dcc_ref, sv_ref, wdw_ref, wsc_ref, dpc_ref, wg_ref, ge, xe, ce, gs, xs):
        i = pl.program_id(0)

        @pl.when(i == 0)
        def _():
            wg_ref[...] = jnp.zeros_like(wg_ref)

        a, gt = pc_ref[:, 0:DC], pc_ref[:, DC:2 * DC]
        c, u = pc_ref[:, 3 * DC:4 * DC], pc_ref[:, 4 * DC:5 * DC]
        sig = _sigmoid(gt)
        _fill_halo(xe, hl_ref[:, 0:DC] * _sigmoid(hl_ref[:, DC:2 * DC]), a * sig, i == 0)
        _fill_halo(ce, hl_ref[:, 3 * DC:4 * DC] * hl_ref[:, 4 * DC:5 * DC], c * u, i == 0)
        ge[0:TM, :] = gc_ref[...]
        ge[TM:TM + HALO, :] = jnp.where(i == nb - 1, 0.0, gn_ref[...])
        _shift_copies(gs, ge, slice(0, DC))
        _shift_copies(xs, xe)
        dcv = gc_ref[:, DC:2 * DC]
        for r0 in range(0, TM, RC):
            rs = slice(r0, r0 + RC)
            ddw = gc_ref[rs, 0:DC]
            dglu = jnp.zeros((RC, DC), F32)
            for k in range(CK):
                dglu = dglu + wdw_ref[k:k + 1, :] * _window(ge, gs, CK - 1 - k, slice(0, DC), r0, RC)
                wg_ref[k:k + 1, :] += jnp.sum(ddw * _window(xe, xs, HALO - (CK - 1) + k, r0=r0, rows=RC), axis=0,
                                              keepdims=True)
            sig_c = _sigmoid(pc_ref[rs, DC:2 * DC])
            dpc_ref[rs, 0:DC] = (dglu * sig_c).astype(MM)
            dpc_ref[rs, DC:2 * DC] = (dglu * pc_ref[rs, 0:DC] * sig_c * (1.0 - sig_c)).astype(MM)
        dcu = jnp.zeros((TM, DC), F32)
        for k in range(SK):
            dcu = dcu + wsc_ref[k:k + 1, :] * ge[pl.ds(SK - 1 - k, TM), DC:2 * DC]
            wg_ref[32 + k:33 + k, :] += jnp.sum(dcv * ce[pl.ds(HALO - (SK - 1) + k, TM), :], axis=0, keepdims=True)
        dpc_ref[:, 2 * DC:3 * DC] = (dcc_ref[:, DC:2 * DC] * sv_ref[:, DC:2 * DC]).astype(MM)
        dpc_ref[:, 3 * DC:4 * DC] = (dcu * u).astype(MM)
        dpc_ref[:, 4 * DC:5 * DC] = (dcu * c).astype(MM)

    rows = lambda n: pl.BlockSpec((TM, n), lambda i: (i, 0))
    return pl.pallas_call(
        body, name=name, grid=(nb,),
        in_specs=[rows(2 * DC), pl.BlockSpec((HALO, 2 * DC), lambda i: (jnp.minimum((i + 1) * hb, nb * hb - 1), 0)),
                  rows(5 * DC), pl.BlockSpec((HALO, 5 * DC), lambda i: (jnp.maximum(i * hb - 1, 0), 0)),
                  rows(2 * DC), rows(2 * DC), _full((32, DC)), _full((8, DC))],
        out_specs=[rows(5 * DC), _full((40, DC))],
        out_shape=[S((tp, 5 * DC), MM), S((40, DC), F32)],
        scratch_shapes=[pltpu.VMEM((TM + HALO, 2 * DC), F32), pltpu.VMEM((HALO + TM, DC), F32),
                        pltpu.VMEM((HALO + TM, DC), F32), pltpu.VMEM((7, TM + HALO - 8, DC), F32),
                        pltpu.VMEM((7, TM + HALO - 8, DC), F32)],
        compiler_params=_params(1),
    )(gc, gc, pc, pc, dcc, sv, wdw, wsc)


def in_proj_bwd(dproj, w, h, g, dh_in, name, rider=None):
    tp = h.shape[0]
    nb = tp // TM

    def body(*refs):
        (dp_ref, w_ref, h_ref, g_ref, di_ref, dh_ref, dg_ref), copies = _split_rider(refs, 5, 2, rider)
        i = pl.program_id(0)

        @pl.when(i == 0)
        def _():
            if copies:
                copies[0]()
            dg_ref[...] = jnp.zeros_like(dg_ref)

        dhn = lax.dot_general(dp_ref[...], w_ref[...], NT, preferred_element_type=F32)
        dx, dg = _rmsnorm_bwd(h_ref[...], g_ref[...], dhn)
        dh_ref[...] = di_ref[...] + dx
        dg_ref[...] += dg

        if copies:
            @pl.when(i == nb - 1)
            def _():
                copies[1]()

    rows = lambda n: pl.BlockSpec((TM, n), lambda i: (i, 0))
    r_n = rider.k_n if rider else 0
    return pl.pallas_call(
        body, name=name, grid=(nb,),
        in_specs=[rows(NP_IN), _full((D, NP_IN)), rows(D), _full((1, D)), rows(D)] + [ANY] * r_n,
        out_specs=[rows(D), _full((1, D))] + [ANY] * r_n,
        out_shape=[S((tp, D), F32), S((1, D), F32)] + (rider.out_shape if rider else []),
        scratch_shapes=rider.scratch if rider else [],
        compiler_params=_params(1),
    )(dproj, w, h, g, dh_in, *(rider.srcs if rider else []))


def adamw(recvs, w, m, v, rb, name):
    l_n, r_n, c_n = w.shape

    def body(*refs):
        p_refs = refs[:l_n]
        w_ref, m_ref, v_ref, g_ref, d_ref, m2_ref, v2_ref = refs[l_n:]
        for l in range(l_n):
            g = p_refs[l][0].astype(F32)
            for s in range(1, NDEV):
                g = g + p_refs[l][s].astype(F32)
            m2 = ADAM_B1 * m_ref[l] + (1.0 - ADAM_B1) * g
            v2 = ADAM_B2 * v_ref[l] + (1.0 - ADAM_B2) * (g * g)
            m_hat = m2 / (1.0 - ADAM_B1 ** ADAM_STEP)
            v_hat = v2 / (1.0 - ADAM_B2 ** ADAM_STEP)
            g_ref[l] = g
            d_ref[l] = -ADAM_LR * (m_hat / (jnp.sqrt(v_hat) + ADAM_EPS) + ADAM_WD * w_ref[l])
            m2_ref[l] = m2
            v2_ref[l] = v2

    blk = pl.BlockSpec((l_n, rb, c_n), lambda r: (0, r, 0))
    return pl.pallas_call(
        body, name=name, grid=(r_n // rb,),
        in_specs=[pl.BlockSpec((NDEV, rb, c_n), lambda r: (0, r, 0))] * l_n + [blk, blk, blk],
        out_specs=[blk] * 4, out_shape=[S(w.shape, F32)] * 4,
        compiler_params=_params(1),
    )(*recvs, w, m, v)


TINY_ROWS = 168
REP_ROWS = 64


def _pack(parts, rows):
    flat = jnp.concatenate([p.reshape(-1) for p in parts])
    return jnp.pad(flat, (0, rows * 128 - flat.shape[0])).reshape(rows, 128)


def _unpack(buf, shapes):
    flat, out, o = buf.reshape(-1), [], 0
    for s in shapes:
        n = 1
        for d in s:
            n *= d
        out.append(flat[o:o + n].reshape(s))
        o += n
    return out


TINY_SHAPES = [(DEPTH, CK, DC // NDEV), (DEPTH, SK, DC // NDEV), (NM, D // NDEV), (DEPTH, DC // NDEV, DC)]
REP_SHAPES = [(DEPTH, D), (DEPTH, H), (DEPTH, DC), (DEPTH, DC), (DEPTH, DC), (DEPTH, DC), (DEPTH, D), (D,)]


def _to_padded_cols(w):
    pad = jnp.zeros(w.shape[:-1] + (NP_IN - N_IN,), w.dtype)
    return jnp.concatenate([w[..., :C0], w[..., C0 + H:], w[..., C0:C0 + H], pad], axis=-1)


def _from_padded_cols(w):
    return jnp.concatenate([w[..., :C0], w[..., F0:F0 + H], w[..., C0:F0]], axis=-1)


def kernel(x, meta_tokens, mix_norm_g, w_in, b_forget, w_conf_dw, b_conf_dw, conf_ln_g, conf_ln_b, w_conf_pw, b_conf_pw, w_sc_conv, w_out, mlp_norm_g, w_mlp1, w_mlp2, final_norm_g, loss_target, m_meta_tokens, m_mix_norm_g, m_w_in, m_b_forget, m_w_conf_dw, m_b_conf_dw, m_conf_ln_g, m_conf_ln_b, m_w_conf_pw, m_b_conf_pw, m_w_sc_conv, m_w_out, m_mlp_norm_g, m_w_mlp1, m_w_mlp2, m_final_norm_g, v_meta_tokens, v_mix_norm_g, v_w_in, v_b_forget, v_w_conf_dw, v_b_conf_dw, v_conf_ln_g, v_conf_ln_b, v_w_conf_pw, v_b_conf_pw, v_w_sc_conv, v_w_out, v_mlp_norm_g, v_w_mlp1, v_w_mlp2, v_final_norm_g):
    seq = x.shape[1]
    t_real = NM + seq
    tp = -(-t_real // TM) * TM

    tiny_w = _pack([w_conf_dw, w_sc_conv, meta_tokens, w_conf_pw], TINY_ROWS)
    g_in, g_tiny = gather([w_in[0].astype(MM), tiny_w], "gather_weights0")
    n_sh = w_in.shape[-1]
    assert w_mlp1.shape[-1] == FC and w_mlp2.shape[-2] == FC
    tiny = [_unpack(g_tiny[s], TINY_SHAPES) for s in range(NDEV)]
    wdw = jnp.concatenate([t[0] for t in tiny], axis=-1)
    wsc = jnp.concatenate([t[1] for t in tiny], axis=-1)
    meta = jnp.concatenate([t[2] for t in tiny], axis=-1)
    wpw = jnp.concatenate([t[3] for t in tiny], axis=1).astype(MM)
    wdw = jnp.pad(wdw, ((0, 0), (0, 32 - CK), (0, 0)))
    wsc = jnp.pad(wsc, ((0, 0), (0, 8 - SK), (0, 0)))
    bfp = jnp.pad(b_forget, ((0, 0), (0, 128 - H)))

    row = lambda a: a.reshape(1, -1)

    h = jnp.concatenate([meta, x[0], jnp.zeros((tp - t_real, D), F32)], axis=0)
    tgt = jnp.pad(loss_target[0], ((NM, tp - t_real), (0, 0)))
    saved, gathered = [], {}
    for l in range(DEPTH):
        win = _to_padded_cols(g_in.transpose(1, 0, 2).reshape(D, NDEV * n_sh))
        hn, qa, ka, va, pc, sg = in_proj(h, row(mix_norm_g[l]), win, row(bfp[l]), f"in_proj{l}")
        riding = [w_out[l].astype(MM), w_mlp1[l].astype(MM), w_mlp2[l].astype(MM)]
        riding += [w_in[l + 1].astype(MM)] if l + 1 < DEPTH else []
        o, lse, g_out, g_w1, g_w2, *nxt = attn_fwd(qa, ka, va, f"attn_fwd{l}", AllToAll(riding, [False] * len(riding)))
        gathered[l] = (g_out, g_w1, g_w2)
        if nxt:
            g_in = nxt[0]
        cc, sv = conv_fwd(pc, wdw[l], row(b_conf_dw[l]), row(conf_ln_g[l]), row(conf_ln_b[l]), wpw[l],
                          row(b_conf_pw[l]), wsc[l], f"conv_fwd{l}")
        h2, hn2 = out_proj(h, o, cc, g_out, row(mlp_norm_g[l]), f"out_proj{l}")
        r, z, h3 = mlp_fwd(hn2, h2, g_w1, g_w2, f"mlp_fwd{l}")
        saved.append((h, hn, qa, ka, va, pc, sg, o, lse, cc, sv, h2, hn2, r, z, win))
        h = h3

    dh, loss_part, d_gf = loss_head(h, tgt, row(final_norm_g), t_real, "loss_head")
    loss = lax.psum(loss_part[0, 0], ("x", "y", "c"))

    gw, recv, d_win = {}, {}, {}
    for l in reversed(range(DEPTH)):
        h0, hn, qa, ka, va, pc, sg, o, lse, cc, sv, h2, hn2, r, z, win = saved[l]
        g_out, g_w1, g_w2 = gathered[l]
        da, dh2, gw["mlp_g", l] = mlp_bwd(dh, r, g_w1, g_w2, h2, row(mlp_norm_g[l]), f"mlp_bwd{l}")
        d_w1 = matmul_tn(hn2, da, D, 4 * FC, f"dw_mlp1_{l}", out_dtype=MM, shard="n", sw=FC)
        d_w2 = matmul_tn(z, dh, 4 * FC, D, f"dw_mlp2_{l}", out_dtype=MM, shard="m", sw=FC)
        do, dcc = out_proj_bwd(dh2, g_out, f"out_proj_bwd{l}")
        d_wout = jnp.concatenate([matmul_tn(o, dh2, DA, D, f"dw_out_a{l}"), matmul_tn(cc, dh2, 2 * DC, D, f"dw_out_c{l}")],
                                 axis=0).reshape(NDEV, D // NDEV, D).astype(MM)
        qb, dob, qbt, dobt = attn_bwd_prep(qa, do, o, lse, f"attn_bwd_prep{l}")
        riding = [("wout", l, d_wout), ("w1", l, d_w1), ("w2", l, d_w2)] + ([("win", l + 1, d_win[l + 1])] if l + 1 < DEPTH else [])
        dq, dk, dv, dcq, dck, *arrived = attn_bwd(qb, dob, qbt, dobt, ka, va, f"attn_bwd{l}",
                                                  AllToAll([a for _, _, a in riding], [True] * len(riding)))
        for (k, kl, _), a in zip(riding, arrived):
            recv[k, kl] = a
        dz, gw["bf", l] = fgate_bwd(dcq, jnp.pad(dck.T, ((0, 0), (0, 128 - H))), sg, f"fgate_bwd{l}")
        gc, act, pg = conv_bwd_pointwise(dcc, sv, pc, row(conf_ln_g[l]), row(conf_ln_b[l]), wpw[l], f"conv_bwd_pw{l}")
        gw["wpw", l] = matmul_tn(act, dcc, DC, DC, f"dw_conf_pw{l}", n=DC)
        dpc, wg = conv_bwd_taps(gc, pc, dcc, sv, wdw[l], wsc[l], f"conv_bwd_taps{l}")
        gw["pg", l], gw["wg", l] = pg, wg
        dproj = jnp.concatenate([dq, dk, dv, dpc, dz], axis=1)
        d_win[l] = _from_padded_cols(matmul_tn(hn, dproj, 512, NP_IN, f"dw_in{l}")).reshape(D, NDEV, n_sh).transpose(
            1, 0, 2).astype(MM)
        rider = AllToAll([d_win[0]], [True]) if l == 0 else None
        dh, gw["mix_g", l], *arrived = in_proj_bwd(dproj, win, h0, row(mix_norm_g[l]), dh2, f"in_proj_bwd{l}", rider)
        if rider:
            recv["win", 0] = arrived[0]

    grad_x = dh[NM:t_real][None]
    stack = lambda k: jnp.stack([gw[k, l] for l in range(DEPTH)])

    d_wdw = stack("wg")[:, 0:CK].reshape(DEPTH, CK, NDEV, DC // NDEV).transpose(2, 0, 1, 3)
    d_wsc = stack("wg")[:, 32:32 + SK].reshape(DEPTH, SK, NDEV, DC // NDEV).transpose(2, 0, 1, 3)
    d_meta = dh[0:NM].reshape(NM, NDEV, D // NDEV).transpose(1, 0, 2)
    d_wpw = stack("wpw").reshape(DEPTH, NDEV, DC // NDEV, DC).transpose(1, 0, 2, 3)
    d_tiny = jnp.stack([_pack([d_wdw[p], d_wsc[p], d_meta[p], d_wpw[p]], TINY_ROWS) for p in range(NDEV)])
    pgs = stack("pg")
    d_rep = _pack([stack("mix_g").reshape(DEPTH, D), stack("bf")[:, 0, :H], pgs[:, 3], pgs[:, 1], pgs[:, 2], pgs[:, 0],
                   stack("mlp_g").reshape(DEPTH, D), d_gf.reshape(D)], REP_ROWS)
    r_tiny, r_rep = exchange([d_tiny, d_rep], [True, False], "exchange_grads")

    per_layer = lambda k: [recv[k, l] for l in range(DEPTH)]
    res = {}
    res["w_in"] = adamw(per_layer("win"), w_in, m_w_in, v_w_in, 256, "adamw_w_in")
    res["w_out"] = adamw(per_layer("wout"), w_out, m_w_out, v_w_out, D // NDEV, "adamw_w_out")
    res["w_mlp1"] = adamw(per_layer("w1"), w_mlp1, m_w_mlp1, v_w_mlp1, 256, "adamw_w_mlp1")
    res["w_mlp2"] = adamw(per_layer("w2"), w_mlp2, m_w_mlp2, v_w_mlp2, 128, "adamw_w_mlp2")
    tiny_names = ["w_conf_dw", "w_sc_conv", "meta_tokens", "w_conf_pw"]
    tiny_wmv = [[w_conf_dw, w_sc_conv, meta_tokens, w_conf_pw], [m_w_conf_dw, m_w_sc_conv, m_meta_tokens, m_w_conf_pw],
                [v_w_conf_dw, v_w_sc_conv, v_meta_tokens, v_w_conf_pw]]
    rep_names = ["mix_norm_g", "b_forget", "b_conf_dw", "conf_ln_g", "conf_ln_b", "b_conf_pw", "mlp_norm_g", "final_norm_g"]
    rep_wmv = [[mix_norm_g, b_forget, b_conf_dw, conf_ln_g, conf_ln_b, b_conf_pw, mlp_norm_g, final_norm_g],
               [m_mix_norm_g, m_b_forget, m_b_conf_dw, m_conf_ln_g, m_conf_ln_b, m_b_conf_pw, m_mlp_norm_g, m_final_norm_g],
               [v_mix_norm_g, v_b_forget, v_b_conf_dw, v_conf_ln_g, v_conf_ln_b, v_b_conf_pw, v_mlp_norm_g, v_final_norm_g]]
    for names, wmv, shapes, recv_pack, rows_n, nm in ((tiny_names, tiny_wmv, TINY_SHAPES, r_tiny, TINY_ROWS, "adamw_tiny"),
                                                 (rep_names, rep_wmv, REP_SHAPES, r_rep, REP_ROWS, "adamw_rep")):
        packed = [_pack(group, rows_n)[None] for group in wmv]
        outs = adamw([recv_pack], *packed, rows_n, nm)
        parts = [_unpack(a[0], shapes) for a in outs]
        for k, n in enumerate(names):
            res[n] = [parts[q][k] for q in range(4)]

    order = ["meta_tokens", "mix_norm_g", "w_in", "b_forget", "w_conf_dw", "b_conf_dw", "conf_ln_g", "conf_ln_b",
             "w_conf_pw", "b_conf_pw", "w_sc_conv", "w_out", "mlp_norm_g", "w_mlp1", "w_mlp2", "final_norm_g"]
    return (loss, grad_x, *[res[n][0] for n in order], *[res[n][1] for n in order],
            *[res[n][2] for n in order], *[res[n][3] for n in order])
```

```python
import functools

import jax
import jax.numpy as jnp
from jax import lax
from jax.experimental import pallas as pl
from jax.experimental.pallas import tpu as pltpu

F32 = jnp.float32
MM = jnp.bfloat16

D = 1024
H = 8
DH = 64
DA = H * DH
HT = 128
DC = 256
NM = 16
CK = 31
SK = 3
DFF = 4096
DEPTH = 2
N_IN = 3 * DA + H + 2 * DC + 3 * DC
NP_IN = 3 * DA + 5 * DC + 128
C0 = 3 * DA
F0 = 3 * DA + 5 * DC
EPS = 1e-6
TM = 640
HALO = 32
FC = 512
FS = 4
NDEV = 8
SCALE = DH ** -0.5
NEG = -1e30

ADAM_LR, ADAM_B1, ADAM_B2, ADAM_EPS, ADAM_WD, ADAM_STEP = 0.001, 0.9, 0.999, 1e-08, 0.01, 10

VMEM_LIMIT = 56 * 1024 * 1024

S = jax.ShapeDtypeStruct
NT = (((1,), (1,)), ((), ()))
TN = (((0,), (0,)), ((), ()))


def _params(n_grid):
    return pltpu.CompilerParams(dimension_semantics=("arbitrary",) * n_grid, vmem_limit_bytes=VMEM_LIMIT)


def _sigmoid(x):
    return 1.0 / (1.0 + jnp.exp(-x))


def _full(shape):
    n = len(shape)
    return pl.BlockSpec(shape, lambda *_: (0,) * n)


def _lane_put(dst, col, h):
    lane = lax.broadcasted_iota(jnp.int32, dst.shape, 1)
    return jnp.where(lane == h, col, dst)


ANY = pl.BlockSpec(memory_space=pl.ANY)


class AllToAll:
    def __init__(self, srcs, per_peer):
        self.srcs, self.per_peer, self.k_n = list(srcs), list(per_peer), len(srcs)
        self.out_shape = [S((NDEV,) + (a.shape[1:] if pp else a.shape), a.dtype) for a, pp in zip(srcs, per_peer)]
        self.scratch = [pltpu.SemaphoreType.DMA((self.k_n, NDEV - 1)), pltpu.SemaphoreType.DMA((self.k_n, NDEV - 1)),
                        pltpu.SemaphoreType.DMA((self.k_n,))]

    def copies(self, src, out, send_sems, recv_sems, local_sems):
        x, y, c = lax.axis_index("x"), lax.axis_index("y"), lax.axis_index("c")
        me = 4 * x + 2 * y + c

        def piece(k, p):
            return src[k].at[p] if self.per_peer[k] else src[k]

        local = [pltpu.make_async_copy(piece(k, me), out[k].at[me], local_sems.at[k]) for k in range(self.k_n)]
        sends, recvs = [], []
        for r in range(1, NDEV):
            px = 1 - x if (r >> 2) & 1 else x
            py = 1 - y if (r >> 1) & 1 else y
            pc = 1 - c if r & 1 else c
            pidx = 4 * px + 2 * py + pc
            for k in range(self.k_n):
                sends.append(pltpu.make_async_remote_copy(
                    src_ref=piece(k, pidx), dst_ref=out[k].at[me],
                    send_sem=send_sems.at[k, r - 1], recv_sem=recv_sems.at[k, r - 1],
                    device_id=(px, py, pc), device_id_type=pl.DeviceIdType.MESH))
                recvs.append(pltpu.make_async_remote_copy(
                    src_ref=piece(k, pidx), dst_ref=out[k].at[pidx],
                    send_sem=send_sems.at[k, r - 1], recv_sem=recv_sems.at[k, r - 1],
                    device_id=(px, py, pc), device_id_type=pl.DeviceIdType.MESH))

        def start():
            for cp in local + sends:
                cp.start()

        def wait():
            for cp in recvs:
                cp.wait_recv()
            for cp in sends:
                cp.wait_send()
            for cp in local:
                cp.wait()

        return start, wait


def exchange(srcs, per_peer, name):
    plan = AllToAll(srcs, per_peer)

    def body(*refs):
        start, wait = plan.copies(refs[:plan.k_n], refs[plan.k_n:2 * plan.k_n], *refs[2 * plan.k_n:])
        start()
        wait()

    return pl.pallas_call(
        body, name=name, out_shape=plan.out_shape, in_specs=[ANY] * plan.k_n, out_specs=[ANY] * plan.k_n,
        scratch_shapes=plan.scratch,
    )(*srcs)


def gather(srcs, name):
    k_n = len(srcs)
    out_shape = [S((NDEV,) + a.shape, a.dtype) for a in srcs]

    def body(*refs):
        src, out = refs[:k_n], refs[k_n:2 * k_n]
        send_sems, recv_sems, local_sems = refs[2 * k_n:]
        x, y, c = lax.axis_index("x"), lax.axis_index("y"), lax.axis_index("c")
        me, sibling = (x, y, c), (x, y, 1 - c)
        chips = [(1 - x, y), (x, 1 - y), (1 - x, 1 - y)]

        def slot(k, dev):
            return out[k].at[4 * dev[0] + 2 * dev[1] + dev[2]]

        def copy(k, r, block, to, from_src=False):
            return pltpu.make_async_remote_copy(
                src_ref=src[k] if from_src else slot(k, block), dst_ref=slot(k, block),
                send_sem=send_sems.at[k, r], recv_sem=recv_sems.at[k, r],
                device_id=to, device_id_type=pl.DeviceIdType.MESH)

        local = [pltpu.make_async_copy(src[k], slot(k, me), local_sems.at[k]) for k in range(k_n)]
        first = [copy(k, 0, me, sibling, True) for k in range(k_n)]
        first += [copy(k, 1 + n, me, (*chip, c), True) for n, chip in enumerate(chips) for k in range(k_n)]
        for cp in local + first:
            cp.start()
        passed = []
        for n, chip in enumerate(chips):
            for k in range(k_n):
                copy(k, 1 + n, (*chip, c), me).wait_recv()
                passed.append(copy(k, 4 + n, (*chip, c), sibling))
                passed[-1].start()
        for k in range(k_n):
            copy(k, 0, sibling, me).wait_recv()
            for n, chip in enumerate(chips):
                copy(k, 4 + n, (*chip, 1 - c), me).wait_recv()
        for cp in first + passed:
            cp.wait_send()
        for cp in local:
            cp.wait()

    any_spec = pl.BlockSpec(memory_space=pl.ANY)
    return pl.pallas_call(
        body, name=name, out_shape=out_shape,
        in_specs=[any_spec] * k_n, out_specs=[any_spec] * k_n,
        scratch_shapes=[pltpu.SemaphoreType.DMA((k_n, NDEV - 1)), pltpu.SemaphoreType.DMA((k_n, NDEV - 1)),
                        pltpu.SemaphoreType.DMA((k_n,))],
    )(*srcs)


def in_proj(h, g, w, bf, name):
    tp = h.shape[0]

    def body(h_ref, g_ref, w_ref, bf_ref, hn_ref, qa_ref, ka_ref, va_ref, pc_ref, sg_ref, carry):
        i = pl.program_id(0)

        @pl.when(i == 0)
        def _():
            carry[...] = jnp.zeros_like(carry)

        x = h_ref[...]
        r = lax.rsqrt(jnp.mean(x * x, axis=-1, keepdims=True) + EPS)
        hn = (x * r * g_ref[...]).astype(MM)
        hn_ref[...] = hn
        pc_ref[...] = jnp.dot(hn, w_ref[:, C0:F0], preferred_element_type=F32)
        z = jnp.dot(hn, w_ref[:, F0:NP_IN], preferred_element_type=F32) + bf_ref[...]
        lane = lax.broadcasted_iota(jnp.int32, z.shape, 1)
        logf = jnp.where(lane < H, jnp.minimum(z, 0.0) - jnp.log(1.0 + jnp.exp(-jnp.abs(z))), 0.0)
        sg_ref[...] = 1.0 / (1.0 + jnp.exp(z))
        row = lax.broadcasted_iota(jnp.int32, (TM, TM), 0)
        col = lax.broadcasted_iota(jnp.int32, (TM, TM), 1)
        c = _tri_matmul(row >= col, logf) + carry[...]
        carry[...] = c[TM - 1:TM, :]
        qkv = jnp.dot(hn, w_ref[:, 0:C0], preferred_element_type=F32)
        one = [1.0, 1.0, 1.0]
        for hd in range(H):
            t = slice(hd * HT, (hd + 1) * HT)
            cs = list(_split3(c[:, hd:hd + 1]))
            qa_ref[:, t] = _aug_tile(hd, qkv[:, hd * DH:(hd + 1) * DH] * SCALE, cs + one).astype(MM)
            ka_ref[:, t] = _aug_tile(hd, qkv[:, DA + hd * DH:DA + (hd + 1) * DH], one + [-v for v in cs] + one).astype(MM)
            va_ref[:, t] = _aug_tile(hd, qkv[:, 2 * DA + hd * DH:2 * DA + (hd + 1) * DH], [1.0] + one).astype(MM)

    rows = lambda n: pl.BlockSpec((TM, n), lambda i: (i, 0))
    return pl.pallas_call(
        body, name=name, grid=(tp // TM,),
        in_specs=[rows(D), _full((1, D)), _full((D, NP_IN)), _full((1, 128))],
        out_specs=[rows(D), rows(H * HT), rows(H * HT), rows(H * HT), rows(5 * DC), rows(128)],
        out_shape=[S((tp, D), MM), S((tp, H * HT), MM), S((tp, H * HT), MM), S((tp, H * HT), MM),
                   S((tp, 5 * DC), F32), S((tp, 128), F32)],
        scratch_shapes=[pltpu.VMEM((1, 128), F32)],
        compiler_params=_params(1),
    )(h, g, w, bf)


def _split3(c):
    hi = c.astype(MM).astype(F32)
    mid = (c - hi).astype(MM).astype(F32)
    lo = (c - hi - mid).astype(MM).astype(F32)
    return hi, mid, lo


def _tri_matmul(keep, x):
    pieces = jnp.concatenate(_split3(x), axis=1).astype(MM)
    r = jnp.dot(jnp.where(keep, 1.0, 0.0).astype(MM), pieces, preferred_element_type=F32)
    return r[:, 0:128] + r[:, 128:256] + r[:, 256:384]


def _main(hd):
    return slice(0, DH) if hd % 2 == 0 else slice(DH, HT)


def _aug_lane(hd, pos):
    return pos + (DH if hd % 2 == 0 else 0)


def _aug_tile(hd, main, cols):
    lane = lax.broadcasted_iota(jnp.int32, main.shape, 1)
    aug = jnp.zeros(main.shape, F32)
    for pos, val in enumerate(cols):
        aug = jnp.where(lane == pos, val, aug)
    return jnp.concatenate([main, aug] if hd % 2 == 0 else [aug, main], axis=1)


def _merge_pairs(tiles):
    lane = lax.broadcasted_iota(jnp.int32, tiles[0].shape, 1)
    return jnp.concatenate([jnp.where(lane < DH, tiles[2 * m], tiles[2 * m + 1]) for m in range(H // 2)], axis=1)


def _causal_mask():
    return lax.broadcasted_iota(jnp.int32, (TM, TM), 0) >= lax.broadcasted_iota(jnp.int32, (TM, TM), 1)


def _split_rider(refs, n_in, n_out, rider):
    if rider is None:
        return refs, None
    k = rider.k_n
    own = refs[:n_in] + refs[n_in + k:n_in + k + n_out] + refs[n_in + 2 * k + n_out:-3]
    return own, rider.copies(refs[n_in:n_in + k], refs[n_in + k + n_out:n_in + 2 * k + n_out], *refs[-3:])


def attn_fwd(qa, ka, va, name, rider=None):
    tp = qa.shape[0]
    nb = tp // TM
    pairs = [(i, j) for i in range(nb) for j in range(i + 1)]
    i_tab, j_tab = (jnp.asarray([p[a] for p in pairs], jnp.int32) for a in (0, 1))

    def body(i_ref, j_ref, *refs):
        (q_ref, k_ref, v_ref, o_ref, lse_ref, *scr), copies = _split_rider(refs, 3, 2, rider)
        m_scr, acc_scr, bias = scr[:H], scr[H:2 * H], scr[2 * H]
        n = pl.program_id(0)
        i, j = i_ref[n], j_ref[n]

        @pl.when(n == 0)
        def _():
            if copies:
                copies[0]()
            bias[...] = jnp.where(_causal_mask(), 0.0, NEG)

        @pl.when(j == 0)
        def _():
            for hd in range(H):
                m_scr[hd][...] = jnp.full((TM, 1), NEG, F32)
                acc_scr[hd][...] = jnp.zeros((TM, HT), F32)

        def step(diag):
            units = [(hd, slice(r0, r0 + TM // 2)) for hd in range(H) for r0 in (0, TM // 2)]

            def logits(hd, rs):
                t = slice(hd * HT, (hd + 1) * HT)
                s = lax.dot_general(q_ref[rs, t], k_ref[:, t], NT, preferred_element_type=F32)
                return s + bias[rs, :] if diag else s

            s_next = logits(*units[0])
            for n, (hd, rs) in enumerate(units):
                t = slice(hd * HT, (hd + 1) * HT)
                s = s_next
                if n + 1 < len(units):
                    s_next = logits(*units[n + 1])
                m_prev = m_scr[hd][rs, :]
                m_new = jnp.maximum(m_prev, jnp.max(s, axis=1, keepdims=True))
                p = jnp.exp(s - m_new).astype(MM)
                acc_scr[hd][rs, :] = (jnp.exp(m_prev - m_new) * acc_scr[hd][rs, :]
                                      + jnp.dot(p, v_ref[:, t], preferred_element_type=F32))
                m_scr[hd][rs, :] = m_new

        @pl.when(j < i)
        def _():
            step(False)

        @pl.when(j == i)
        def _():
            step(True)

        @pl.when(j == i)
        def _():
            lse = jnp.zeros((TM, 128), F32)
            outs = []
            for hd in range(H):
                acc = acc_scr[hd][...]
                l = acc[:, _aug_lane(hd, 0):_aug_lane(hd, 0) + 1]
                outs.append(acc * (1.0 / l))
                lse = _lane_put(lse, m_scr[hd][...] + jnp.log(l), hd)
            o_ref[...] = _merge_pairs(outs).astype(MM)
            lse_ref[...] = lse

        if copies:
            @pl.when(n == len(pairs) - 1)
            def _():
                copies[1]()

    qi = lambda w: pl.BlockSpec((TM, w), lambda n, it, jt: (it[n], 0))
    kv = pl.BlockSpec((TM, H * HT), lambda n, it, jt: (jt[n], 0))
    r_n = rider.k_n if rider else 0
    return pl.pallas_call(
        body, name=name,
        grid_spec=pltpu.PrefetchScalarGridSpec(
            num_scalar_prefetch=2, grid=(len(pairs),),
            in_specs=[qi(H * HT), kv, kv] + [ANY] * r_n, out_specs=[qi(DA), qi(128)] + [ANY] * r_n,
            scratch_shapes=[pltpu.VMEM((TM, 1), F32)] * H + [pltpu.VMEM((TM, HT), F32)] * H + [pltpu.VMEM((TM, TM), F32)]
            + (rider.scratch if rider else [])),
        out_shape=[S((tp, DA), MM), S((tp, 128), F32)] + (rider.out_shape if rider else []),
        compiler_params=_params(1),
    )(i_tab, j_tab, qa, ka, va, *(rider.srcs if rider else []))


def _layernorm_parts(dw, g, b):
    mu = jnp.mean(dw, axis=-1, keepdims=True)
    xc = dw - mu
    rs = lax.rsqrt(jnp.mean(xc * xc, axis=-1, keepdims=True) + EPS)
    xh = xc * rs
    return xh, rs, xh * g + b


def _fill_halo(ext, halo, cur, first):
    ext[0:HALO, :] = jnp.where(first, 0.0, halo)
    ext[HALO:HALO + TM, :] = cur


def _shift_copies(dst, src, lanes=slice(None)):
    for r in range(1, 8):
        dst[r - 1] = src[pl.ds(r, dst.shape[1]), lanes]


RC = 64


def _window(src, shifted, off, lanes=slice(None), r0=0, rows=None):
    rows = TM if rows is None else rows
    if off % 8 == 0:
        return src[pl.ds(off + r0, rows), lanes]
    return shifted[off % 8 - 1, pl.ds(off - off % 8 + r0, rows), :]


def conv_fwd(pc, wdw, bdw, lng, lnb, wpw, bpw, wsc, name):
    tp = pc.shape[0]

    def body(pc_ref, hl_ref, wdw_ref, bdw_ref, lng_ref, lnb_ref, wpw_ref, bpw_ref, wsc_ref, cc_ref, sv_ref, xe, ce, xs):
        first = pl.program_id(0) == 0
        glu = pc_ref[:, 0:DC] * _sigmoid(pc_ref[:, DC:2 * DC])
        cu = pc_ref[:, 3 * DC:4 * DC] * pc_ref[:, 4 * DC:5 * DC]
        _fill_halo(xe, hl_ref[:, 0:DC] * _sigmoid(hl_ref[:, DC:2 * DC]), glu, first)
        _fill_halo(ce, hl_ref[:, 3 * DC:4 * DC] * hl_ref[:, 4 * DC:5 * DC], cu, first)
        _shift_copies(xs, xe)
        for r0 in range(0, TM, RC):
            acc = jnp.zeros((RC, DC), F32) + bdw_ref[...]
            for k in range(CK):
                acc = acc + wdw_ref[k:k + 1, :] * _window(xe, xs, HALO - (CK - 1) + k, r0=r0, rows=RC)
            sv_ref[r0:r0 + RC, 0:DC] = acc
        dw = sv_ref[:, 0:DC]
        cv = jnp.zeros((TM, DC), F32)
        for k in range(SK):
            cv = cv + wsc_ref[k:k + 1, :] * ce[pl.ds(HALO - (SK - 1) + k, TM), :]
        _, _, ln = _layernorm_parts(dw, lng_ref[...], lnb_ref[...])
        act = ln * _sigmoid(ln)
        conf = jnp.dot(act.astype(MM), wpw_ref[...], preferred_element_type=F32) + bpw_ref[...]
        cc_ref[:, 0:DC] = conf.astype(MM)
        cc_ref[:, DC:2 * DC] = (pc_ref[:, 2 * DC:3 * DC] * cv).astype(MM)
        sv_ref[:, DC:2 * DC] = cv

    hb = TM // HALO
    rows = lambda n: pl.BlockSpec((TM, n), lambda i: (i, 0))
    return pl.pallas_call(
        body, name=name, grid=(tp // TM,),
        in_specs=[rows(5 * DC), pl.BlockSpec((HALO, 5 * DC), lambda i: (jnp.maximum(i * hb - 1, 0), 0)),
                  _full((32, DC)), _full((1, DC)), _full((1, DC)), _full((1, DC)), _full((DC, DC)), _full((1, DC)),
                  _full((8, DC))],
        out_specs=[rows(2 * DC), rows(2 * DC)],
        out_shape=[S((tp, 2 * DC), MM), S((tp, 2 * DC), F32)],
        scratch_shapes=[pltpu.VMEM((HALO + TM, DC), F32), pltpu.VMEM((HALO + TM, DC), F32),
                        pltpu.VMEM((7, TM + HALO - 8, DC), F32)],
        compiler_params=_params(1),
    )(pc, pc, wdw, bdw, lng, lnb, wpw, bpw, wsc)


def out_proj(h, o, cc, wo, g2, name):
    tp = h.shape[0]

    def body(h_ref, o_ref, cc_ref, wo_ref, g_ref, h2_ref, hn_ref):
        wo = wo_ref[...].reshape(D, D)
        h2 = (h_ref[...] + jnp.dot(o_ref[...], wo[0:DA, :], preferred_element_type=F32)
              + jnp.dot(cc_ref[...], wo[DA:D, :], preferred_element_type=F32))
        h2_ref[...] = h2
        r = lax.rsqrt(jnp.mean(h2 * h2, axis=-1, keepdims=True) + EPS)
        hn_ref[...] = (h2 * r * g_ref[...]).astype(MM)

    rows = lambda n: pl.BlockSpec((TM, n), lambda i: (i, 0))
    return pl.pallas_call(
        body, name=name, grid=(tp // TM,),
        in_specs=[rows(D), rows(DA), rows(2 * DC), _full((NDEV, D // NDEV, D)), _full((1, D))],
        out_specs=[rows(D), rows(D)],
        out_shape=[S((tp, D), F32), S((tp, D), MM)],
        compiler_params=_params(1),
    )(h, o, cc, wo, g2)


def mlp_fwd(hn, h2, w1, w2, name):
    tp = hn.shape[0]
    nf = DFF // (FS * FC)

    def body(hn_ref, h2_ref, w1_ref, w2_ref, r_ref, z_ref, h3_ref, acc):
        j = pl.program_id(1)

        @pl.when(j == 0)
        def _():
            acc[...] = jnp.zeros_like(acc)

        for s in range(FS):
            cols = slice(s * FC, (s + 1) * FC)
            r = jnp.maximum(jnp.dot(hn_ref[...], w1_ref[s], preferred_element_type=F32), 0.0)
            zb = (r * r).astype(MM)
            r_ref[:, cols] = r.astype(MM)
            z_ref[:, cols] = zb
            acc[...] += jnp.dot(zb, w2_ref[s], preferred_element_type=F32)

        @pl.when(j == nf - 1)
        def _():
            h3_ref[...] = h2_ref[...] + acc[...]

    return pl.pallas_call(
        body, name=name, grid=(tp // TM, nf),
        in_specs=[pl.BlockSpec((TM, D), lambda i, j: (i, 0)), pl.BlockSpec((TM, D), lambda i, j: (i, 0)),
                  pl.BlockSpec((FS, D, FC), lambda i, j: (j, 0, 0)), pl.BlockSpec((FS, FC, D), lambda i, j: (j, 0, 0))],
        out_specs=[pl.BlockSpec((TM, FS * FC), lambda i, j: (i, j)), pl.BlockSpec((TM, FS * FC), lambda i, j: (i, j)),
                   pl.BlockSpec((TM, D), lambda i, j: (i, 0))],
        out_shape=[S((tp, DFF), MM), S((tp, DFF), MM), S((tp, D), F32)],
        scratch_shapes=[pltpu.VMEM((TM, D), F32)],
        compiler_params=_params(2),
    )(hn, h2, w1, w2)


def loss_head(h, tgt, g, t_real, name):
    tp = h.shape[0]

    def body(h_ref, t_ref, g_ref, dh_ref, loss_ref, dg_ref):
        i = pl.program_id(0)

        @pl.when(i == 0)
        def _():
            loss_ref[...] = jnp.zeros_like(loss_ref)
            dg_ref[...] = jnp.zeros_like(dg_ref)

        x = h_ref[...]
        gg = g_ref[...]
        r = lax.rsqrt(jnp.mean(x * x, axis=-1, keepdims=True) + EPS)
        xn = x * r
        row = i * TM + lax.broadcasted_iota(jnp.int32, (TM, 1), 0)
        e = jnp.where((row >= NM) & (row < t_real), xn * gg - t_ref[...], 0.0)
        loss_ref[...] += jnp.sum(e * e) * (0.5 / D)
        dy = e * (1.0 / D)
        dg_ref[...] += jnp.sum(dy * xn, axis=0, keepdims=True)
        u = dy * gg
        dh_ref[...] = r * (u - xn * jnp.mean(u * xn, axis=-1, keepdims=True))

    rows = lambda n: pl.BlockSpec((TM, n), lambda i: (i, 0))
    return pl.pallas_call(
        body, name=name, grid=(tp // TM,),
        in_specs=[rows(D), rows(D), _full((1, D))],
        out_specs=[rows(D), _full((8, 128)), _full((1, D))],
        out_shape=[S((tp, D), F32), S((8, 128), F32), S((1, D), F32)],
        compiler_params=_params(1),
    )(h, tgt, g)


def _rmsnorm_bwd(x, g, dy):
    r = lax.rsqrt(jnp.mean(x * x, axis=-1, keepdims=True) + EPS)
    xn = x * r
    u = dy * g
    dx = r * (u - xn * jnp.mean(u * xn, axis=-1, keepdims=True))
    return dx, jnp.sum(dy * xn, axis=0, keepdims=True)


def matmul_tn(a, b, tm, tn, name, n=None, out_dtype=F32, shard=None, sw=None):
    tp, m = a.shape
    n = b.shape[1] if n is None else n
    nk = tp // TM

    def body(a_ref, b_ref, o_ref, acc):
        k = pl.program_id(2)

        @pl.when(k == 0)
        def _():
            acc[...] = jnp.zeros_like(acc)

        acc[...] += lax.dot_general(a_ref[...].astype(MM), b_ref[...].astype(MM), TN, preferred_element_type=F32)

        @pl.when(k == nk - 1)
        def _():
            if shard is None:
                o_ref[...] = acc[...].astype(out_dtype)
            elif shard == "m":
                for s in range(tm // sw):
                    o_ref[s] = acc[s * sw:(s + 1) * sw, :].astype(out_dtype)
            else:
                for s in range(tn // sw):
                    o_ref[s] = acc[:, s * sw:(s + 1) * sw].astype(out_dtype)

    if shard is None:
        out_spec, out_shape = pl.BlockSpec((tm, tn), lambda mi, ni, k: (mi, ni)), (m, n)
    elif shard == "m":
        out_spec, out_shape = pl.BlockSpec((tm // sw, sw, tn), lambda mi, ni, k: (mi, 0, ni)), (m // sw, sw, n)
    else:
        out_spec, out_shape = pl.BlockSpec((tn // sw, tm, sw), lambda mi, ni, k: (ni, mi, 0)), (n // sw, m, sw)
    return pl.pallas_call(
        body, name=name, grid=(m // tm, n // tn, nk),
        in_specs=[pl.BlockSpec((TM, tm), lambda mi, ni, k: (k, mi)), pl.BlockSpec((TM, tn), lambda mi, ni, k: (k, ni))],
        out_specs=out_spec, out_shape=S(out_shape, out_dtype),
        scratch_shapes=[pltpu.VMEM((tm, tn), F32)],
        compiler_params=_params(3),
    )(a, b)


def mlp_bwd(dh3, r, w1, w2, h2, g, name):
    tp = dh3.shape[0]
    nf = DFF // (FS * FC)

    def body(dh3_ref, r_ref, w1_ref, w2_ref, h2_ref, g_ref, da_ref, dh2_ref, dg_ref, acc, dhb):
        i, j = pl.program_id(0), pl.program_id(1)

        @pl.when((i == 0) & (j == 0))
        def _():
            dg_ref[...] = jnp.zeros_like(dg_ref)

        @pl.when(j == 0)
        def _():
            acc[...] = jnp.zeros_like(acc)
            dhb[...] = dh3_ref[...].astype(MM)

        for s in range(FS):
            cols = slice(s * FC, (s + 1) * FC)
            dz = lax.dot_general(dhb[...], w2_ref[s], NT, preferred_element_type=F32)
            da = (dz * (2.0 * r_ref[:, cols].astype(F32))).astype(MM)
            da_ref[:, cols] = da
            acc[...] += lax.dot_general(da, w1_ref[s], NT, preferred_element_type=F32)

        @pl.when(j == nf - 1)
        def _():
            dx, dg = _rmsnorm_bwd(h2_ref[...], g_ref[...], acc[...])
            dh2_ref[...] = dh3_ref[...] + dx
            dg_ref[...] += dg

    return pl.pallas_call(
        body, name=name, grid=(tp // TM, nf),
        in_specs=[pl.BlockSpec((TM, D), lambda i, j: (i, 0)), pl.BlockSpec((TM, FS * FC), lambda i, j: (i, j)),
                  pl.BlockSpec((FS, D, FC), lambda i, j: (j, 0, 0)), pl.BlockSpec((FS, FC, D), lambda i, j: (j, 0, 0)),
                  pl.BlockSpec((TM, D), lambda i, j: (i, 0)), _full((1, D))],
        out_specs=[pl.BlockSpec((TM, FS * FC), lambda i, j: (i, j)), pl.BlockSpec((TM, D), lambda i, j: (i, 0)),
                   _full((1, D))],
        out_shape=[S((tp, DFF), MM), S((tp, D), F32), S((1, D), F32)],
        scratch_shapes=[pltpu.VMEM((TM, D), F32), pltpu.VMEM((TM, D), MM)],
        compiler_params=_params(2),
    )(dh3, r, w1, w2, h2, g)


def out_proj_bwd(dh2, wo, name):
    tp = dh2.shape[0]

    def body(dh_ref, wo_ref, do_ref, dcc_ref):
        dcat = lax.dot_general(dh_ref[...].astype(MM), wo_ref[...].reshape(D, D), NT, preferred_element_type=F32)
        do_ref[...] = dcat[:, 0:DA].astype(MM)
        dcc_ref[...] = dcat[:, DA:D]

    rows = lambda n: pl.BlockSpec((TM, n), lambda i: (i, 0))
    return pl.pallas_call(
        body, name=name, grid=(tp // TM,),
        in_specs=[rows(D), _full((NDEV, D // NDEV, D))],
        out_specs=[rows(DA), rows(2 * DC)],
        out_shape=[S((tp, DA), MM), S((tp, 2 * DC), F32)],
        compiler_params=_params(1),
    )(dh2, wo)


def attn_bwd_prep(qa, do, o, lse, name):
    tp = qa.shape[0]

    def body(q_ref, do_ref, o_ref, lse_ref, qb_ref, dob_ref, qbt_ref, dobt_ref):
        lse = lse_ref[...]
        lane = lax.broadcasted_iota(jnp.int32, (TM, HT), 1)
        for hd in range(H):
            t, hs = slice(hd * HT, (hd + 1) * HT), slice(hd * DH, (hd + 1) * DH)
            dof = do_ref[:, hs].astype(F32)
            dd = jnp.sum(dof * o_ref[:, hs].astype(F32), axis=1, keepdims=True)
            dob = _aug_tile(hd, dof, [0.0] + [-v for v in _split3(dd)])
            qb = q_ref[:, t].astype(F32)
            for pos, v in enumerate(_split3(lse[:, hd:hd + 1])):
                qb = jnp.where(lane == _aug_lane(hd, 6 + pos), -v, qb)
            qb_ref[:, t] = qb.astype(MM)
            dob_ref[:, t] = dob.astype(MM)
            qbt_ref[t, :] = qb.T.astype(MM)
            dobt_ref[t, :] = dob.T.astype(MM)

    rows = lambda n: pl.BlockSpec((TM, n), lambda i: (i, 0))
    cols = pl.BlockSpec((H * HT, TM), lambda i: (0, i))
    return pl.pallas_call(
        body, name=name, grid=(tp // TM,),
        in_specs=[rows(H * HT), rows(DA), rows(DA), rows(128)],
        out_specs=[rows(H * HT), rows(H * HT), cols, cols],
        out_shape=[S((tp, H * HT), MM), S((tp, H * HT), MM), S((H * HT, tp), MM), S((H * HT, tp), MM)],
        compiler_params=_params(1),
    )(qa, do, o, lse)


def attn_bwd(qb, dob, qbt, dobt, ka, va, name, rider=None):
    tp = qb.shape[0]
    nb = tp // TM

    pairs = [(j, i) for j in range(nb) for i in range(j, nb)]
    j_tab, i_tab = (jnp.asarray([p[a] for p in pairs], jnp.int32) for a in (0, 1))

    def body(j_ref, i_ref, *refs):
        own, copies = _split_rider(refs, 6, 5, rider)
        (qb_ref, dob_ref, qbt_ref, dobt_ref, k_ref, v_ref, dq_ref, dk_ref, dv_ref, dcq_ref, dck_ref,
         dq_acc, dcq_acc, dkt_acc, dvt_acc, dck_acc, stage, sems) = own
        n = pl.program_id(0)
        j, i = j_ref[n], i_ref[n]

        @pl.when(n == 0)
        def _():
            if copies:
                copies[0]()
            dq_acc[...] = jnp.zeros_like(dq_acc)
            dcq_acc[...] = jnp.zeros_like(dcq_acc)

        @pl.when(i == j)
        def _():
            dkt_acc[...] = jnp.zeros_like(dkt_acc)
            dvt_acc[...] = jnp.zeros_like(dvt_acc)
            dck_acc[...] = jnp.zeros_like(dck_acc)

        def step(diag):
            rows = pl.ds(pl.multiple_of(i * TM, TM), TM)
            dcq = dcq_acc[rows, :]
            lane = lax.broadcasted_iota(jnp.int32, (TM, HT), 1)
            mask = _causal_mask() if diag else None
            for hd in range(H):
                t = slice(hd * HT, (hd + 1) * HT)
                s = lax.dot_general(qb_ref[:, t], k_ref[:, t], NT, preferred_element_type=F32)
                dp = lax.dot_general(dob_ref[:, t], v_ref[:, t], NT, preferred_element_type=F32)
                if diag:
                    s = jnp.where(mask, s, NEG)
                p = jnp.exp(s)
                ds = p * dp
                dsb = ds.astype(MM)
                dvt_acc[t, :] += jnp.dot(dobt_ref[t, :], p.astype(MM), preferred_element_type=F32)
                dkt_acc[t, :] += jnp.dot(qbt_ref[t, :], dsb, preferred_element_type=F32)
                dqh = jnp.dot(dsb, k_ref[:, t], preferred_element_type=F32)
                if hd % 2 == 0:
                    dq_even = dqh
                else:
                    pair = slice((hd // 2) * HT, (hd // 2 + 1) * HT)
                    dq_acc[rows, pair] += jnp.where(lane < DH, dq_even, dqh)
                dcq = dcq + jnp.where(lane == hd, jnp.sum(ds, axis=1, keepdims=True), 0.0)
                dck_acc[hd:hd + 1, :] += jnp.sum(ds, axis=0, keepdims=True)
            dcq_acc[rows, :] = dcq
            return rows

        @pl.when(i > j)
        def _():
            step(False)

        @pl.when(i == j)
        def _():
            rows = step(True)
            stage[...] = (dq_acc[rows, :] * SCALE).astype(MM)
            out = [pltpu.make_async_copy(stage, dq_ref.at[rows, :], sems.at[0]),
                   pltpu.make_async_copy(dcq_acc.at[rows, :], dcq_ref.at[rows, :], sems.at[1])]
            for cp in out:
                cp.start()
            for cp in out:
                cp.wait()

        @pl.when(i == nb - 1)
        def _():
            dk_ref[...] = _merge_pairs([dkt_acc[hd * HT:(hd + 1) * HT, :].T for hd in range(H)]).astype(MM)
            dv_ref[...] = _merge_pairs([dvt_acc[hd * HT:(hd + 1) * HT, :].T for hd in range(H)]).astype(MM)
            dck_ref[...] = dck_acc[...]

        if copies:
            @pl.when(n == len(pairs) - 1)
            def _():
                copies[1]()

    qi = lambda w: pl.BlockSpec((TM, w), lambda n, jt, it: (it[n], 0))
    qt = pl.BlockSpec((H * HT, TM), lambda n, jt, it: (0, it[n]))
    kj = lambda w: pl.BlockSpec((TM, w), lambda n, jt, it: (jt[n], 0))
    r_n = rider.k_n if rider else 0
    return pl.pallas_call(
        body, name=name,
        grid_spec=pltpu.PrefetchScalarGridSpec(
            num_scalar_prefetch=2, grid=(len(pairs),),
            in_specs=[qi(H * HT), qi(H * HT), qt, qt, kj(H * HT), kj(H * HT)] + [ANY] * r_n,
            out_specs=[ANY, kj(DA), kj(DA), ANY, pl.BlockSpec((H, TM), lambda n, jt, it: (0, jt[n]))] + [ANY] * r_n,
            scratch_shapes=[pltpu.VMEM((tp, DA), F32), pltpu.VMEM((tp, 128), F32), pltpu.VMEM((H * HT, TM), F32),
                            pltpu.VMEM((H * HT, TM), F32), pltpu.VMEM((H, TM), F32), pltpu.VMEM((TM, DA), MM),
                            pltpu.SemaphoreType.DMA((2,))] + (rider.scratch if rider else [])),
        out_shape=[S((tp, DA), MM), S((tp, DA), MM), S((tp, DA), MM), S((tp, 128), F32), S((H, tp), F32)]
        + (rider.out_shape if rider else []),
        compiler_params=_params(1),
    )(j_tab, i_tab, qb, dob, qbt, dobt, ka, va, *(rider.srcs if rider else []))


def fgate_bwd(dcq, dck, sg, name):
    tp = dcq.shape[0]
    nb = tp // TM

    def body(dcq_ref, dck_ref, sg_ref, dz_ref, db_ref, carry):
        @pl.when(pl.program_id(0) == 0)
        def _():
            carry[...] = jnp.zeros_like(carry)
            db_ref[...] = jnp.zeros_like(db_ref)

        row = lax.broadcasted_iota(jnp.int32, (TM, TM), 0)
        col = lax.broadcasted_iota(jnp.int32, (TM, TM), 1)
        dl = _tri_matmul(row <= col, dcq_ref[...] - dck_ref[...]) + carry[...]
        carry[...] = dl[0:1, :]
        dz = dl * sg_ref[...]
        dz_ref[...] = dz.astype(MM)
        db_ref[...] += jnp.sum(dz, axis=0, keepdims=True)

    rev = pl.BlockSpec((TM, 128), lambda i: (nb - 1 - i, 0))
    return pl.pallas_call(
        body, name=name, grid=(nb,),
        in_specs=[rev, rev, rev], out_specs=[rev, _full((1, 128))],
        out_shape=[S((tp, 128), MM), S((1, 128), F32)],
        scratch_shapes=[pltpu.VMEM((1, 128), F32)],
        compiler_params=_params(1),
    )(dcq, dck, sg)


def conv_bwd_pointwise(dcc, sv, pc, lng, lnb, wpw, name):
    tp = dcc.shape[0]

    def body(dcc_ref, sv_ref, b_ref, lng_ref, lnb_ref, wpw_ref, gc_ref, act_ref, pg_ref):
        @pl.when(pl.program_id(0) == 0)
        def _():
            pg_ref[...] = jnp.zeros_like(pg_ref)

        dconf = dcc_ref[:, 0:DC]
        g = lng_ref[...]
        xh, rs, ln = _layernorm_parts(sv_ref[:, 0:DC], g, lnb_ref[...])
        sig = _sigmoid(ln)
        act_ref[...] = (ln * sig).astype(MM)
        dact = lax.dot_general(dconf.astype(MM), wpw_ref[...], NT, preferred_element_type=F32)
        dln = dact * (sig * (1.0 + ln * (1.0 - sig)))
        dxh = dln * g
        ddw = rs * (dxh - jnp.mean(dxh, axis=-1, keepdims=True) - xh * jnp.mean(dxh * xh, axis=-1, keepdims=True))
        gc_ref[:, 0:DC] = ddw
        gc_ref[:, DC:2 * DC] = dcc_ref[:, DC:2 * DC] * b_ref[...]
        cs = lambda t: jnp.sum(t, axis=0, keepdims=True)
        pg_ref[0:1, :] += cs(dconf)
        pg_ref[1:2, :] += cs(dln * xh)
        pg_ref[2:3, :] += cs(dln)
        pg_ref[3:4, :] += cs(ddw)

    rows = lambda n: pl.BlockSpec((TM, n), lambda i: (i, 0))
    return pl.pallas_call(
        body, name=name, grid=(tp // TM,),
        in_specs=[rows(2 * DC), rows(2 * DC), pl.BlockSpec((TM, DC), lambda i: (i, 2)),
                  _full((1, DC)), _full((1, DC)), _full((DC, DC))],
        out_specs=[rows(2 * DC), rows(DC), _full((8, DC))],
        out_shape=[S((tp, 2 * DC), F32), S((tp, DC), MM), S((8, DC), F32)],
        compiler_params=_params(1),
    )(dcc, sv, pc, lng, lnb, wpw)


def conv_bwd_taps(gc, pc, dcc, sv, wdw, wsc, name):
    tp = gc.shape[0]
    nb = tp // TM
    hb = TM // HALO

    def body(gc_ref, gn_ref, pc_ref, hl_ref, dcc_ref, sv_ref, wdw_ref, wsc_ref, dpc_ref, wg_ref, ge, xe, ce, gs, xs):
        i = pl.program_id(0)

        @pl.when(i == 0)
        def _():
            wg_ref[...] = jnp.zeros_like(wg_ref)

        a, gt = pc_ref[:, 0:DC], pc_ref[:, DC:2 * DC]
        c, u = pc_ref[:, 3 * DC:4 * DC], pc_ref[:, 4 * DC:5 * DC]
        sig = _sigmoid(gt)
        _fill_halo(xe, hl_ref[:, 0:DC] * _sigmoid(hl_ref[:, DC:2 * DC]), a * sig, i == 0)
        _fill_halo(ce, hl_ref[:, 3 * DC:4 * DC] * hl_ref[:, 4 * DC:5 * DC], c * u, i == 0)
        ge[0:TM, :] = gc_ref[...]
        ge[TM:TM + HALO, :] = jnp.where(i == nb - 1, 0.0, gn_ref[...])
        _shift_copies(gs, ge, slice(0, DC))
        _shift_copies(xs, xe)
        dcv = gc_ref[:, DC:2 * DC]
        for r0 in range(0, TM, RC):
            rs = slice(r0, r0 + RC)
            ddw = gc_ref[rs, 0:DC]
            dglu = jnp.zeros((RC, DC), F32)
            for k in range(CK):
                dglu = dglu + wdw_ref[k:k + 1, :] * _window(ge, gs, CK - 1 - k, slice(0, DC), r0, RC)
                wg_ref[k:k + 1, :] += jnp.sum(ddw * _window(xe, xs, HALO - (CK - 1) + k, r0=r0, rows=RC), axis=0,
                                              keepdims=True)
            sig_c = _sigmoid(pc_ref[rs, DC:2 * DC])
            dpc_ref[rs, 0:DC] = (dglu * sig_c).astype(MM)
            dpc_ref[rs, DC:2 * DC] = (dglu * pc_ref[rs, 0:DC] * sig_c * (1.0 - sig_c)).astype(MM)
        dcu = jnp.zeros((TM, DC), F32)
        for k in range(SK):
            dcu = dcu + wsc_ref[k:k + 1, :] * ge[pl.ds(SK - 1 - k, TM), DC:2 * DC]
            wg_ref[32 + k:33 + k, :] += jnp.sum(dcv * ce[pl.ds(HALO - (SK - 1) + k, TM), :], axis=0, keepdims=True)
        dpc_ref[:, 2 * DC:3 * DC] = (dcc_ref[:, DC:2 * DC] * sv_ref[:, DC:2 * DC]).astype(MM)
        dpc_ref[:, 3 * DC:4 * DC] = (dcu * u).astype(MM)
        dpc_ref[:, 4 * DC:5 * DC] = (dcu * c).astype(MM)

    rows = lambda n: pl.BlockSpec((TM, n), lambda i: (i, 0))
    return pl.pallas_call(
        body, name=name, grid=(nb,),
        in_specs=[rows(2 * DC), pl.BlockSpec((HALO, 2 * DC), lambda i: (jnp.minimum((i + 1) * hb, nb * hb - 1), 0)),
                  rows(5 * DC), pl.BlockSpec((HALO, 5 * DC), lambda i: (jnp.maximum(i * hb - 1, 0), 0)),
                  rows(2 * DC), rows(2 * DC), _full((32, DC)), _full((8, DC))],
        out_specs=[rows(5 * DC), _full((40, DC))],
        out_shape=[S((tp, 5 * DC), MM), S((40, DC), F32)],
        scratch_shapes=[pltpu.VMEM((TM + HALO, 2 * DC), F32), pltpu.VMEM((HALO + TM, DC), F32),
                        pltpu.VMEM((HALO + TM, DC), F32), pltpu.VMEM((7, TM + HALO - 8, DC), F32),
                        pltpu.VMEM((7, TM + HALO - 8, DC), F32)],
        compiler_params=_params(1),
    )(gc, gc, pc, pc, dcc, sv, wdw, wsc)


def in_proj_bwd(dproj, w, h, g, dh_in, name, rider=None):
    tp = h.shape[0]
    nb = tp // TM

    def body(*refs):
        (dp_ref, w_ref, h_ref, g_ref, di_ref, dh_ref, dg_ref), copies = _split_rider(refs, 5, 2, rider)
        i = pl.program_id(0)

        @pl.when(i == 0)
        def _():
            if copies:
                copies[0]()
            dg_ref[...] = jnp.zeros_like(dg_ref)

        dhn = lax.dot_general(dp_ref[...], w_ref[...], NT, preferred_element_type=F32)
        dx, dg = _rmsnorm_bwd(h_ref[...], g_ref[...], dhn)
        dh_ref[...] = di_ref[...] + dx
        dg_ref[...] += dg

        if copies:
            @pl.when(i == nb - 1)
            def _():
                copies[1]()

    rows = lambda n: pl.BlockSpec((TM, n), lambda i: (i, 0))
    r_n = rider.k_n if rider else 0
    return pl.pallas_call(
        body, name=name, grid=(nb,),
        in_specs=[rows(NP_IN), _full((D, NP_IN)), rows(D), _full((1, D)), rows(D)] + [ANY] * r_n,
        out_specs=[rows(D), _full((1, D))] + [ANY] * r_n,
        out_shape=[S((tp, D), F32), S((1, D), F32)] + (rider.out_shape if rider else []),
        scratch_shapes=rider.scratch if rider else [],
        compiler_params=_params(1),
    )(dproj, w, h, g, dh_in, *(rider.srcs if rider else []))


def adamw(recvs, w, m, v, rb, name):
    l_n, r_n, c_n = w.shape

    def body(*refs):
        p_refs = refs[:l_n]
        w_ref, m_ref, v_ref, g_ref, d_ref, m2_ref, v2_ref = refs[l_n:]
        for l in range(l_n):
            g = p_refs[l][0].astype(F32)
            for s in range(1, NDEV):
                g = g + p_refs[l][s].astype(F32)
            m2 = ADAM_B1 * m_ref[l] + (1.0 - ADAM_B1) * g
            v2 = ADAM_B2 * v_ref[l] + (1.0 - ADAM_B2) * (g * g)
            m_hat = m2 / (1.0 - ADAM_B1 ** ADAM_STEP)
            v_hat = v2 / (1.0 - ADAM_B2 ** ADAM_STEP)
            g_ref[l] = g
            d_ref[l] = -ADAM_LR * (m_hat / (jnp.sqrt(v_hat) + ADAM_EPS) + ADAM_WD * w_ref[l])
            m2_ref[l] = m2
            v2_ref[l] = v2

    blk = pl.BlockSpec((l_n, rb, c_n), lambda r: (0, r, 0))
    return pl.pallas_call(
        body, name=name, grid=(r_n // rb,),
        in_specs=[pl.BlockSpec((NDEV, rb, c_n), lambda r: (0, r, 0))] * l_n + [blk, blk, blk],
        out_specs=[blk] * 4, out_shape=[S(w.shape, F32)] * 4,
        compiler_params=_params(1),
    )(*recvs, w, m, v)


TINY_ROWS = 168
REP_ROWS = 64


def _pack(parts, rows):
    flat = jnp.concatenate([p.reshape(-1) for p in parts])
    return jnp.pad(flat, (0, rows * 128 - flat.shape[0])).reshape(rows, 128)


def _unpack(buf, shapes):
    flat, out, o = buf.reshape(-1), [], 0
    for s in shapes:
        n = 1
        for d in s:
            n *= d
        out.append(flat[o:o + n].reshape(s))
        o += n
    return out


TINY_SHAPES = [(DEPTH, CK, DC // NDEV), (DEPTH, SK, DC // NDEV), (NM, D // NDEV), (DEPTH, DC // NDEV, DC)]
REP_SHAPES = [(DEPTH, D), (DEPTH, H), (DEPTH, DC), (DEPTH, DC), (DEPTH, DC), (DEPTH, DC), (DEPTH, D), (D,)]


def _to_padded_cols(w):
    pad = jnp.zeros(w.shape[:-1] + (NP_IN - N_IN,), w.dtype)
    return jnp.concatenate([w[..., :C0], w[..., C0 + H:], w[..., C0:C0 + H], pad], axis=-1)


def _from_padded_cols(w):
    return jnp.concatenate([w[..., :C0], w[..., F0:F0 + H], w[..., C0:F0]], axis=-1)


def kernel(x, meta_tokens, mix_norm_g, w_in, b_forget, w_conf_dw, b_conf_dw, conf_ln_g, conf_ln_b, w_conf_pw, b_conf_pw, w_sc_conv, w_out, mlp_norm_g, w_mlp1, w_mlp2, final_norm_g, loss_target, m_meta_tokens, m_mix_norm_g, m_w_in, m_b_forget, m_w_conf_dw, m_b_conf_dw, m_conf_ln_g, m_conf_ln_b, m_w_conf_pw, m_b_conf_pw, m_w_sc_conv, m_w_out, m_mlp_norm_g, m_w_mlp1, m_w_mlp2, m_final_norm_g, v_meta_tokens, v_mix_norm_g, v_w_in, v_b_forget, v_w_conf_dw, v_b_conf_dw, v_conf_ln_g, v_conf_ln_b, v_w_conf_pw, v_b_conf_pw, v_w_sc_conv, v_w_out, v_mlp_norm_g, v_w_mlp1, v_w_mlp2, v_final_norm_g):
    seq = x.shape[1]
    t_real = NM + seq
    tp = -(-t_real // TM) * TM

    tiny_w = _pack([w_conf_dw, w_sc_conv, meta_tokens, w_conf_pw], TINY_ROWS)
    g_in, g_tiny = gather([w_in[0].astype(MM), tiny_w], "gather_weights0")
    n_sh = w_in.shape[-1]
    assert w_mlp1.shape[-1] == FC and w_mlp2.shape[-2] == FC
    tiny = [_unpack(g_tiny[s], TINY_SHAPES) for s in range(NDEV)]
    wdw = jnp.concatenate([t[0] for t in tiny], axis=-1)
    wsc = jnp.concatenate([t[1] for t in tiny], axis=-1)
    meta = jnp.concatenate([t[2] for t in tiny], axis=-1)
    wpw = jnp.concatenate([t[3] for t in tiny], axis=1).astype(MM)
    wdw = jnp.pad(wdw, ((0, 0), (0, 32 - CK), (0, 0)))
    wsc = jnp.pad(wsc, ((0, 0), (0, 8 - SK), (0, 0)))
    bfp = jnp.pad(b_forget, ((0, 0), (0, 128 - H)))

    row = lambda a: a.reshape(1, -1)

    h = jnp.concatenate([meta, x[0], jnp.zeros((tp - t_real, D), F32)], axis=0)
    tgt = jnp.pad(loss_target[0], ((NM, tp - t_real), (0, 0)))
    saved, gathered = [], {}
    for l in range(DEPTH):
        win = _to_padded_cols(g_in.transpose(1, 0, 2).reshape(D, NDEV * n_sh))
        hn, qa, ka, va, pc, sg = in_proj(h, row(mix_norm_g[l]), win, row(bfp[l]), f"in_proj{l}")
        riding = [w_out[l].astype(MM), w_mlp1[l].astype(MM), w_mlp2[l].astype(MM)]
        riding += [w_in[l + 1].astype(MM)] if l + 1 < DEPTH else []
        o, lse, g_out, g_w1, g_w2, *nxt = attn_fwd(qa, ka, va, f"attn_fwd{l}", AllToAll(riding, [False] * len(riding)))
        gathered[l] = (g_out, g_w1, g_w2)
        if nxt:
            g_in = nxt[0]
        cc, sv = conv_fwd(pc, wdw[l], row(b_conf_dw[l]), row(conf_ln_g[l]), row(conf_ln_b[l]), wpw[l],
                          row(b_conf_pw[l]), wsc[l], f"conv_fwd{l}")
        h2, hn2 = out_proj(h, o, cc, g_out, row(mlp_norm_g[l]), f"out_proj{l}")
        r, z, h3 = mlp_fwd(hn2, h2, g_w1, g_w2, f"mlp_fwd{l}")
        saved.append((h, hn, qa, ka, va, pc, sg, o, lse, cc, sv, h2, hn2, r, z, win))
        h = h3

    dh, loss_part, d_gf = loss_head(h, tgt, row(final_norm_g), t_real, "loss_head")
    loss = lax.psum(loss_part[0, 0], ("x", "y", "c"))

    gw, recv, d_win = {}, {}, {}
    for l in reversed(range(DEPTH)):
        h0, hn, qa, ka, va, pc, sg, o, lse, cc, sv, h2, hn2, r, z, win = saved[l]
        g_out, g_w1, g_w2 = gathered[l]
        da, dh2, gw["mlp_g", l] = mlp_bwd(dh, r, g_w1, g_w2, h2, row(mlp_norm_g[l]), f"mlp_bwd{l}")
        d_w1 = matmul_tn(hn2, da, D, 4 * FC, f"dw_mlp1_{l}", out_dtype=MM, shard="n", sw=FC)
        d_w2 = matmul_tn(z, dh, 4 * FC, D, f"dw_mlp2_{l}", out_dtype=MM, shard="m", sw=FC)
        do, dcc = out_proj_bwd(dh2, g_out, f"out_proj_bwd{l}")
        d_wout = jnp.concatenate([matmul_tn(o, dh2, DA, D, f"dw_out_a{l}"), matmul_tn(cc, dh2, 2 * DC, D, f"dw_out_c{l}")],
                                 axis=0).reshape(NDEV, D // NDEV, D).astype(MM)
        qb, dob, qbt, dobt = attn_bwd_prep(qa, do, o, lse, f"attn_bwd_prep{l}")
        riding = [("wout", l, d_wout), ("w1", l, d_w1), ("w2", l, d_w2)] + ([("win", l + 1, d_win[l + 1])] if l + 1 < DEPTH else [])
        dq, dk, dv, dcq, dck, *arrived = attn_bwd(qb, dob, qbt, dobt, ka, va, f"attn_bwd{l}",
                                                  AllToAll([a for _, _, a in riding], [True] * len(riding)))
        for (k, kl, _), a in zip(riding, arrived):
            recv[k, kl] = a
        dz, gw["bf", l] = fgate_bwd(dcq, jnp.pad(dck.T, ((0, 0), (0, 128 - H))), sg, f"fgate_bwd{l}")
        gc, act, pg = conv_bwd_pointwise(dcc, sv, pc, row(conf_ln_g[l]), row(conf_ln_b[l]), wpw[l], f"conv_bwd_pw{l}")
        gw["wpw", l] = matmul_tn(act, dcc, DC, DC, f"dw_conf_pw{l}", n=DC)
        dpc, wg = conv_bwd_taps(gc, pc, dcc, sv, wdw[l], wsc[l], f"conv_bwd_taps{l}")
        gw["pg", l], gw["wg", l] = pg, wg
        dproj = jnp.concatenate([dq, dk, dv, dpc, dz], axis=1)
        d_win[l] = _from_padded_cols(matmul_tn(hn, dproj, 512, NP_IN, f"dw_in{l}")).reshape(D, NDEV, n_sh).transpose(
            1, 0, 2).astype(MM)
        rider = AllToAll([d_win[0]], [True]) if l == 0 else None
        dh, gw["mix_g", l], *arrived = in_proj_bwd(dproj, win, h0, row(mix_norm_g[l]), dh2, f"in_proj_bwd{l}", rider)
        if rider:
            recv["win", 0] = arrived[0]

    grad_x = dh[NM:t_real][None]
    stack = lambda k: jnp.stack([gw[k, l] for l in range(DEPTH)])

    d_wdw = stack("wg")[:, 0:CK].reshape(DEPTH, CK, NDEV, DC // NDEV).transpose(2, 0, 1, 3)
    d_wsc = stack("wg")[:, 32:32 + SK].reshape(DEPTH, SK, NDEV, DC // NDEV).transpose(2, 0, 1, 3)
    d_meta = dh[0:NM].reshape(NM, NDEV, D // NDEV).transpose(1, 0, 2)
    d_wpw = stack("wpw").reshape(DEPTH, NDEV, DC // NDEV, DC).transpose(1, 0, 2, 3)
    d_tiny = jnp.stack([_pack([d_wdw[p], d_wsc[p], d_meta[p], d_wpw[p]], TINY_ROWS) for p in range(NDEV)])
    pgs = stack("pg")
    d_rep = _pack([stack("mix_g").reshape(DEPTH, D), stack("bf")[:, 0, :H], pgs[:, 3], pgs[:, 1], pgs[:, 2], pgs[:, 0],
                   stack("mlp_g").reshape(DEPTH, D), d_gf.reshape(D)], REP_ROWS)
    r_tiny, r_rep = exchange([d_tiny, d_rep], [True, False], "exchange_grads")

    per_layer = lambda k: [recv[k, l] for l in range(DEPTH)]
    res = {}
    res["w_in"] = adamw(per_layer("win"), w_in, m_w_in, v_w_in, 256, "adamw_w_in")
    res["w_out"] = adamw(per_layer("wout"), w_out, m_w_out, v_w_out, D // NDEV, "adamw_w_out")
    res["w_mlp1"] = adamw(per_layer("w1"), w_mlp1, m_w_mlp1, v_w_mlp1, 256, "adamw_w_mlp1")
    res["w_mlp2"] = adamw(per_layer("w2"), w_mlp2, m_w_mlp2, v_w_mlp2, 128, "adamw_w_mlp2")
    tiny_names = ["w_conf_dw", "w_sc_conv", "meta_tokens", "w_conf_pw"]
    tiny_wmv = [[w_conf_dw, w_sc_conv, meta_tokens, w_conf_pw], [m_w_conf_dw, m_w_sc_conv, m_meta_tokens, m_w_conf_pw],
                [v_w_conf_dw, v_w_sc_conv, v_meta_tokens, v_w_conf_pw]]
    rep_names = ["mix_norm_g", "b_forget", "b_conf_dw", "conf_ln_g", "conf_ln_b", "b_conf_pw", "mlp_norm_g", "final_norm_g"]
    rep_wmv = [[mix_norm_g, b_forget, b_conf_dw, conf_ln_g, conf_ln_b, b_conf_pw, mlp_norm_g, final_norm_g],
               [m_mix_norm_g, m_b_forget, m_b_conf_dw, m_conf_ln_g, m_conf_ln_b, m_b_conf_pw, m_mlp_norm_g, m_final_norm_g],
               [v_mix_norm_g, v_b_forget, v_b_conf_dw, v_conf_ln_g, v_conf_ln_b, v_b_conf_pw, v_mlp_norm_g, v_final_norm_g]]
    for names, wmv, shapes, recv_pack, rows_n, nm in ((tiny_names, tiny_wmv, TINY_SHAPES, r_tiny, TINY_ROWS, "adamw_tiny"),
                                                 (rep_names, rep_wmv, REP_SHAPES, r_rep, REP_ROWS, "adamw_rep")):
        packed = [_pack(group, rows_n)[None] for group in wmv]
        outs = adamw([recv_pack], *packed, rows_n, nm)
        parts = [_unpack(a[0], shapes) for a in outs]
        for k, n in enumerate(names):
            res[n] = [parts[q][k] for q in range(4)]

    order = ["meta_tokens", "mix_norm_g", "w_in", "b_forget", "w_conf_dw", "b_conf_dw", "conf_ln_g", "conf_ln_b",
             "w_conf_pw", "b_conf_pw", "w_sc_conv", "w_out", "mlp_norm_g", "w_mlp1", "w_mlp2", "final_norm_g"]
    return (loss, grad_x, *[res[n][0] for n in order], *[res[n][1] for n in order],
            *[res[n][2] for n in order], *[res[n][3] for n in order])
```
